```python
import math
import jax
import jax.numpy as jnp
from jax import lax
import numpy as np

D_MODEL = 1024
BATCH = 4
SEQ = 4096
DEPTH = 1
DEC_BATCH = 128
DEC_SEQ = 8
PAST_LEN = 8192
PAGE_SIZE = 128

MLA_HEADS = 8
MLA_Q_LORA = 384
MLA_KV_LORA = 256
MLA_NOPE = 64
MLA_ROPE = 32
MLA_V = 64
NSA_HEADS = 8
NSA_KV_HEADS = 2
NSA_GROUP = NSA_HEADS // NSA_KV_HEADS
HEAD_DIM = 64
CMP_LEN = 32
CMP_STRIDE = 16
CMP_HIDDEN = 128
SEL_BLOCK = 64
SEL_TOPN = 16
WINDOW = 512
KV_COLS = 2 * NSA_KV_HEADS * HEAD_DIM
MIX_DIM = MLA_HEADS * MLA_V + NSA_HEADS * HEAD_DIM
IN_DIM = MLA_Q_LORA + MLA_KV_LORA + MLA_ROPE + NSA_HEADS * HEAD_DIM + 3 * KV_COLS + 3 * NSA_HEADS
N_EXPERTS = 32
TOP_K = 4
D_FF = 1024
SWIGLU_LIMIT = 7.0
SWIGLU_ALPHA = 1.702
NUM_BUCKETS = 32
MAX_DISTANCE = 128
ROPE_THETA = 10000.0
NORM_EPS = 1e-6
Q_BLOCK = 128

kernel_name = 'hymba_mla_nsa_moe_adaln_decode_step'


def rms_norm(x, g):
    xf = x.astype(jnp.float32)
    y = xf * lax.rsqrt(jnp.mean(xf * xf, axis=-1, keepdims=True) + NORM_EPS)
    return (y * g.astype(jnp.float32)).astype(x.dtype)


def masked_softmax(logits, mask, axis=-1):
    logits = jnp.where(mask, logits.astype(jnp.float32), -jnp.inf)
    m = jnp.max(logits, axis=axis, keepdims=True)
    m = jnp.where(jnp.isfinite(m), m, 0.0)
    p = jnp.exp(logits - m)
    s = jnp.sum(p, axis=axis, keepdims=True)
    return p / jnp.where(s > 0.0, s, 1.0)


def rope(x, pos):
    half = x.shape[-1] // 2
    inv_freq = 1.0 / (ROPE_THETA ** (jnp.arange(half, dtype=jnp.float32) / half))
    ang = pos.astype(jnp.float32)[:, None] * inv_freq[None, :]
    shape = (1, pos.shape[0]) + (1,) * (x.ndim - 3) + (half,)
    cos = jnp.cos(ang).reshape(shape)
    sin = jnp.sin(ang).reshape(shape)
    xf = x.astype(jnp.float32)
    x1, x2 = xf[..., :half], xf[..., half:]
    return jnp.concatenate([x1 * cos - x2 * sin, x1 * sin + x2 * cos], axis=-1).astype(x.dtype)


def t5_bucket(dist):
    n = jnp.maximum(dist, 0)
    max_exact = NUM_BUCKETS // 2
    nf = jnp.maximum(n, 1).astype(jnp.float32)
    large = max_exact + (jnp.log(nf / max_exact) / math.log(MAX_DISTANCE / max_exact)
                         * (NUM_BUCKETS - max_exact)).astype(jnp.int32)
    return jnp.where(n < max_exact, n, jnp.minimum(large, NUM_BUCKETS - 1))


def rel_bias_grouped(rel_bias, dist):
    b = rel_bias[t5_bucket(dist)].reshape(dist.shape + (NSA_KV_HEADS, NSA_GROUP))
    return jnp.transpose(b, (0, 2, 3, 1))


def query_blocks(fn, q_pos, *arrays):
    n_q = q_pos.shape[0]
    qb = min(Q_BLOCK, n_q)
    nb = n_q // qb

    def split(a):
        return jnp.moveaxis(a.reshape((a.shape[0], nb, qb) + a.shape[2:]), 1, 0)

    def merge(o):
        return jnp.moveaxis(o, 0, 1).reshape((o.shape[1], nb * qb) + o.shape[3:])

    out = lax.map(lambda args: fn(*args), (q_pos.reshape(nb, qb),) + tuple(split(a) for a in arrays))
    return jax.tree_util.tree_map(merge, out)


def project(h, pos, P):
    B, S, _ = h.shape
    sizes = (MLA_Q_LORA, MLA_KV_LORA, MLA_ROPE, NSA_HEADS * HEAD_DIM, KV_COLS, KV_COLS, KV_COLS, 3 * NSA_HEADS)
    cuts = np.cumsum(sizes)[:-1].tolist()
    q_a, c_kv, k_rope, q_n, kv_cmp, kv_slc, kv_swa, g = jnp.split(h @ P['w_in'], cuts, axis=-1)
    q = jnp.einsum('bsc,chd->bshd', rms_norm(q_a, P['g_q_a']), P['w_q_b'])
    q_rope = rope(q[..., MLA_NOPE:], pos)
    q_lat = jnp.einsum('bshn,chn->bshc', q[..., :MLA_NOPE], P['w_kv_b'][..., :MLA_NOPE])
    mla_rows = jnp.concatenate([rms_norm(c_kv, P['g_kv_a']), rope(k_rope, pos)], axis=-1)
    kv_shape = (B, S, 2, NSA_KV_HEADS, HEAD_DIM)
    gates = jax.nn.sigmoid(g.reshape(B, S, NSA_KV_HEADS, NSA_GROUP, 3))
    return (q_lat, q_rope, mla_rows, q_n.reshape(B, S, NSA_KV_HEADS, NSA_GROUP, HEAD_DIM),
            kv_cmp.reshape(kv_shape), kv_slc.reshape(kv_shape), kv_swa.reshape(kv_shape), gates)


def mla_attend(q_lat, q_rope, q_pos, mla_keys, k_pos):
    ckv = mla_keys[..., :MLA_KV_LORA]
    krope = mla_keys[..., MLA_KV_LORA:]
    scale = (MLA_NOPE + MLA_ROPE) ** -0.5

    def block(qp, ql, qr):
        logits = (jnp.einsum('bqhc,bkc->bqhk', ql, ckv)
                  + jnp.einsum('bqhr,bkr->bqhk', qr, krope)).astype(jnp.float32) * scale
        p = masked_softmax(logits, (k_pos[None, :] <= qp[:, None])[None, :, None, :])
        return jnp.einsum('bqhk,bkc->bqhc', p.astype(ckv.dtype), ckv)

    return query_blocks(block, q_pos, q_lat, q_rope)


def compress(kv, P):
    B, L = kv.shape[:2]
    n_half = L // CMP_STRIDE
    halves = kv[:, :n_half * CMP_STRIDE].reshape((B, n_half, CMP_STRIDE) + kv.shape[2:])
    w1 = P['cmp_w1']
    first = jnp.einsum('bnpjkd,jpdh->bnjkh', halves, w1[:, :CMP_STRIDE])
    second = jnp.einsum('bnpjkd,jpdh->bnjkh', halves, w1[:, CMP_STRIDE:])
    pe_term = jnp.einsum('jpd,jpdh->jh', P['cmp_pe'], w1)[:, None, :]
    hidden = jax.nn.gelu(first[:, :-1] + second[:, 1:] + pe_term, approximate=False)
    out = jnp.einsum('bnjkh,jhd->bnjkd', hidden, P['cmp_w2'])
    cmp_end = jnp.arange(n_half - 1) * CMP_STRIDE + CMP_LEN - 1
    return out[:, :, 0], out[:, :, 1], cmp_end


def nsa_cmp_slc(q, q_pos, k_cmp, v_cmp, cmp_end, fetch, n_sel, rel_bias):
    scale = HEAD_DIM ** -0.5
    dist_c = q_pos[:, None] - cmp_end[None, :]
    logits = jnp.einsum('bqkgd,bnkd->bqkgn', q, k_cmp).astype(jnp.float32) * scale
    logits = logits + rel_bias_grouped(rel_bias, dist_c)[None]
    p_cmp = masked_softmax(logits, (dist_c >= 0)[None, :, None, None, :])
    o_cmp = jnp.einsum('bqkgn,bnkd->bqkgd', p_cmp.astype(v_cmp.dtype), v_cmp)
    c_start = jnp.arange(cmp_end.shape[0]) * CMP_STRIDE
    s_start = jnp.arange(n_sel) * SEL_BLOCK
    overlap = ((c_start[:, None] < s_start[None, :] + SEL_BLOCK)
               & (c_start[:, None] + CMP_LEN > s_start[None, :])).astype(jnp.float32)
    imp = jnp.einsum('bqkgn,ns->bqks', p_cmp, overlap)
    blk = jnp.arange(n_sel)[None, :]
    cur = (q_pos // SEL_BLOCK)[:, None]
    forced = (blk == 0) | (blk == cur) | (blk == cur - 1)
    future = blk > cur
    imp = jnp.where(future[None, :, None, :], -jnp.inf, jnp.where(forced[None, :, None, :], jnp.inf, imp))
    top_val, sel = lax.top_k(imp, min(SEL_TOPN, n_sel))
    valid = top_val > -jnp.inf
    kv_sel = fetch(sel)
    k_pos = sel[..., None] * SEL_BLOCK + jnp.arange(SEL_BLOCK)
    dist_s = q_pos[None, :, None, None, None] - k_pos
    mask = (valid[..., None] & (dist_s >= 0))[:, :, :, None]
    tbl = jnp.transpose(rel_bias.reshape(NUM_BUCKETS, NSA_KV_HEADS, NSA_GROUP), (1, 0, 2))
    kidx = jnp.arange(NSA_KV_HEADS)[None, None, :, None, None]
    bias_s = jnp.moveaxis(tbl[kidx, t5_bucket(dist_s)], -1, 3)
    logits_s = jnp.einsum('bqkgd,bqkntd->bqkgnt', q, kv_sel[..., 0, :]).astype(jnp.float32) * scale + bias_s
    p_s = masked_softmax(logits_s, mask, axis=(-2, -1))
    o_slc = jnp.einsum('bqkgnt,bqkntd->bqkgd', p_s.astype(kv_sel.dtype), kv_sel[..., 1, :])
    return o_cmp, o_slc


def swa_band(q, kv, rel_bias):
    B, S = q.shape[:2]
    n_qb = S // Q_BLOCK
    n_band = WINDOW // Q_BLOCK + 1
    kvp = jnp.pad(kv, ((0, 0), (WINDOW, 0), (0, 0), (0, 0), (0, 0)))
    kvp = kvp.reshape((B, n_qb + n_band - 1, Q_BLOCK) + kv.shape[2:])
    band = jnp.concatenate([kvp[:, j:j + n_qb] for j in range(n_band)], axis=2)
    qb = q.reshape((B, n_qb, Q_BLOCK) + q.shape[2:])
    logits = jnp.einsum('bnqkgd,bnskd->bnqkgs', qb, band[:, :, :, 0]).astype(jnp.float32) * HEAD_DIM ** -0.5
    n_keys = n_band * Q_BLOCK
    dist = jnp.arange(Q_BLOCK)[:, None] + WINDOW - jnp.arange(n_keys)[None, :]
    k_pos = (jnp.arange(n_qb) * Q_BLOCK - WINDOW)[:, None] + jnp.arange(n_keys)[None, :]
    mask = (((dist >= 0) & (dist < WINDOW))[None, None, :, None, None, :]
            & (k_pos >= 0)[None, :, None, None, None, :])
    p = masked_softmax(logits + rel_bias_grouped(rel_bias, dist)[None, None], mask)
    out = jnp.einsum('bnqkgs,bnskd->bnqkgd', p.astype(kv.dtype), band[:, :, :, 1])
    return out.reshape(q.shape)


def swa_dense(q, q_pos, kv, k_pos, rel_bias):
    logits = jnp.einsum('bqkgd,bskd->bqkgs', q, kv[:, :, 0]).astype(jnp.float32) * HEAD_DIM ** -0.5
    dist = q_pos[:, None] - k_pos[None, :]
    mask = ((dist >= 0) & (dist < WINDOW))[None, :, None, None, :]
    p = masked_softmax(logits + rel_bias_grouped(rel_bias, dist)[None], mask)
    return jnp.einsum('bqkgs,bskd->bqkgd', p.astype(kv.dtype), kv[:, :, 1])


def to_blocks(rows, n_blocks):
    pad = n_blocks * SEL_BLOCK - rows.shape[1]
    rows = jnp.pad(rows, ((0, 0), (0, pad)) + ((0, 0),) * (rows.ndim - 2))
    return rows.reshape((rows.shape[0], n_blocks, SEL_BLOCK) + rows.shape[2:])


def mixer_out(o_lat, o_cmp, o_slc, o_swa, gates, P):
    B, S = o_lat.shape[:2]
    o_mla = jnp.einsum('bshc,chv->bshv', o_lat, P['w_kv_b'][..., MLA_NOPE:])
    o_nsa = gates[..., 0:1] * o_cmp + gates[..., 1:2] * o_slc + gates[..., 2:3] * o_swa
    mix = jnp.concatenate([o_mla.reshape(B, S, -1), o_nsa.reshape(B, S, -1)], axis=-1)
    return mix @ P['w_o']


def prompt_mixer(h, P, rel_bias):
    B, S, _ = h.shape
    pos = jnp.arange(S)
    q_lat, q_rope, mla_rows, q_n, kv_cmp, kv_slc, kv_swa, gates = project(h, pos, P)
    o_lat = mla_attend(q_lat, q_rope, pos, mla_rows, pos)
    k_cmp, v_cmp, cmp_end = compress(kv_cmp, P)
    n_sel = -(-S // SEL_BLOCK)
    blocks = to_blocks(kv_slc, n_sel)
    bidx = jnp.arange(B)[:, None, None, None]
    kidx = jnp.arange(NSA_KV_HEADS)[None, None, :, None]

    def fetch(sel):
        return blocks[bidx, sel, :, :, kidx, :]

    o_cmp, o_slc = query_blocks(
        lambda qp, qb: nsa_cmp_slc(qb, qp, k_cmp, v_cmp, cmp_end, fetch, n_sel, rel_bias), pos, q_n)
    o_swa = swa_band(q_n, kv_swa, rel_bias)
    y = mixer_out(o_lat, o_cmp, o_slc, o_swa, gates, P)
    return y, (mla_rows, kv_cmp, kv_slc, kv_swa[:, S - min(WINDOW, S):])


def sample_mixer(h, P, cache_mla, cache_cmp, cache_slc, state_swa, page_table, rel_bias):
    Bd, Q, _ = h.shape
    page = cache_mla.shape[1]
    past = page_table.shape[1] * page
    pos = past + jnp.arange(Q)
    q_lat, q_rope, mla_rows, q_n, kv_cmp, kv_slc, kv_swa, gates = project(h, pos, P)

    def gather_past(pool):
        return pool[page_table].reshape((Bd, past) + pool.shape[2:])

    k_pos = jnp.arange(past + Q)
    o_lat = mla_attend(q_lat, q_rope, pos, jnp.concatenate([gather_past(cache_mla), mla_rows], axis=1), k_pos)
    k_cmp, v_cmp, cmp_end = compress(jnp.concatenate([gather_past(cache_cmp), kv_cmp], axis=1), P)
    n_sel = -(-(past + Q) // SEL_BLOCK)
    blocks_per_page = page // SEL_BLOCK
    n_past_blk = past // SEL_BLOCK
    pool_blocks = cache_slc.reshape((-1, SEL_BLOCK) + cache_slc.shape[2:])
    new_blocks = to_blocks(kv_slc, n_sel - n_past_blk)
    bidx = jnp.arange(Bd)[:, None, None, None]
    kidx = jnp.arange(NSA_KV_HEADS)[None, None, :, None]

    def fetch(sel):
        in_past = sel < n_past_blk
        pb = jnp.minimum(sel, n_past_blk - 1)
        phys = page_table[bidx, pb // blocks_per_page] * blocks_per_page + pb % blocks_per_page
        past_kv = pool_blocks[phys, :, :, kidx, :]
        new_kv = new_blocks[bidx, jnp.maximum(sel - n_past_blk, 0), :, :, kidx, :]
        return jnp.where(in_past[..., None, None, None], past_kv, new_kv)

    o_cmp, o_slc = query_blocks(
        lambda qp, qb: nsa_cmp_slc(qb, qp, k_cmp, v_cmp, cmp_end, fetch, n_sel, rel_bias), pos, q_n)
    buf = state_swa.shape[1]
    swa_keys = jnp.concatenate([state_swa, kv_swa], axis=1)
    swa_pos = past - buf + jnp.arange(buf + Q)
    o_swa = swa_dense(q_n, pos, swa_keys, swa_pos, rel_bias)
    y = mixer_out(o_lat, o_cmp, o_slc, o_swa, gates, P)
    keep = min(WINDOW, buf + Q)
    return y, (mla_rows, kv_cmp, kv_slc, swa_keys[:, buf + Q - keep:])


def moe(h, P):
    B, S, D = h.shape
    t = h.reshape(B * S, D)
    logits = (t @ P['w_router'] + P['b_router']).astype(jnp.float32)
    top_val, top_idx = lax.top_k(logits, TOP_K)
    weights = jax.nn.softmax(top_val, axis=-1)
    combine = jnp.sum(weights[..., None] * jax.nn.one_hot(top_idx, N_EXPERTS, dtype=jnp.float32), axis=1)
    out = jnp.zeros((B * S, D), jnp.float32)
    for e in range(N_EXPERTS):
        gu = t @ P['w_gate_up'][e] + P['b_gate_up'][e]
        glu = jnp.minimum(gu[:, :D_FF], SWIGLU_LIMIT)
        lin = jnp.clip(gu[:, D_FF:], -SWIGLU_LIMIT, SWIGLU_LIMIT)
        act = glu * jax.nn.sigmoid(SWIGLU_ALPHA * glu) * (lin + 1.0)
        out = out + combine[:, e:e + 1] * (act @ P['w_down'][e] + P['b_down'][e]).astype(jnp.float32)
    return out.reshape(B, S, D).astype(h.dtype)


def decoder_layer(x, c, P, mixer):
    mod = (jax.nn.silu(c) @ P['w_ada'] + P['b_ada'])[:, None, :]
    sh1, sc1, g1, sh2, sc2, g2 = jnp.split(mod, 6, axis=-1)
    mix, state = mixer(rms_norm(x, P['g_norm1']) * (1.0 + sc1) + sh1)
    x = x + g1 * mix
    x = x + g2 * moe(rms_norm(x, P['g_norm2']) * (1.0 + sc2) + sh2, P)
    return x, state


def setup_inputs(seed: int = 0) -> dict:
    key = jax.random.key(seed)
    ks = iter(jax.random.split(key, 40))

    def nrm(shape, scale=1.0):
        return jax.random.normal(next(ks), shape, jnp.float32) * scale

    n_pages = PAST_LEN // PAGE_SIZE
    n_pool = (DEC_BATCH * n_pages * 5) // 4
    swa_buf = min(WINDOW, PAST_LEN)
    kv_tail = (2, NSA_KV_HEADS, HEAD_DIM)
    x_prompt = nrm((BATCH, SEQ, D_MODEL))
    x_sample = nrm((DEC_BATCH, DEC_SEQ, D_MODEL))
    c_prompt = nrm((BATCH, D_MODEL))
    c_sample = nrm((DEC_BATCH, D_MODEL))
    cache_mla = nrm((DEPTH, n_pool, PAGE_SIZE, MLA_KV_LORA + MLA_ROPE))
    cache_nsa_cmp = nrm((DEPTH, n_pool, PAGE_SIZE) + kv_tail)
    cache_nsa_slc = nrm((DEPTH, n_pool, PAGE_SIZE) + kv_tail)
    state_nsa_swa = nrm((DEPTH, DEC_BATCH, swa_buf) + kv_tail)
    perm = jax.random.permutation(next(ks), n_pool)[:DEC_BATCH * n_pages]
    page_table = perm.reshape(DEC_BATCH, n_pages).astype(jnp.int32)
    return {
        'x_prompt': x_prompt, 'x_sample': x_sample, 'c_prompt': c_prompt, 'c_sample': c_sample,
        'cache_mla': cache_mla, 'cache_nsa_cmp': cache_nsa_cmp, 'cache_nsa_slc': cache_nsa_slc,
        'state_nsa_swa': state_nsa_swa, 'page_table': page_table,
        'rel_bias': nrm((NUM_BUCKETS, NSA_HEADS), 0.1),
        'w_ada': nrm((DEPTH, D_MODEL, 6 * D_MODEL), D_MODEL ** -0.5),
        'b_ada': nrm((DEPTH, 6 * D_MODEL), 0.02),
        'g_norm1': 1.0 + nrm((DEPTH, D_MODEL), 0.02),
        'g_norm2': 1.0 + nrm((DEPTH, D_MODEL), 0.02),
        'w_in': nrm((DEPTH, D_MODEL, IN_DIM), D_MODEL ** -0.5),
        'g_q_a': 1.0 + nrm((DEPTH, MLA_Q_LORA), 0.02),
        'w_q_b': nrm((DEPTH, MLA_Q_LORA, MLA_HEADS, MLA_NOPE + MLA_ROPE), MLA_Q_LORA ** -0.5),
        'g_kv_a': 1.0 + nrm((DEPTH, MLA_KV_LORA), 0.02),
        'w_kv_b': nrm((DEPTH, MLA_KV_LORA, MLA_HEADS, MLA_NOPE + MLA_V), MLA_KV_LORA ** -0.5),
        'cmp_pe': nrm((DEPTH, 2, CMP_LEN, HEAD_DIM), 0.1),
        'cmp_w1': nrm((DEPTH, 2, CMP_LEN, HEAD_DIM, CMP_HIDDEN), (CMP_LEN * HEAD_DIM) ** -0.5),
        'cmp_w2': nrm((DEPTH, 2, CMP_HIDDEN, HEAD_DIM), CMP_HIDDEN ** -0.5),
        'w_o': nrm((DEPTH, MIX_DIM, D_MODEL), MIX_DIM ** -0.5),
        'w_router': nrm((DEPTH, D_MODEL, N_EXPERTS), D_MODEL ** -0.5),
        'b_router': nrm((DEPTH, N_EXPERTS), 0.01),
        'w_gate_up': nrm((DEPTH, N_EXPERTS, D_MODEL, 2 * D_FF), D_MODEL ** -0.5),
        'b_gate_up': nrm((DEPTH, N_EXPERTS, 2 * D_FF), 0.01),
        'w_down': nrm((DEPTH, N_EXPERTS, D_FF, D_MODEL), D_FF ** -0.5),
        'b_down': nrm((DEPTH, N_EXPERTS, D_MODEL), 0.01),
        'g_final': 1.0 + nrm((D_MODEL,), 0.02),
    }


def reference(x_prompt, x_sample, c_prompt, c_sample, cache_mla, cache_nsa_cmp, cache_nsa_slc, state_nsa_swa,
              page_table, rel_bias, w_ada, b_ada, g_norm1, g_norm2, w_in, g_q_a, w_q_b, g_kv_a, w_kv_b,
              cmp_pe, cmp_w1, cmp_w2, w_o, w_router, b_router, w_gate_up, b_gate_up, w_down, b_down, g_final):
    xp, xs = x_prompt, x_sample
    st_p, st_s = [], []
    for l in range(DEPTH):
        P = dict(w_ada=w_ada[l], b_ada=b_ada[l], g_norm1=g_norm1[l], g_norm2=g_norm2[l], w_in=w_in[l],
                 g_q_a=g_q_a[l], w_q_b=w_q_b[l], g_kv_a=g_kv_a[l], w_kv_b=w_kv_b[l], cmp_pe=cmp_pe[l],
                 cmp_w1=cmp_w1[l], cmp_w2=cmp_w2[l], w_o=w_o[l], w_router=w_router[l], b_router=b_router[l],
                 w_gate_up=w_gate_up[l], b_gate_up=b_gate_up[l], w_down=w_down[l], b_down=b_down[l])
        xp, sp = decoder_layer(xp, c_prompt, P, lambda h: prompt_mixer(h, P, rel_bias))
        xs, ss = decoder_layer(xs, c_sample, P, lambda h: sample_mixer(
            h, P, cache_mla[l], cache_nsa_cmp[l], cache_nsa_slc[l], state_nsa_swa[l], page_table, rel_bias))
        st_p.append(sp)
        st_s.append(ss)
    y_prompt = rms_norm(xp, g_final)
    y_sample = rms_norm(xs, g_final)
    new_mla_prompt = jnp.stack([s[0] for s in st_p])
    new_mla_sample = jnp.stack([s[0] for s in st_s])
    new_cmp_prompt = jnp.stack([s[1] for s in st_p])
    new_cmp_sample = jnp.stack([s[1] for s in st_s])
    new_slc_prompt = jnp.stack([s[2] for s in st_p])
    new_slc_sample = jnp.stack([s[2] for s in st_s])
    new_swa_prompt = jnp.stack([s[3] for s in st_p])
    new_swa_sample = jnp.stack([s[3] for s in st_s])
    return (y_prompt, y_sample, new_mla_prompt, new_mla_sample, new_cmp_prompt, new_cmp_sample,
            new_slc_prompt, new_slc_sample, new_swa_prompt, new_swa_sample)
```

```python
import functools
import math

import jax
import jax.numpy as jnp
from jax import lax
from jax.experimental import pallas as pl
from jax.experimental.pallas import tpu as pltpu

F32, BF16, I32 = jnp.float32, jnp.bfloat16, jnp.int32

MLA_HEADS = 8
MLA_Q_LORA = 384
MLA_KV_LORA = 256
MLA_NOPE = 64
MLA_ROPE = 32
MLA_V = 64
NSA_HEADS = 8
NSA_KV_HEADS = 2
NSA_GROUP = NSA_HEADS // NSA_KV_HEADS
HEAD_DIM = 64
CMP_LEN = 32
CMP_STRIDE = 16
CMP_HIDDEN = 128
SEL_BLOCK = 64
SEL_TOPN = 16
WINDOW = 512
KV_COLS = 2 * NSA_KV_HEADS * HEAD_DIM
N_EXPERTS = 32
TOP_K = 4
SWIGLU_LIMIT = 7.0
SWIGLU_ALPHA = 1.702
NUM_BUCKETS = 32
MAX_DISTANCE = 128
ROPE_THETA = 10000.0
NORM_EPS = 1e-6

LANE = 128
VMEM_LIMIT = 56 * 1024 * 1024
NEG = -1e30

MLA_QW = MLA_KV_LORA + LANE
MLA_SCALE = (MLA_NOPE + MLA_ROPE) ** -0.5
NSA_SCALE = HEAD_DIM ** -0.5

_C_QA = 0
_C_CKV = _C_QA + MLA_Q_LORA
_C_QN = _C_CKV + MLA_KV_LORA
_C_CMP = _C_QN + NSA_HEADS * LANE
_C_SLC = _C_CMP + KV_COLS
_C_SWA = _C_SLC + KV_COLS
_C_KR = _C_SWA + KV_COLS
_C_KRR = _C_KR + LANE
_C_G = _C_KRR + LANE
_C_END = _C_G + NSA_KV_HEADS * LANE


def _cparams(sem, vmem=VMEM_LIMIT):
    return pltpu.CompilerParams(dimension_semantics=sem, vmem_limit_bytes=vmem)


def _dot(a, b):
    return jnp.dot(a, b, preferred_element_type=F32)


def _dot_nt(a, b):
    return lax.dot_general(a, b, (((1,), (1,)), ((), ())), preferred_element_type=F32)


def _rms(x, g):
    return x * lax.rsqrt(jnp.mean(x * x, axis=-1, keepdims=True) + NORM_EPS) * g


def _sigmoid(x):
    return 1.0 / (1.0 + jnp.exp(-x))


def _ada_kernel(c_ref, w_ref, b_ref, o_ref):
    c = c_ref[...]
    o_ref[...] = _dot((c * _sigmoid(c)).astype(BF16), w_ref[...]) + b_ref[...]


def _ada_mod(c_all, w_ada, b_ada):
    m, d = c_all.shape
    n = w_ada.shape[1]
    tn = 1536
    return pl.pallas_call(
        _ada_kernel, grid=(n // tn,),
        in_specs=[pl.BlockSpec((m, d), lambda i: (0, 0)), pl.BlockSpec((d, tn), lambda i: (0, i)),
                  pl.BlockSpec((1, tn), lambda i: (0, i))],
        out_specs=pl.BlockSpec((m, tn), lambda i: (0, i)),
        out_shape=jax.ShapeDtypeStruct((m, n), F32),
        compiler_params=_cparams(("arbitrary",)), name="ada_mod")(c_all, w_ada, b_ada)


def _proj_kernel(x_ref, sc_ref, sh_ref, g1_ref, cs_ref, win_ref, gqa_ref, wqb_ref, gkva_ref, wkn_ref,
                 qmla_ref, kq_ref, mla_ref, qn_ref, cmp_ref, slc_ref, swa_ref, slcb_ref, swab_ref, gate_ref):
    h = _rms(x_ref[...], g1_ref[...]) * (1.0 + sc_ref[...]) + sh_ref[...]
    proj = _dot(h.astype(BF16), win_ref[...])
    cs = cs_ref[...]
    cosp, sinp = cs[:, :LANE], cs[:, LANE:]
    qa = _rms(proj[:, _C_QA:_C_CKV], gqa_ref[...])
    q = _dot(qa.astype(BF16), wqb_ref[...])
    hw = MLA_HEADS * LANE
    for h_i in range(MLA_HEADS):
        lo = h_i * LANE
        qlat = _dot(q[:, lo:lo + LANE].astype(BF16), wkn_ref[h_i])
        qr = q[:, hw + lo:hw + lo + LANE] * cosp + q[:, 2 * hw + lo:2 * hw + lo + LANE] * sinp
        qmla_ref[:, h_i * MLA_QW:h_i * MLA_QW + MLA_KV_LORA] = qlat.astype(BF16)
        qmla_ref[:, h_i * MLA_QW + MLA_KV_LORA:(h_i + 1) * MLA_QW] = qr.astype(BF16)
    ckv = _rms(proj[:, _C_CKV:_C_QN], gkva_ref[...])
    kr = proj[:, _C_KR:_C_KRR] * cosp + proj[:, _C_KRR:_C_G] * sinp
    kq_ref[:, :MLA_KV_LORA] = ckv.astype(BF16)
    kq_ref[:, MLA_KV_LORA:] = kr.astype(BF16)
    mla_ref[:, :MLA_KV_LORA] = ckv
    mla_ref[:, MLA_KV_LORA:] = kr[:, :MLA_ROPE]
    qn_ref[...] = proj[:, _C_QN:_C_CMP].astype(BF16)
    cmp_ref[...] = proj[:, _C_CMP:_C_SLC]
    slc = proj[:, _C_SLC:_C_SWA]
    swa = proj[:, _C_SWA:_C_KR]
    slc_ref[...] = slc
    swa_ref[...] = swa
    slcb_ref[...] = slc.astype(BF16)
    swab_ref[...] = swa.astype(BF16)
    gate_ref[...] = _sigmoid(proj[:, _C_G:_C_END])


def _proj(x, sc, sh, cs, W, *, rows_per_mod, cs_period_tiles, tm):
    t, d = x.shape
    nt = t // tm
    if rows_per_mod == 1:
        tiles_per_mod = sc.shape[0] and (t // sc.shape[0]) // tm
        mod_map = lambda i: (i // tiles_per_mod, 0, 0)
    else:
        mod_map = lambda i: (i, 0, 0)
    cs_map = (lambda i: (i % cs_period_tiles, 0)) if cs_period_tiles > 1 else (lambda i: (0, 0))
    const2 = lambda i: (0, 0)
    row = lambda i: (i, 0)
    outs = [
        (MLA_HEADS * MLA_QW, BF16), (MLA_QW, BF16), (MLA_KV_LORA + MLA_ROPE, F32), (NSA_HEADS * LANE, BF16),
        (KV_COLS, F32), (KV_COLS, F32), (KV_COLS, F32), (KV_COLS, BF16), (KV_COLS, BF16), (NSA_KV_HEADS * LANE, F32),
    ]
    return pl.pallas_call(
        _proj_kernel, grid=(nt,),
        in_specs=[
            pl.BlockSpec((tm, d), row),
            pl.BlockSpec((None, rows_per_mod, d), mod_map),
            pl.BlockSpec((None, rows_per_mod, d), mod_map),
            pl.BlockSpec((1, d), const2),
            pl.BlockSpec((tm, 2 * LANE), cs_map),
            pl.BlockSpec(W["w_in"].shape, const2),
            pl.BlockSpec((1, MLA_Q_LORA), const2),
            pl.BlockSpec(W["w_qb"].shape, const2),
            pl.BlockSpec((1, MLA_KV_LORA), const2),
            pl.BlockSpec(W["w_kn"].shape, lambda i: (0, 0, 0)),
        ],
        out_specs=[pl.BlockSpec((tm, w), row) for w, _ in outs],
        out_shape=[jax.ShapeDtypeStruct((t, w), dt) for w, dt in outs],
        compiler_params=_cparams(("arbitrary",)), name="proj_in",
    )(x, sc, sh, W["g_norm1"], cs, W["w_in"], W["g_q_a"], W["w_qb"], W["g_kv_a"], W["w_kn"])


def _mla_prompt_kernel(q_ref, k_ref, wv_ref, o_ref, m_ref, l_ref, acc_ref, *, tq, tk):
    i, j = pl.program_id(1), pl.program_id(2)
    nk = pl.num_programs(2)

    @pl.when(j == 0)
    def _():
        m_ref[...] = jnp.full_like(m_ref, NEG)
        l_ref[...] = jnp.zeros_like(l_ref)
        acc_ref[...] = jnp.zeros_like(acc_ref)

    @pl.when(j * tk <= i * tq + tq - 1)
    def _():
        k = k_ref[...]
        v = k[:, :MLA_KV_LORA]
        qpos = i * tq + lax.broadcasted_iota(I32, (tq, tk), 0)
        kpos = j * tk + lax.broadcasted_iota(I32, (tq, tk), 1)
        mask = kpos <= qpos
        for h in range(MLA_HEADS):
            r = slice(h * tq, (h + 1) * tq)
            s = _dot_nt(q_ref[:, h * MLA_QW:(h + 1) * MLA_QW], k) * MLA_SCALE
            s = jnp.where(mask, s, NEG)
            m_prev = m_ref[r]
            m_new = jnp.maximum(m_prev, jnp.max(s, axis=-1, keepdims=True))
            p = jnp.where(mask, jnp.exp(s - m_new), 0.0)
            alpha = jnp.exp(m_prev - m_new)
            l_ref[r] = alpha * l_ref[r] + jnp.sum(p, axis=-1, keepdims=True)
            acc_ref[r] = alpha * acc_ref[r] + _dot(p.astype(BF16), v)
            m_ref[r] = m_new

    @pl.when(j == nk - 1)
    def _():
        out = jnp.zeros(o_ref.shape, F32)
        for h in range(MLA_HEADS):
            r = slice(h * tq, (h + 1) * tq)
            l = l_ref[r]
            o_lat = acc_ref[r] / jnp.where(l > 0.0, l, 1.0)
            out = out + _dot(o_lat.astype(BF16), wv_ref[h])
        o_ref[...] = out.astype(o_ref.dtype)


def _mla_prompt(qmla, kq, wv, *, nb, seq):
    tq = 128
    tk = min(512, seq)
    nq, nk = seq // tq, seq // tk
    ow = MLA_HEADS * MLA_V

    def k_map(b, i, j):
        return (b * nk + jnp.minimum(j, (i * tq + tq - 1) // tk), 0)

    return pl.pallas_call(
        functools.partial(_mla_prompt_kernel, tq=tq, tk=tk), grid=(nb, nq, nk),
        in_specs=[pl.BlockSpec((tq, MLA_HEADS * MLA_QW), lambda b, i, j: (b * nq + i, 0)),
                  pl.BlockSpec((tk, MLA_QW), k_map),
                  pl.BlockSpec(wv.shape, lambda b, i, j: (0, 0, 0))],
        out_specs=pl.BlockSpec((tq, ow), lambda b, i, j: (b * nq + i, 0)),
        out_shape=jax.ShapeDtypeStruct((nb * seq, ow), BF16),
        scratch_shapes=[pltpu.VMEM((MLA_HEADS * tq, 1), F32), pltpu.VMEM((MLA_HEADS * tq, 1), F32),
                        pltpu.VMEM((MLA_HEADS * tq, MLA_KV_LORA), F32)],
        compiler_params=_cparams(("arbitrary", "arbitrary", "arbitrary")), name="mla_prompt",
    )(qmla, kq, wv)


def _mla_sample_kernel(pt_ref, q_ref, kn_ref, wv_ref, *rest, g, nq):
    pages = rest[:g]
    o_ref, m_ref, l_ref, acc_ref = rest[g:]
    s_idx = pl.program_id(1)
    rows = MLA_HEADS * nq

    @pl.when(s_idx == 0)
    def _():
        m_ref[...] = jnp.full_like(m_ref, NEG)
        l_ref[...] = jnp.zeros_like(l_ref)
        acc_ref[...] = jnp.zeros_like(acc_ref)

    q = jnp.concatenate([q_ref[:, h * MLA_QW:(h + 1) * MLA_QW] for h in range(MLA_HEADS)], axis=0)
    qlat, qr = q[:, :MLA_KV_LORA], q[:, MLA_KV_LORA:MLA_KV_LORA + MLA_ROPE]
    vs, ss = [], []
    for u in range(g):
        page = pages[u][...]
        ckv = page[:, :MLA_KV_LORA].astype(BF16)
        kr = page[:, MLA_KV_LORA:].astype(BF16)
        ss.append(_dot_nt(qlat, ckv) + _dot_nt(qr, kr))
        vs.append(ckv)
    s = jnp.concatenate(ss, axis=1) * MLA_SCALE
    m_prev = m_ref[...]
    m_new = jnp.maximum(m_prev, jnp.max(s, axis=-1, keepdims=True))
    p = jnp.exp(s - m_new)
    alpha = jnp.exp(m_prev - m_new)
    pv = jnp.zeros((rows, MLA_KV_LORA), F32)
    for u in range(g):
        pv = pv + _dot(p[:, u * LANE:(u + 1) * LANE].astype(BF16), vs[u])
    l_ref[...] = alpha * l_ref[...] + jnp.sum(p, axis=-1, keepdims=True)
    acc_ref[...] = alpha * acc_ref[...] + pv
    m_ref[...] = m_new

    @pl.when(s_idx == pl.num_programs(1) - 1)
    def _():
        kn = kn_ref[...]
        sn = _dot_nt(q, kn) * MLA_SCALE
        qi = lax.broadcasted_iota(I32, sn.shape, 0) % nq
        kt = lax.broadcasted_iota(I32, sn.shape, 1)
        mask = kt <= qi
        sn = jnp.where(mask, sn, NEG)
        m_prev = m_ref[...]
        m_new = jnp.maximum(m_prev, jnp.max(sn, axis=-1, keepdims=True))
        p = jnp.where(mask, jnp.exp(sn - m_new), 0.0)
        alpha = jnp.exp(m_prev - m_new)
        l = alpha * l_ref[...] + jnp.sum(p, axis=-1, keepdims=True)
        acc = alpha * acc_ref[...] + _dot(p.astype(BF16), kn[:, :MLA_KV_LORA])
        o_lat = (acc / jnp.where(l > 0.0, l, 1.0)).astype(BF16)
        out = jnp.zeros(o_ref.shape, F32)
        for h in range(MLA_HEADS):
            out = out + _dot(o_lat[h * nq:(h + 1) * nq], wv_ref[h])
        o_ref[...] = out.astype(o_ref.dtype)


def _page_specs(g, block, n_lane_blocks=1, flat=False):
    def index(b, s, pt, u, c):
        return ((pt[s * g + u] if flat else pt[b, s * g + u]), 0, c)

    return [pl.BlockSpec(block, functools.partial(index, u=u, c=c)) for u in range(g) for c in range(n_lane_blocks)]


def _mla_sample(page_table, qmla, kq_new_pad, wv, cache_mla, *, g):
    nb, nq = qmla.shape[0], qmla.shape[1]
    n_pages = page_table.shape[1]
    page = cache_mla.shape[1]
    ow = MLA_HEADS * MLA_V
    rows = MLA_HEADS * nq
    gs = pltpu.PrefetchScalarGridSpec(
        num_scalar_prefetch=1, grid=(nb, n_pages // g),
        in_specs=[pl.BlockSpec((None, nq, MLA_HEADS * MLA_QW), lambda b, s, pt: (b, 0, 0)),
                  pl.BlockSpec((None, LANE, MLA_QW), lambda b, s, pt: (b, 0, 0)),
                  pl.BlockSpec(wv.shape, lambda b, s, pt: (0, 0, 0))]
        + _page_specs(g, (None, page, cache_mla.shape[2])),
        out_specs=pl.BlockSpec((None, nq, ow), lambda b, s, pt: (b, 0, 0)),
        scratch_shapes=[pltpu.VMEM((rows, 1), F32), pltpu.VMEM((rows, 1), F32), pltpu.VMEM((rows, MLA_KV_LORA), F32)],
    )
    return pl.pallas_call(
        functools.partial(_mla_sample_kernel, g=g, nq=nq), grid_spec=gs,
        out_shape=jax.ShapeDtypeStruct((nb, nq, ow), BF16),
        compiler_params=_cparams(("arbitrary", "arbitrary")), name="mla_sample",
    )(page_table, qmla, kq_new_pad, wv, *([cache_mla] * g))


def _cmp_first_kernel(pl_ref, w_ref, *rest, g):
    pages = rest[:2 * g]
    o_ref = rest[2 * g]
    half_w = NSA_KV_HEADS * 2 * CMP_HIDDEN
    for j in range(2):
        acc = jnp.zeros((o_ref.shape[0], half_w), F32)
        for pp in range(CMP_STRIDE // 2):
            xa = jnp.concatenate([pages[2 * u + j][pl.ds(2 * pp, 8, stride=CMP_STRIDE), :] for u in range(g)], axis=0)
            xb = jnp.concatenate([pages[2 * u + j][pl.ds(2 * pp + 1, 8, stride=CMP_STRIDE), :] for u in range(g)], axis=0)
            acc = acc + _dot(jnp.concatenate([xa, xb], axis=1).astype(BF16), w_ref[j, pp])
        o_ref[:, j * half_w:(j + 1) * half_w] = acc


def _cmp_first(page_list, pool, w_pair, *, g):
    n = page_list.shape[0]
    page = pool.shape[1]
    halves = page // CMP_STRIDE
    ow = 2 * NSA_KV_HEADS * 2 * CMP_HIDDEN
    gs = pltpu.PrefetchScalarGridSpec(
        num_scalar_prefetch=1, grid=(1, n // g),
        in_specs=[pl.BlockSpec(w_pair.shape, lambda b, s, pt: (0, 0, 0, 0))] + _page_specs(g, (None, page, LANE), 2, flat=True),
        out_specs=pl.BlockSpec((g * halves, ow), lambda b, s, pt: (s, 0)),
    )
    return pl.pallas_call(
        functools.partial(_cmp_first_kernel, g=g), grid_spec=gs,
        out_shape=jax.ShapeDtypeStruct((n * halves, ow), F32),
        compiler_params=_cparams(("arbitrary", "arbitrary")), name="cmp_first",
    )(page_list, w_pair, *([pool] * (2 * g)))


def _cmp_finish_kernel(ab_ref, pe_ref, w1_ref, w2_ref, kc_ref, vc_ref):
    nh = ab_ref.shape[0]
    outs = []
    for j in range(2):
        pe_term = _dot(pe_ref[j].astype(BF16), w1_ref[j])[0:1]
        acc = jnp.zeros((nh, LANE), F32)
        for kv in range(NSA_KV_HEADS):
            base = (j * NSA_KV_HEADS + kv) * 2 * CMP_HIDDEN
            first = ab_ref[:, base:base + CMP_HIDDEN]
            second = ab_ref[:, base + CMP_HIDDEN:base + 2 * CMP_HIDDEN]
            hid = first + pltpu.roll(second, nh - 1, 0) + pe_term
            hid = 0.5 * hid * (1.0 + lax.erf(hid * math.sqrt(0.5)))
            acc = acc + _dot(hid.astype(BF16), w2_ref[j, kv])
        outs.append(acc)
    kc_ref[...] = outs[0].astype(kc_ref.dtype)
    vc_ref[...] = outs[1].astype(vc_ref.dtype)


def _cmp_finish(ab, pe8, w1r, w2pad, *, nb):
    nh = ab.shape[0] // nb
    c3 = lambda b: (0, 0, 0)
    return pl.pallas_call(
        _cmp_finish_kernel, grid=(nb,),
        in_specs=[pl.BlockSpec((nh, ab.shape[1]), lambda b: (b, 0)), pl.BlockSpec(pe8.shape, c3),
                  pl.BlockSpec(w1r.shape, c3), pl.BlockSpec(w2pad.shape, lambda b: (0, 0, 0, 0))],
        out_specs=[pl.BlockSpec((None, nh, LANE), lambda b: (b, 0, 0))] * 2,
        out_shape=[jax.ShapeDtypeStruct((nb, nh, LANE), BF16)] * 2,
        compiler_params=_cparams(("arbitrary",)), name="cmp_finish")(ab, pe8, w1r, w2pad)


def _topk_mask(v, k):
    lane = lax.broadcasted_iota(I32, v.shape, 1)
    sel = jnp.zeros(v.shape, F32)
    for _ in range(k):
        m = jnp.max(v, axis=-1, keepdims=True)
        idx = jnp.min(jnp.where(v == m, lane, 1 << 20), axis=-1, keepdims=True)
        pick = lane == idx
        sel = jnp.where(pick & (m > -jnp.inf), 1.0, sel)
        v = jnp.where(pick, -jnp.inf, v)
    return sel


def _softmax_masked(s, mask):
    s = jnp.where(mask, s, NEG)
    m = jnp.max(s, axis=-1, keepdims=True)
    p = jnp.where(mask, jnp.exp(s - m), 0.0)
    l = jnp.sum(p, axis=-1, keepdims=True)
    return p / jnp.where(l > 0.0, l, 1.0)


def _flash_step(s, mask, v, m_ref, l_ref, acc_ref):
    s = jnp.where(mask, s, NEG)
    m_prev = m_ref[...]
    m_new = jnp.maximum(m_prev, jnp.max(s, axis=-1, keepdims=True))
    p = jnp.where(mask, jnp.exp(s - m_new), 0.0)
    alpha = jnp.exp(m_prev - m_new)
    l_ref[...] = alpha * l_ref[...] + jnp.sum(p, axis=-1, keepdims=True)
    acc_ref[...] = alpha * acc_ref[...] + _dot(p.astype(BF16), v)
    m_ref[...] = m_new


def _flash_init(m_ref, l_ref, acc_ref):
    m_ref[...] = jnp.full_like(m_ref, NEG)
    l_ref[...] = jnp.zeros_like(l_ref)
    acc_ref[...] = jnp.zeros_like(acc_ref)


def _flash_out(l_ref, acc_ref):
    l = l_ref[...]
    return acc_ref[...] / jnp.where(l > 0.0, l, 1.0)


def _nsa_prompt_kernel(q_ref, kc_ref, vc_ref, bc_ref, ovl_ref, slc_ref, swa_ref, tt_ref, gate_ref, o_ref,
                       m_ref, l_ref, acc_ref, *, tq, n_cmp, n_sel):
    i = pl.program_id(2)
    rows = NSA_GROUP * tq
    q = jnp.concatenate([q_ref[:, g * LANE:(g + 1) * LANE] for g in range(NSA_GROUP)], axis=0)
    nhp = kc_ref.shape[0]
    pos_r = i * tq + lax.broadcasted_iota(I32, (rows, 1), 0) % tq

    s = _dot_nt(q, kc_ref[...]) * NSA_SCALE + bc_ref[...].reshape(rows, nhp)
    n_idx = lax.broadcasted_iota(I32, (rows, nhp), 1)
    mask_c = (n_idx * CMP_STRIDE + CMP_LEN - 1 <= pos_r) & (n_idx < n_cmp)
    p_cmp = _softmax_masked(s, mask_c)
    o_cmp = _dot(p_cmp.astype(BF16), vc_ref[...])

    psum = p_cmp[0:tq]
    for g in range(1, NSA_GROUP):
        psum = psum + p_cmp[g * tq:(g + 1) * tq]
    imp = jnp.dot(psum, ovl_ref[...], precision=lax.Precision.HIGHEST, preferred_element_type=F32)
    blk = lax.broadcasted_iota(I32, imp.shape, 1)
    cur = (i * tq + lax.broadcasted_iota(I32, imp.shape, 0)) // SEL_BLOCK
    forced = (blk == 0) | (blk == cur) | (blk == cur - 1)
    future = (blk > cur) | (blk >= n_sel)
    imp = jnp.where(future, -jnp.inf, jnp.where(forced, jnp.inf, imp))
    sel = _topk_mask(imp, min(SEL_TOPN, n_sel)).astype(BF16)
    sel4 = jnp.concatenate([sel] * NSA_GROUP, axis=0)

    def tile_bias(j):
        return tt_ref[jnp.where(j == i, 0, jnp.where(j == i - 1, 1, 2))]

    def dist_tile(j):
        kpos = j * tq + lax.broadcasted_iota(I32, (rows, tq), 1)
        return pos_r - kpos

    def slc_body(j, c):
        kv = slc_ref[pl.ds(pl.multiple_of(j * tq, tq), tq), :]
        sb = lax.broadcasted_iota(I32, (LANE, tq), 0)
        kt = lax.broadcasted_iota(I32, (LANE, tq), 1)
        expand = jnp.where(sb == j * (tq // SEL_BLOCK) + kt // SEL_BLOCK, 1.0, 0.0).astype(BF16)
        mask = (_dot(sel4, expand) > 0.5) & (dist_tile(j) >= 0)
        s_ = _dot_nt(q, kv[:, :LANE]) * NSA_SCALE + tile_bias(j)
        _flash_step(s_, mask, kv[:, LANE:], m_ref, l_ref, acc_ref)
        return c

    _flash_init(m_ref, l_ref, acc_ref)
    lax.fori_loop(0, i + 1, slc_body, 0)
    o_slc = _flash_out(l_ref, acc_ref)

    def swa_body(j, c):
        kv = swa_ref[pl.ds(pl.multiple_of(j * tq, tq), tq), :]
        d = dist_tile(j)
        mask = (d >= 0) & (d < WINDOW)
        s_ = _dot_nt(q, kv[:, :LANE]) * NSA_SCALE + tile_bias(j)
        _flash_step(s_, mask, kv[:, LANE:], m_ref, l_ref, acc_ref)
        return c

    _flash_init(m_ref, l_ref, acc_ref)
    lax.fori_loop(jnp.maximum(i - WINDOW // tq, 0), i + 1, swa_body, 0)
    o_swa = _flash_out(l_ref, acc_ref)

    gates = gate_ref[...]
    for g in range(NSA_GROUP):
        r = slice(g * tq, (g + 1) * tq)
        o = (gates[:, g:g + 1] * o_cmp[r] + gates[:, NSA_GROUP + g:NSA_GROUP + g + 1] * o_slc[r]
             + gates[:, 2 * NSA_GROUP + g:2 * NSA_GROUP + g + 1] * o_swa[r])
        o_ref[:, g * LANE:(g + 1) * LANE] = o.astype(o_ref.dtype)


def _nsa_prompt(qn, kc, vc, bias_c, ovl, slc_b, swa_b, tt, gates, *, nb, seq, n_cmp, n_sel):
    tq = 128
    nq = seq // tq
    rows = NSA_GROUP * tq
    gw = NSA_GROUP * LANE
    nhp = kc.shape[1]
    return pl.pallas_call(
        functools.partial(_nsa_prompt_kernel, tq=tq, n_cmp=n_cmp, n_sel=n_sel), grid=(nb, NSA_KV_HEADS, nq),
        in_specs=[
            pl.BlockSpec((tq, gw), lambda b, k, i: (b * nq + i, k)),
            pl.BlockSpec((None, nhp, LANE), lambda b, k, i: (b, 0, 0)),
            pl.BlockSpec((None, nhp, LANE), lambda b, k, i: (b, 0, 0)),
            pl.BlockSpec((None, NSA_GROUP, tq, nhp), lambda b, k, i: (k, 0, i, 0)),
            pl.BlockSpec(ovl.shape, lambda b, k, i: (0, 0)),
            pl.BlockSpec((None, seq, KV_COLS), lambda b, k, i: (b, 0, 0)),
            pl.BlockSpec((None, seq, KV_COLS), lambda b, k, i: (b, 0, 0)),
            pl.BlockSpec((None, 3, rows, tq), lambda b, k, i: (k, 0, 0, 0)),
            pl.BlockSpec((tq, LANE), lambda b, k, i: (b * nq + i, k)),
        ],
        out_specs=pl.BlockSpec((tq, gw), lambda b, k, i: (b * nq + i, k)),
        out_shape=jax.ShapeDtypeStruct((nb * seq, NSA_HEADS * LANE), BF16),
        scratch_shapes=[pltpu.VMEM((rows, 1), F32), pltpu.VMEM((rows, 1), F32), pltpu.VMEM((rows, LANE), F32)],
        compiler_params=_cparams(("arbitrary", "arbitrary", "arbitrary")), name="nsa_prompt",
    )(qn, kc, vc, bias_c, ovl, slc_b, swa_b, tt, gates)


def _nsa_sample_kernel(pt_ref, q_ref, kc_ref, vc_ref, bc_ref, ovl_ref, bs_ref, sn_ref, bsn_ref, st_ref, wn_ref,
                       bw_ref, bwn_ref, gate_ref, *rest, g, nq, n_cmp, n_sel, buf):
    pages = rest[:g]
    o_ref, sel_ref, ocmp_ref, oswa_ref, m_ref, l_ref, acc_ref = rest[g:]
    s_idx = pl.program_id(1)
    last = pl.num_programs(1) - 1
    rows = NSA_HEADS * nq
    q = jnp.concatenate([q_ref[:, h * LANE:(h + 1) * LANE] for h in range(NSA_HEADS)], axis=0)
    qi = lax.broadcasted_iota(I32, (rows, 1), 0) % nq

    @pl.when(s_idx == 0)
    def _():
        nhp = kc_ref.shape[0]
        s = _dot_nt(q, kc_ref[...]) * NSA_SCALE + bc_ref[...]
        mask_c = lax.broadcasted_iota(I32, (rows, nhp), 1) < n_cmp
        p_cmp = _softmax_masked(s, mask_c)
        ocmp_ref[...] = _dot(p_cmp.astype(BF16), vc_ref[...])
        psums = []
        for kv in range(NSA_KV_HEADS):
            ps = p_cmp[kv * NSA_GROUP * nq:kv * NSA_GROUP * nq + nq]
            for gg in range(1, NSA_GROUP):
                lo = (kv * NSA_GROUP + gg) * nq
                ps = ps + p_cmp[lo:lo + nq]
            psums.append(ps)
        psum = jnp.concatenate(psums, axis=0)
        imp = jnp.dot(psum, ovl_ref[...], precision=lax.Precision.HIGHEST, preferred_element_type=F32)
        blk = lax.broadcasted_iota(I32, imp.shape, 1)
        cur = n_sel - 1
        forced = (blk == 0) | (blk == cur) | (blk == cur - 1)
        imp = jnp.where(blk >= n_sel, -jnp.inf, jnp.where(forced, jnp.inf, imp))
        sel = _topk_mask(imp, min(SEL_TOPN, n_sel)).astype(BF16)
        sel_ref[...] = jnp.concatenate(
            [sel[kv * nq:(kv + 1) * nq] for kv in range(NSA_KV_HEADS) for _ in range(NSA_GROUP)], axis=0)

        st = st_ref[...]
        s_w = _dot_nt(q, st[:, :LANE].astype(BF16)) * NSA_SCALE + bw_ref[...]
        d_w = buf + qi - lax.broadcasted_iota(I32, (rows, buf), 1)
        mask_w = (d_w >= 0) & (d_w < WINDOW)
        wn = wn_ref[...]
        s_n = _dot_nt(q, wn[:, :LANE]) * NSA_SCALE + bwn_ref[...]
        mask_n = lax.broadcasted_iota(I32, (rows, LANE), 1) <= qi
        s_w = jnp.where(mask_w, s_w, NEG)
        s_n = jnp.where(mask_n, s_n, NEG)
        m = jnp.maximum(jnp.max(s_w, axis=-1, keepdims=True), jnp.max(s_n, axis=-1, keepdims=True))
        p_w = jnp.where(mask_w, jnp.exp(s_w - m), 0.0)
        p_n = jnp.where(mask_n, jnp.exp(s_n - m), 0.0)
        l = jnp.sum(p_w, axis=-1, keepdims=True) + jnp.sum(p_n, axis=-1, keepdims=True)
        o = _dot(p_w.astype(BF16), st[:, LANE:].astype(BF16)) + _dot(p_n.astype(BF16), wn[:, LANE:])
        oswa_ref[...] = o / jnp.where(l > 0.0, l, 1.0)
        _flash_init(m_ref, l_ref, acc_ref)

    page = pages[0].shape[0]
    kcat = jnp.concatenate([pages[u][:, :LANE].astype(BF16) for u in range(g)], axis=0)
    vcat = jnp.concatenate([pages[u][:, LANE:].astype(BF16) for u in range(g)], axis=0)
    nk = g * page
    s = _dot_nt(q, kcat) * NSA_SCALE + bs_ref[jnp.where(s_idx == last, 1, 0)]
    sb = lax.broadcasted_iota(I32, (sel_ref.shape[1], nk), 0)
    kt = lax.broadcasted_iota(I32, (sel_ref.shape[1], nk), 1)
    expand = jnp.where(sb == s_idx * (nk // SEL_BLOCK) + kt // SEL_BLOCK, 1.0, 0.0).astype(BF16)
    mask = _dot(sel_ref[...], expand) > 0.5
    _flash_step(s, mask, vcat, m_ref, l_ref, acc_ref)

    @pl.when(s_idx == last)
    def _():
        sn = sn_ref[...]
        s_n = _dot_nt(q, sn[:, :LANE]) * NSA_SCALE + bsn_ref[...]
        mask_n = (lax.broadcasted_iota(I32, (rows, LANE), 1) <= qi) & (sel_ref[:, n_sel - 1:n_sel] > 0.5)
        _flash_step(s_n, mask_n, sn[:, LANE:], m_ref, l_ref, acc_ref)
        o_slc = _flash_out(l_ref, acc_ref)
        o_cmp, o_swa = ocmp_ref[...], oswa_ref[...]
        gates = gate_ref[...]
        for h in range(NSA_HEADS):
            kv, gg = divmod(h, NSA_GROUP)
            r = slice(h * nq, (h + 1) * nq)
            c = kv * LANE + gg
            o = (gates[:, c:c + 1] * o_cmp[r] + gates[:, c + NSA_GROUP:c + NSA_GROUP + 1] * o_slc[r]
                 + gates[:, c + 2 * NSA_GROUP:c + 2 * NSA_GROUP + 1] * o_swa[r])
            o_ref[:, h * LANE:(h + 1) * LANE] = o.astype(o_ref.dtype)


def _nsa_sample(page_table, qn, kc, vc, bias_c, ovl, bias_s, slc_new, bias_sn, state_swa, swa_new, bias_w, bias_wn,
                gates, cache_slc, *, g, n_cmp, n_sel):
    nb, nq = qn.shape[0], qn.shape[1]
    n_pages = page_table.shape[1]
    page = cache_slc.shape[1]
    rows = NSA_HEADS * nq
    buf = state_swa.shape[1]
    hw = NSA_HEADS * LANE
    bmap = lambda b, s, pt: (b, 0, 0)
    c2 = lambda b, s, pt: (0, 0)
    gs = pltpu.PrefetchScalarGridSpec(
        num_scalar_prefetch=1, grid=(nb, n_pages // g),
        in_specs=[
            pl.BlockSpec((None, nq, hw), bmap),
            pl.BlockSpec((None,) + kc.shape[1:], bmap),
            pl.BlockSpec((None,) + vc.shape[1:], bmap),
            pl.BlockSpec(bias_c.shape, c2),
            pl.BlockSpec(ovl.shape, c2),
            pl.BlockSpec(bias_s.shape, lambda b, s, pt: (0, 0, 0)),
            pl.BlockSpec((None,) + slc_new.shape[1:], bmap),
            pl.BlockSpec(bias_sn.shape, c2),
            pl.BlockSpec((None,) + state_swa.shape[1:], bmap),
            pl.BlockSpec((None,) + swa_new.shape[1:], bmap),
            pl.BlockSpec(bias_w.shape, c2),
            pl.BlockSpec(bias_wn.shape, c2),
            pl.BlockSpec((None, nq, NSA_KV_HEADS * LANE), bmap),
        ] + _page_specs(g, (None, page, KV_COLS)),
        out_specs=pl.BlockSpec((None, nq, hw), bmap),
        scratch_shapes=[pltpu.VMEM((rows, ovl.shape[1]), BF16), pltpu.VMEM((rows, LANE), F32), pltpu.VMEM((rows, LANE), F32),
                        pltpu.VMEM((rows, 1), F32), pltpu.VMEM((rows, 1), F32), pltpu.VMEM((rows, LANE), F32)],
    )
    return pl.pallas_call(
        functools.partial(_nsa_sample_kernel, g=g, nq=nq, n_cmp=n_cmp, n_sel=n_sel, buf=buf), grid_spec=gs,
        out_shape=jax.ShapeDtypeStruct((nb, nq, hw), BF16),
        compiler_params=_cparams(("arbitrary", "arbitrary")), name="nsa_sample",
    )(page_table, qn, kc, vc, bias_c, ovl, bias_s, slc_new, bias_sn, state_swa, swa_new, bias_w, bias_wn, gates,
      *([cache_slc] * g))


def _outproj_kernel(x_ref, om_ref, on_ref, g1_ref, sc_ref, sh_ref, wom_ref, won_ref, gn2_ref, wr_ref, br_ref,
                    x1_ref, h2_ref, ti_ref, tw_ref):
    mix = _dot(om_ref[...], wom_ref[...]) + _dot(on_ref[...], won_ref[...])
    x1 = x_ref[...] + g1_ref[...] * mix
    x1_ref[...] = x1
    h2 = _rms(x1, gn2_ref[...]) * (1.0 + sc_ref[...]) + sh_ref[...]
    h2_ref[...] = h2.astype(h2_ref.dtype)
    logits = jnp.dot(h2, wr_ref[...], precision=lax.Precision.HIGHEST, preferred_element_type=F32) + br_ref[...]
    lane = lax.broadcasted_iota(I32, logits.shape, 1)
    v = logits
    vals, idxs = [], []
    for _ in range(TOP_K):
        m = jnp.max(v, axis=-1, keepdims=True)
        idx = jnp.min(jnp.where(v == m, lane, 1 << 20), axis=-1, keepdims=True)
        vals.append(m)
        idxs.append(idx)
        v = jnp.where(lane == idx, -jnp.inf, v)
    es = [jnp.exp(m - vals[0]) for m in vals]
    tot = es[0]
    for e in es[1:]:
        tot = tot + e
    ti = jnp.zeros(logits.shape, I32)
    tw = jnp.zeros(logits.shape, F32)
    for k in range(TOP_K):
        ti = jnp.where(lane == k, idxs[k], ti)
        tw = jnp.where(lane == k, es[k] / tot, tw)
    ti_ref[...] = ti
    tw_ref[...] = tw


def _outproj(x, o_mla, o_nsa, g1, sc, sh, W, *, rows_per_mod, tm):
    t, d = x.shape
    nt = t // tm
    if rows_per_mod == 1:
        tiles_per_mod = (t // g1.shape[0]) // tm
        mod_map = lambda i: (i // tiles_per_mod, 0, 0)
    else:
        mod_map = lambda i: (i, 0, 0)
    row = lambda i: (i, 0)
    c2 = lambda i: (0, 0)
    mod = pl.BlockSpec((None, rows_per_mod, d), mod_map)
    return pl.pallas_call(
        _outproj_kernel, grid=(nt,),
        in_specs=[pl.BlockSpec((tm, d), row), pl.BlockSpec((tm, o_mla.shape[1]), row), pl.BlockSpec((tm, o_nsa.shape[1]), row),
                  mod, mod, mod, pl.BlockSpec(W["w_o_mla"].shape, c2), pl.BlockSpec(W["w_o_nsa"].shape, c2),
                  pl.BlockSpec((1, d), c2), pl.BlockSpec(W["w_router"].shape, c2), pl.BlockSpec((1, LANE), c2)],
        out_specs=[pl.BlockSpec((tm, d), row), pl.BlockSpec((tm, d), row), pl.BlockSpec((tm, LANE), row),
                   pl.BlockSpec((tm, LANE), row)],
        out_shape=[jax.ShapeDtypeStruct((t, d), F32), jax.ShapeDtypeStruct((t, d), BF16),
                   jax.ShapeDtypeStruct((t, LANE), I32), jax.ShapeDtypeStruct((t, LANE), F32)],
        compiler_params=_cparams(("arbitrary",)), name="out_proj",
    )(x, o_mla, o_nsa, g1, sc, sh, W["w_o_mla"], W["w_o_nsa"], W["g_norm2"], W["w_router"], W["b_router"])


def _expert_kernel(te_ref, tv_ref, x_ref, wgu_ref, bgu_ref, wd_ref, bd_ref, ws_ref, o_ref):
    t = pl.program_id(0)

    @pl.when(tv_ref[t] > 0)
    def _():
        d_ff = wd_ref.shape[0]
        gu = _dot(x_ref[...], wgu_ref[...]) + bgu_ref[...]
        glu = jnp.minimum(gu[:, :d_ff], SWIGLU_LIMIT)
        lin = jnp.clip(gu[:, d_ff:], -SWIGLU_LIMIT, SWIGLU_LIMIT)
        act = glu * _sigmoid(SWIGLU_ALPHA * glu) * (lin + 1.0)
        o_ref[...] = ws_ref[...] * (_dot(act.astype(BF16), wd_ref[...]) + bd_ref[...])

    @pl.when(tv_ref[t] == 0)
    def _():
        o_ref[...] = jnp.zeros_like(o_ref)


def _experts(tile_expert, tile_valid, x_sorted, w_slot, W, *, tm):
    ns, d = x_sorted.shape
    gs = pltpu.PrefetchScalarGridSpec(
        num_scalar_prefetch=2, grid=(ns // tm,),
        in_specs=[pl.BlockSpec((tm, d), lambda t, te, tv: (t, 0)),
                  pl.BlockSpec((None,) + W["w_gate_up"].shape[1:], lambda t, te, tv: (te[t], 0, 0)),
                  pl.BlockSpec((None,) + W["b_gate_up"].shape[1:], lambda t, te, tv: (te[t], 0, 0)),
                  pl.BlockSpec((None,) + W["w_down"].shape[1:], lambda t, te, tv: (te[t], 0, 0)),
                  pl.BlockSpec((None,) + W["b_down"].shape[1:], lambda t, te, tv: (te[t], 0, 0)),
                  pl.BlockSpec((tm, 1), lambda t, te, tv: (t, 0))],
        out_specs=pl.BlockSpec((tm, d), lambda t, te, tv: (t, 0)),
    )
    return pl.pallas_call(
        _expert_kernel, grid_spec=gs, out_shape=jax.ShapeDtypeStruct((ns, d), F32),
        compiler_params=_cparams(("arbitrary",)), name="experts",
    )(tile_expert, tile_valid, x_sorted, W["w_gate_up"], W["b_gate_up"], W["w_down"], W["b_down"], w_slot)


def _final_kernel(x1_ref, moe_ref, g2_ref, gf_ref, o_ref):
    o_ref[...] = _rms(x1_ref[...] + g2_ref[...] * moe_ref[...], gf_ref[...])


def _final(x1, moe, g2, g_final, *, rows_per_mod, tm):
    t, d = x1.shape
    if rows_per_mod == 1:
        tiles_per_mod = (t // g2.shape[0]) // tm
        mod_map = lambda i: (i // tiles_per_mod, 0, 0)
    else:
        mod_map = lambda i: (i, 0, 0)
    row = lambda i: (i, 0)
    return pl.pallas_call(
        _final_kernel, grid=(t // tm,),
        in_specs=[pl.BlockSpec((tm, d), row), pl.BlockSpec((tm, d), row), pl.BlockSpec((None, rows_per_mod, d), mod_map),
                  pl.BlockSpec((1, d), lambda i: (0, 0))],
        out_specs=pl.BlockSpec((tm, d), row), out_shape=jax.ShapeDtypeStruct((t, d), F32),
        compiler_params=_cparams(("arbitrary",)), name="final_norm")(x1, moe, g2, g_final)


def _t5_bucket(dist):
    n = jnp.maximum(dist, 0)
    max_exact = NUM_BUCKETS // 2
    nf = jnp.maximum(n, 1).astype(F32)
    large = max_exact + (jnp.log(nf / max_exact) / math.log(MAX_DISTANCE / max_exact)
                         * (NUM_BUCKETS - max_exact)).astype(I32)
    return jnp.where(n < max_exact, n, jnp.minimum(large, NUM_BUCKETS - 1))


def _bias_rows(rel_bias, dist):
    return jnp.moveaxis(rel_bias[_t5_bucket(dist)], -1, 0)


def _rope_table(pos):
    half = MLA_ROPE // 2
    inv_freq = 1.0 / (ROPE_THETA ** (jnp.arange(half, dtype=F32) / half))
    ang = pos.astype(F32)[:, None] * inv_freq[None, :]
    pad = jnp.zeros((pos.shape[0], LANE - MLA_ROPE), F32)
    cos, sin = jnp.cos(ang), jnp.sin(ang)
    return jnp.concatenate([cos, cos, pad, sin, sin, pad], axis=1)


def _rot_cols(w):
    half = MLA_ROPE // 2
    return jnp.concatenate([-w[..., half:], w[..., :half]], axis=-1)


def _pad_last(w, n):
    return jnp.pad(w, [(0, 0)] * (w.ndim - 1) + [(0, n - w.shape[-1])])


def _pack_weights(w_in, g_norm1, g_norm2, g_q_a, w_q_b, g_kv_a, w_kv_b, cmp_pe, cmp_w1, cmp_w2, w_o, w_router,
                  b_router, w_gate_up, b_gate_up, w_down, b_down):
    d = w_in.shape[0]
    sizes = (MLA_Q_LORA, MLA_KV_LORA, MLA_ROPE, NSA_HEADS * HEAD_DIM, KV_COLS, KV_COLS, KV_COLS, 3 * NSA_HEADS)
    offs = [0]
    for s in sizes:
        offs.append(offs[-1] + s)
    w_qa, w_ckv, w_kr, w_qn, w_cmp, w_slc, w_swa, w_g = [w_in[:, offs[k]:offs[k + 1]] for k in range(8)]
    wq = w_qn.reshape(d, NSA_KV_HEADS, NSA_GROUP, HEAD_DIM)
    wq_pad = jnp.concatenate(
        [jnp.pad(wq[:, kv], ((0, 0), (0, 0), (kv * HEAD_DIM, LANE - (kv + 1) * HEAD_DIM))).reshape(d, NSA_GROUP * LANE)
         for kv in range(NSA_KV_HEADS)], axis=1)
    wg = jnp.transpose(w_g.reshape(d, NSA_KV_HEADS, NSA_GROUP, 3), (0, 1, 3, 2)).reshape(d, NSA_KV_HEADS, 3 * NSA_GROUP)
    wg_pad = _pad_last(wg, LANE).reshape(d, NSA_KV_HEADS * LANE)
    w_in_p = jnp.concatenate([w_qa, w_ckv, wq_pad, w_cmp, w_slc, w_swa, _pad_last(w_kr, LANE),
                              _pad_last(_rot_cols(w_kr), LANE), wg_pad], axis=1).astype(BF16)
    assert w_in_p.shape[1] == _C_END
    nope = _pad_last(w_q_b[:, :, :MLA_NOPE], LANE).reshape(MLA_Q_LORA, MLA_HEADS * LANE)
    rp = w_q_b[:, :, MLA_NOPE:]
    w_qb = jnp.concatenate([nope, _pad_last(rp, LANE).reshape(MLA_Q_LORA, -1),
                            _pad_last(_rot_cols(rp), LANE).reshape(MLA_Q_LORA, -1)], axis=1).astype(BF16)
    w_kn = jnp.transpose(w_kv_b[:, :, :MLA_NOPE], (1, 2, 0))
    w_kn = jnp.pad(w_kn, ((0, 0), (0, LANE - MLA_NOPE), (0, 0))).astype(BF16)
    wv = jnp.transpose(w_kv_b[:, :, MLA_NOPE:], (1, 0, 2))
    wv_pad = jnp.stack([jnp.pad(wv[h], ((0, 0), (h * MLA_V, (MLA_HEADS - 1 - h) * MLA_V))) for h in range(MLA_HEADS)]).astype(BF16)
    n_mla = MLA_HEADS * MLA_V
    won = w_o[n_mla:].reshape(NSA_KV_HEADS, NSA_GROUP, HEAD_DIM, d)
    won_pad = jnp.concatenate(
        [jnp.pad(won[kv], ((0, 0), (kv * HEAD_DIM, LANE - (kv + 1) * HEAD_DIM), (0, 0))).reshape(NSA_GROUP * LANE, d)
         for kv in range(NSA_KV_HEADS)], axis=0).astype(BF16)
    base = jnp.concatenate([cmp_w1[:, :CMP_STRIDE], cmp_w1[:, CMP_STRIDE:]], axis=-1)
    z = jnp.zeros_like(base)
    blk = jnp.concatenate([jnp.concatenate([base, z], axis=-1), jnp.concatenate([z, base], axis=-1)], axis=2)
    w_pair = blk.reshape(2, CMP_STRIDE // 2, 2 * NSA_KV_HEADS * HEAD_DIM, NSA_KV_HEADS * 2 * CMP_HIDDEN).astype(BF16)
    pe8 = jnp.broadcast_to(cmp_pe.reshape(2, 1, CMP_LEN * HEAD_DIM), (2, 8, CMP_LEN * HEAD_DIM))
    w1r = cmp_w1.reshape(2, CMP_LEN * HEAD_DIM, CMP_HIDDEN).astype(BF16)
    w2pad = jnp.stack([jnp.stack([jnp.pad(cmp_w2[j], ((0, 0), (kv * HEAD_DIM, LANE - (kv + 1) * HEAD_DIM)))
                                  for kv in range(NSA_KV_HEADS)]) for j in range(2)]).astype(BF16)
    return dict(
        w_in=w_in_p, g_norm1=g_norm1[None], g_norm2=g_norm2[None], g_q_a=g_q_a[None], g_kv_a=g_kv_a[None],
        w_qb=w_qb, w_kn=w_kn, wv=wv_pad, w_o_mla=w_o[:n_mla].astype(BF16), w_o_nsa=won_pad,
        w_pair=w_pair, pe8=pe8, w1r=w1r, w2pad=w2pad,
        w_router=_pad_last(w_router, LANE), b_router=jnp.pad(b_router, (0, LANE - N_EXPERTS), constant_values=NEG)[None],
        w_gate_up=w_gate_up.astype(BF16), b_gate_up=b_gate_up[:, None, :], w_down=w_down.astype(BF16),
        b_down=b_down[:, None, :])


def _overlap(n_half_pad, n_sel, n_sel_pad):
    c_start = jnp.arange(n_half_pad) * CMP_STRIDE
    s_start = jnp.arange(n_sel_pad) * SEL_BLOCK
    ov = (c_start[:, None] < s_start[None, :] + SEL_BLOCK) & (c_start[:, None] + CMP_LEN > s_start[None, :])
    return (ov & (jnp.arange(n_sel_pad) < n_sel)[None, :]).astype(F32)


def _moe_dispatch(top_i, top_w, tm):
    t = top_i.shape[0]
    a = t * TOP_K
    n_tiles = -(-a // tm) + N_EXPERTS
    ns = n_tiles * tm
    e_flat = top_i[:, :TOP_K].reshape(a)
    w_flat = top_w[:, :TOP_K].reshape(a)
    order = jnp.argsort(e_flat, stable=True).astype(I32)
    e_sorted = e_flat[order]
    counts = jnp.sum(e_flat[:, None] == jnp.arange(N_EXPERTS, dtype=I32)[None, :], axis=0, dtype=I32)
    padded = ((counts + tm - 1) // tm) * tm
    pend = jnp.cumsum(padded)
    pstart = pend - padded
    start = jnp.cumsum(counts) - counts
    slot = pstart[e_sorted] + jnp.arange(a, dtype=I32) - start[e_sorted]
    tok_of_slot = jnp.zeros((ns,), I32).at[slot].set(order // TOP_K)
    w_of_slot = jnp.zeros((ns,), F32).at[slot].set(w_flat[order])
    slot_of_assign = jnp.zeros((a,), I32).at[order].set(slot).reshape(t, TOP_K)
    tile_start = jnp.arange(n_tiles, dtype=I32) * tm
    tile_expert = jnp.minimum(jnp.searchsorted(pend, tile_start, side="right"), N_EXPERTS - 1).astype(I32)
    tile_valid = (tile_start < pend[-1]).astype(I32)
    return tok_of_slot, w_of_slot[:, None], slot_of_assign, tile_expert, tile_valid


def kernel(x_prompt, x_sample, c_prompt, c_sample, cache_mla, cache_nsa_cmp, cache_nsa_slc, state_nsa_swa, page_table, rel_bias, w_ada, b_ada, g_norm1, g_norm2, w_in, g_q_a, w_q_b, g_kv_a, w_kv_b, cmp_pe, cmp_w1, cmp_w2, w_o, w_router, b_router, w_gate_up, b_gate_up, w_down, b_down, g_final):
    depth = w_in.shape[0]
    assert depth == 1, "single-layer decoder step"
    nb, seq, d = x_prompt.shape
    nbd, nq, _ = x_sample.shape
    n_pages = page_table.shape[1]
    page = cache_mla.shape[2]
    past = n_pages * page
    buf = state_nsa_swa.shape[2]
    assert page == LANE and nq <= 8 and seq % 512 == 0 and past % SEL_BLOCK == 0 and seq >= WINDOW
    tp, ts = nb * seq, nbd * nq
    W = _pack_weights(w_in[0], g_norm1[0], g_norm2[0], g_q_a[0], w_q_b[0], g_kv_a[0], w_kv_b[0], cmp_pe[0], cmp_w1[0],
                      cmp_w2[0], w_o[0], w_router[0], b_router[0], w_gate_up[0], b_gate_up[0], w_down[0], b_down[0])

    n_c = nb + nbd
    n_c_pad = -(-n_c // 8) * 8
    c_all = jnp.pad(jnp.concatenate([c_prompt, c_sample], axis=0), ((0, n_c_pad - n_c), (0, 0)))
    mod = _ada_mod(c_all, w_ada[0].astype(BF16), b_ada[0][None])
    mod_p = [m[:, None, :] for m in jnp.split(mod[:nb], 6, axis=-1)]
    tm_s = min(256, ts)
    mod_s = [jnp.repeat(m, nq, axis=0).reshape(ts // tm_s, tm_s, d) for m in jnp.split(mod[nb:n_c], 6, axis=-1)]

    tm_p = 256
    pos_p = jnp.arange(seq)
    pos_s = past + jnp.arange(nq)
    P = _proj(x_prompt.reshape(tp, d), mod_p[1], mod_p[0], _rope_table(pos_p), W, rows_per_mod=1,
              cs_period_tiles=seq // tm_p, tm=tm_p)
    cs_s = jnp.tile(_rope_table(pos_s), (tm_s // nq, 1))
    S_ = _proj(x_sample.reshape(ts, d), mod_s[1], mod_s[0], cs_s, W, rows_per_mod=tm_s, cs_period_tiles=1, tm=tm_s)
    p_qmla, p_kq, p_mla, p_qn, p_cmp, p_slc, p_swa, p_slcb, p_swab, p_gate = P
    s_qmla, s_kq, s_mla, s_qn, s_cmp, s_slc, s_swa, s_slcb, s_swab, s_gate = S_

    def pad_new(a):
        return jnp.pad(a.reshape(nbd, nq, a.shape[1]), ((0, 0), (0, LANE - nq), (0, 0)))

    o_mla_p = _mla_prompt(p_qmla, p_kq, W["wv"], nb=nb, seq=seq)
    g_pages = min(16, n_pages)
    o_mla_s = _mla_sample(page_table, s_qmla.reshape(nbd, nq, -1), pad_new(s_kq), W["wv"], cache_mla[0], g=g_pages)

    assert (past + nq) // CMP_STRIDE == past // CMP_STRIDE
    n_pp = tp // LANE
    g_cp = min(32, n_pp)
    ab_p = _cmp_first(jnp.arange(n_pp, dtype=I32), p_cmp.reshape(n_pp, LANE, KV_COLS), W["w_pair"], g=g_cp)
    kc_p, vc_p = _cmp_finish(ab_p, W["pe8"], W["w1r"], W["w2pad"], nb=nb)
    cache_cmp = cache_nsa_cmp[0].reshape(-1, page, KV_COLS)
    g_cs = min(32, nbd * n_pages)
    ab_s = _cmp_first(page_table.reshape(nbd * n_pages), cache_cmp, W["w_pair"], g=g_cs)
    kc_s, vc_s = _cmp_finish(ab_s, W["pe8"], W["w1r"], W["w2pad"], nb=nbd)

    nh_p = seq // CMP_STRIDE
    n_sel_p = -(-seq // SEL_BLOCK)
    tq = 128
    cmp_end_p = jnp.arange(nh_p) * CMP_STRIDE + CMP_LEN - 1
    bias_c_p = _bias_rows(rel_bias, pos_p[:, None] - cmp_end_p[None, :]).reshape(NSA_KV_HEADS, NSA_GROUP, seq, nh_p)
    di = jnp.arange(tq)[:, None] - jnp.arange(tq)[None, :]
    tt = jnp.stack([_bias_rows(rel_bias, di), _bias_rows(rel_bias, di + tq),
                    _bias_rows(rel_bias, jnp.full((tq, tq), MAX_DISTANCE))], axis=1)
    tt = tt.reshape(NSA_KV_HEADS, NSA_GROUP, 3, tq, tq).transpose(0, 2, 1, 3, 4).reshape(NSA_KV_HEADS, 3, NSA_GROUP * tq, tq)
    ovl_p = _overlap(nh_p, n_sel_p, LANE)
    o_nsa_p = _nsa_prompt(p_qn, kc_p, vc_p, bias_c_p, ovl_p, p_slcb.reshape(nb, seq, KV_COLS),
                          p_swab.reshape(nb, seq, KV_COLS), tt, p_gate, nb=nb, seq=seq, n_cmp=nh_p - 1, n_sel=n_sel_p)

    nh_s = past // CMP_STRIDE
    n_sel_s = -(-(past + nq) // SEL_BLOCK)
    assert n_sel_s == past // SEL_BLOCK + 1
    n_sel_pad = -(-n_sel_s // LANE) * LANE
    rows_s = NSA_HEADS * nq
    cmp_end_s = jnp.arange(nh_s) * CMP_STRIDE + CMP_LEN - 1
    bias_c_s = _bias_rows(rel_bias, pos_s[:, None] - cmp_end_s[None, :]).reshape(rows_s, nh_s)
    nk_step = g_pages * page
    far = jnp.broadcast_to(_bias_rows(rel_bias, jnp.full((nq, 1), MAX_DISTANCE)).reshape(rows_s, 1), (rows_s, nk_step))
    tail_pos = past - nk_step + jnp.arange(nk_step)
    bias_s = jnp.stack([far, _bias_rows(rel_bias, pos_s[:, None] - tail_pos[None, :]).reshape(rows_s, nk_step)])
    new_pos = past + jnp.arange(LANE)
    bias_new = _bias_rows(rel_bias, pos_s[:, None] - new_pos[None, :]).reshape(rows_s, LANE)
    swa_pos = past - buf + jnp.arange(buf)
    bias_w = _bias_rows(rel_bias, pos_s[:, None] - swa_pos[None, :]).reshape(rows_s, buf)
    ovl_s = _overlap(nh_s, n_sel_s, n_sel_pad)
    o_nsa_s = _nsa_sample(page_table, s_qn.reshape(nbd, nq, -1), kc_s, vc_s, bias_c_s, ovl_s, bias_s, pad_new(s_slcb),
                          bias_new, state_nsa_swa[0].reshape(nbd, buf, KV_COLS), pad_new(s_swab), bias_w, bias_new,
                          s_gate.reshape(nbd, nq, -1), cache_nsa_slc[0].reshape(-1, page, KV_COLS), g=g_pages,
                          n_cmp=nh_s - 1, n_sel=n_sel_s)

    x1_p, h2_p, ti_p, tw_p = _outproj(x_prompt.reshape(tp, d), o_mla_p, o_nsa_p, mod_p[2], mod_p[4], mod_p[3], W,
                                      rows_per_mod=1, tm=tm_p)
    x1_s, h2_s, ti_s, tw_s = _outproj(x_sample.reshape(ts, d), o_mla_s.reshape(ts, -1), o_nsa_s.reshape(ts, -1),
                                      mod_s[2], mod_s[4], mod_s[3], W, rows_per_mod=tm_s, tm=tm_s)

    tm_e = 256
    h2 = jnp.concatenate([h2_p, h2_s], axis=0)
    tok_of_slot, w_slot, slot_of_assign, tile_expert, tile_valid = _moe_dispatch(
        jnp.concatenate([ti_p, ti_s], axis=0), jnp.concatenate([tw_p, tw_s], axis=0), tm_e)
    y_sorted = _experts(tile_expert, tile_valid, h2[tok_of_slot], w_slot, W, tm=tm_e)
    moe = jnp.sum(y_sorted[slot_of_assign], axis=1)

    y_p = _final(x1_p, moe[:tp], mod_p[5], g_final[None], rows_per_mod=1, tm=tm_p)
    y_s = _final(x1_s, moe[tp:], mod_s[5], g_final[None], rows_per_mod=tm_s, tm=tm_s)

    kv_tail = (2, NSA_KV_HEADS, HEAD_DIM)
    keep_p = min(WINDOW, seq)
    swa_keys = jnp.concatenate([state_nsa_swa[0], s_swa.reshape((nbd, nq) + kv_tail)], axis=1)
    keep_s = min(WINDOW, buf + nq)
    return (y_p.reshape(nb, seq, d), y_s.reshape(nbd, nq, d),
            p_mla.reshape(1, nb, seq, -1), s_mla.reshape(1, nbd, nq, -1),
            p_cmp.reshape((1, nb, seq) + kv_tail), s_cmp.reshape((1, nbd, nq) + kv_tail),
            p_slc.reshape((1, nb, seq) + kv_tail), s_slc.reshape((1, nbd, nq) + kv_tail),
            p_swa.reshape((nb, seq) + kv_tail)[None, :, seq - keep_p:],
            swa_keys[None, :, buf + nq - keep_s:])
```

```python
import functools
import math

import jax
import jax.numpy as jnp
from jax import lax
from jax.experimental import pallas as pl
from jax.experimental.pallas import tpu as pltpu

F32, BF16, I32 = jnp.float32, jnp.bfloat16, jnp.int32

MLA_HEADS = 8
MLA_Q_LORA = 384
MLA_KV_LORA = 256
MLA_NOPE = 64
MLA_ROPE = 32
MLA_V = 64
NSA_HEADS = 8
NSA_KV_HEADS = 2
NSA_GROUP = NSA_HEADS // NSA_KV_HEADS
HEAD_DIM = 64
CMP_LEN = 32
CMP_STRIDE = 16
CMP_HIDDEN = 128
SEL_BLOCK = 64
SEL_TOPN = 16
WINDOW = 512
KV_COLS = 2 * NSA_KV_HEADS * HEAD_DIM
N_EXPERTS = 32
TOP_K = 4
SWIGLU_LIMIT = 7.0
SWIGLU_ALPHA = 1.702
NUM_BUCKETS = 32
MAX_DISTANCE = 128
ROPE_THETA = 10000.0
NORM_EPS = 1e-6

LANE = 128
VMEM_LIMIT = 56 * 1024 * 1024
NEG = -1e30
CMP_WIN_LO = 16

MLA_QW = MLA_KV_LORA + LANE
MLA_SCALE = (MLA_NOPE + MLA_ROPE) ** -0.5
MLA_QSCALE = MLA_SCALE * math.log2(math.e)
LOG2E = math.log2(math.e)
NSA_QSCALE = HEAD_DIM ** -0.5 * LOG2E

_C_QA = 0
_C_CKV = _C_QA + MLA_Q_LORA
_C_QN = _C_CKV + MLA_KV_LORA
_C_CMP = _C_QN + NSA_HEADS * LANE
_C_SLC = _C_CMP + KV_COLS
_C_SWA = _C_SLC + KV_COLS
_C_KR = _C_SWA + KV_COLS
_C_KRR = _C_KR + LANE
_C_G = _C_KRR + LANE
_C_END = _C_G + NSA_KV_HEADS * LANE


def _cparams(sem, vmem=VMEM_LIMIT):
    return pltpu.CompilerParams(dimension_semantics=sem, vmem_limit_bytes=vmem)


def _dot(a, b):
    return jnp.dot(a, b, preferred_element_type=F32)


def _dot_nt(a, b):
    return lax.dot_general(a, b, (((1,), (1,)), ((), ())), preferred_element_type=F32)


def _row_reduce(x, op, reduce):
    n = x.shape[1] // LANE
    if x.shape[1] % LANE or n <= 1:
        return reduce(x, axis=-1, keepdims=True)
    t = x[:, :LANE]
    for c in range(1, n):
        t = op(t, x[:, c * LANE:(c + 1) * LANE])
    return reduce(t, axis=-1, keepdims=True)


def _rowmax(x):
    return _row_reduce(x, jnp.maximum, jnp.max)


def _rowsum(x):
    return _row_reduce(x, jnp.add, jnp.sum)


def _rms(x, g):
    return x * lax.rsqrt(jnp.mean(x * x, axis=-1, keepdims=True) + NORM_EPS) * g


def _sigmoid(x):
    return 1.0 / (1.0 + jnp.exp(-x))


def _ada_kernel(c_ref, w_ref, b_ref, o_ref):
    c = c_ref[...]
    o_ref[...] = _dot((c * _sigmoid(c)).astype(BF16), w_ref[...]) + b_ref[...]


def _ada_mod(c_all, w_ada, b_ada):
    m, d = c_all.shape
    n = w_ada.shape[1]
    tn = 1536
    return pl.pallas_call(
        _ada_kernel, grid=(n // tn,),
        in_specs=[pl.BlockSpec((m, d), lambda i: (0, 0)), pl.BlockSpec((d, tn), lambda i: (0, i)),
                  pl.BlockSpec((1, tn), lambda i: (0, i))],
        out_specs=pl.BlockSpec((m, tn), lambda i: (0, i)),
        out_shape=jax.ShapeDtypeStruct((m, n), F32),
        compiler_params=_cparams(("arbitrary",)), name="ada_mod")(c_all, w_ada, b_ada)


def _proj_kernel(x_ref, sc_ref, sh_ref, g1_ref, cs_ref, win_ref, gqa_ref, wqb_ref, gkva_ref, wkn_ref,
                 qmla_ref, kq_ref, mla_ref, qn_ref, cmp_ref, slc_ref, swa_ref, slcb_ref, swab_ref, gate_ref):
    h = _rms(x_ref[...], g1_ref[...]) * (1.0 + sc_ref[...]) + sh_ref[...]
    proj = _dot(h.astype(BF16), win_ref[...])
    cs = cs_ref[...]
    cosp, sinp = cs[:, :LANE], cs[:, LANE:]
    qa = _rms(proj[:, _C_QA:_C_CKV], gqa_ref[...])
    q = _dot(qa.astype(BF16), wqb_ref[...])
    hw = MLA_HEADS * LANE
    for h_i in range(MLA_HEADS):
        lo = h_i * LANE
        qlat = _dot(q[:, lo:lo + LANE].astype(BF16), wkn_ref[h_i]) * MLA_QSCALE
        qr = (q[:, hw + lo:hw + lo + LANE] * cosp + q[:, 2 * hw + lo:2 * hw + lo + LANE] * sinp) * MLA_QSCALE
        qmla_ref[:, h_i * MLA_QW:h_i * MLA_QW + MLA_KV_LORA] = qlat.astype(BF16)
        qmla_ref[:, h_i * MLA_QW + MLA_KV_LORA:(h_i + 1) * MLA_QW] = qr.astype(BF16)
    ckv = _rms(proj[:, _C_CKV:_C_QN], gkva_ref[...])
    kr = proj[:, _C_KR:_C_KRR] * cosp + proj[:, _C_KRR:_C_G] * sinp
    kq_ref[:, :MLA_KV_LORA] = ckv.astype(BF16)
    kq_ref[:, MLA_KV_LORA:] = kr.astype(BF16)
    mla_ref[:, :MLA_KV_LORA] = ckv
    mla_ref[:, MLA_KV_LORA:] = kr[:, :MLA_ROPE]
    qn_ref[...] = (proj[:, _C_QN:_C_CMP] * NSA_QSCALE).astype(BF16)
    cmp_ref[...] = proj[:, _C_CMP:_C_SLC]
    slc = proj[:, _C_SLC:_C_SWA]
    swa = proj[:, _C_SWA:_C_KR]
    slc_ref[...] = slc
    swa_ref[...] = swa
    slcb_ref[...] = slc.astype(BF16)
    swab_ref[...] = swa.astype(BF16)
    gate_ref[...] = _sigmoid(proj[:, _C_G:_C_END])


def _proj(x, sc, sh, cs, W, *, rows_per_mod, cs_period_tiles, tm):
    t, d = x.shape
    nt = t // tm
    if rows_per_mod == 1:
        tiles_per_mod = sc.shape[0] and (t // sc.shape[0]) // tm
        mod_map = lambda i: (i // tiles_per_mod, 0, 0)
    else:
        mod_map = lambda i: (i, 0, 0)
    cs_map = (lambda i: (i % cs_period_tiles, 0)) if cs_period_tiles > 1 else (lambda i: (0, 0))
    const2 = lambda i: (0, 0)
    row = lambda i: (i, 0)
    outs = [
        (MLA_HEADS * MLA_QW, BF16), (MLA_QW, BF16), (MLA_KV_LORA + MLA_ROPE, F32), (NSA_HEADS * LANE, BF16),
        (KV_COLS, F32), (KV_COLS, F32), (KV_COLS, F32), (KV_COLS, BF16), (KV_COLS, BF16), (NSA_KV_HEADS * LANE, F32),
    ]
    return pl.pallas_call(
        _proj_kernel, grid=(nt,),
        in_specs=[
            pl.BlockSpec((tm, d), row),
            pl.BlockSpec((None, rows_per_mod, d), mod_map),
            pl.BlockSpec((None, rows_per_mod, d), mod_map),
            pl.BlockSpec((1, d), const2),
            pl.BlockSpec((tm, 2 * LANE), cs_map),
            pl.BlockSpec(W["w_in"].shape, const2),
            pl.BlockSpec((1, MLA_Q_LORA), const2),
            pl.BlockSpec(W["w_qb"].shape, const2),
            pl.BlockSpec((1, MLA_KV_LORA), const2),
            pl.BlockSpec(W["w_kn"].shape, lambda i: (0, 0, 0)),
        ],
        out_specs=[pl.BlockSpec((tm, w), row) for w, _ in outs],
        out_shape=[jax.ShapeDtypeStruct((t, w), dt) for w, dt in outs],
        compiler_params=_cparams(("arbitrary",)), name="proj_in",
    )(x, sc, sh, W["g_norm1"], cs, W["w_in"], W["g_q_a"], W["w_qb"], W["g_kv_a"], W["w_kn"])


def _mla_prompt_kernel(q_ref, k_ref, wv_ref, o_ref, m_ref, l_ref, acc_ref, *, tq, tk):
    i, j = pl.program_id(1), pl.program_id(2)
    nk = pl.num_programs(2)

    @pl.when(j == 0)
    def _():
        m_ref[...] = jnp.full_like(m_ref, NEG)
        l_ref[...] = jnp.zeros_like(l_ref)
        acc_ref[...] = jnp.zeros_like(acc_ref)

    def step(masked):
        k = k_ref[...]
        v = k[:, :MLA_KV_LORA]
        if masked:
            qpos = i * tq + lax.broadcasted_iota(I32, (tq, tk), 0)
            kpos = j * tk + lax.broadcasted_iota(I32, (tq, tk), 1)
            mask = kpos <= qpos
        ss = [_dot_nt(q_ref[:, h * MLA_QW:(h + 1) * MLA_QW], k) for h in range(MLA_HEADS)]
        ps, alphas = [], []
        for h in range(MLA_HEADS):
            r = slice(h * tq, (h + 1) * tq)
            s = jnp.where(mask, ss[h], NEG) if masked else ss[h]
            m_prev = m_ref[r]
            m_new = jnp.maximum(m_prev, _rowmax(s))
            p = jnp.exp2(s - m_new)
            if masked:
                p = jnp.where(mask, p, 0.0)
            alpha = jnp.exp2(m_prev - m_new)
            l_ref[r] = alpha * l_ref[r] + _rowsum(p)
            m_ref[r] = m_new
            ps.append(p.astype(BF16))
            alphas.append(alpha)
        for h in range(MLA_HEADS):
            r = slice(h * tq, (h + 1) * tq)
            acc_ref[r] = alphas[h] * acc_ref[r] + _dot(ps[h], v)

    first_key, last_key = j * tk, j * tk + tk - 1
    pl.when(last_key <= i * tq)(functools.partial(step, False))
    pl.when((first_key <= i * tq + tq - 1) & (last_key > i * tq))(functools.partial(step, True))

    @pl.when(j == nk - 1)
    def _():
        out = jnp.zeros(o_ref.shape, F32)
        for h in range(MLA_HEADS):
            r = slice(h * tq, (h + 1) * tq)
            l = l_ref[r]
            o_lat = acc_ref[r] / jnp.where(l > 0.0, l, 1.0)
            out = out + _dot(o_lat.astype(BF16), wv_ref[h])
        o_ref[...] = out.astype(o_ref.dtype)


def _mla_prompt(qmla, kq, wv, *, nb, seq, tq=256, tk=512):
    tk = min(tk, seq)
    nq, nk = seq // tq, seq // tk
    ow = MLA_HEADS * MLA_V

    def k_map(b, i, j):
        return (b * nk + jnp.minimum(j, (i * tq + tq - 1) // tk), 0)

    return pl.pallas_call(
        functools.partial(_mla_prompt_kernel, tq=tq, tk=tk), grid=(nb, nq, nk),
        in_specs=[pl.BlockSpec((tq, MLA_HEADS * MLA_QW), lambda b, i, j: (b * nq + i, 0)),
                  pl.BlockSpec((tk, MLA_QW), k_map),
                  pl.BlockSpec(wv.shape, lambda b, i, j: (0, 0, 0))],
        out_specs=pl.BlockSpec((tq, ow), lambda b, i, j: (b * nq + i, 0)),
        out_shape=jax.ShapeDtypeStruct((nb * seq, ow), BF16),
        scratch_shapes=[pltpu.VMEM((MLA_HEADS * tq, 1), F32), pltpu.VMEM((MLA_HEADS * tq, 1), F32),
                        pltpu.VMEM((MLA_HEADS * tq, MLA_KV_LORA), F32)],
        compiler_params=_cparams(("arbitrary", "arbitrary", "arbitrary")), name="mla_prompt",
    )(qmla, kq, wv)


def _mla_sample_kernel(pt_ref, q_ref, kn_ref, wv_ref, *rest, g, nq):
    pages = rest[:g]
    o_ref, m_ref, l_ref, acc_ref = rest[g:]
    s_idx = pl.program_id(1)
    rows = MLA_HEADS * nq

    @pl.when(s_idx == 0)
    def _():
        m_ref[...] = jnp.full_like(m_ref, NEG)
        l_ref[...] = jnp.zeros_like(l_ref)
        acc_ref[...] = jnp.zeros_like(acc_ref)

    q = jnp.concatenate([q_ref[:, h * MLA_QW:(h + 1) * MLA_QW] for h in range(MLA_HEADS)], axis=0)
    qlat, qr = q[:, :MLA_KV_LORA], q[:, MLA_KV_LORA:MLA_KV_LORA + MLA_ROPE]
    vs, ss = [], []
    for u in range(g):
        page = pages[u][...]
        ckv_t = page[:MLA_KV_LORA].astype(BF16)
        kr_t = page[MLA_KV_LORA:].astype(BF16)
        ss.append(_dot(qlat, ckv_t) + _dot(qr, kr_t))
        vs.append(ckv_t)
    s = jnp.concatenate(ss, axis=1)
    m_prev = m_ref[...]
    m_new = jnp.maximum(m_prev, _rowmax(s))
    p = jnp.exp2(s - m_new)
    alpha = jnp.exp2(m_prev - m_new)
    pv = jnp.zeros((rows, MLA_KV_LORA), F32)
    for u in range(g):
        pv = pv + _dot_nt(p[:, u * LANE:(u + 1) * LANE].astype(BF16), vs[u])
    l_ref[...] = alpha * l_ref[...] + _rowsum(p)
    acc_ref[...] = alpha * acc_ref[...] + pv
    m_ref[...] = m_new

    @pl.when(s_idx == pl.num_programs(1) - 1)
    def _():
        kn = kn_ref[...]
        sn = _dot_nt(q, kn)
        qi = lax.broadcasted_iota(I32, sn.shape, 0) % nq
        kt = lax.broadcasted_iota(I32, sn.shape, 1)
        mask = kt <= qi
        sn = jnp.where(mask, sn, NEG)
        m_prev = m_ref[...]
        m_new = jnp.maximum(m_prev, jnp.max(sn, axis=-1, keepdims=True))
        p = jnp.where(mask, jnp.exp2(sn - m_new), 0.0)
        alpha = jnp.exp2(m_prev - m_new)
        l = alpha * l_ref[...] + _rowsum(p)
        acc = alpha * acc_ref[...] + _dot(p.astype(BF16), kn[:, :MLA_KV_LORA])
        o_lat = (acc / jnp.where(l > 0.0, l, 1.0)).astype(BF16)
        out = jnp.zeros(o_ref.shape, F32)
        for h in range(MLA_HEADS):
            out = out + _dot(o_lat[h * nq:(h + 1) * nq], wv_ref[h])
        o_ref[...] = out.astype(o_ref.dtype)


def _page_specs(g, block, n_lane_blocks=1, flat=False):
    def index(b, s, pt, u, c):
        return ((pt[s * g + u] if flat else pt[b, s * g + u]), 0, c)

    return [pl.BlockSpec(block, functools.partial(index, u=u, c=c)) for u in range(g) for c in range(n_lane_blocks)]


def _mla_sample(page_table, qmla, kq_new_pad, wv, cache_mla, *, g):
    nb, nq = qmla.shape[0], qmla.shape[1]
    n_pages = page_table.shape[1]
    ow = MLA_HEADS * MLA_V
    rows = MLA_HEADS * nq
    gs = pltpu.PrefetchScalarGridSpec(
        num_scalar_prefetch=1, grid=(nb, n_pages // g),
        in_specs=[pl.BlockSpec((None, nq, MLA_HEADS * MLA_QW), lambda b, s, pt: (b, 0, 0)),
                  pl.BlockSpec((None, LANE, MLA_QW), lambda b, s, pt: (b, 0, 0)),
                  pl.BlockSpec(wv.shape, lambda b, s, pt: (0, 0, 0))]
        + _page_specs(g, (None,) + cache_mla.shape[1:]),
        out_specs=pl.BlockSpec((None, nq, ow), lambda b, s, pt: (b, 0, 0)),
        scratch_shapes=[pltpu.VMEM((rows, 1), F32), pltpu.VMEM((rows, 1), F32), pltpu.VMEM((rows, MLA_KV_LORA), F32)],
    )
    return pl.pallas_call(
        functools.partial(_mla_sample_kernel, g=g, nq=nq), grid_spec=gs,
        out_shape=jax.ShapeDtypeStruct((nb, nq, ow), BF16),
        compiler_params=_cparams(("arbitrary", "arbitrary")), name="mla_sample",
    )(page_table, qmla, kq_new_pad, wv, *([cache_mla] * g))


def _cmp_first_kernel(pl_ref, w_ref, *rest, g, transposed):
    pages = rest[:g]
    o_ref, xs_ref = rest[g:]
    half_w = NSA_KV_HEADS * 2 * CMP_HIDDEN
    n_half = o_ref.shape[0]
    for u in range(g):
        x = pages[u][...]
        for j in range(2):
            if transposed:
                xs_ref[j, u * LANE:(u + 1) * LANE, :] = x[j * LANE:(j + 1) * LANE, :].T
            else:
                xs_ref[j, u * LANE:(u + 1) * LANE, :] = x[:, j * LANE:(j + 1) * LANE]
    for j in range(2):
        acc = jnp.zeros((n_half, half_w), F32)
        for pp in range(CMP_STRIDE // 2):
            xa = xs_ref[j, pl.ds(2 * pp, n_half, stride=CMP_STRIDE), :]
            xb = xs_ref[j, pl.ds(2 * pp + 1, n_half, stride=CMP_STRIDE), :]
            acc = acc + _dot(jnp.concatenate([xa, xb], axis=1).astype(BF16), w_ref[j, pp])
        o_ref[:, j * half_w:(j + 1) * half_w] = acc


def _cmp_first(page_list, pool, w_pair, *, g, transposed):
    n = page_list.shape[0]
    halves = LANE // CMP_STRIDE
    ow = 2 * NSA_KV_HEADS * 2 * CMP_HIDDEN
    gs = pltpu.PrefetchScalarGridSpec(
        num_scalar_prefetch=1, grid=(1, n // g),
        in_specs=[pl.BlockSpec(w_pair.shape, lambda b, s, pt: (0, 0, 0, 0))]
        + _page_specs(g, (None,) + pool.shape[1:], flat=True),
        out_specs=pl.BlockSpec((g * halves, ow), lambda b, s, pt: (s, 0)),
        scratch_shapes=[pltpu.VMEM((2, g * LANE, LANE), F32)],
    )
    return pl.pallas_call(
        functools.partial(_cmp_first_kernel, g=g, transposed=transposed), grid_spec=gs,
        out_shape=jax.ShapeDtypeStruct((n * halves, ow), F32),
        compiler_params=_cparams(("arbitrary", "arbitrary")), name="cmp_first",
    )(page_list, w_pair, *([pool] * g))


def _cmp_finish_kernel(ab_ref, pe_ref, w1_ref, w2_ref, kc_ref, vc_ref):
    nh = ab_ref.shape[0]
    outs = []
    for j in range(2):
        pe_term = _dot(pe_ref[j].astype(BF16), w1_ref[j])[0:1]
        acc = jnp.zeros((nh, LANE), F32)
        for kv in range(NSA_KV_HEADS):
            base = (j * NSA_KV_HEADS + kv) * 2 * CMP_HIDDEN
            first = ab_ref[:, base:base + CMP_HIDDEN]
            second = ab_ref[:, base + CMP_HIDDEN:base + 2 * CMP_HIDDEN]
            hid = first + pltpu.roll(second, nh - 1, 0) + pe_term
            hid = 0.5 * hid * (1.0 + lax.erf(hid * math.sqrt(0.5)))
            acc = acc + _dot(hid.astype(BF16), w2_ref[j, kv])
        outs.append(acc)
    kc_ref[...] = outs[0].astype(kc_ref.dtype)
    vc_ref[...] = outs[1].astype(vc_ref.dtype)


def _cmp_finish(ab, pe8, w1r, w2pad, *, nb):
    nh = ab.shape[0] // nb
    c3 = lambda b: (0, 0, 0)
    return pl.pallas_call(
        _cmp_finish_kernel, grid=(nb,),
        in_specs=[pl.BlockSpec((nh, ab.shape[1]), lambda b: (b, 0)), pl.BlockSpec(pe8.shape, c3),
                  pl.BlockSpec(w1r.shape, c3), pl.BlockSpec(w2pad.shape, lambda b: (0, 0, 0, 0))],
        out_specs=[pl.BlockSpec((None, nh, LANE), lambda b: (b, 0, 0))] * 2,
        out_shape=[jax.ShapeDtypeStruct((nb, nh, LANE), BF16)] * 2,
        compiler_params=_cparams(("arbitrary",)), name="cmp_finish")(ab, pe8, w1r, w2pad)


def _topk_mask(v, k, axis=1):
    lane = lax.broadcasted_iota(I32, v.shape, axis)
    sel = jnp.zeros(v.shape, F32)
    for _ in range(k):
        m = jnp.max(v, axis=axis, keepdims=True)
        idx = jnp.min(jnp.where(v == m, lane, 1 << 20), axis=axis, keepdims=True)
        pick = lane == idx
        sel = jnp.where(pick & (m > -jnp.inf), 1.0, sel)
        v = jnp.where(pick, -jnp.inf, v)
    return sel


def _softmax_masked(s, mask):
    s = jnp.where(mask, s, NEG)
    m = _rowmax(s)
    p = jnp.where(mask, jnp.exp2(s - m), 0.0)
    l = _rowsum(p)
    return p / jnp.where(l > 0.0, l, 1.0)


def _flash_step(s, mask, v, m_ref, l_ref, acc_ref, v_keys_on_lanes=False):
    s = jnp.where(mask, s, NEG)
    m_prev = m_ref[...]
    m_new = jnp.maximum(m_prev, _rowmax(s))
    p = jnp.where(mask, jnp.exp2(s - m_new), 0.0)
    alpha = jnp.exp2(m_prev - m_new)
    l_ref[...] = alpha * l_ref[...] + _rowsum(p)
    pv = _dot_nt(p.astype(BF16), v) if v_keys_on_lanes else _dot(p.astype(BF16), v)
    acc_ref[...] = alpha * acc_ref[...] + pv
    m_ref[...] = m_new


def _flash_init(m_ref, l_ref, acc_ref):
    m_ref[...] = jnp.full_like(m_ref, NEG)
    l_ref[...] = jnp.zeros_like(l_ref)
    acc_ref[...] = jnp.zeros_like(acc_ref)


def _flash_out(l_ref, acc_ref):
    l = l_ref[...]
    return acc_ref[...] / jnp.where(l > 0.0, l, 1.0)


def _nsa_prompt_kernel(q_ref, kc_ref, vc_ref, bc_ref, ovl_ref, slc_ref, swa_ref, tt_ref, gate_ref, o_ref,
                       m_ref, l_ref, acc_ref, *, tq, tkf, n_cmp, n_sel, seq_len):
    i = pl.program_id(2)
    rows = NSA_GROUP * tq
    n_sel_rows = -(-n_sel // 8) * 8
    q = jnp.concatenate([q_ref[:, g * LANE:(g + 1) * LANE] for g in range(NSA_GROUP)], axis=0)
    nhp = kc_ref.shape[0]
    pos_r = i * tq + lax.broadcasted_iota(I32, (rows, 1), 0) % tq

    wu = lax.broadcasted_iota(I32, (LANE, nhp), 0)
    wn = lax.broadcasted_iota(I32, (LANE, nhp), 1)
    shift = jnp.where(wn == i * (tq // CMP_STRIDE) - CMP_WIN_LO + wu, 1.0, 0.0)
    far_col = tt_ref[2][:, :1]
    bias_c = jnp.dot(bc_ref[...], shift, precision=lax.Precision.HIGHEST, preferred_element_type=F32) + far_col
    s = _dot_nt(q, kc_ref[...]) + bias_c
    n_idx = lax.broadcasted_iota(I32, (rows, nhp), 1)
    mask_c = (n_idx * CMP_STRIDE + CMP_LEN - 1 <= pos_r) & (n_idx < n_cmp)
    p_cmp = _softmax_masked(s, mask_c)
    o_cmp = _dot(p_cmp.astype(BF16), vc_ref[...])

    psum = p_cmp[0:tq]
    for g in range(1, NSA_GROUP):
        psum = psum + p_cmp[g * tq:(g + 1) * tq]
    imp_t = lax.dot_general(ovl_ref[...], psum, (((1,), (1,)), ((), ())), precision=lax.Precision.HIGHEST,
                            preferred_element_type=F32)[:n_sel_rows]
    blk = lax.broadcasted_iota(I32, imp_t.shape, 0)
    cur = (i * tq + lax.broadcasted_iota(I32, imp_t.shape, 1)) // SEL_BLOCK
    forced = (blk == 0) | (blk == cur) | (blk == cur - 1)
    future = (blk > cur) | (blk >= n_sel)
    imp_t = jnp.where(future, -jnp.inf, jnp.where(forced, jnp.inf, imp_t))
    sel_t = _topk_mask(imp_t, min(SEL_TOPN, n_sel), axis=0)
    if n_sel_rows < LANE:
        sel_t = jnp.concatenate([sel_t, jnp.zeros((LANE - n_sel_rows, tq), F32)], axis=0)
    sel = sel_t.T.astype(BF16)

    near0 = pl.multiple_of(jnp.maximum(i - 1, 0) * tq, tq)
    near_bias = jnp.concatenate([tt_ref[jnp.where(i == 0, 0, 1)], tt_ref[0]], axis=1)
    pos_q = i * tq + lax.broadcasted_iota(I32, (tq, 1), 0)
    d_near = pos_q - (near0 + lax.broadcasted_iota(I32, (tq, 2 * tq), 1))

    def heads(pen):
        return jnp.concatenate([pen] * NSA_GROUP, axis=0)

    def block_sel(first_key, n_keys, key_limit):
        sb = lax.broadcasted_iota(I32, (LANE, n_keys), 0)
        kt = lax.broadcasted_iota(I32, (LANE, n_keys), 1)
        hit = (sb == first_key // SEL_BLOCK + kt // SEL_BLOCK) & (first_key + kt < key_limit)
        return _dot(sel, jnp.where(hit, 1.0, 0.0).astype(BF16)) > 0.5

    n_far_w = WINDOW - tq
    far0 = pl.multiple_of(jnp.maximum(i - WINDOW // tq, 0) * tq, tq)
    kv_n = swa_ref[pl.ds(near0, 2 * tq), :]
    kv_f = swa_ref[pl.ds(far0, n_far_w), :]
    kpos_f = far0 + lax.broadcasted_iota(I32, (tq, n_far_w), 1)
    pen_n = jnp.where((d_near >= 0) & (d_near < WINDOW), 0.0, NEG)
    pen_f = jnp.where((kpos_f < near0) & (pos_q - kpos_f < WINDOW), 0.0, NEG)
    s_n = _dot_nt(q, kv_n[:, :LANE]) + near_bias + heads(pen_n)
    s_f = _dot_nt(q, kv_f[:, :LANE]) + far_col + heads(pen_f)
    m_w = jnp.maximum(_rowmax(s_n), _rowmax(s_f))
    p_n = jnp.exp2(s_n - m_w)
    p_f = jnp.exp2(s_f - m_w)
    l_w = _rowsum(p_n) + _rowsum(p_f)
    o_swa = (_dot(p_n.astype(BF16), kv_n[:, LANE:]) + _dot(p_f.astype(BF16), kv_f[:, LANE:])) / l_w

    kv_n = slc_ref[pl.ds(near0, 2 * tq), :]
    pen_n = jnp.where(block_sel(near0, 2 * tq, seq_len) & (d_near >= 0), 0.0, NEG)
    s_n = _dot_nt(q, kv_n[:, :LANE]) + near_bias + heads(pen_n)
    m_s = _rowmax(s_n)
    p_n = jnp.exp2(s_n - m_s)
    m_ref[...] = m_s
    l_ref[...] = _rowsum(p_n)
    acc_ref[...] = _dot(p_n.astype(BF16), kv_n[:, LANE:])

    def slc_far(c, carry):
        first = pl.multiple_of(c * tkf, tkf)
        kv = slc_ref[pl.ds(first, tkf), :]
        pen = jnp.where(block_sel(first, tkf, near0), 0.0, NEG)
        s_ = _dot_nt(q, kv[:, :LANE]) + far_col + heads(pen)
        m_prev = m_ref[...]
        m_new = jnp.maximum(m_prev, _rowmax(s_))
        p = jnp.exp2(s_ - m_new)
        alpha = jnp.exp2(m_prev - m_new)
        l_ref[...] = alpha * l_ref[...] + _rowsum(p)
        acc_ref[...] = alpha * acc_ref[...] + _dot(p.astype(BF16), kv[:, LANE:])
        m_ref[...] = m_new
        return carry

    lax.fori_loop(0, (near0 + tkf - 1) // tkf, slc_far, 0)
    o_slc = acc_ref[...] / l_ref[...]

    gates = gate_ref[...]
    for g in range(NSA_GROUP):
        r = slice(g * tq, (g + 1) * tq)
        o = (gates[:, g:g + 1] * o_cmp[r] + gates[:, NSA_GROUP + g:NSA_GROUP + g + 1] * o_slc[r]
             + gates[:, 2 * NSA_GROUP + g:2 * NSA_GROUP + g + 1] * o_swa[r])
        o_ref[:, g * LANE:(g + 1) * LANE] = o.astype(o_ref.dtype)


def _nsa_prompt(qn, kc, vc, bias_c, ovl, slc_b, swa_b, tt, gates, *, nb, seq, n_cmp, n_sel):
    tq = 128
    nq = seq // tq
    rows = NSA_GROUP * tq
    gw = NSA_GROUP * LANE
    nhp = kc.shape[1]
    return pl.pallas_call(
        functools.partial(_nsa_prompt_kernel, tq=tq, tkf=min(512, seq), n_cmp=n_cmp, n_sel=n_sel, seq_len=seq),
        grid=(nb, NSA_KV_HEADS, nq),
        in_specs=[
            pl.BlockSpec((tq, gw), lambda b, k, i: (b * nq + i, k)),
            pl.BlockSpec((None, nhp, LANE), lambda b, k, i: (b, 0, 0)),
            pl.BlockSpec((None, nhp, LANE), lambda b, k, i: (b, 0, 0)),
            pl.BlockSpec((None, rows, LANE), lambda b, k, i: (k, 0, 0)),
            pl.BlockSpec(ovl.shape, lambda b, k, i: (0, 0)),
            pl.BlockSpec((None, seq, KV_COLS), lambda b, k, i: (b, 0, 0)),
            pl.BlockSpec((None, seq, KV_COLS), lambda b, k, i: (b, 0, 0)),
            pl.BlockSpec((None, 3, rows, tq), lambda b, k, i: (k, 0, 0, 0)),
            pl.BlockSpec((tq, LANE), lambda b, k, i: (b * nq + i, k)),
        ],
        out_specs=pl.BlockSpec((tq, gw), lambda b, k, i: (b * nq + i, k)),
        out_shape=jax.ShapeDtypeStruct((nb * seq, NSA_HEADS * LANE), BF16),
        scratch_shapes=[pltpu.VMEM((rows, 1), F32), pltpu.VMEM((rows, 1), F32), pltpu.VMEM((rows, LANE), F32)],
        compiler_params=_cparams(("arbitrary", "arbitrary", "arbitrary")), name="nsa_prompt",
    )(qn, kc, vc, bias_c, ovl, slc_b, swa_b, tt, gates)


def _nsa_sample_kernel(pt_ref, q_ref, kc_ref, vc_ref, bc_ref, ovl_ref, bs_ref, sn_ref, bsn_ref, st_ref, wn_ref,
                       bw_ref, bwn_ref, gate_ref, *rest, g, nq, n_cmp, n_sel, buf):
    pages = rest[:g]
    o_ref, sel_ref, ocmp_ref, oswa_ref, m_ref, l_ref, acc_ref = rest[g:]
    s_idx = pl.program_id(1)
    last = pl.num_programs(1) - 1
    rows = NSA_HEADS * nq
    q = jnp.concatenate([q_ref[:, h * LANE:(h + 1) * LANE] for h in range(NSA_HEADS)], axis=0)
    qi = lax.broadcasted_iota(I32, (rows, 1), 0) % nq

    @pl.when(s_idx == 0)
    def _():
        nhp = kc_ref.shape[0]
        s = _dot_nt(q, kc_ref[...]) + bc_ref[...]
        mask_c = lax.broadcasted_iota(I32, (rows, nhp), 1) < n_cmp
        p_cmp = _softmax_masked(s, mask_c)
        ocmp_ref[...] = _dot(p_cmp.astype(BF16), vc_ref[...])
        psums = []
        for kv in range(NSA_KV_HEADS):
            ps = p_cmp[kv * NSA_GROUP * nq:kv * NSA_GROUP * nq + nq]
            for gg in range(1, NSA_GROUP):
                lo = (kv * NSA_GROUP + gg) * nq
                ps = ps + p_cmp[lo:lo + nq]
            psums.append(ps)
        psum = jnp.concatenate(psums, axis=0)
        imp = jnp.dot(psum, ovl_ref[...], precision=lax.Precision.HIGHEST, preferred_element_type=F32)
        blk = lax.broadcasted_iota(I32, imp.shape, 1)
        cur = n_sel - 1
        forced = (blk == 0) | (blk == cur) | (blk == cur - 1)
        imp = jnp.where(blk >= n_sel, -jnp.inf, jnp.where(forced, jnp.inf, imp))
        sel = _topk_mask(imp, min(SEL_TOPN, n_sel)).astype(BF16)
        sel_ref[...] = jnp.concatenate(
            [sel[kv * nq:(kv + 1) * nq] for kv in range(NSA_KV_HEADS) for _ in range(NSA_GROUP)], axis=0)

        st = st_ref[...]
        s_w = _dot(q, st[:LANE].astype(BF16)) + bw_ref[...]
        d_w = buf + qi - lax.broadcasted_iota(I32, (rows, buf), 1)
        mask_w = (d_w >= 0) & (d_w < WINDOW)
        wn = wn_ref[...]
        s_n = _dot_nt(q, wn[:, :LANE]) + bwn_ref[...]
        mask_n = lax.broadcasted_iota(I32, (rows, LANE), 1) <= qi
        s_w = jnp.where(mask_w, s_w, NEG)
        s_n = jnp.where(mask_n, s_n, NEG)
        m = jnp.maximum(_rowmax(s_w), _rowmax(s_n))
        p_w = jnp.where(mask_w, jnp.exp2(s_w - m), 0.0)
        p_n = jnp.where(mask_n, jnp.exp2(s_n - m), 0.0)
        l = _rowsum(p_w) + _rowsum(p_n)
        o = _dot_nt(p_w.astype(BF16), st[LANE:].astype(BF16)) + _dot(p_n.astype(BF16), wn[:, LANE:])
        oswa_ref[...] = o / jnp.where(l > 0.0, l, 1.0)
        _flash_init(m_ref, l_ref, acc_ref)

    page = pages[0].shape[1]
    kcat = jnp.concatenate([pages[u][:LANE].astype(BF16) for u in range(g)], axis=1)
    vcat = jnp.concatenate([pages[u][LANE:].astype(BF16) for u in range(g)], axis=1)
    nk = g * page
    s = _dot(q, kcat) + bs_ref[jnp.where(s_idx == last, 1, 0)]
    sb = lax.broadcasted_iota(I32, (sel_ref.shape[1], nk), 0)
    kt = lax.broadcasted_iota(I32, (sel_ref.shape[1], nk), 1)
    expand = jnp.where(sb == s_idx * (nk // SEL_BLOCK) + kt // SEL_BLOCK, 1.0, 0.0).astype(BF16)
    mask = _dot(sel_ref[...], expand) > 0.5
    _flash_step(s, mask, vcat, m_ref, l_ref, acc_ref, v_keys_on_lanes=True)

    @pl.when(s_idx == last)
    def _():
        sn = sn_ref[...]
        s_n = _dot_nt(q, sn[:, :LANE]) + bsn_ref[...]
        mask_n = (lax.broadcasted_iota(I32, (rows, LANE), 1) <= qi) & (sel_ref[:, n_sel - 1:n_sel] > 0.5)
        _flash_step(s_n, mask_n, sn[:, LANE:], m_ref, l_ref, acc_ref)
        o_slc = _flash_out(l_ref, acc_ref)
        o_cmp, o_swa = ocmp_ref[...], oswa_ref[...]
        gates = gate_ref[...]
        for h in range(NSA_HEADS):
            kv, gg = divmod(h, NSA_GROUP)
            r = slice(h * nq, (h + 1) * nq)
            c = kv * LANE + gg
            o = (gates[:, c:c + 1] * o_cmp[r] + gates[:, c + NSA_GROUP:c + NSA_GROUP + 1] * o_slc[r]
                 + gates[:, c + 2 * NSA_GROUP:c + 2 * NSA_GROUP + 1] * o_swa[r])
            o_ref[:, h * LANE:(h + 1) * LANE] = o.astype(o_ref.dtype)


def _nsa_sample(page_table, qn, kc, vc, bias_c, ovl, bias_s, slc_new, bias_sn, state_swa, swa_new, bias_w, bias_wn,
                gates, cache_slc, *, g, n_cmp, n_sel):
    nb, nq = qn.shape[0], qn.shape[1]
    n_pages = page_table.shape[1]
    rows = NSA_HEADS * nq
    buf = state_swa.shape[2]
    hw = NSA_HEADS * LANE
    bmap = lambda b, s, pt: (b, 0, 0)
    c2 = lambda b, s, pt: (0, 0)
    gs = pltpu.PrefetchScalarGridSpec(
        num_scalar_prefetch=1, grid=(nb, n_pages // g),
        in_specs=[
            pl.BlockSpec((None, nq, hw), bmap),
            pl.BlockSpec((None,) + kc.shape[1:], bmap),
            pl.BlockSpec((None,) + vc.shape[1:], bmap),
            pl.BlockSpec(bias_c.shape, c2),
            pl.BlockSpec(ovl.shape, c2),
            pl.BlockSpec(bias_s.shape, lambda b, s, pt: (0, 0, 0)),
            pl.BlockSpec((None,) + slc_new.shape[1:], bmap),
            pl.BlockSpec(bias_sn.shape, c2),
            pl.BlockSpec((None,) + state_swa.shape[1:], bmap),
            pl.BlockSpec((None,) + swa_new.shape[1:], bmap),
            pl.BlockSpec(bias_w.shape, c2),
            pl.BlockSpec(bias_wn.shape, c2),
            pl.BlockSpec((None, nq, NSA_KV_HEADS * LANE), bmap),
        ] + _page_specs(g, (None,) + cache_slc.shape[1:]),
        out_specs=pl.BlockSpec((None, nq, hw), bmap),
        scratch_shapes=[pltpu.VMEM((rows, ovl.shape[1]), BF16), pltpu.VMEM((rows, LANE), F32), pltpu.VMEM((rows, LANE), F32),
                        pltpu.VMEM((rows, 1), F32), pltpu.VMEM((rows, 1), F32), pltpu.VMEM((rows, LANE), F32)],
    )
    return pl.pallas_call(
        functools.partial(_nsa_sample_kernel, g=g, nq=nq, n_cmp=n_cmp, n_sel=n_sel, buf=buf), grid_spec=gs,
        out_shape=jax.ShapeDtypeStruct((nb, nq, hw), BF16),
        compiler_params=_cparams(("arbitrary", "arbitrary")), name="nsa_sample",
    )(page_table, qn, kc, vc, bias_c, ovl, bias_s, slc_new, bias_sn, state_swa, swa_new, bias_w, bias_wn, gates,
      *([cache_slc] * g))


def _outproj_kernel(x_ref, om_ref, on_ref, g1_ref, sc_ref, sh_ref, wom_ref, won_ref, gn2_ref, wr_ref, br_ref,
                    x1_ref, h2_ref, ti_ref, tw_ref):
    mix = _dot(om_ref[...], wom_ref[...]) + _dot(on_ref[...], won_ref[...])
    x1 = x_ref[...] + g1_ref[...] * mix
    x1_ref[...] = x1
    h2 = _rms(x1, gn2_ref[...]) * (1.0 + sc_ref[...]) + sh_ref[...]
    h2_ref[...] = h2.astype(h2_ref.dtype)
    logits = jnp.dot(h2, wr_ref[...], precision=lax.Precision.HIGHEST, preferred_element_type=F32) + br_ref[...]
    lane = lax.broadcasted_iota(I32, logits.shape, 1)
    v = logits
    vals, idxs = [], []
    for _ in range(TOP_K):
        m = jnp.max(v, axis=-1, keepdims=True)
        idx = jnp.min(jnp.where(v == m, lane, 1 << 20), axis=-1, keepdims=True)
        vals.append(m)
        idxs.append(idx)
        v = jnp.where(lane == idx, -jnp.inf, v)
    es = [jnp.exp(m - vals[0]) for m in vals]
    tot = es[0]
    for e in es[1:]:
        tot = tot + e
    ti = jnp.zeros(logits.shape, I32)
    tw = jnp.zeros(logits.shape, F32)
    for k in range(TOP_K):
        ti = jnp.where(lane == k, idxs[k], ti)
        tw = jnp.where(lane == k, es[k] / tot, tw)
    ti_ref[...] = ti
    tw_ref[...] = tw


def _outproj(x, o_mla, o_nsa, g1, sc, sh, W, *, rows_per_mod, tm):
    t, d = x.shape
    nt = t // tm
    if rows_per_mod == 1:
        tiles_per_mod = (t // g1.shape[0]) // tm
        mod_map = lambda i: (i // tiles_per_mod, 0, 0)
    else:
        mod_map = lambda i: (i, 0, 0)
    row = lambda i: (i, 0)
    c2 = lambda i: (0, 0)
    mod = pl.BlockSpec((None, rows_per_mod, d), mod_map)
    return pl.pallas_call(
        _outproj_kernel, grid=(nt,),
        in_specs=[pl.BlockSpec((tm, d), row), pl.BlockSpec((tm, o_mla.shape[1]), row), pl.BlockSpec((tm, o_nsa.shape[1]), row),
                  mod, mod, mod, pl.BlockSpec(W["w_o_mla"].shape, c2), pl.BlockSpec(W["w_o_nsa"].shape, c2),
                  pl.BlockSpec((1, d), c2), pl.BlockSpec(W["w_router"].shape, c2), pl.BlockSpec((1, LANE), c2)],
        out_specs=[pl.BlockSpec((tm, d), row), pl.BlockSpec((tm, d), row), pl.BlockSpec((tm, LANE), row),
                   pl.BlockSpec((tm, LANE), row)],
        out_shape=[jax.ShapeDtypeStruct((t, d), F32), jax.ShapeDtypeStruct((t, d), F32),
                   jax.ShapeDtypeStruct((t, LANE), I32), jax.ShapeDtypeStruct((t, LANE), F32)],
        compiler_params=_cparams(("arbitrary",)), name="out_proj",
    )(x, o_mla, o_nsa, g1, sc, sh, W["w_o_mla"], W["w_o_nsa"], W["g_norm2"], W["w_router"], W["b_router"])


def _expert_kernel(te_ref, tv_ref, x_ref, wgu_ref, bgu_ref, wd_ref, bd_ref, ws_ref, o_ref):
    t = pl.program_id(0)

    @pl.when(tv_ref[t] > 0)
    def _():
        d_ff = wd_ref.shape[0]
        gu = _dot(x_ref[...].astype(BF16), wgu_ref[...]) + bgu_ref[...]
        glu = jnp.minimum(gu[:, :d_ff], SWIGLU_LIMIT)
        lin = jnp.clip(gu[:, d_ff:], -SWIGLU_LIMIT, SWIGLU_LIMIT)
        act = glu * _sigmoid(SWIGLU_ALPHA * glu) * (lin + 1.0)
        o_ref[...] = ws_ref[...] * (_dot(act.astype(BF16), wd_ref[...]) + bd_ref[...])

    @pl.when(tv_ref[t] == 0)
    def _():
        o_ref[...] = jnp.zeros_like(o_ref)


def _experts(tile_expert, tile_valid, x_sorted, w_slot, W, *, tm):
    ns, d = x_sorted.shape
    gs = pltpu.PrefetchScalarGridSpec(
        num_scalar_prefetch=2, grid=(ns // tm,),
        in_specs=[pl.BlockSpec((tm, d), lambda t, te, tv: (t, 0)),
                  pl.BlockSpec((None,) + W["w_gate_up"].shape[1:], lambda t, te, tv: (te[t], 0, 0)),
                  pl.BlockSpec((None,) + W["b_gate_up"].shape[1:], lambda t, te, tv: (te[t], 0, 0)),
                  pl.BlockSpec((None,) + W["w_down"].shape[1:], lambda t, te, tv: (te[t], 0, 0)),
                  pl.BlockSpec((None,) + W["b_down"].shape[1:], lambda t, te, tv: (te[t], 0, 0)),
                  pl.BlockSpec((tm, 1), lambda t, te, tv: (t, 0))],
        out_specs=pl.BlockSpec((tm, d), lambda t, te, tv: (t, 0)),
    )
    return pl.pallas_call(
        _expert_kernel, grid_spec=gs, out_shape=jax.ShapeDtypeStruct((ns, d), F32),
        compiler_params=_cparams(("arbitrary",)), name="experts",
    )(tile_expert, tile_valid, x_sorted, W["w_gate_up"], W["b_gate_up"], W["w_down"], W["b_down"], w_slot)


def _final_kernel(x1_ref, moe_ref, g2_ref, gf_ref, o_ref):
    o_ref[...] = _rms(x1_ref[...] + g2_ref[...] * moe_ref[...], gf_ref[...])


def _final(x1, moe, g2, g_final, *, rows_per_mod, tm):
    t, d = x1.shape
    if rows_per_mod == 1:
        tiles_per_mod = (t // g2.shape[0]) // tm
        mod_map = lambda i: (i // tiles_per_mod, 0, 0)
    else:
        mod_map = lambda i: (i, 0, 0)
    row = lambda i: (i, 0)
    return pl.pallas_call(
        _final_kernel, grid=(t // tm,),
        in_specs=[pl.BlockSpec((tm, d), row), pl.BlockSpec((tm, d), row), pl.BlockSpec((None, rows_per_mod, d), mod_map),
                  pl.BlockSpec((1, d), lambda i: (0, 0))],
        out_specs=pl.BlockSpec((tm, d), row), out_shape=jax.ShapeDtypeStruct((t, d), F32),
        compiler_params=_cparams(("arbitrary",)), name="final_norm")(x1, moe, g2, g_final)


def _t5_bucket(dist):
    n = jnp.maximum(dist, 0)
    max_exact = NUM_BUCKETS // 2
    nf = jnp.maximum(n, 1).astype(F32)
    large = max_exact + (jnp.log(nf / max_exact) / math.log(MAX_DISTANCE / max_exact)
                         * (NUM_BUCKETS - max_exact)).astype(I32)
    return jnp.where(n < max_exact, n, jnp.minimum(large, NUM_BUCKETS - 1))


def _bias_rows(rel_bias, dist):
    return jnp.moveaxis(rel_bias[_t5_bucket(dist)], -1, 0) * LOG2E


def _rope_table(pos):
    half = MLA_ROPE // 2
    inv_freq = 1.0 / (ROPE_THETA ** (jnp.arange(half, dtype=F32) / half))
    ang = pos.astype(F32)[:, None] * inv_freq[None, :]
    pad = jnp.zeros((pos.shape[0], LANE - MLA_ROPE), F32)
    cos, sin = jnp.cos(ang), jnp.sin(ang)
    return jnp.concatenate([cos, cos, pad, sin, sin, pad], axis=1)


def _rot_cols(w):
    half = MLA_ROPE // 2
    return jnp.concatenate([-w[..., half:], w[..., :half]], axis=-1)


def _pad_last(w, n):
    return jnp.pad(w, [(0, 0)] * (w.ndim - 1) + [(0, n - w.shape[-1])])


def _pack_weights(w_in, g_norm1, g_norm2, g_q_a, w_q_b, g_kv_a, w_kv_b, cmp_pe, cmp_w1, cmp_w2, w_o, w_router,
                  b_router, w_gate_up, b_gate_up, w_down, b_down):
    d = w_in.shape[0]
    sizes = (MLA_Q_LORA, MLA_KV_LORA, MLA_ROPE, NSA_HEADS * HEAD_DIM, KV_COLS, KV_COLS, KV_COLS, 3 * NSA_HEADS)
    offs = [0]
    for s in sizes:
        offs.append(offs[-1] + s)
    w_qa, w_ckv, w_kr, w_qn, w_cmp, w_slc, w_swa, w_g = [w_in[:, offs[k]:offs[k + 1]] for k in range(8)]
    wq = w_qn.reshape(d, NSA_KV_HEADS, NSA_GROUP, HEAD_DIM)
    wq_pad = jnp.concatenate(
        [jnp.pad(wq[:, kv], ((0, 0), (0, 0), (kv * HEAD_DIM, LANE - (kv + 1) * HEAD_DIM))).reshape(d, NSA_GROUP * LANE)
         for kv in range(NSA_KV_HEADS)], axis=1)
    wg = jnp.transpose(w_g.reshape(d, NSA_KV_HEADS, NSA_GROUP, 3), (0, 1, 3, 2)).reshape(d, NSA_KV_HEADS, 3 * NSA_GROUP)
    wg_pad = _pad_last(wg, LANE).reshape(d, NSA_KV_HEADS * LANE)
    w_in_p = jnp.concatenate([w_qa, w_ckv, wq_pad, w_cmp, w_slc, w_swa, _pad_last(w_kr, LANE),
                              _pad_last(_rot_cols(w_kr), LANE), wg_pad], axis=1).astype(BF16)
    assert w_in_p.shape[1] == _C_END
    nope = _pad_last(w_q_b[:, :, :MLA_NOPE], LANE).reshape(MLA_Q_LORA, MLA_HEADS * LANE)
    rp = w_q_b[:, :, MLA_NOPE:]
    w_qb = jnp.concatenate([nope, _pad_last(rp, LANE).reshape(MLA_Q_LORA, -1),
                            _pad_last(_rot_cols(rp), LANE).reshape(MLA_Q_LORA, -1)], axis=1).astype(BF16)
    w_kn = jnp.transpose(w_kv_b[:, :, :MLA_NOPE], (1, 2, 0))
    w_kn = jnp.pad(w_kn, ((0, 0), (0, LANE - MLA_NOPE), (0, 0))).astype(BF16)
    wv = jnp.transpose(w_kv_b[:, :, MLA_NOPE:], (1, 0, 2))
    wv_pad = jnp.stack([jnp.pad(wv[h], ((0, 0), (h * MLA_V, (MLA_HEADS - 1 - h) * MLA_V))) for h in range(MLA_HEADS)]).astype(BF16)
    n_mla = MLA_HEADS * MLA_V
    won = w_o[n_mla:].reshape(NSA_KV_HEADS, NSA_GROUP, HEAD_DIM, d)
    won_pad = jnp.concatenate(
        [jnp.pad(won[kv], ((0, 0), (kv * HEAD_DIM, LANE - (kv + 1) * HEAD_DIM), (0, 0))).reshape(NSA_GROUP * LANE, d)
         for kv in range(NSA_KV_HEADS)], axis=0).astype(BF16)
    base = jnp.concatenate([cmp_w1[:, :CMP_STRIDE], cmp_w1[:, CMP_STRIDE:]], axis=-1)
    z = jnp.zeros_like(base)
    blk = jnp.concatenate([jnp.concatenate([base, z], axis=-1), jnp.concatenate([z, base], axis=-1)], axis=2)
    w_pair = blk.reshape(2, CMP_STRIDE // 2, 2 * NSA_KV_HEADS * HEAD_DIM, NSA_KV_HEADS * 2 * CMP_HIDDEN).astype(BF16)
    pe8 = jnp.broadcast_to(cmp_pe.reshape(2, 1, CMP_LEN * HEAD_DIM), (2, 8, CMP_LEN * HEAD_DIM))
    w1r = cmp_w1.reshape(2, CMP_LEN * HEAD_DIM, CMP_HIDDEN).astype(BF16)
    w2pad = jnp.stack([jnp.stack([jnp.pad(cmp_w2[j], ((0, 0), (kv * HEAD_DIM, LANE - (kv + 1) * HEAD_DIM)))
                                  for kv in range(NSA_KV_HEADS)]) for j in range(2)]).astype(BF16)
    return dict(
        w_in=w_in_p, g_norm1=g_norm1[None], g_norm2=g_norm2[None], g_q_a=g_q_a[None], g_kv_a=g_kv_a[None],
        w_qb=w_qb, w_kn=w_kn, wv=wv_pad, w_o_mla=w_o[:n_mla].astype(BF16), w_o_nsa=won_pad,
        w_pair=w_pair, pe8=pe8, w1r=w1r, w2pad=w2pad,
        w_router=_pad_last(w_router, LANE), b_router=jnp.pad(b_router, (0, LANE - N_EXPERTS), constant_values=NEG)[None],
        w_gate_up=w_gate_up.astype(BF16), b_gate_up=b_gate_up[:, None, :], w_down=w_down.astype(BF16),
        b_down=b_down[:, None, :])


def _overlap(n_half_pad, n_sel, n_sel_pad):
    c_start = jnp.arange(n_half_pad) * CMP_STRIDE
    s_start = jnp.arange(n_sel_pad) * SEL_BLOCK
    ov = (c_start[:, None] < s_start[None, :] + SEL_BLOCK) & (c_start[:, None] + CMP_LEN > s_start[None, :])
    return (ov & (jnp.arange(n_sel_pad) < n_sel)[None, :]).astype(F32)


def _moe_dispatch(top_i, top_w, tm):
    t = top_i.shape[0]
    a = t * TOP_K
    n_tiles = -(-a // tm) + N_EXPERTS
    ns = n_tiles * tm
    e_flat = top_i[:, :TOP_K].reshape(a)
    w_flat = top_w[:, :TOP_K].reshape(a)
    onehot = (e_flat[:, None] == jnp.arange(N_EXPERTS, dtype=I32)[None, :]).astype(I32)
    csum = jnp.cumsum(onehot, axis=0)
    counts = csum[-1]
    padded = ((counts + tm - 1) // tm) * tm
    pend = jnp.cumsum(padded)
    pstart = pend - padded
    start = jnp.cumsum(counts) - counts
    slot_of_assign = jnp.sum(onehot * (pstart[None, :] + csum - 1), axis=1).reshape(t, TOP_K)
    _, order, w_sorted = lax.sort((e_flat, jnp.arange(a, dtype=I32), w_flat), num_keys=1, is_stable=True)
    tile_start = jnp.arange(n_tiles, dtype=I32) * tm
    tile_expert = jnp.minimum(jnp.sum((pend[None, :] <= tile_start[:, None]).astype(I32), axis=1), N_EXPERTS - 1)
    tile_valid = (tile_start < pend[-1]).astype(I32)
    e_hot = (tile_expert[:, None] == jnp.arange(N_EXPERTS, dtype=I32)[None, :]).astype(I32)
    lane = jnp.arange(tm, dtype=I32)[None, :]
    rank = tile_start[:, None] - jnp.sum(e_hot * pstart[None, :], axis=1, keepdims=True) + lane
    valid = (rank < jnp.sum(e_hot * counts[None, :], axis=1, keepdims=True)) & (tile_valid[:, None] > 0)
    src = jnp.clip(jnp.sum(e_hot * start[None, :], axis=1, keepdims=True) + rank, 0, a - 1).reshape(ns)
    valid = valid.reshape(ns)
    tok_of_slot = jnp.where(valid, order[src] // TOP_K, 0)
    w_of_slot = jnp.where(valid, w_sorted[src], 0.0)
    return tok_of_slot, w_of_slot[:, None], slot_of_assign, tile_expert, tile_valid


def kernel(x_prompt, x_sample, c_prompt, c_sample, cache_mla, cache_nsa_cmp, cache_nsa_slc, state_nsa_swa, page_table, rel_bias, w_ada, b_ada, g_norm1, g_norm2, w_in, g_q_a, w_q_b, g_kv_a, w_kv_b, cmp_pe, cmp_w1, cmp_w2, w_o, w_router, b_router, w_gate_up, b_gate_up, w_down, b_down, g_final):
    depth = w_in.shape[0]
    assert depth == 1, "single-layer decoder step"
    nb, seq, d = x_prompt.shape
    nbd, nq, _ = x_sample.shape
    n_pages = page_table.shape[1]
    page = cache_mla.shape[2]
    past = n_pages * page
    buf = state_nsa_swa.shape[2]
    assert page == LANE and nq <= 8 and seq % 512 == 0 and past % SEL_BLOCK == 0 and seq >= WINDOW
    tp, ts = nb * seq, nbd * nq
    W = _pack_weights(w_in[0], g_norm1[0], g_norm2[0], g_q_a[0], w_q_b[0], g_kv_a[0], w_kv_b[0], cmp_pe[0], cmp_w1[0],
                      cmp_w2[0], w_o[0], w_router[0], b_router[0], w_gate_up[0], b_gate_up[0], w_down[0], b_down[0])

    n_c = nb + nbd
    n_c_pad = -(-n_c // 8) * 8
    c_all = jnp.pad(jnp.concatenate([c_prompt, c_sample], axis=0), ((0, n_c_pad - n_c), (0, 0)))
    mod = _ada_mod(c_all, w_ada[0].astype(BF16), b_ada[0][None])
    mod_p = [m[:, None, :] for m in jnp.split(mod[:nb], 6, axis=-1)]
    tm_s = min(256, ts)
    mod_s = [jnp.repeat(m, nq, axis=0).reshape(ts // tm_s, tm_s, d) for m in jnp.split(mod[nb:n_c], 6, axis=-1)]

    tm_p = 256
    pos_p = jnp.arange(seq)
    pos_s = past + jnp.arange(nq)
    P = _proj(x_prompt.reshape(tp, d), mod_p[1], mod_p[0], _rope_table(pos_p), W, rows_per_mod=1,
              cs_period_tiles=seq // tm_p, tm=tm_p)
    cs_s = jnp.tile(_rope_table(pos_s), (tm_s // nq, 1))
    S_ = _proj(x_sample.reshape(ts, d), mod_s[1], mod_s[0], cs_s, W, rows_per_mod=tm_s, cs_period_tiles=1, tm=tm_s)
    p_qmla, p_kq, p_mla, p_qn, p_cmp, p_slc, p_swa, p_slcb, p_swab, p_gate = P
    s_qmla, s_kq, s_mla, s_qn, s_cmp, s_slc, s_swa, s_slcb, s_swab, s_gate = S_

    def pad_new(a):
        return jnp.pad(a.reshape(nbd, nq, a.shape[1]), ((0, 0), (0, LANE - nq), (0, 0)))

    o_mla_p = _mla_prompt(p_qmla, p_kq, W["wv"], nb=nb, seq=seq)
    g_pages = min(16, n_pages)
    cache_mla_t = jnp.swapaxes(cache_mla[0], 1, 2)
    cache_cmp_t = jnp.moveaxis(cache_nsa_cmp[0], 1, -1).reshape(-1, KV_COLS, page)
    cache_slc_t = jnp.moveaxis(cache_nsa_slc[0], 1, -1).reshape(-1, KV_COLS, page)
    state_swa_t = jnp.moveaxis(state_nsa_swa[0], 1, -1).reshape(nbd, KV_COLS, buf)
    o_mla_s = _mla_sample(page_table, s_qmla.reshape(nbd, nq, -1), pad_new(s_kq), W["wv"], cache_mla_t, g=g_pages)

    assert (past + nq) // CMP_STRIDE == past // CMP_STRIDE
    n_pp = tp // LANE
    g_cp = min(32, n_pp)
    ab_p = _cmp_first(jnp.arange(n_pp, dtype=I32), p_cmp.reshape(n_pp, LANE, KV_COLS), W["w_pair"], g=g_cp,
                      transposed=False)
    kc_p, vc_p = _cmp_finish(ab_p, W["pe8"], W["w1r"], W["w2pad"], nb=nb)
    g_cs = min(32, nbd * n_pages)
    ab_s = _cmp_first(page_table.reshape(nbd * n_pages), cache_cmp_t, W["w_pair"], g=g_cs, transposed=True)
    kc_s, vc_s = _cmp_finish(ab_s, W["pe8"], W["w1r"], W["w2pad"], nb=nbd)

    nh_p = seq // CMP_STRIDE
    n_sel_p = -(-seq // SEL_BLOCK)
    tq = 128
    far_h = _bias_rows(rel_bias, jnp.full((1, 1), MAX_DISTANCE))
    d_win = (jnp.arange(tq)[:, None] - CMP_STRIDE * (jnp.arange(LANE)[None, :] - CMP_WIN_LO) - (CMP_LEN - 1))
    assert CMP_WIN_LO * CMP_STRIDE >= MAX_DISTANCE + CMP_LEN and LANE - CMP_WIN_LO >= tq // CMP_STRIDE
    bias_c_p = jnp.where((d_win >= 0)[None], _bias_rows(rel_bias, d_win) - far_h, 0.0)
    bias_c_p = bias_c_p.reshape(NSA_KV_HEADS, NSA_GROUP * tq, LANE)
    di = jnp.arange(tq)[:, None] - jnp.arange(tq)[None, :]
    tt = jnp.stack([_bias_rows(rel_bias, di), _bias_rows(rel_bias, di + tq),
                    _bias_rows(rel_bias, jnp.full((tq, tq), MAX_DISTANCE))], axis=1)
    tt = tt.reshape(NSA_KV_HEADS, NSA_GROUP, 3, tq, tq).transpose(0, 2, 1, 3, 4).reshape(NSA_KV_HEADS, 3, NSA_GROUP * tq, tq)
    assert n_sel_p <= LANE
    ovl_p = _overlap(nh_p, n_sel_p, LANE).T
    o_nsa_p = _nsa_prompt(p_qn, kc_p, vc_p, bias_c_p, ovl_p, p_slcb.reshape(nb, seq, KV_COLS),
                          p_swab.reshape(nb, seq, KV_COLS), tt, p_gate, nb=nb, seq=seq, n_cmp=nh_p - 1, n_sel=n_sel_p)

    nh_s = past // CMP_STRIDE
    n_sel_s = -(-(past + nq) // SEL_BLOCK)
    assert n_sel_s == past // SEL_BLOCK + 1
    n_sel_pad = -(-n_sel_s // LANE) * LANE
    rows_s = NSA_HEADS * nq
    cmp_end_s = jnp.arange(nh_s) * CMP_STRIDE + CMP_LEN - 1
    bias_c_s = _bias_rows(rel_bias, pos_s[:, None] - cmp_end_s[None, :]).reshape(rows_s, nh_s)
    nk_step = g_pages * page
    far = jnp.broadcast_to(_bias_rows(rel_bias, jnp.full((nq, 1), MAX_DISTANCE)).reshape(rows_s, 1), (rows_s, nk_step))
    tail_pos = past - nk_step + jnp.arange(nk_step)
    bias_s = jnp.stack([far, _bias_rows(rel_bias, pos_s[:, None] - tail_pos[None, :]).reshape(rows_s, nk_step)])
    new_pos = past + jnp.arange(LANE)
    bias_new = _bias_rows(rel_bias, pos_s[:, None] - new_pos[None, :]).reshape(rows_s, LANE)
    swa_pos = past - buf + jnp.arange(buf)
    bias_w = _bias_rows(rel_bias, pos_s[:, None] - swa_pos[None, :]).reshape(rows_s, buf)
    ovl_s = _overlap(nh_s, n_sel_s, n_sel_pad)
    o_nsa_s = _nsa_sample(page_table, s_qn.reshape(nbd, nq, -1), kc_s, vc_s, bias_c_s, ovl_s, bias_s, pad_new(s_slcb),
                          bias_new, state_swa_t, pad_new(s_swab), bias_w, bias_new,
                          s_gate.reshape(nbd, nq, -1), cache_slc_t, g=g_pages, n_cmp=nh_s - 1, n_sel=n_sel_s)

    x1_p, h2_p, ti_p, tw_p = _outproj(x_prompt.reshape(tp, d), o_mla_p, o_nsa_p, mod_p[2], mod_p[4], mod_p[3], W,
                                      rows_per_mod=1, tm=tm_p)
    x1_s, h2_s, ti_s, tw_s = _outproj(x_sample.reshape(ts, d), o_mla_s.reshape(ts, -1), o_nsa_s.reshape(ts, -1),
                                      mod_s[2], mod_s[4], mod_s[3], W, rows_per_mod=tm_s, tm=tm_s)

    tm_e = 256
    h2 = jnp.concatenate([h2_p, h2_s], axis=0)
    tok_of_slot, w_slot, slot_of_assign, tile_expert, tile_valid = _moe_dispatch(
        jnp.concatenate([ti_p, ti_s], axis=0), jnp.concatenate([tw_p, tw_s], axis=0), tm_e)
    y_sorted = _experts(tile_expert, tile_valid, h2[tok_of_slot], w_slot, W, tm=tm_e)
    moe = jnp.sum(y_sorted[slot_of_assign], axis=1)

    y_p = _final(x1_p, moe[:tp], mod_p[5], g_final[None], rows_per_mod=1, tm=tm_p)
    y_s = _final(x1_s, moe[tp:], mod_s[5], g_final[None], rows_per_mod=tm_s, tm=tm_s)

    kv_tail = (2, NSA_KV_HEADS, HEAD_DIM)
    keep_p = min(WINDOW, seq)
    swa_keys = jnp.concatenate([state_nsa_swa[0], s_swa.reshape((nbd, nq) + kv_tail)], axis=1)
    keep_s = min(WINDOW, buf + nq)
    return (y_p.reshape(nb, seq, d), y_s.reshape(nbd, nq, d),
            p_mla.reshape(1, nb, seq, -1), s_mla.reshape(1, nbd, nq, -1),
            p_cmp.reshape((1, nb, seq) + kv_tail), s_cmp.reshape((1, nbd, nq) + kv_tail),
            p_slc.reshape((1, nb, seq) + kv_tail), s_slc.reshape((1, nbd, nq) + kv_tail),
            p_swa.reshape((nb, seq) + kv_tail)[None, :, seq - keep_p:],
            swa_keys[None, :, buf + nq - keep_s:])
```

```python
import functools
import math

import jax
import jax.numpy as jnp
from jax import lax
from jax.experimental import pallas as pl
from jax.experimental.pallas import tpu as pltpu

F32, BF16, I32 = jnp.float32, jnp.bfloat16, jnp.int32

MLA_HEADS = 8
MLA_Q_LORA = 384
MLA_KV_LORA = 256
MLA_NOPE = 64
MLA_ROPE = 32
MLA_V = 64
NSA_HEADS = 8
NSA_KV_HEADS = 2
NSA_GROUP = NSA_HEADS // NSA_KV_HEADS
HEAD_DIM = 64
CMP_LEN = 32
CMP_STRIDE = 16
CMP_HIDDEN = 128
SEL_BLOCK = 64
SEL_TOPN = 16
WINDOW = 512
KV_COLS = 2 * NSA_KV_HEADS * HEAD_DIM
N_EXPERTS = 32
TOP_K = 4
SWIGLU_LIMIT = 7.0
SWIGLU_ALPHA = 1.702
NUM_BUCKETS = 32
MAX_DISTANCE = 128
ROPE_THETA = 10000.0
NORM_EPS = 1e-6

LANE = 128
VMEM_LIMIT = 56 * 1024 * 1024
NEG = -1e30
CMP_WIN_LO = 16

MLA_QW = MLA_KV_LORA + LANE
MLA_SCALE = (MLA_NOPE + MLA_ROPE) ** -0.5
MLA_QSCALE = MLA_SCALE * math.log2(math.e)
LOG2E = math.log2(math.e)
NSA_QSCALE = HEAD_DIM ** -0.5 * LOG2E

_C_QA = 0
_C_CKV = _C_QA + MLA_Q_LORA
_C_QN = _C_CKV + MLA_KV_LORA
_C_CMP = _C_QN + NSA_HEADS * LANE
_C_SLC = _C_CMP + KV_COLS
_C_SWA = _C_SLC + KV_COLS
_C_KR = _C_SWA + KV_COLS
_C_KRR = _C_KR + LANE
_C_G = _C_KRR + LANE
_C_END = _C_G + NSA_KV_HEADS * LANE


def _cparams(sem, vmem=VMEM_LIMIT):
    return pltpu.CompilerParams(dimension_semantics=sem, vmem_limit_bytes=vmem)


def _dot(a, b):
    return jnp.dot(a, b, preferred_element_type=F32)


def _dot_nt(a, b):
    return lax.dot_general(a, b, (((1,), (1,)), ((), ())), preferred_element_type=F32)


def _row_reduce(x, op, reduce):
    n = x.shape[1] // LANE
    if x.shape[1] % LANE or n <= 1:
        return reduce(x, axis=-1, keepdims=True)
    t = x[:, :LANE]
    for c in range(1, n):
        t = op(t, x[:, c * LANE:(c + 1) * LANE])
    return reduce(t, axis=-1, keepdims=True)


def _rowmax(x):
    return _row_reduce(x, jnp.maximum, jnp.max)


def _rowsum(x):
    return _row_reduce(x, jnp.add, jnp.sum)


def _rms(x, g):
    return x * lax.rsqrt(jnp.mean(x * x, axis=-1, keepdims=True) + NORM_EPS) * g


def _sigmoid(x):
    return 1.0 / (1.0 + jnp.exp(-x))


def _ada_kernel(c_ref, w_ref, b_ref, o_ref):
    c = c_ref[...]
    o_ref[...] = _dot((c * _sigmoid(c)).astype(BF16), w_ref[...]) + b_ref[...]


def _ada_mod(c_all, w_ada, b_ada):
    m, d = c_all.shape
    n = w_ada.shape[1]
    tn = 1536
    return pl.pallas_call(
        _ada_kernel, grid=(n // tn,),
        in_specs=[pl.BlockSpec((m, d), lambda i: (0, 0)), pl.BlockSpec((d, tn), lambda i: (0, i)),
                  pl.BlockSpec((1, tn), lambda i: (0, i))],
        out_specs=pl.BlockSpec((m, tn), lambda i: (0, i)),
        out_shape=jax.ShapeDtypeStruct((m, n), F32),
        compiler_params=_cparams(("arbitrary",)), name="ada_mod")(c_all, w_ada, b_ada)


def _proj_kernel(x_ref, sc_ref, sh_ref, g1_ref, cs_ref, win_ref, gqa_ref, wqb_ref, gkva_ref, wkn_ref,
                 qmla_ref, kq_ref, mla_ref, qn_ref, cmp_ref, cmps_ref, slc_ref, swa_ref, slcb_ref, swab_ref, gate_ref,
                 *, states_t):
    h = _rms(x_ref[...], g1_ref[...]) * (1.0 + sc_ref[...]) + sh_ref[...]
    proj = _dot(h.astype(BF16), win_ref[...])
    cs = cs_ref[...]
    cosp, sinp = cs[:, :LANE], cs[:, LANE:]
    qa = _rms(proj[:, _C_QA:_C_CKV], gqa_ref[...])
    q = _dot(qa.astype(BF16), wqb_ref[...])
    hw = MLA_HEADS * LANE
    for h_i in range(MLA_HEADS):
        lo = h_i * LANE
        qlat = _dot(q[:, lo:lo + LANE].astype(BF16), wkn_ref[h_i]) * MLA_QSCALE
        qr = (q[:, hw + lo:hw + lo + LANE] * cosp + q[:, 2 * hw + lo:2 * hw + lo + LANE] * sinp) * MLA_QSCALE
        qmla_ref[:, h_i * MLA_QW:h_i * MLA_QW + MLA_KV_LORA] = qlat.astype(BF16)
        qmla_ref[:, h_i * MLA_QW + MLA_KV_LORA:(h_i + 1) * MLA_QW] = qr.astype(BF16)
    ckv = _rms(proj[:, _C_CKV:_C_QN], gkva_ref[...])
    kr = proj[:, _C_KR:_C_KRR] * cosp + proj[:, _C_KRR:_C_G] * sinp
    kq_ref[:, :MLA_KV_LORA] = ckv.astype(BF16)
    kq_ref[:, MLA_KV_LORA:] = kr.astype(BF16)
    qn_ref[...] = (proj[:, _C_QN:_C_CMP] * NSA_QSCALE).astype(BF16)
    cmp = proj[:, _C_CMP:_C_SLC]
    slc = proj[:, _C_SLC:_C_SWA]
    swa = proj[:, _C_SWA:_C_KR]
    cmp_ref[...] = cmp
    if states_t:
        mla_ref[:MLA_KV_LORA, :] = ckv.T
        mla_ref[MLA_KV_LORA:, :] = kr.T[:MLA_ROPE]
        cmps_ref[...] = cmp.T
        slc_ref[...] = slc.T
        swa_ref[...] = swa.T
    else:
        mla_ref[:, :MLA_KV_LORA] = ckv
        mla_ref[:, MLA_KV_LORA:] = kr[:, :MLA_ROPE]
        cmps_ref[...] = cmp
        slc_ref[...] = slc
        swa_ref[...] = swa
    slcb_ref[...] = slc.astype(BF16)
    swab_ref[...] = swa.astype(BF16)
    gate_ref[...] = _sigmoid(proj[:, _C_G:_C_END])


def _proj(x, sc, sh, cs, W, *, rows_per_mod, cs_period_tiles, tm, states_t_batches=0):
    t, d = x.shape
    nt = t // tm
    if rows_per_mod == 1:
        tiles_per_mod = sc.shape[0] and (t // sc.shape[0]) // tm
        mod_map = lambda i: (i // tiles_per_mod, 0, 0)
    else:
        mod_map = lambda i: (i, 0, 0)
    cs_map = (lambda i: (i % cs_period_tiles, 0)) if cs_period_tiles > 1 else (lambda i: (0, 0))
    const2 = lambda i: (0, 0)
    row = lambda i: (i, 0)
    outs = [
        (MLA_HEADS * MLA_QW, BF16), (MLA_QW, BF16), (MLA_KV_LORA + MLA_ROPE, F32), (NSA_HEADS * LANE, BF16),
        (KV_COLS, F32), (KV_COLS, F32), (KV_COLS, F32), (KV_COLS, F32), (KV_COLS, BF16), (KV_COLS, BF16),
        (NSA_KV_HEADS * LANE, F32),
    ]
    state_outs = (2, 5, 6, 7) if states_t_batches else ()
    tiles_per_batch = (t // states_t_batches) // tm if states_t_batches else 0
    out_specs = [pl.BlockSpec((None, w, tm), lambda i: (i // tiles_per_batch, 0, i % tiles_per_batch))
                 if k in state_outs else pl.BlockSpec((tm, w), row) for k, (w, _) in enumerate(outs)]
    out_shape = [jax.ShapeDtypeStruct((states_t_batches, w, t // states_t_batches) if k in state_outs else (t, w), dt)
                 for k, (w, dt) in enumerate(outs)]
    return pl.pallas_call(
        functools.partial(_proj_kernel, states_t=bool(states_t_batches)), grid=(nt,),
        in_specs=[
            pl.BlockSpec((tm, d), row),
            pl.BlockSpec((None, rows_per_mod, d), mod_map),
            pl.BlockSpec((None, rows_per_mod, d), mod_map),
            pl.BlockSpec((1, d), const2),
            pl.BlockSpec((tm, 2 * LANE), cs_map),
            pl.BlockSpec(W["w_in"].shape, const2),
            pl.BlockSpec((1, MLA_Q_LORA), const2),
            pl.BlockSpec(W["w_qb"].shape, const2),
            pl.BlockSpec((1, MLA_KV_LORA), const2),
            pl.BlockSpec(W["w_kn"].shape, lambda i: (0, 0, 0)),
        ],
        out_specs=out_specs, out_shape=out_shape,
        compiler_params=_cparams(("arbitrary",)), name="proj_in",
    )(x, sc, sh, W["g_norm1"], cs, W["w_in"], W["g_q_a"], W["w_qb"], W["g_kv_a"], W["w_kn"])


def _mla_prompt_kernel(q_ref, k_ref, wv_ref, o_ref, m_ref, l_ref, acc_ref, *, tq, tk):
    i, j = pl.program_id(1), pl.program_id(2)
    nk = pl.num_programs(2)

    @pl.when(j == 0)
    def _():
        m_ref[...] = jnp.full_like(m_ref, NEG)
        l_ref[...] = jnp.zeros_like(l_ref)
        acc_ref[...] = jnp.zeros_like(acc_ref)

    def step(masked):
        k = k_ref[...]
        v = k[:, :MLA_KV_LORA]
        if masked:
            qpos = i * tq + lax.broadcasted_iota(I32, (tq, tk), 0)
            kpos = j * tk + lax.broadcasted_iota(I32, (tq, tk), 1)
            mask = kpos <= qpos
        ss = [_dot_nt(q_ref[:, h * MLA_QW:(h + 1) * MLA_QW], k) for h in range(MLA_HEADS)]
        ps, alphas = [], []
        for h in range(MLA_HEADS):
            r = slice(h * tq, (h + 1) * tq)
            s = jnp.where(mask, ss[h], NEG) if masked else ss[h]
            m_prev = m_ref[r]
            m_new = jnp.maximum(m_prev, _rowmax(s))
            p = jnp.exp2(s - m_new)
            if masked:
                p = jnp.where(mask, p, 0.0)
            alpha = jnp.exp2(m_prev - m_new)
            l_ref[r] = alpha * l_ref[r] + _rowsum(p)
            m_ref[r] = m_new
            ps.append(p.astype(BF16))
            alphas.append(alpha)
        for h in range(MLA_HEADS):
            r = slice(h * tq, (h + 1) * tq)
            acc_ref[r] = alphas[h] * acc_ref[r] + _dot(ps[h], v)

    first_key, last_key = j * tk, j * tk + tk - 1
    pl.when(last_key <= i * tq)(functools.partial(step, False))
    pl.when((first_key <= i * tq + tq - 1) & (last_key > i * tq))(functools.partial(step, True))

    @pl.when(j == nk - 1)
    def _():
        out = jnp.zeros(o_ref.shape, F32)
        for h in range(MLA_HEADS):
            r = slice(h * tq, (h + 1) * tq)
            l = l_ref[r]
            o_lat = acc_ref[r] / jnp.where(l > 0.0, l, 1.0)
            out = out + _dot(o_lat.astype(BF16), wv_ref[h])
        o_ref[...] = out.astype(o_ref.dtype)


def _mla_prompt(qmla, kq, wv, *, nb, seq, tq=256, tk=512):
    tk = min(tk, seq)
    nq, nk = seq // tq, seq // tk
    ow = MLA_HEADS * MLA_V

    def k_map(b, i, j):
        return (b * nk + jnp.minimum(j, (i * tq + tq - 1) // tk), 0)

    return pl.pallas_call(
        functools.partial(_mla_prompt_kernel, tq=tq, tk=tk), grid=(nb, nq, nk),
        in_specs=[pl.BlockSpec((tq, MLA_HEADS * MLA_QW), lambda b, i, j: (b * nq + i, 0)),
                  pl.BlockSpec((tk, MLA_QW), k_map),
                  pl.BlockSpec(wv.shape, lambda b, i, j: (0, 0, 0))],
        out_specs=pl.BlockSpec((tq, ow), lambda b, i, j: (b * nq + i, 0)),
        out_shape=jax.ShapeDtypeStruct((nb * seq, ow), BF16),
        scratch_shapes=[pltpu.VMEM((MLA_HEADS * tq, 1), F32), pltpu.VMEM((MLA_HEADS * tq, 1), F32),
                        pltpu.VMEM((MLA_HEADS * tq, MLA_KV_LORA), F32)],
        compiler_params=_cparams(("arbitrary", "arbitrary", "arbitrary")), name="mla_prompt",
    )(qmla, kq, wv)


def _mla_sample_kernel(pt_ref, q_ref, kn_ref, wv_ref, *rest, g, nq, nbs):
    pages = rest[:nbs * g]
    o_ref, m_ref, l_ref, acc_ref = rest[nbs * g:]
    s_idx = pl.program_id(1)
    rows = MLA_HEADS * nq

    @pl.when(s_idx == 0)
    def _():
        m_ref[...] = jnp.full_like(m_ref, NEG)
        l_ref[...] = jnp.zeros_like(l_ref)
        acc_ref[...] = jnp.zeros_like(acc_ref)

    qs = []
    for bl in range(nbs):
        r = slice(bl * rows, (bl + 1) * rows)
        q = jnp.concatenate([q_ref[bl, :, h * MLA_QW:(h + 1) * MLA_QW] for h in range(MLA_HEADS)], axis=0)
        qs.append(q)
        qlat, qr = q[:, :MLA_KV_LORA], q[:, MLA_KV_LORA:MLA_KV_LORA + MLA_ROPE]
        pg = pages[bl * g:(bl + 1) * g]
        ckv_t = jnp.concatenate([pg[u][:MLA_KV_LORA].astype(BF16) for u in range(g)], axis=1)
        kr_t = jnp.concatenate([pg[u][MLA_KV_LORA:].astype(BF16) for u in range(g)], axis=1)
        s = _dot(qlat, ckv_t) + _dot(qr, kr_t)
        m_prev = m_ref[r]
        m_new = jnp.maximum(m_prev, _rowmax(s))
        p = jnp.exp2(s - m_new)
        alpha = jnp.exp2(m_prev - m_new)
        l_ref[r] = alpha * l_ref[r] + _rowsum(p)
        acc_ref[r] = alpha * acc_ref[r] + _dot_nt(p.astype(BF16), ckv_t)
        m_ref[r] = m_new

    @pl.when(s_idx == pl.num_programs(1) - 1)
    def _():
        for bl in range(nbs):
            r = slice(bl * rows, (bl + 1) * rows)
            kn = kn_ref[bl]
            sn = _dot_nt(qs[bl], kn)
            qi = lax.broadcasted_iota(I32, sn.shape, 0) % nq
            kt = lax.broadcasted_iota(I32, sn.shape, 1)
            mask = kt <= qi
            sn = jnp.where(mask, sn, NEG)
            m_prev = m_ref[r]
            m_new = jnp.maximum(m_prev, jnp.max(sn, axis=-1, keepdims=True))
            p = jnp.where(mask, jnp.exp2(sn - m_new), 0.0)
            alpha = jnp.exp2(m_prev - m_new)
            l = alpha * l_ref[r] + _rowsum(p)
            acc = alpha * acc_ref[r] + _dot(p.astype(BF16), kn[:, :MLA_KV_LORA])
            o_lat = (acc / jnp.where(l > 0.0, l, 1.0)).astype(BF16)
            res = _dot(o_lat, wv_ref[...])
            head_of_lane = lax.broadcasted_iota(I32, (nq, res.shape[1]), 1) // MLA_V
            out = jnp.zeros((nq, res.shape[1]), F32)
            for h in range(MLA_HEADS):
                out = out + jnp.where(head_of_lane == h, res[h * nq:(h + 1) * nq], 0.0)
            o_ref[bl] = out.astype(o_ref.dtype)


def _page_specs(g, block, n_lane_blocks=1, flat=False, nbs=1):
    def index(b, s, pt, bl, u, c):
        return ((pt[s * g + u] if flat else pt[b * nbs + bl, s * g + u]), 0, c)

    return [pl.BlockSpec(block, functools.partial(index, bl=bl, u=u, c=c))
            for bl in range(nbs) for u in range(g) for c in range(n_lane_blocks)]


def _mla_sample(page_table, qmla, kq_new_pad, wv, cache_mla, *, g, nbs):
    nb, nq = qmla.shape[0], qmla.shape[1]
    n_pages = page_table.shape[1]
    ow = MLA_HEADS * MLA_V
    rows = nbs * MLA_HEADS * nq
    bmap = lambda b, s, pt: (b, 0, 0)
    gs = pltpu.PrefetchScalarGridSpec(
        num_scalar_prefetch=1, grid=(nb // nbs, n_pages // g),
        in_specs=[pl.BlockSpec((nbs, nq, MLA_HEADS * MLA_QW), bmap),
                  pl.BlockSpec((nbs, LANE, MLA_QW), bmap),
                  pl.BlockSpec(wv.shape, lambda b, s, pt: (0, 0))]
        + _page_specs(g, (None,) + cache_mla.shape[1:], nbs=nbs),
        out_specs=pl.BlockSpec((nbs, nq, ow), bmap),
        scratch_shapes=[pltpu.VMEM((rows, 1), F32), pltpu.VMEM((rows, 1), F32), pltpu.VMEM((rows, MLA_KV_LORA), F32)],
    )
    return pl.pallas_call(
        functools.partial(_mla_sample_kernel, g=g, nq=nq, nbs=nbs), grid_spec=gs,
        out_shape=jax.ShapeDtypeStruct((nb, nq, ow), BF16),
        compiler_params=_cparams(("arbitrary", "arbitrary")), name="mla_sample",
    )(page_table, qmla, kq_new_pad, wv, *([cache_mla] * (nbs * g)))


def _cmp_first_kernel(pl_ref, w_ref, *rest, g, transposed):
    pages = rest[:g]
    o_ref, xs_ref = rest[g:]
    half_w = NSA_KV_HEADS * 2 * CMP_HIDDEN
    n_half = o_ref.shape[0]
    for u in range(g):
        x = pages[u][...]
        for j in range(2):
            if transposed:
                xs_ref[j, u * LANE:(u + 1) * LANE, :] = x[j * LANE:(j + 1) * LANE, :].T
            else:
                xs_ref[j, u * LANE:(u + 1) * LANE, :] = x[:, j * LANE:(j + 1) * LANE]
    for j in range(2):
        acc = jnp.zeros((n_half, half_w), F32)
        for pp in range(CMP_STRIDE // 2):
            xa = xs_ref[j, pl.ds(2 * pp, n_half, stride=CMP_STRIDE), :]
            xb = xs_ref[j, pl.ds(2 * pp + 1, n_half, stride=CMP_STRIDE), :]
            acc = acc + _dot(jnp.concatenate([xa, xb], axis=1).astype(BF16), w_ref[j, pp])
        o_ref[:, j * half_w:(j + 1) * half_w] = acc


def _cmp_first(page_list, pool, w_pair, *, g, transposed):
    n = page_list.shape[0]
    halves = LANE // CMP_STRIDE
    ow = 2 * NSA_KV_HEADS * 2 * CMP_HIDDEN
    gs = pltpu.PrefetchScalarGridSpec(
        num_scalar_prefetch=1, grid=(1, n // g),
        in_specs=[pl.BlockSpec(w_pair.shape, lambda b, s, pt: (0, 0, 0, 0))]
        + _page_specs(g, (None,) + pool.shape[1:], flat=True),
        out_specs=pl.BlockSpec((g * halves, ow), lambda b, s, pt: (s, 0)),
        scratch_shapes=[pltpu.VMEM((2, g * LANE, LANE), F32)],
    )
    return pl.pallas_call(
        functools.partial(_cmp_first_kernel, g=g, transposed=transposed), grid_spec=gs,
        out_shape=jax.ShapeDtypeStruct((n * halves, ow), F32),
        compiler_params=_cparams(("arbitrary", "arbitrary")), name="cmp_first",
    )(page_list, w_pair, *([pool] * g))


def _cmp_finish_kernel(ab_ref, pe_ref, w1_ref, w2_ref, kc_ref, vc_ref):
    nh = ab_ref.shape[0]
    outs = []
    for j in range(2):
        pe_term = _dot(pe_ref[j].astype(BF16), w1_ref[j])[0:1]
        acc = jnp.zeros((nh, LANE), F32)
        for kv in range(NSA_KV_HEADS):
            base = (j * NSA_KV_HEADS + kv) * 2 * CMP_HIDDEN
            first = ab_ref[:, base:base + CMP_HIDDEN]
            second = ab_ref[:, base + CMP_HIDDEN:base + 2 * CMP_HIDDEN]
            hid = first + pltpu.roll(second, nh - 1, 0) + pe_term
            hid = 0.5 * hid * (1.0 + lax.erf(hid * math.sqrt(0.5)))
            acc = acc + _dot(hid.astype(BF16), w2_ref[j, kv])
        outs.append(acc)
    kc_ref[...] = outs[0].astype(kc_ref.dtype)
    vc_ref[...] = outs[1].astype(vc_ref.dtype)


def _cmp_finish(ab, pe8, w1r, w2pad, *, nb):
    nh = ab.shape[0] // nb
    c3 = lambda b: (0, 0, 0)
    return pl.pallas_call(
        _cmp_finish_kernel, grid=(nb,),
        in_specs=[pl.BlockSpec((nh, ab.shape[1]), lambda b: (b, 0)), pl.BlockSpec(pe8.shape, c3),
                  pl.BlockSpec(w1r.shape, c3), pl.BlockSpec(w2pad.shape, lambda b: (0, 0, 0, 0))],
        out_specs=[pl.BlockSpec((None, nh, LANE), lambda b: (b, 0, 0))] * 2,
        out_shape=[jax.ShapeDtypeStruct((nb, nh, LANE), BF16)] * 2,
        compiler_params=_cparams(("arbitrary",)), name="cmp_finish")(ab, pe8, w1r, w2pad)


def _topk_mask(v, k, axis=1):
    lane = lax.broadcasted_iota(I32, v.shape, axis)
    sel = jnp.zeros(v.shape, F32)
    for _ in range(k):
        m = jnp.max(v, axis=axis, keepdims=True)
        idx = jnp.min(jnp.where(v == m, lane, 1 << 20), axis=axis, keepdims=True)
        pick = lane == idx
        sel = jnp.where(pick & (m > -jnp.inf), 1.0, sel)
        v = jnp.where(pick, -jnp.inf, v)
    return sel


def _softmax_masked(s, mask):
    s = jnp.where(mask, s, NEG)
    m = _rowmax(s)
    p = jnp.where(mask, jnp.exp2(s - m), 0.0)
    l = _rowsum(p)
    return p / jnp.where(l > 0.0, l, 1.0)


def _flash_step(s, mask, v, m_ref, l_ref, acc_ref, v_keys_on_lanes=False):
    s = jnp.where(mask, s, NEG)
    m_prev = m_ref[...]
    m_new = jnp.maximum(m_prev, _rowmax(s))
    p = jnp.where(mask, jnp.exp2(s - m_new), 0.0)
    alpha = jnp.exp2(m_prev - m_new)
    l_ref[...] = alpha * l_ref[...] + _rowsum(p)
    pv = _dot_nt(p.astype(BF16), v) if v_keys_on_lanes else _dot(p.astype(BF16), v)
    acc_ref[...] = alpha * acc_ref[...] + pv
    m_ref[...] = m_new


def _flash_init(m_ref, l_ref, acc_ref):
    m_ref[...] = jnp.full_like(m_ref, NEG)
    l_ref[...] = jnp.zeros_like(l_ref)
    acc_ref[...] = jnp.zeros_like(acc_ref)


def _flash_out(l_ref, acc_ref):
    l = l_ref[...]
    return acc_ref[...] / jnp.where(l > 0.0, l, 1.0)


def _nsa_prompt_kernel(q_ref, kc_ref, vc_ref, bc_ref, ovl_ref, slc_ref, swa_ref, tt_ref, gate_ref, o_ref,
                       m_ref, l_ref, acc_ref, *, tq, tkf, n_cmp, n_sel, seq_len):
    i = pl.program_id(1)
    rows = NSA_HEADS * tq
    n_sel_rows = -(-n_sel // 8) * 8
    q = jnp.concatenate([q_ref[:, h * LANE:(h + 1) * LANE] for h in range(NSA_HEADS)], axis=0)
    nhp = kc_ref.shape[0]
    pos_r = i * tq + lax.broadcasted_iota(I32, (rows, 1), 0) % tq

    wu = lax.broadcasted_iota(I32, (LANE, nhp), 0)
    wn = lax.broadcasted_iota(I32, (LANE, nhp), 1)
    shift = jnp.where(wn == i * (tq // CMP_STRIDE) - CMP_WIN_LO + wu, 1.0, 0.0)
    far_col = tt_ref[2][:, :1]
    bias_c = jnp.dot(bc_ref[...], shift, precision=lax.Precision.HIGHEST, preferred_element_type=F32) + far_col
    s = _dot_nt(q, kc_ref[...]) + bias_c
    n_idx = lax.broadcasted_iota(I32, (rows, nhp), 1)
    mask_c = (n_idx * CMP_STRIDE + CMP_LEN - 1 <= pos_r) & (n_idx < n_cmp)
    p_cmp = _softmax_masked(s, mask_c)
    o_cmp = _dot(p_cmp.astype(BF16), vc_ref[...])

    imps = []
    for kv in range(NSA_KV_HEADS):
        lo = kv * NSA_GROUP * tq
        psum = p_cmp[lo:lo + tq]
        for g in range(1, NSA_GROUP):
            psum = psum + p_cmp[lo + g * tq:lo + (g + 1) * tq]
        imps.append(lax.dot_general(ovl_ref[...], psum, (((1,), (1,)), ((), ())), precision=lax.Precision.HIGHEST,
                                    preferred_element_type=F32)[:n_sel_rows])
    imp_t = jnp.concatenate(imps, axis=1)
    blk = lax.broadcasted_iota(I32, imp_t.shape, 0)
    cur = (i * tq + lax.broadcasted_iota(I32, imp_t.shape, 1) % tq) // SEL_BLOCK
    forced = (blk == 0) | (blk == cur) | (blk == cur - 1)
    future = (blk > cur) | (blk >= n_sel)
    imp_t = jnp.where(future, -jnp.inf, jnp.where(forced, jnp.inf, imp_t))
    sel_t = _topk_mask(imp_t, min(SEL_TOPN, n_sel), axis=0)
    if n_sel_rows < LANE:
        sel_t = jnp.concatenate([sel_t, jnp.zeros((LANE - n_sel_rows, NSA_KV_HEADS * tq), F32)], axis=0)
    sels = [sel_t[:, kv * tq:(kv + 1) * tq].T.astype(BF16) for kv in range(NSA_KV_HEADS)]

    near0 = pl.multiple_of(jnp.maximum(i - 1, 0) * tq, tq)
    near_bias = jnp.concatenate([tt_ref[jnp.where(i == 0, 0, 1)], tt_ref[0]], axis=1)
    pos_q = i * tq + lax.broadcasted_iota(I32, (tq, 1), 0)
    d_near = pos_q - (near0 + lax.broadcasted_iota(I32, (tq, 2 * tq), 1))

    def heads(pens):
        return jnp.concatenate([p_ for p_ in pens for _ in range(NSA_GROUP)], axis=0)

    def block_sel(first_key, n_keys, key_limit):
        sb = lax.broadcasted_iota(I32, (LANE, n_keys), 0)
        kt = lax.broadcasted_iota(I32, (LANE, n_keys), 1)
        hit = (sb == first_key // SEL_BLOCK + kt // SEL_BLOCK) & (first_key + kt < key_limit)
        expand = jnp.where(hit, 1.0, 0.0).astype(BF16)
        return [_dot(sel, expand) > 0.5 for sel in sels]

    n_far_w = WINDOW - tq
    far0 = pl.multiple_of(jnp.maximum(i - WINDOW // tq, 0) * tq, tq)
    kv_n = swa_ref[pl.ds(near0, 2 * tq), :]
    kv_f = swa_ref[pl.ds(far0, n_far_w), :]
    kpos_f = far0 + lax.broadcasted_iota(I32, (tq, n_far_w), 1)
    pen_n = jnp.where((d_near >= 0) & (d_near < WINDOW), 0.0, NEG)
    pen_f = jnp.where((kpos_f < near0) & (pos_q - kpos_f < WINDOW), 0.0, NEG)
    s_n = _dot_nt(q, kv_n[:, :LANE]) + near_bias + heads([pen_n] * NSA_KV_HEADS)
    s_f = _dot_nt(q, kv_f[:, :LANE]) + far_col + heads([pen_f] * NSA_KV_HEADS)
    m_w = jnp.maximum(_rowmax(s_n), _rowmax(s_f))
    p_n = jnp.exp2(s_n - m_w)
    p_f = jnp.exp2(s_f - m_w)
    l_w = _rowsum(p_n) + _rowsum(p_f)
    o_swa = (_dot(p_n.astype(BF16), kv_n[:, LANE:]) + _dot(p_f.astype(BF16), kv_f[:, LANE:])) / l_w

    kv_n = slc_ref[pl.ds(near0, 2 * tq), :]
    pens = [jnp.where(hit & (d_near >= 0), 0.0, NEG) for hit in block_sel(near0, 2 * tq, seq_len)]
    s_n = _dot_nt(q, kv_n[:, :LANE]) + near_bias + heads(pens)
    m_s = _rowmax(s_n)
    p_n = jnp.exp2(s_n - m_s)
    m_ref[...] = m_s
    l_ref[...] = _rowsum(p_n)
    acc_ref[...] = _dot(p_n.astype(BF16), kv_n[:, LANE:])

    def slc_far(c, carry):
        first = pl.multiple_of(c * tkf, tkf)
        kv = slc_ref[pl.ds(first, tkf), :]
        hits = block_sel(first, tkf, near0)
        half = NSA_GROUP * tq
        for kvh in range(NSA_KV_HEADS):
            r = slice(kvh * half, (kvh + 1) * half)
            pen = jnp.where(hits[kvh], 0.0, NEG)
            s_ = _dot_nt(q[r], kv[:, :LANE]) + far_col[r] + jnp.concatenate([pen] * NSA_GROUP, axis=0)
            m_prev = m_ref[r]
            m_new = jnp.maximum(m_prev, _rowmax(s_))
            p = jnp.exp2(s_ - m_new)
            alpha = jnp.exp2(m_prev - m_new)
            l_ref[r] = alpha * l_ref[r] + _rowsum(p)
            acc_ref[r] = alpha * acc_ref[r] + _dot(p.astype(BF16), kv[:, LANE:])
            m_ref[r] = m_new
        return carry

    lax.fori_loop(0, (near0 + tkf - 1) // tkf, slc_far, 0)
    o_slc = acc_ref[...] / l_ref[...]

    gates = gate_ref[...]
    for h in range(NSA_HEADS):
        kv, g = divmod(h, NSA_GROUP)
        r = slice(h * tq, (h + 1) * tq)
        c = kv * LANE + g
        o = (gates[:, c:c + 1] * o_cmp[r] + gates[:, c + NSA_GROUP:c + NSA_GROUP + 1] * o_slc[r]
             + gates[:, c + 2 * NSA_GROUP:c + 2 * NSA_GROUP + 1] * o_swa[r])
        o_ref[:, h * LANE:(h + 1) * LANE] = o.astype(o_ref.dtype)


def _nsa_prompt(qn, kc, vc, bias_c, ovl, slc_b, swa_b, tt, gates, *, nb, seq, n_cmp, n_sel):
    tq = 128
    nq = seq // tq
    rows = NSA_HEADS * tq
    hw = NSA_HEADS * LANE
    nhp = kc.shape[1]
    return pl.pallas_call(
        functools.partial(_nsa_prompt_kernel, tq=tq, tkf=min(512, seq), n_cmp=n_cmp, n_sel=n_sel, seq_len=seq),
        grid=(nb, nq),
        in_specs=[
            pl.BlockSpec((tq, hw), lambda b, i: (b * nq + i, 0)),
            pl.BlockSpec((None, nhp, LANE), lambda b, i: (b, 0, 0)),
            pl.BlockSpec((None, nhp, LANE), lambda b, i: (b, 0, 0)),
            pl.BlockSpec(bias_c.shape, lambda b, i: (0, 0)),
            pl.BlockSpec(ovl.shape, lambda b, i: (0, 0)),
            pl.BlockSpec((None, seq, KV_COLS), lambda b, i: (b, 0, 0)),
            pl.BlockSpec((None, seq, KV_COLS), lambda b, i: (b, 0, 0)),
            pl.BlockSpec(tt.shape, lambda b, i: (0, 0, 0)),
            pl.BlockSpec((tq, NSA_KV_HEADS * LANE), lambda b, i: (b * nq + i, 0)),
        ],
        out_specs=pl.BlockSpec((tq, hw), lambda b, i: (b * nq + i, 0)),
        out_shape=jax.ShapeDtypeStruct((nb * seq, hw), BF16),
        scratch_shapes=[pltpu.VMEM((rows, 1), F32), pltpu.VMEM((rows, 1), F32), pltpu.VMEM((rows, LANE), F32)],
        compiler_params=_cparams(("arbitrary", "arbitrary")), name="nsa_prompt",
    )(qn, kc, vc, bias_c, ovl, slc_b, swa_b, tt, gates)


def _nsa_sample_kernel(pt_ref, q_ref, kc_ref, vc_ref, bc_ref, ovl_ref, bs_ref, sn_ref, bsn_ref, st_ref, wn_ref,
                       bw_ref, bwn_ref, gate_ref, *rest, g, nq, n_cmp, n_sel, buf):
    pages = rest[:g]
    o_ref, sel_ref, ocmp_ref, oswa_ref, m_ref, l_ref, acc_ref = rest[g:]
    s_idx = pl.program_id(1)
    last = pl.num_programs(1) - 1
    rows = NSA_HEADS * nq
    q = jnp.concatenate([q_ref[:, h * LANE:(h + 1) * LANE] for h in range(NSA_HEADS)], axis=0)
    qi = lax.broadcasted_iota(I32, (rows, 1), 0) % nq

    @pl.when(s_idx == 0)
    def _():
        nhp = kc_ref.shape[0]
        s = _dot_nt(q, kc_ref[...]) + bc_ref[...]
        mask_c = lax.broadcasted_iota(I32, (rows, nhp), 1) < n_cmp
        p_cmp = _softmax_masked(s, mask_c)
        ocmp_ref[...] = _dot(p_cmp.astype(BF16), vc_ref[...])
        psums = []
        for kv in range(NSA_KV_HEADS):
            ps = p_cmp[kv * NSA_GROUP * nq:kv * NSA_GROUP * nq + nq]
            for gg in range(1, NSA_GROUP):
                lo = (kv * NSA_GROUP + gg) * nq
                ps = ps + p_cmp[lo:lo + nq]
            psums.append(ps)
        n_kq = NSA_KV_HEADS * nq
        psum = jnp.concatenate(psums + [jnp.zeros((LANE - n_kq, nhp), F32)], axis=0)
        n_blk = ovl_ref.shape[0]
        n_sel_rows = -(-n_sel // 8) * 8
        imp_t = lax.dot_general(ovl_ref[...], psum, (((1,), (1,)), ((), ())), precision=lax.Precision.HIGHEST,
                                preferred_element_type=F32)[:n_sel_rows]
        blk = lax.broadcasted_iota(I32, imp_t.shape, 0)
        cur = n_sel - 1
        forced = (blk == 0) | (blk == cur) | (blk == cur - 1)
        imp_t = jnp.where(blk >= n_sel, -jnp.inf, jnp.where(forced, jnp.inf, imp_t))
        sel_t = _topk_mask(imp_t, min(SEL_TOPN, n_sel), axis=0)
        if n_sel_rows < n_blk:
            sel_t = jnp.concatenate([sel_t, jnp.zeros((n_blk - n_sel_rows, LANE), F32)], axis=0)
        sel = sel_t.T.astype(BF16)
        sel_ref[...] = jnp.concatenate(
            [sel[kv * nq:(kv + 1) * nq] for kv in range(NSA_KV_HEADS) for _ in range(NSA_GROUP)], axis=0)

        st = st_ref[...]
        s_w = _dot(q, st[:LANE].astype(BF16)) + bw_ref[...]
        d_w = buf + qi - lax.broadcasted_iota(I32, (rows, buf), 1)
        mask_w = (d_w >= 0) & (d_w < WINDOW)
        wn = wn_ref[...]
        s_n = _dot_nt(q, wn[:, :LANE]) + bwn_ref[...]
        mask_n = lax.broadcasted_iota(I32, (rows, LANE), 1) <= qi
        s_w = jnp.where(mask_w, s_w, NEG)
        s_n = jnp.where(mask_n, s_n, NEG)
        m = jnp.maximum(_rowmax(s_w), _rowmax(s_n))
        p_w = jnp.where(mask_w, jnp.exp2(s_w - m), 0.0)
        p_n = jnp.where(mask_n, jnp.exp2(s_n - m), 0.0)
        l = _rowsum(p_w) + _rowsum(p_n)
        o = _dot_nt(p_w.astype(BF16), st[LANE:].astype(BF16)) + _dot(p_n.astype(BF16), wn[:, LANE:])
        oswa_ref[...] = o / jnp.where(l > 0.0, l, 1.0)
        _flash_init(m_ref, l_ref, acc_ref)

    page = pages[0].shape[1]
    kcat = jnp.concatenate([pages[u][:LANE].astype(BF16) for u in range(g)], axis=1)
    vcat = jnp.concatenate([pages[u][LANE:].astype(BF16) for u in range(g)], axis=1)
    nk = g * page
    s = _dot(q, kcat) + bs_ref[jnp.where(s_idx == last, 1, 0)]
    sb = lax.broadcasted_iota(I32, (sel_ref.shape[1], nk), 0)
    kt = lax.broadcasted_iota(I32, (sel_ref.shape[1], nk), 1)
    expand = jnp.where(sb == s_idx * (nk // SEL_BLOCK) + kt // SEL_BLOCK, 1.0, 0.0).astype(BF16)
    mask = _dot(sel_ref[...], expand) > 0.5
    _flash_step(s, mask, vcat, m_ref, l_ref, acc_ref, v_keys_on_lanes=True)

    @pl.when(s_idx == last)
    def _():
        sn = sn_ref[...]
        s_n = _dot_nt(q, sn[:, :LANE]) + bsn_ref[...]
        mask_n = (lax.broadcasted_iota(I32, (rows, LANE), 1) <= qi) & (sel_ref[:, n_sel - 1:n_sel] > 0.5)
        _flash_step(s_n, mask_n, sn[:, LANE:], m_ref, l_ref, acc_ref)
        o_slc = _flash_out(l_ref, acc_ref)
        o_cmp, o_swa = ocmp_ref[...], oswa_ref[...]
        gates = gate_ref[...]
        for h in range(NSA_HEADS):
            kv, gg = divmod(h, NSA_GROUP)
            r = slice(h * nq, (h + 1) * nq)
            c = kv * LANE + gg
            o = (gates[:, c:c + 1] * o_cmp[r] + gates[:, c + NSA_GROUP:c + NSA_GROUP + 1] * o_slc[r]
                 + gates[:, c + 2 * NSA_GROUP:c + 2 * NSA_GROUP + 1] * o_swa[r])
            o_ref[:, h * LANE:(h + 1) * LANE] = o.astype(o_ref.dtype)


def _nsa_sample(page_table, qn, kc, vc, bias_c, ovl, bias_s, slc_new, bias_sn, state_swa, swa_new, bias_w, bias_wn,
                gates, cache_slc, *, g, n_cmp, n_sel):
    nb, nq = qn.shape[0], qn.shape[1]
    n_pages = page_table.shape[1]
    rows = NSA_HEADS * nq
    buf = state_swa.shape[2]
    hw = NSA_HEADS * LANE
    bmap = lambda b, s, pt: (b, 0, 0)
    c2 = lambda b, s, pt: (0, 0)
    gs = pltpu.PrefetchScalarGridSpec(
        num_scalar_prefetch=1, grid=(nb, n_pages // g),
        in_specs=[
            pl.BlockSpec((None, nq, hw), bmap),
            pl.BlockSpec((None,) + kc.shape[1:], bmap),
            pl.BlockSpec((None,) + vc.shape[1:], bmap),
            pl.BlockSpec(bias_c.shape, c2),
            pl.BlockSpec(ovl.shape, c2),
            pl.BlockSpec(bias_s.shape, lambda b, s, pt: (0, 0, 0)),
            pl.BlockSpec((None,) + slc_new.shape[1:], bmap),
            pl.BlockSpec(bias_sn.shape, c2),
            pl.BlockSpec((None,) + state_swa.shape[1:], bmap),
            pl.BlockSpec((None,) + swa_new.shape[1:], bmap),
            pl.BlockSpec(bias_w.shape, c2),
            pl.BlockSpec(bias_wn.shape, c2),
            pl.BlockSpec((None, nq, NSA_KV_HEADS * LANE), bmap),
        ] + _page_specs(g, (None,) + cache_slc.shape[1:]),
        out_specs=pl.BlockSpec((None, nq, hw), bmap),
        scratch_shapes=[pltpu.VMEM((rows, ovl.shape[0]), BF16), pltpu.VMEM((rows, LANE), F32), pltpu.VMEM((rows, LANE), F32),
                        pltpu.VMEM((rows, 1), F32), pltpu.VMEM((rows, 1), F32), pltpu.VMEM((rows, LANE), F32)],
    )
    return pl.pallas_call(
        functools.partial(_nsa_sample_kernel, g=g, nq=nq, n_cmp=n_cmp, n_sel=n_sel, buf=buf), grid_spec=gs,
        out_shape=jax.ShapeDtypeStruct((nb, nq, hw), BF16),
        compiler_params=_cparams(("arbitrary", "arbitrary")), name="nsa_sample",
    )(page_table, qn, kc, vc, bias_c, ovl, bias_s, slc_new, bias_sn, state_swa, swa_new, bias_w, bias_wn, gates,
      *([cache_slc] * g))


def _outproj_kernel(x_ref, om_ref, on_ref, g1_ref, sc_ref, sh_ref, wom_ref, won_ref, gn2_ref, wr_ref, br_ref,
                    x1_ref, h2_ref, ti_ref, tw_ref):
    mix = _dot(om_ref[...], wom_ref[...]) + _dot(on_ref[...], won_ref[...])
    x1 = x_ref[...] + g1_ref[...] * mix
    x1_ref[...] = x1
    h2 = _rms(x1, gn2_ref[...]) * (1.0 + sc_ref[...]) + sh_ref[...]
    h2_ref[...] = h2.astype(h2_ref.dtype)
    logits = jnp.dot(h2, wr_ref[...], precision=lax.Precision.HIGHEST, preferred_element_type=F32) + br_ref[...]
    lane = lax.broadcasted_iota(I32, logits.shape, 1)
    v = logits
    vals, idxs = [], []
    for _ in range(TOP_K):
        m = jnp.max(v, axis=-1, keepdims=True)
        idx = jnp.min(jnp.where(v == m, lane, 1 << 20), axis=-1, keepdims=True)
        vals.append(m)
        idxs.append(idx)
        v = jnp.where(lane == idx, -jnp.inf, v)
    es = [jnp.exp(m - vals[0]) for m in vals]
    tot = es[0]
    for e in es[1:]:
        tot = tot + e
    ti = jnp.zeros(logits.shape, I32)
    tw = jnp.zeros(logits.shape, F32)
    for k in range(TOP_K):
        ti = jnp.where(lane == k, idxs[k], ti)
        tw = jnp.where(lane == k, es[k] / tot, tw)
    ti_ref[...] = ti
    tw_ref[...] = tw


def _outproj(x, o_mla, o_nsa, g1, sc, sh, W, *, rows_per_mod, tm):
    t, d = x.shape
    nt = t // tm
    if rows_per_mod == 1:
        tiles_per_mod = (t // g1.shape[0]) // tm
        mod_map = lambda i: (i // tiles_per_mod, 0, 0)
    else:
        mod_map = lambda i: (i, 0, 0)
    row = lambda i: (i, 0)
    c2 = lambda i: (0, 0)
    mod = pl.BlockSpec((None, rows_per_mod, d), mod_map)
    return pl.pallas_call(
        _outproj_kernel, grid=(nt,),
        in_specs=[pl.BlockSpec((tm, d), row), pl.BlockSpec((tm, o_mla.shape[1]), row), pl.BlockSpec((tm, o_nsa.shape[1]), row),
                  mod, mod, mod, pl.BlockSpec(W["w_o_mla"].shape, c2), pl.BlockSpec(W["w_o_nsa"].shape, c2),
                  pl.BlockSpec((1, d), c2), pl.BlockSpec(W["w_router"].shape, c2), pl.BlockSpec((1, LANE), c2)],
        out_specs=[pl.BlockSpec((tm, d), row), pl.BlockSpec((tm, d), row), pl.BlockSpec((tm, LANE), row),
                   pl.BlockSpec((tm, LANE), row)],
        out_shape=[jax.ShapeDtypeStruct((t, d), F32), jax.ShapeDtypeStruct((t, d), F32),
                   jax.ShapeDtypeStruct((t, LANE), I32), jax.ShapeDtypeStruct((t, LANE), F32)],
        compiler_params=_cparams(("arbitrary",)), name="out_proj",
    )(x, o_mla, o_nsa, g1, sc, sh, W["w_o_mla"], W["w_o_nsa"], W["g_norm2"], W["w_router"], W["b_router"])


def _expert_kernel(te_ref, tv_ref, x_ref, wgu_ref, bgu_ref, wd_ref, bd_ref, ws_ref, o_ref):
    t = pl.program_id(0)

    @pl.when(tv_ref[t] > 0)
    def _():
        d_ff = wd_ref.shape[0]
        gu = _dot(x_ref[...].astype(BF16), wgu_ref[...]) + bgu_ref[...]
        glu = jnp.minimum(gu[:, :d_ff], SWIGLU_LIMIT)
        lin = jnp.clip(gu[:, d_ff:], -SWIGLU_LIMIT, SWIGLU_LIMIT)
        act = glu * _sigmoid(SWIGLU_ALPHA * glu) * (lin + 1.0)
        o_ref[...] = ws_ref[...] * (_dot(act.astype(BF16), wd_ref[...]) + bd_ref[...])

    @pl.when(tv_ref[t] == 0)
    def _():
        o_ref[...] = jnp.zeros_like(o_ref)


def _experts(tile_expert, tile_valid, x_sorted, w_slot, W, *, tm):
    ns, d = x_sorted.shape
    gs = pltpu.PrefetchScalarGridSpec(
        num_scalar_prefetch=2, grid=(ns // tm,),
        in_specs=[pl.BlockSpec((tm, d), lambda t, te, tv: (t, 0)),
                  pl.BlockSpec((None,) + W["w_gate_up"].shape[1:], lambda t, te, tv: (te[t], 0, 0)),
                  pl.BlockSpec((None,) + W["b_gate_up"].shape[1:], lambda t, te, tv: (te[t], 0, 0)),
                  pl.BlockSpec((None,) + W["w_down"].shape[1:], lambda t, te, tv: (te[t], 0, 0)),
                  pl.BlockSpec((None,) + W["b_down"].shape[1:], lambda t, te, tv: (te[t], 0, 0)),
                  pl.BlockSpec((tm, 1), lambda t, te, tv: (t, 0))],
        out_specs=pl.BlockSpec((tm, d), lambda t, te, tv: (t, 0)),
    )
    return pl.pallas_call(
        _expert_kernel, grid_spec=gs, out_shape=jax.ShapeDtypeStruct((ns, d), F32),
        compiler_params=_cparams(("arbitrary",)), name="experts",
    )(tile_expert, tile_valid, x_sorted, W["w_gate_up"], W["b_gate_up"], W["w_down"], W["b_down"], w_slot)


def _final_kernel(x1_ref, *rest):
    parts, (g2_ref, gf_ref, o_ref) = rest[:TOP_K], rest[TOP_K:]
    moe = parts[0][...]
    for p_ref in parts[1:]:
        moe = moe + p_ref[...]
    o_ref[...] = _rms(x1_ref[...] + g2_ref[...] * moe, gf_ref[...])


def _final(x1, moe_parts, g2, g_final, *, rows_per_mod, tm):
    t, d = x1.shape
    if rows_per_mod == 1:
        tiles_per_mod = (t // g2.shape[0]) // tm
        mod_map = lambda i: (i // tiles_per_mod, 0, 0)
    else:
        mod_map = lambda i: (i, 0, 0)
    row = lambda i: (i, 0)
    return pl.pallas_call(
        _final_kernel, grid=(t // tm,),
        in_specs=[pl.BlockSpec((tm, d), row)] * (1 + TOP_K)
        + [pl.BlockSpec((None, rows_per_mod, d), mod_map), pl.BlockSpec((1, d), lambda i: (0, 0))],
        out_specs=pl.BlockSpec((tm, d), row), out_shape=jax.ShapeDtypeStruct((t, d), F32),
        compiler_params=_cparams(("arbitrary",)), name="final_norm")(x1, *moe_parts, g2, g_final)


def _t5_bucket(dist):
    n = jnp.maximum(dist, 0)
    max_exact = NUM_BUCKETS // 2
    nf = jnp.maximum(n, 1).astype(F32)
    large = max_exact + (jnp.log(nf / max_exact) / math.log(MAX_DISTANCE / max_exact)
                         * (NUM_BUCKETS - max_exact)).astype(I32)
    return jnp.where(n < max_exact, n, jnp.minimum(large, NUM_BUCKETS - 1))


def _bias_rows(rel_bias, dist):
    return jnp.moveaxis(rel_bias[_t5_bucket(dist)], -1, 0) * LOG2E


def _rope_table(pos):
    half = MLA_ROPE // 2
    inv_freq = 1.0 / (ROPE_THETA ** (jnp.arange(half, dtype=F32) / half))
    ang = pos.astype(F32)[:, None] * inv_freq[None, :]
    pad = jnp.zeros((pos.shape[0], LANE - MLA_ROPE), F32)
    cos, sin = jnp.cos(ang), jnp.sin(ang)
    return jnp.concatenate([cos, cos, pad, sin, sin, pad], axis=1)


def _rot_cols(w):
    half = MLA_ROPE // 2
    return jnp.concatenate([-w[..., half:], w[..., :half]], axis=-1)


def _pad_last(w, n):
    return jnp.pad(w, [(0, 0)] * (w.ndim - 1) + [(0, n - w.shape[-1])])


def _pack_weights(w_in, g_norm1, g_norm2, g_q_a, w_q_b, g_kv_a, w_kv_b, cmp_pe, cmp_w1, cmp_w2, w_o, w_router,
                  b_router, w_gate_up, b_gate_up, w_down, b_down):
    d = w_in.shape[0]
    sizes = (MLA_Q_LORA, MLA_KV_LORA, MLA_ROPE, NSA_HEADS * HEAD_DIM, KV_COLS, KV_COLS, KV_COLS, 3 * NSA_HEADS)
    offs = [0]
    for s in sizes:
        offs.append(offs[-1] + s)
    w_qa, w_ckv, w_kr, w_qn, w_cmp, w_slc, w_swa, w_g = [w_in[:, offs[k]:offs[k + 1]] for k in range(8)]
    wq = w_qn.reshape(d, NSA_KV_HEADS, NSA_GROUP, HEAD_DIM)
    wq_pad = jnp.concatenate(
        [jnp.pad(wq[:, kv], ((0, 0), (0, 0), (kv * HEAD_DIM, LANE - (kv + 1) * HEAD_DIM))).reshape(d, NSA_GROUP * LANE)
         for kv in range(NSA_KV_HEADS)], axis=1)
    wg = jnp.transpose(w_g.reshape(d, NSA_KV_HEADS, NSA_GROUP, 3), (0, 1, 3, 2)).reshape(d, NSA_KV_HEADS, 3 * NSA_GROUP)
    wg_pad = _pad_last(wg, LANE).reshape(d, NSA_KV_HEADS * LANE)
    w_in_p = jnp.concatenate([w_qa, w_ckv, wq_pad, w_cmp, w_slc, w_swa, _pad_last(w_kr, LANE),
                              _pad_last(_rot_cols(w_kr), LANE), wg_pad], axis=1).astype(BF16)
    assert w_in_p.shape[1] == _C_END
    nope = _pad_last(w_q_b[:, :, :MLA_NOPE], LANE).reshape(MLA_Q_LORA, MLA_HEADS * LANE)
    rp = w_q_b[:, :, MLA_NOPE:]
    w_qb = jnp.concatenate([nope, _pad_last(rp, LANE).reshape(MLA_Q_LORA, -1),
                            _pad_last(_rot_cols(rp), LANE).reshape(MLA_Q_LORA, -1)], axis=1).astype(BF16)
    w_kn = jnp.transpose(w_kv_b[:, :, :MLA_NOPE], (1, 2, 0))
    w_kn = jnp.pad(w_kn, ((0, 0), (0, LANE - MLA_NOPE), (0, 0))).astype(BF16)
    wv = jnp.transpose(w_kv_b[:, :, MLA_NOPE:], (1, 0, 2))
    wv_pad = jnp.stack([jnp.pad(wv[h], ((0, 0), (h * MLA_V, (MLA_HEADS - 1 - h) * MLA_V))) for h in range(MLA_HEADS)]).astype(BF16)
    n_mla = MLA_HEADS * MLA_V
    won = w_o[n_mla:].reshape(NSA_KV_HEADS, NSA_GROUP, HEAD_DIM, d)
    won_pad = jnp.concatenate(
        [jnp.pad(won[kv], ((0, 0), (kv * HEAD_DIM, LANE - (kv + 1) * HEAD_DIM), (0, 0))).reshape(NSA_GROUP * LANE, d)
         for kv in range(NSA_KV_HEADS)], axis=0).astype(BF16)
    base = jnp.concatenate([cmp_w1[:, :CMP_STRIDE], cmp_w1[:, CMP_STRIDE:]], axis=-1)
    z = jnp.zeros_like(base)
    blk = jnp.concatenate([jnp.concatenate([base, z], axis=-1), jnp.concatenate([z, base], axis=-1)], axis=2)
    w_pair = blk.reshape(2, CMP_STRIDE // 2, 2 * NSA_KV_HEADS * HEAD_DIM, NSA_KV_HEADS * 2 * CMP_HIDDEN).astype(BF16)
    pe8 = jnp.broadcast_to(cmp_pe.reshape(2, 1, CMP_LEN * HEAD_DIM), (2, 8, CMP_LEN * HEAD_DIM))
    w1r = cmp_w1.reshape(2, CMP_LEN * HEAD_DIM, CMP_HIDDEN).astype(BF16)
    w2pad = jnp.stack([jnp.stack([jnp.pad(cmp_w2[j], ((0, 0), (kv * HEAD_DIM, LANE - (kv + 1) * HEAD_DIM)))
                                  for kv in range(NSA_KV_HEADS)]) for j in range(2)]).astype(BF16)
    return dict(
        w_in=w_in_p, g_norm1=g_norm1[None], g_norm2=g_norm2[None], g_q_a=g_q_a[None], g_kv_a=g_kv_a[None],
        w_qb=w_qb, w_kn=w_kn, wv=wv_pad, wv_cat=w_kv_b[:, :, MLA_NOPE:].reshape(MLA_KV_LORA, n_mla).astype(BF16),
        w_o_mla=w_o[:n_mla].astype(BF16), w_o_nsa=won_pad,
        w_pair=w_pair, pe8=pe8, w1r=w1r, w2pad=w2pad,
        w_router=_pad_last(w_router, LANE), b_router=jnp.pad(b_router, (0, LANE - N_EXPERTS), constant_values=NEG)[None],
        w_gate_up=w_gate_up.astype(BF16), b_gate_up=b_gate_up[:, None, :], w_down=w_down.astype(BF16),
        b_down=b_down[:, None, :])


def _overlap(n_half_pad, n_sel, n_sel_pad):
    c_start = jnp.arange(n_half_pad) * CMP_STRIDE
    s_start = jnp.arange(n_sel_pad) * SEL_BLOCK
    ov = (c_start[:, None] < s_start[None, :] + SEL_BLOCK) & (c_start[:, None] + CMP_LEN > s_start[None, :])
    return (ov & (jnp.arange(n_sel_pad) < n_sel)[None, :]).astype(F32)


def _moe_dispatch(top_i, top_w, tm):
    t = top_i.shape[0]
    a = t * TOP_K
    n_tiles = -(-a // tm) + N_EXPERTS
    ns = n_tiles * tm
    e_flat = top_i[:, :TOP_K].reshape(a)
    w_flat = top_w[:, :TOP_K].reshape(a)
    onehot = (e_flat[:, None] == jnp.arange(N_EXPERTS, dtype=I32)[None, :]).astype(I32)
    csum = jnp.cumsum(onehot, axis=0)
    counts = csum[-1]
    padded = ((counts + tm - 1) // tm) * tm
    pend = jnp.cumsum(padded)
    pstart = pend - padded
    start = jnp.cumsum(counts) - counts
    slot_of_assign = jnp.sum(onehot * (pstart[None, :] + csum - 1), axis=1).reshape(t, TOP_K)
    _, order, w_sorted = lax.sort((e_flat, jnp.arange(a, dtype=I32), w_flat), num_keys=1, is_stable=True)
    tile_start = jnp.arange(n_tiles, dtype=I32) * tm
    tile_expert = jnp.minimum(jnp.sum((pend[None, :] <= tile_start[:, None]).astype(I32), axis=1), N_EXPERTS - 1)
    tile_valid = (tile_start < pend[-1]).astype(I32)
    e_hot = (tile_expert[:, None] == jnp.arange(N_EXPERTS, dtype=I32)[None, :]).astype(I32)
    lane = jnp.arange(tm, dtype=I32)[None, :]
    rank = tile_start[:, None] - jnp.sum(e_hot * pstart[None, :], axis=1, keepdims=True) + lane
    valid = (rank < jnp.sum(e_hot * counts[None, :], axis=1, keepdims=True)) & (tile_valid[:, None] > 0)
    src = jnp.clip(jnp.sum(e_hot * start[None, :], axis=1, keepdims=True) + rank, 0, a - 1).reshape(ns)
    valid = valid.reshape(ns)
    tok_of_slot = jnp.where(valid, order[src] // TOP_K, 0)
    w_of_slot = jnp.where(valid, w_sorted[src], 0.0)
    return tok_of_slot, w_of_slot[:, None], slot_of_assign, tile_expert, tile_valid


def kernel(x_prompt, x_sample, c_prompt, c_sample, cache_mla, cache_nsa_cmp, cache_nsa_slc, state_nsa_swa, page_table, rel_bias, w_ada, b_ada, g_norm1, g_norm2, w_in, g_q_a, w_q_b, g_kv_a, w_kv_b, cmp_pe, cmp_w1, cmp_w2, w_o, w_router, b_router, w_gate_up, b_gate_up, w_down, b_down, g_final):
    depth = w_in.shape[0]
    assert depth == 1, "single-layer decoder step"
    nb, seq, d = x_prompt.shape
    nbd, nq, _ = x_sample.shape
    n_pages = page_table.shape[1]
    page = cache_mla.shape[2]
    past = n_pages * page
    buf = state_nsa_swa.shape[2]
    assert page == LANE and nq <= 8 and seq % 512 == 0 and past % SEL_BLOCK == 0 and seq >= WINDOW
    tp, ts = nb * seq, nbd * nq
    W = _pack_weights(w_in[0], g_norm1[0], g_norm2[0], g_q_a[0], w_q_b[0], g_kv_a[0], w_kv_b[0], cmp_pe[0], cmp_w1[0],
                      cmp_w2[0], w_o[0], w_router[0], b_router[0], w_gate_up[0], b_gate_up[0], w_down[0], b_down[0])

    n_c = nb + nbd
    n_c_pad = -(-n_c // 8) * 8
    c_all = jnp.pad(jnp.concatenate([c_prompt, c_sample], axis=0), ((0, n_c_pad - n_c), (0, 0)))
    mod = _ada_mod(c_all, w_ada[0].astype(BF16), b_ada[0][None])
    mod_p = [m[:, None, :] for m in jnp.split(mod[:nb], 6, axis=-1)]
    tm_s = min(256, ts)
    mod_s = [jnp.repeat(m, nq, axis=0).reshape(ts // tm_s, tm_s, d) for m in jnp.split(mod[nb:n_c], 6, axis=-1)]

    tm_p = 256
    pos_p = jnp.arange(seq)
    pos_s = past + jnp.arange(nq)
    P = _proj(x_prompt.reshape(tp, d), mod_p[1], mod_p[0], _rope_table(pos_p), W, rows_per_mod=1,
              cs_period_tiles=seq // tm_p, tm=tm_p, states_t_batches=nb)
    cs_s = jnp.tile(_rope_table(pos_s), (tm_s // nq, 1))
    S_ = _proj(x_sample.reshape(ts, d), mod_s[1], mod_s[0], cs_s, W, rows_per_mod=tm_s, cs_period_tiles=1, tm=tm_s)
    p_qmla, p_kq, p_mla_t, p_qn, p_cmp, p_cmp_t, p_slc_t, p_swa_t, p_slcb, p_swab, p_gate = P
    s_qmla, s_kq, s_mla, s_qn, _, s_cmp, s_slc, s_swa, s_slcb, s_swab, s_gate = S_

    def pad_new(a):
        return jnp.pad(a.reshape(nbd, nq, a.shape[1]), ((0, 0), (0, LANE - nq), (0, 0)))

    o_mla_p = _mla_prompt(p_qmla, p_kq, W["wv"], nb=nb, seq=seq)
    g_pages = min(16, n_pages)
    cache_mla_t = jnp.swapaxes(cache_mla[0], 1, 2)
    cache_cmp_t = jnp.moveaxis(cache_nsa_cmp[0], 1, -1).reshape(-1, KV_COLS, page)
    cache_slc_t = jnp.moveaxis(cache_nsa_slc[0], 1, -1).reshape(-1, KV_COLS, page)
    state_swa_t = jnp.moveaxis(state_nsa_swa[0], 1, -1).reshape(nbd, KV_COLS, buf)
    o_mla_s = _mla_sample(page_table, s_qmla.reshape(nbd, nq, -1), pad_new(s_kq), W["wv_cat"], cache_mla_t, g=g_pages,
                          nbs=2 if nbd % 2 == 0 else 1)

    assert (past + nq) // CMP_STRIDE == past // CMP_STRIDE
    n_pp = tp // LANE
    g_cp = min(32, n_pp)
    ab_p = _cmp_first(jnp.arange(n_pp, dtype=I32), p_cmp.reshape(n_pp, LANE, KV_COLS), W["w_pair"], g=g_cp,
                      transposed=False)
    kc_p, vc_p = _cmp_finish(ab_p, W["pe8"], W["w1r"], W["w2pad"], nb=nb)
    g_cs = min(32, nbd * n_pages)
    ab_s = _cmp_first(page_table.reshape(nbd * n_pages), cache_cmp_t, W["w_pair"], g=g_cs, transposed=True)
    kc_s, vc_s = _cmp_finish(ab_s, W["pe8"], W["w1r"], W["w2pad"], nb=nbd)

    nh_p = seq // CMP_STRIDE
    n_sel_p = -(-seq // SEL_BLOCK)
    tq = 128
    far_h = _bias_rows(rel_bias, jnp.full((1, 1), MAX_DISTANCE))
    d_win = (jnp.arange(tq)[:, None] - CMP_STRIDE * (jnp.arange(LANE)[None, :] - CMP_WIN_LO) - (CMP_LEN - 1))
    assert CMP_WIN_LO * CMP_STRIDE >= MAX_DISTANCE + CMP_LEN and LANE - CMP_WIN_LO >= tq // CMP_STRIDE
    bias_c_p = jnp.where((d_win >= 0)[None], _bias_rows(rel_bias, d_win) - far_h, 0.0)
    bias_c_p = bias_c_p.reshape(NSA_HEADS * tq, LANE)
    di = jnp.arange(tq)[:, None] - jnp.arange(tq)[None, :]
    tt = jnp.stack([_bias_rows(rel_bias, di), _bias_rows(rel_bias, di + tq),
                    _bias_rows(rel_bias, jnp.full((tq, tq), MAX_DISTANCE))], axis=1)
    tt = tt.transpose(1, 0, 2, 3).reshape(3, NSA_HEADS * tq, tq)
    assert n_sel_p <= LANE
    ovl_p = _overlap(nh_p, n_sel_p, LANE).T
    o_nsa_p = _nsa_prompt(p_qn, kc_p, vc_p, bias_c_p, ovl_p, p_slcb.reshape(nb, seq, KV_COLS),
                          p_swab.reshape(nb, seq, KV_COLS), tt, p_gate, nb=nb, seq=seq, n_cmp=nh_p - 1, n_sel=n_sel_p)

    nh_s = past // CMP_STRIDE
    n_sel_s = -(-(past + nq) // SEL_BLOCK)
    assert n_sel_s == past // SEL_BLOCK + 1
    n_sel_pad = -(-n_sel_s // LANE) * LANE
    rows_s = NSA_HEADS * nq
    cmp_end_s = jnp.arange(nh_s) * CMP_STRIDE + CMP_LEN - 1
    bias_c_s = _bias_rows(rel_bias, pos_s[:, None] - cmp_end_s[None, :]).reshape(rows_s, nh_s)
    nk_step = g_pages * page
    far = jnp.broadcast_to(_bias_rows(rel_bias, jnp.full((nq, 1), MAX_DISTANCE)).reshape(rows_s, 1), (rows_s, nk_step))
    tail_pos = past - nk_step + jnp.arange(nk_step)
    bias_s = jnp.stack([far, _bias_rows(rel_bias, pos_s[:, None] - tail_pos[None, :]).reshape(rows_s, nk_step)])
    new_pos = past + jnp.arange(LANE)
    bias_new = _bias_rows(rel_bias, pos_s[:, None] - new_pos[None, :]).reshape(rows_s, LANE)
    swa_pos = past - buf + jnp.arange(buf)
    bias_w = _bias_rows(rel_bias, pos_s[:, None] - swa_pos[None, :]).reshape(rows_s, buf)
    ovl_s = _overlap(nh_s, n_sel_s, n_sel_pad).T
    o_nsa_s = _nsa_sample(page_table, s_qn.reshape(nbd, nq, -1), kc_s, vc_s, bias_c_s, ovl_s, bias_s, pad_new(s_slcb),
                          bias_new, state_swa_t, pad_new(s_swab), bias_w, bias_new,
                          s_gate.reshape(nbd, nq, -1), cache_slc_t, g=g_pages, n_cmp=nh_s - 1, n_sel=n_sel_s)

    x1_p, h2_p, ti_p, tw_p = _outproj(x_prompt.reshape(tp, d), o_mla_p, o_nsa_p, mod_p[2], mod_p[4], mod_p[3], W,
                                      rows_per_mod=1, tm=tm_p)
    x1_s, h2_s, ti_s, tw_s = _outproj(x_sample.reshape(ts, d), o_mla_s.reshape(ts, -1), o_nsa_s.reshape(ts, -1),
                                      mod_s[2], mod_s[4], mod_s[3], W, rows_per_mod=tm_s, tm=tm_s)

    tm_e = 256
    h2 = jnp.concatenate([h2_p, h2_s], axis=0)
    tok_of_slot, w_slot, slot_of_assign, tile_expert, tile_valid = _moe_dispatch(
        jnp.concatenate([ti_p, ti_s], axis=0), jnp.concatenate([tw_p, tw_s], axis=0), tm_e)
    y_sorted = _experts(tile_expert, tile_valid, h2[tok_of_slot], w_slot, W, tm=tm_e)
    parts_p = [y_sorted[slot_of_assign[:tp, k]] for k in range(TOP_K)]
    parts_s = [y_sorted[slot_of_assign[tp:, k]] for k in range(TOP_K)]
    y_p = _final(x1_p, parts_p, mod_p[5], g_final[None], rows_per_mod=1, tm=tm_p)
    y_s = _final(x1_s, parts_s, mod_s[5], g_final[None], rows_per_mod=tm_s, tm=tm_s)

    kv_tail = (2, NSA_KV_HEADS, HEAD_DIM)
    keep_p = min(WINDOW, seq)
    swa_keys = jnp.concatenate([state_nsa_swa[0], s_swa.reshape((nbd, nq) + kv_tail)], axis=1)
    keep_s = min(WINDOW, buf + nq)

    def rows_last(a_t):
        return jnp.moveaxis(a_t.reshape((nb,) + kv_tail + (a_t.shape[-1],)), -1, 1)[None]

    return (y_p.reshape(nb, seq, d), y_s.reshape(nbd, nq, d),
            jnp.swapaxes(p_mla_t, 1, 2)[None], s_mla.reshape(1, nbd, nq, -1),
            rows_last(p_cmp_t), s_cmp.reshape((1, nbd, nq) + kv_tail),
            rows_last(p_slc_t), s_slc.reshape((1, nbd, nq) + kv_tail),
            rows_last(p_swa_t[:, :, seq - keep_p:]),
            swa_keys[None, :, buf + nq - keep_s:])
```

```python
import functools
import math

import jax
import jax.numpy as jnp
from jax import lax
from jax.experimental import pallas as pl
from jax.experimental.pallas import tpu as pltpu

F32, BF16, I32 = jnp.float32, jnp.bfloat16, jnp.int32

MLA_HEADS = 8
MLA_Q_LORA = 384
MLA_KV_LORA = 256
MLA_NOPE = 64
MLA_ROPE = 32
MLA_V = 64
NSA_HEADS = 8
NSA_KV_HEADS = 2
NSA_GROUP = NSA_HEADS // NSA_KV_HEADS
HEAD_DIM = 64
CMP_LEN = 32
CMP_STRIDE = 16
CMP_HIDDEN = 128
SEL_BLOCK = 64
SEL_TOPN = 16
WINDOW = 512
KV_COLS = 2 * NSA_KV_HEADS * HEAD_DIM
N_EXPERTS = 32
TOP_K = 4
SWIGLU_LIMIT = 7.0
SWIGLU_ALPHA = 1.702
NUM_BUCKETS = 32
MAX_DISTANCE = 128
ROPE_THETA = 10000.0
NORM_EPS = 1e-6

LANE = 128
VMEM_LIMIT = 56 * 1024 * 1024
NEG = -1e30
CMP_WIN_LO = 16

MLA_QW = MLA_KV_LORA + LANE
MLA_SCALE = (MLA_NOPE + MLA_ROPE) ** -0.5
MLA_QSCALE = MLA_SCALE * math.log2(math.e)
LOG2E = math.log2(math.e)
NSA_QSCALE = HEAD_DIM ** -0.5 * LOG2E

_C_QA = 0
_C_CKV = _C_QA + MLA_Q_LORA
_C_QN = _C_CKV + MLA_KV_LORA
_C_CMP = _C_QN + NSA_HEADS * LANE
_C_SLC = _C_CMP + KV_COLS
_C_SWA = _C_SLC + KV_COLS
_C_KR = _C_SWA + KV_COLS
_C_KRR = _C_KR + LANE
_C_G = _C_KRR + LANE
_C_END = _C_G + NSA_KV_HEADS * LANE


def _cparams(sem, vmem=VMEM_LIMIT):
    return pltpu.CompilerParams(dimension_semantics=sem, vmem_limit_bytes=vmem)


def _dot(a, b):
    return jnp.dot(a, b, preferred_element_type=F32)


def _dot_nt(a, b):
    return lax.dot_general(a, b, (((1,), (1,)), ((), ())), preferred_element_type=F32)


def _row_reduce(x, op, reduce):
    n = x.shape[1] // LANE
    if x.shape[1] % LANE or n <= 1:
        return reduce(x, axis=-1, keepdims=True)
    t = x[:, :LANE]
    for c in range(1, n):
        t = op(t, x[:, c * LANE:(c + 1) * LANE])
    return reduce(t, axis=-1, keepdims=True)


def _rowmax(x):
    return _row_reduce(x, jnp.maximum, jnp.max)


def _rowsum(x):
    return _row_reduce(x, jnp.add, jnp.sum)


def _split3(x):
    hi = x.astype(BF16)
    r1 = x - hi.astype(F32)
    mid = r1.astype(BF16)
    lo = (r1 - mid.astype(F32)).astype(BF16)
    return jnp.concatenate([hi, mid, lo], axis=-1)


def _rms(x, g):
    return x * lax.rsqrt(jnp.mean(x * x, axis=-1, keepdims=True) + NORM_EPS) * g


def _sigmoid(x):
    return 1.0 / (1.0 + jnp.exp(-x))


def _ada_kernel(c_ref, w_ref, b_ref, o_ref):
    c = c_ref[...]
    o_ref[...] = _dot((c * _sigmoid(c)).astype(BF16), w_ref[...]) + b_ref[...]


def _ada_mod(c_all, w_ada, b_ada):
    m, d = c_all.shape
    n = w_ada.shape[1]
    tn = 1536
    return pl.pallas_call(
        _ada_kernel, grid=(n // tn,),
        in_specs=[pl.BlockSpec((m, d), lambda i: (0, 0)), pl.BlockSpec((d, tn), lambda i: (0, i)),
                  pl.BlockSpec((1, tn), lambda i: (0, i))],
        out_specs=pl.BlockSpec((m, tn), lambda i: (0, i)),
        out_shape=jax.ShapeDtypeStruct((m, n), F32),
        compiler_params=_cparams(("arbitrary",)), name="ada_mod")(c_all, w_ada, b_ada)


def _proj_kernel(x_ref, sc_ref, sh_ref, g1_ref, cs_ref, win_ref, gqa_ref, wqb_ref, gkva_ref, wkn_ref,
                 qmla_ref, kq_ref, mla_ref, qn_ref, cmp_ref, cmps_ref, slc_ref, swa_ref, slcb_ref, swab_ref, gate_ref,
                 *, states_t):
    h = _rms(x_ref[...], g1_ref[...]) * (1.0 + sc_ref[...]) + sh_ref[...]
    proj = _dot(h.astype(BF16), win_ref[...])
    cs = cs_ref[...]
    cosp, sinp = cs[:, :LANE], cs[:, LANE:]
    qa = _rms(proj[:, _C_QA:_C_CKV], gqa_ref[...])
    q = _dot(qa.astype(BF16), wqb_ref[...])
    hw = MLA_HEADS * LANE
    for h_i in range(MLA_HEADS):
        lo = h_i * LANE
        qlat = _dot(q[:, lo:lo + LANE].astype(BF16), wkn_ref[h_i]) * MLA_QSCALE
        qr = (q[:, hw + lo:hw + lo + LANE] * cosp + q[:, 2 * hw + lo:2 * hw + lo + LANE] * sinp) * MLA_QSCALE
        qmla_ref[:, h_i * MLA_QW:h_i * MLA_QW + MLA_KV_LORA] = qlat.astype(BF16)
        qmla_ref[:, h_i * MLA_QW + MLA_KV_LORA:(h_i + 1) * MLA_QW] = qr.astype(BF16)
    ckv = _rms(proj[:, _C_CKV:_C_QN], gkva_ref[...])
    kr = proj[:, _C_KR:_C_KRR] * cosp + proj[:, _C_KRR:_C_G] * sinp
    kq_ref[:, :MLA_KV_LORA] = ckv.astype(BF16)
    kq_ref[:, MLA_KV_LORA:] = kr.astype(BF16)
    qn_ref[...] = (proj[:, _C_QN:_C_CMP] * NSA_QSCALE).astype(BF16)
    cmp = proj[:, _C_CMP:_C_SLC]
    slc = proj[:, _C_SLC:_C_SWA]
    swa = proj[:, _C_SWA:_C_KR]
    cmp_ref[...] = cmp
    if states_t:
        mla_ref[:MLA_KV_LORA, :] = ckv.T
        mla_ref[MLA_KV_LORA:, :] = kr.T[:MLA_ROPE]
        cmps_ref[...] = cmp.T
        slc_ref[...] = slc.T
        swa_ref[...] = swa.T
    else:
        mla_ref[:, :MLA_KV_LORA] = ckv
        mla_ref[:, MLA_KV_LORA:] = kr[:, :MLA_ROPE]
        cmps_ref[...] = cmp
        slc_ref[...] = slc
        swa_ref[...] = swa
    slcb_ref[...] = slc.astype(BF16)
    swab_ref[...] = swa.astype(BF16)
    gate_ref[...] = _sigmoid(proj[:, _C_G:_C_END])


def _proj(x, sc, sh, cs, W, *, rows_per_mod, cs_period_tiles, tm, states_t_batches=0):
    t, d = x.shape
    nt = t // tm
    if rows_per_mod == 1:
        tiles_per_mod = sc.shape[0] and (t // sc.shape[0]) // tm
        mod_map = lambda i: (i // tiles_per_mod, 0, 0)
    else:
        mod_map = lambda i: (i, 0, 0)
    cs_map = (lambda i: (i % cs_period_tiles, 0)) if cs_period_tiles > 1 else (lambda i: (0, 0))
    const2 = lambda i: (0, 0)
    row = lambda i: (i, 0)
    outs = [
        (MLA_HEADS * MLA_QW, BF16), (MLA_QW, BF16), (MLA_KV_LORA + MLA_ROPE, F32), (NSA_HEADS * LANE, BF16),
        (KV_COLS, F32), (KV_COLS, F32), (KV_COLS, F32), (KV_COLS, F32), (KV_COLS, BF16), (KV_COLS, BF16),
        (NSA_KV_HEADS * LANE, F32),
    ]
    state_outs = (2, 5, 6, 7) if states_t_batches else ()
    tiles_per_batch = (t // states_t_batches) // tm if states_t_batches else 0
    out_specs = [pl.BlockSpec((None, w, tm), lambda i: (i // tiles_per_batch, 0, i % tiles_per_batch))
                 if k in state_outs else pl.BlockSpec((tm, w), row) for k, (w, _) in enumerate(outs)]
    out_shape = [jax.ShapeDtypeStruct((states_t_batches, w, t // states_t_batches) if k in state_outs else (t, w), dt)
                 for k, (w, dt) in enumerate(outs)]
    return pl.pallas_call(
        functools.partial(_proj_kernel, states_t=bool(states_t_batches)), grid=(nt,),
        in_specs=[
            pl.BlockSpec((tm, d), row),
            pl.BlockSpec((None, rows_per_mod, d), mod_map),
            pl.BlockSpec((None, rows_per_mod, d), mod_map),
            pl.BlockSpec((1, d), const2),
            pl.BlockSpec((tm, 2 * LANE), cs_map),
            pl.BlockSpec(W["w_in"].shape, const2),
            pl.BlockSpec((1, MLA_Q_LORA), const2),
            pl.BlockSpec(W["w_qb"].shape, const2),
            pl.BlockSpec((1, MLA_KV_LORA), const2),
            pl.BlockSpec(W["w_kn"].shape, lambda i: (0, 0, 0)),
        ],
        out_specs=out_specs, out_shape=out_shape,
        compiler_params=_cparams(("arbitrary",)), name="proj_in",
    )(x, sc, sh, W["g_norm1"], cs, W["w_in"], W["g_q_a"], W["w_qb"], W["g_kv_a"], W["w_kn"])


def _mla_prompt_kernel(q_ref, k_ref, wv_ref, o_ref, m_ref, l_ref, acc_ref, *, tq, tk):
    i, j = pl.program_id(1), pl.program_id(2)
    nk = pl.num_programs(2)

    @pl.when(j == 0)
    def _():
        m_ref[...] = jnp.full_like(m_ref, NEG)
        l_ref[...] = jnp.zeros_like(l_ref)
        acc_ref[...] = jnp.zeros_like(acc_ref)

    def step(masked):
        k = k_ref[...]
        v = k[:, :MLA_KV_LORA]
        if masked:
            qpos = i * tq + lax.broadcasted_iota(I32, (tq, tk), 0)
            kpos = j * tk + lax.broadcasted_iota(I32, (tq, tk), 1)
            mask = kpos <= qpos
        ss = [_dot_nt(q_ref[:, h * MLA_QW:(h + 1) * MLA_QW], k) for h in range(MLA_HEADS)]
        ps, alphas = [], []
        for h in range(MLA_HEADS):
            r = slice(h * tq, (h + 1) * tq)
            s = jnp.where(mask, ss[h], NEG) if masked else ss[h]
            m_prev = m_ref[r]
            m_new = jnp.maximum(m_prev, _rowmax(s))
            p = jnp.exp2(s - m_new)
            if masked:
                p = jnp.where(mask, p, 0.0)
            alpha = jnp.exp2(m_prev - m_new)
            l_ref[r] = alpha * l_ref[r] + _rowsum(p)
            m_ref[r] = m_new
            ps.append(p.astype(BF16))
            alphas.append(alpha)
        for h in range(MLA_HEADS):
            r = slice(h * tq, (h + 1) * tq)
            acc_ref[r] = alphas[h] * acc_ref[r] + _dot(ps[h], v)

    first_key, last_key = j * tk, j * tk + tk - 1
    pl.when(last_key <= i * tq)(functools.partial(step, False))
    pl.when((first_key <= i * tq + tq - 1) & (last_key > i * tq))(functools.partial(step, True))

    @pl.when(j == nk - 1)
    def _():
        out = jnp.zeros(o_ref.shape, F32)
        for h in range(MLA_HEADS):
            r = slice(h * tq, (h + 1) * tq)
            l = l_ref[r]
            o_lat = acc_ref[r] / jnp.where(l > 0.0, l, 1.0)
            out = out + _dot(o_lat.astype(BF16), wv_ref[h])
        o_ref[...] = out.astype(o_ref.dtype)


def _mla_prompt(qmla, kq, wv, *, nb, seq, tq=256, tk=512):
    tk = min(tk, seq)
    nq, nk = seq // tq, seq // tk
    ow = MLA_HEADS * MLA_V

    def k_map(b, i, j):
        return (b * nk + jnp.minimum(j, (i * tq + tq - 1) // tk), 0)

    return pl.pallas_call(
        functools.partial(_mla_prompt_kernel, tq=tq, tk=tk), grid=(nb, nq, nk),
        in_specs=[pl.BlockSpec((tq, MLA_HEADS * MLA_QW), lambda b, i, j: (b * nq + i, 0)),
                  pl.BlockSpec((tk, MLA_QW), k_map),
                  pl.BlockSpec(wv.shape, lambda b, i, j: (0, 0, 0))],
        out_specs=pl.BlockSpec((tq, ow), lambda b, i, j: (b * nq + i, 0)),
        out_shape=jax.ShapeDtypeStruct((nb * seq, ow), BF16),
        scratch_shapes=[pltpu.VMEM((MLA_HEADS * tq, 1), F32), pltpu.VMEM((MLA_HEADS * tq, 1), F32),
                        pltpu.VMEM((MLA_HEADS * tq, MLA_KV_LORA), F32)],
        compiler_params=_cparams(("arbitrary", "arbitrary", "arbitrary")), name="mla_prompt",
    )(qmla, kq, wv)


def _mla_sample_kernel(pt_ref, q_ref, kn_ref, wv_ref, *rest, g, nq, nbs):
    pages = rest[:nbs * g]
    o_ref, m_ref, l_ref, acc_ref = rest[nbs * g:]
    s_idx = pl.program_id(1)
    rows = MLA_HEADS * nq

    @pl.when(s_idx == 0)
    def _():
        m_ref[...] = jnp.full_like(m_ref, NEG)
        l_ref[...] = jnp.zeros_like(l_ref)
        acc_ref[...] = jnp.zeros_like(acc_ref)

    qs = []
    for bl in range(nbs):
        r = slice(bl * rows, (bl + 1) * rows)
        q = jnp.concatenate([q_ref[bl, :, h * MLA_QW:(h + 1) * MLA_QW] for h in range(MLA_HEADS)], axis=0)
        qs.append(q)
        qlat, qr = q[:, :MLA_KV_LORA], q[:, MLA_KV_LORA:MLA_KV_LORA + MLA_ROPE]
        pg = pages[bl * g:(bl + 1) * g]
        ckv_t = jnp.concatenate([pg[u][:MLA_KV_LORA].astype(BF16) for u in range(g)], axis=1)
        kr_t = jnp.concatenate([pg[u][MLA_KV_LORA:].astype(BF16) for u in range(g)], axis=1)
        s = _dot(qlat, ckv_t) + _dot(qr, kr_t)
        m_prev = m_ref[r]
        m_new = jnp.maximum(m_prev, _rowmax(s))
        p = jnp.exp2(s - m_new)
        alpha = jnp.exp2(m_prev - m_new)
        l_ref[r] = alpha * l_ref[r] + _rowsum(p)
        acc_ref[r] = alpha * acc_ref[r] + _dot_nt(p.astype(BF16), ckv_t)
        m_ref[r] = m_new

    @pl.when(s_idx == pl.num_programs(1) - 1)
    def _():
        for bl in range(nbs):
            r = slice(bl * rows, (bl + 1) * rows)
            kn = kn_ref[bl]
            sn = _dot_nt(qs[bl], kn)
            qi = lax.broadcasted_iota(I32, sn.shape, 0) % nq
            kt = lax.broadcasted_iota(I32, sn.shape, 1)
            mask = kt <= qi
            sn = jnp.where(mask, sn, NEG)
            m_prev = m_ref[r]
            m_new = jnp.maximum(m_prev, jnp.max(sn, axis=-1, keepdims=True))
            p = jnp.where(mask, jnp.exp2(sn - m_new), 0.0)
            alpha = jnp.exp2(m_prev - m_new)
            l = alpha * l_ref[r] + _rowsum(p)
            acc = alpha * acc_ref[r] + _dot(p.astype(BF16), kn[:, :MLA_KV_LORA])
            o_lat = (acc / jnp.where(l > 0.0, l, 1.0)).astype(BF16)
            res = _dot(o_lat, wv_ref[...])
            head_of_lane = lax.broadcasted_iota(I32, (nq, res.shape[1]), 1) // MLA_V
            out = jnp.zeros((nq, res.shape[1]), F32)
            for h in range(MLA_HEADS):
                out = out + jnp.where(head_of_lane == h, res[h * nq:(h + 1) * nq], 0.0)
            o_ref[bl] = out.astype(o_ref.dtype)


def _page_specs(g, block, n_lane_blocks=1, flat=False, nbs=1):
    def index(b, s, pt, bl, u, c):
        return ((pt[s * g + u] if flat else pt[b * nbs + bl, s * g + u]), 0, c)

    return [pl.BlockSpec(block, functools.partial(index, bl=bl, u=u, c=c))
            for bl in range(nbs) for u in range(g) for c in range(n_lane_blocks)]


def _mla_sample(page_table, qmla, kq_new_pad, wv, cache_mla, *, g, nbs):
    nb, nq = qmla.shape[0], qmla.shape[1]
    n_pages = page_table.shape[1]
    ow = MLA_HEADS * MLA_V
    rows = nbs * MLA_HEADS * nq
    bmap = lambda b, s, pt: (b, 0, 0)
    gs = pltpu.PrefetchScalarGridSpec(
        num_scalar_prefetch=1, grid=(nb // nbs, n_pages // g),
        in_specs=[pl.BlockSpec((nbs, nq, MLA_HEADS * MLA_QW), bmap),
                  pl.BlockSpec((nbs, LANE, MLA_QW), bmap),
                  pl.BlockSpec(wv.shape, lambda b, s, pt: (0, 0))]
        + _page_specs(g, (None,) + cache_mla.shape[1:], nbs=nbs),
        out_specs=pl.BlockSpec((nbs, nq, ow), bmap),
        scratch_shapes=[pltpu.VMEM((rows, 1), F32), pltpu.VMEM((rows, 1), F32), pltpu.VMEM((rows, MLA_KV_LORA), F32)],
    )
    return pl.pallas_call(
        functools.partial(_mla_sample_kernel, g=g, nq=nq, nbs=nbs), grid_spec=gs,
        out_shape=jax.ShapeDtypeStruct((nb, nq, ow), BF16),
        compiler_params=_cparams(("arbitrary", "arbitrary")), name="mla_sample",
    )(page_table, qmla, kq_new_pad, wv, *([cache_mla] * (nbs * g)))


def _cmp_first_kernel(pl_ref, w_ref, *rest, g, transposed):
    pages = rest[:g]
    o_ref, xs_ref = rest[g:]
    half_w = NSA_KV_HEADS * 2 * CMP_HIDDEN
    n_half = o_ref.shape[0]
    for u in range(g):
        x = pages[u][...]
        for j in range(2):
            if transposed:
                xs_ref[j, u * LANE:(u + 1) * LANE, :] = x[j * LANE:(j + 1) * LANE, :].T
            else:
                xs_ref[j, u * LANE:(u + 1) * LANE, :] = x[:, j * LANE:(j + 1) * LANE]
    for j in range(2):
        acc = jnp.zeros((n_half, half_w), F32)
        for pp in range(CMP_STRIDE // 2):
            xa = xs_ref[j, pl.ds(2 * pp, n_half, stride=CMP_STRIDE), :]
            xb = xs_ref[j, pl.ds(2 * pp + 1, n_half, stride=CMP_STRIDE), :]
            acc = acc + _dot(jnp.concatenate([xa, xb], axis=1).astype(BF16), w_ref[j, pp])
        o_ref[:, j * half_w:(j + 1) * half_w] = acc


def _cmp_first(page_list, pool, w_pair, *, g, transposed):
    n = page_list.shape[0]
    halves = LANE // CMP_STRIDE
    ow = 2 * NSA_KV_HEADS * 2 * CMP_HIDDEN
    gs = pltpu.PrefetchScalarGridSpec(
        num_scalar_prefetch=1, grid=(1, n // g),
        in_specs=[pl.BlockSpec(w_pair.shape, lambda b, s, pt: (0, 0, 0, 0))]
        + _page_specs(g, (None,) + pool.shape[1:], flat=True),
        out_specs=pl.BlockSpec((g * halves, ow), lambda b, s, pt: (s, 0)),
        scratch_shapes=[pltpu.VMEM((2, g * LANE, LANE), F32)],
    )
    return pl.pallas_call(
        functools.partial(_cmp_first_kernel, g=g, transposed=transposed), grid_spec=gs,
        out_shape=jax.ShapeDtypeStruct((n * halves, ow), F32),
        compiler_params=_cparams(("arbitrary", "arbitrary")), name="cmp_first",
    )(page_list, w_pair, *([pool] * g))


def _cmp_finish_kernel(ab_ref, pe_ref, w1_ref, w2_ref, kc_ref, vc_ref):
    nh = ab_ref.shape[0]
    outs = []
    for j in range(2):
        pe_term = _dot(pe_ref[j].astype(BF16), w1_ref[j])[0:1]
        acc = jnp.zeros((nh, LANE), F32)
        for kv in range(NSA_KV_HEADS):
            base = (j * NSA_KV_HEADS + kv) * 2 * CMP_HIDDEN
            first = ab_ref[:, base:base + CMP_HIDDEN]
            second = ab_ref[:, base + CMP_HIDDEN:base + 2 * CMP_HIDDEN]
            hid = first + pltpu.roll(second, nh - 1, 0) + pe_term
            hid = 0.5 * hid * (1.0 + lax.erf(hid * math.sqrt(0.5)))
            acc = acc + _dot(hid.astype(BF16), w2_ref[j, kv])
        outs.append(acc)
    kc_ref[...] = outs[0].astype(kc_ref.dtype)
    vc_ref[...] = outs[1].astype(vc_ref.dtype)


def _cmp_finish(ab, pe8, w1r, w2pad, *, nb):
    nh = ab.shape[0] // nb
    c3 = lambda b: (0, 0, 0)
    return pl.pallas_call(
        _cmp_finish_kernel, grid=(nb,),
        in_specs=[pl.BlockSpec((nh, ab.shape[1]), lambda b: (b, 0)), pl.BlockSpec(pe8.shape, c3),
                  pl.BlockSpec(w1r.shape, c3), pl.BlockSpec(w2pad.shape, lambda b: (0, 0, 0, 0))],
        out_specs=[pl.BlockSpec((None, nh, LANE), lambda b: (b, 0, 0))] * 2,
        out_shape=[jax.ShapeDtypeStruct((nb, nh, LANE), BF16)] * 2,
        compiler_params=_cparams(("arbitrary",)), name="cmp_finish")(ab, pe8, w1r, w2pad)


def _topk_mask(v, k, axis=1):
    lane = lax.broadcasted_iota(I32, v.shape, axis)
    sel = jnp.zeros(v.shape, F32)
    for _ in range(k):
        m = jnp.max(v, axis=axis, keepdims=True)
        idx = jnp.min(jnp.where(v == m, lane, 1 << 20), axis=axis, keepdims=True)
        pick = lane == idx
        sel = jnp.where(pick & (m > -jnp.inf), 1.0, sel)
        v = jnp.where(pick, -jnp.inf, v)
    return sel


def _softmax_masked(s, mask):
    s = jnp.where(mask, s, NEG)
    m = _rowmax(s)
    p = jnp.where(mask, jnp.exp2(s - m), 0.0)
    l = _rowsum(p)
    return p / jnp.where(l > 0.0, l, 1.0)


def _flash_step(s, mask, v, m_ref, l_ref, acc_ref, v_keys_on_lanes=False):
    s = jnp.where(mask, s, NEG)
    m_prev = m_ref[...]
    m_new = jnp.maximum(m_prev, _rowmax(s))
    p = jnp.where(mask, jnp.exp2(s - m_new), 0.0)
    alpha = jnp.exp2(m_prev - m_new)
    l_ref[...] = alpha * l_ref[...] + _rowsum(p)
    pv = _dot_nt(p.astype(BF16), v) if v_keys_on_lanes else _dot(p.astype(BF16), v)
    acc_ref[...] = alpha * acc_ref[...] + pv
    m_ref[...] = m_new


def _flash_init(m_ref, l_ref, acc_ref):
    m_ref[...] = jnp.full_like(m_ref, NEG)
    l_ref[...] = jnp.zeros_like(l_ref)
    acc_ref[...] = jnp.zeros_like(acc_ref)


def _flash_out(l_ref, acc_ref):
    l = l_ref[...]
    return acc_ref[...] / jnp.where(l > 0.0, l, 1.0)


def _nsa_prompt_kernel(q_ref, kc_ref, vc_ref, bc_ref, ovl_ref, slc_ref, swa_ref, tt_ref, gate_ref, o_ref,
                       m_ref, l_ref, acc_ref, *, tq, tkf, n_cmp, n_sel, seq_len):
    i = pl.program_id(1)
    rows = NSA_HEADS * tq
    n_sel_rows = -(-n_sel // 8) * 8
    q = jnp.concatenate([q_ref[:, h * LANE:(h + 1) * LANE] for h in range(NSA_HEADS)], axis=0)
    nhp = kc_ref.shape[0]
    pos_r = i * tq + lax.broadcasted_iota(I32, (rows, 1), 0) % tq

    wu = lax.broadcasted_iota(I32, (LANE, nhp), 0)
    wn = lax.broadcasted_iota(I32, (LANE, nhp), 1)
    shift = jnp.where(wn == i * (tq // CMP_STRIDE) - CMP_WIN_LO + wu, 1.0, 0.0).astype(BF16)
    far_col = tt_ref[2][:, :1]
    bias_c = _dot(_split3(bc_ref[...]), jnp.concatenate([shift] * 3, axis=0)) + far_col
    s = _dot_nt(q, kc_ref[...]) + bias_c
    n_idx = lax.broadcasted_iota(I32, (rows, nhp), 1)
    mask_c = (n_idx * CMP_STRIDE + CMP_LEN - 1 <= pos_r) & (n_idx < n_cmp)
    p_cmp = _softmax_masked(s, mask_c)
    o_cmp = _dot(p_cmp.astype(BF16), vc_ref[...])

    imps = []
    for kv in range(NSA_KV_HEADS):
        lo = kv * NSA_GROUP * tq
        psum = p_cmp[lo:lo + tq]
        for g in range(1, NSA_GROUP):
            psum = psum + p_cmp[lo + g * tq:lo + (g + 1) * tq]
        imps.append(_dot_nt(ovl_ref[...], _split3(psum))[:n_sel_rows])
    imp_t = jnp.concatenate(imps, axis=1)
    blk = lax.broadcasted_iota(I32, imp_t.shape, 0)
    cur = (i * tq + lax.broadcasted_iota(I32, imp_t.shape, 1) % tq) // SEL_BLOCK
    forced = (blk == 0) | (blk == cur) | (blk == cur - 1)
    future = (blk > cur) | (blk >= n_sel)
    imp_t = jnp.where(future, -jnp.inf, jnp.where(forced, jnp.inf, imp_t))
    sel_t = _topk_mask(imp_t, min(SEL_TOPN, n_sel), axis=0)
    if n_sel_rows < LANE:
        sel_t = jnp.concatenate([sel_t, jnp.zeros((LANE - n_sel_rows, NSA_KV_HEADS * tq), F32)], axis=0)
    sels = [sel_t[:, kv * tq:(kv + 1) * tq].T.astype(BF16) for kv in range(NSA_KV_HEADS)]

    near0 = pl.multiple_of(jnp.maximum(i - 1, 0) * tq, tq)
    near_bias = jnp.concatenate([tt_ref[jnp.where(i == 0, 0, 1)], tt_ref[0]], axis=1)
    pos_q = i * tq + lax.broadcasted_iota(I32, (tq, 1), 0)
    d_near = pos_q - (near0 + lax.broadcasted_iota(I32, (tq, 2 * tq), 1))

    def heads(pens):
        return jnp.concatenate([p_ for p_ in pens for _ in range(NSA_GROUP)], axis=0)

    def block_sel(first_key, n_keys, key_limit):
        sb = lax.broadcasted_iota(I32, (LANE, n_keys), 0)
        kt = lax.broadcasted_iota(I32, (LANE, n_keys), 1)
        hit = (sb == first_key // SEL_BLOCK + kt // SEL_BLOCK) & (first_key + kt < key_limit)
        expand = jnp.where(hit, 1.0, 0.0).astype(BF16)
        return [_dot(sel, expand) > 0.5 for sel in sels]

    n_far_w = WINDOW - tq
    far0 = pl.multiple_of(jnp.maximum(i - WINDOW // tq, 0) * tq, tq)
    kv_n = swa_ref[pl.ds(near0, 2 * tq), :]
    kv_f = swa_ref[pl.ds(far0, n_far_w), :]
    kpos_f = far0 + lax.broadcasted_iota(I32, (tq, n_far_w), 1)
    pen_n = jnp.where((d_near >= 0) & (d_near < WINDOW), 0.0, NEG)
    pen_f = jnp.where((kpos_f < near0) & (pos_q - kpos_f < WINDOW), 0.0, NEG)
    s_n = _dot_nt(q, kv_n[:, :LANE]) + near_bias + heads([pen_n] * NSA_KV_HEADS)
    s_f = _dot_nt(q, kv_f[:, :LANE]) + far_col + heads([pen_f] * NSA_KV_HEADS)
    m_w = jnp.maximum(_rowmax(s_n), _rowmax(s_f))
    p_n = jnp.exp2(s_n - m_w)
    p_f = jnp.exp2(s_f - m_w)
    l_w = _rowsum(p_n) + _rowsum(p_f)
    o_swa = (_dot(p_n.astype(BF16), kv_n[:, LANE:]) + _dot(p_f.astype(BF16), kv_f[:, LANE:])) / l_w

    kv_n = slc_ref[pl.ds(near0, 2 * tq), :]
    pens = [jnp.where(hit & (d_near >= 0), 0.0, NEG) for hit in block_sel(near0, 2 * tq, seq_len)]
    s_n = _dot_nt(q, kv_n[:, :LANE]) + near_bias + heads(pens)
    m_s = _rowmax(s_n)
    p_n = jnp.exp2(s_n - m_s)
    m_ref[...] = m_s
    l_ref[...] = _rowsum(p_n)
    acc_ref[...] = _dot(p_n.astype(BF16), kv_n[:, LANE:])

    n_far = (near0 + tkf - 1) // tkf
    half = NSA_GROUP * tq

    def far_logits(c):
        first = pl.multiple_of(jnp.minimum(c, jnp.maximum(n_far - 1, 0)) * tkf, tkf)
        k = slc_ref[pl.ds(first, tkf), :LANE]
        return tuple(_dot_nt(q[kvh * half:(kvh + 1) * half], k) for kvh in range(NSA_KV_HEADS))

    def slc_far(c, qk):
        qk_next = far_logits(c + 1)
        first = pl.multiple_of(c * tkf, tkf)
        v = slc_ref[pl.ds(first, tkf), LANE:]
        hits = block_sel(first, tkf, near0)
        for kvh in range(NSA_KV_HEADS):
            r = slice(kvh * half, (kvh + 1) * half)
            pen = jnp.where(hits[kvh], 0.0, NEG)
            s_ = qk[kvh] + far_col[r] + jnp.concatenate([pen] * NSA_GROUP, axis=0)
            m_prev = m_ref[r]
            m_new = jnp.maximum(m_prev, _rowmax(s_))
            p = jnp.exp2(s_ - m_new)
            alpha = jnp.exp2(m_prev - m_new)
            l_ref[r] = alpha * l_ref[r] + _rowsum(p)
            acc_ref[r] = alpha * acc_ref[r] + _dot(p.astype(BF16), v)
            m_ref[r] = m_new
        return qk_next

    lax.fori_loop(0, n_far, slc_far, far_logits(0))
    o_slc = acc_ref[...] / l_ref[...]

    gates = gate_ref[...]
    for h in range(NSA_HEADS):
        kv, g = divmod(h, NSA_GROUP)
        r = slice(h * tq, (h + 1) * tq)
        c = kv * LANE + g
        o = (gates[:, c:c + 1] * o_cmp[r] + gates[:, c + NSA_GROUP:c + NSA_GROUP + 1] * o_slc[r]
             + gates[:, c + 2 * NSA_GROUP:c + 2 * NSA_GROUP + 1] * o_swa[r])
        o_ref[:, h * LANE:(h + 1) * LANE] = o.astype(o_ref.dtype)


def _nsa_prompt(qn, kc, vc, bias_c, ovl, slc_b, swa_b, tt, gates, *, nb, seq, n_cmp, n_sel):
    tq = 128
    nq = seq // tq
    rows = NSA_HEADS * tq
    hw = NSA_HEADS * LANE
    nhp = kc.shape[1]
    return pl.pallas_call(
        functools.partial(_nsa_prompt_kernel, tq=tq, tkf=min(512, seq), n_cmp=n_cmp, n_sel=n_sel, seq_len=seq),
        grid=(nb, nq),
        in_specs=[
            pl.BlockSpec((tq, hw), lambda b, i: (b * nq + i, 0)),
            pl.BlockSpec((None, nhp, LANE), lambda b, i: (b, 0, 0)),
            pl.BlockSpec((None, nhp, LANE), lambda b, i: (b, 0, 0)),
            pl.BlockSpec(bias_c.shape, lambda b, i: (0, 0)),
            pl.BlockSpec(ovl.shape, lambda b, i: (0, 0)),
            pl.BlockSpec((None, seq, KV_COLS), lambda b, i: (b, 0, 0)),
            pl.BlockSpec((None, seq, KV_COLS), lambda b, i: (b, 0, 0)),
            pl.BlockSpec(tt.shape, lambda b, i: (0, 0, 0)),
            pl.BlockSpec((tq, NSA_KV_HEADS * LANE), lambda b, i: (b * nq + i, 0)),
        ],
        out_specs=pl.BlockSpec((tq, hw), lambda b, i: (b * nq + i, 0)),
        out_shape=jax.ShapeDtypeStruct((nb * seq, hw), BF16),
        scratch_shapes=[pltpu.VMEM((rows, 1), F32), pltpu.VMEM((rows, 1), F32), pltpu.VMEM((rows, LANE), F32)],
        compiler_params=_cparams(("arbitrary", "arbitrary")), name="nsa_prompt",
    )(qn, kc, vc, bias_c, ovl, slc_b, swa_b, tt, gates)


def _nsa_sample_kernel(pt_ref, q_ref, kc_ref, vc_ref, bc_ref, ovl_ref, bs_ref, sn_ref, bsn_ref, st_ref, wn_ref,
                       bw_ref, bwn_ref, gate_ref, *rest, g, nq, nbs, n_cmp, n_sel, buf):
    o_ref, sel_ref, ocmp_ref, oswa_ref, m_ref, l_ref, acc_ref = rest[nbs * g:]
    s_idx = pl.program_id(1)
    last = pl.num_programs(1) - 1
    rows = NSA_HEADS * nq
    qi = lax.broadcasted_iota(I32, (rows, 1), 0) % nq
    qs = [jnp.concatenate([q_ref[bl, :, h * LANE:(h + 1) * LANE] for h in range(NSA_HEADS)], axis=0)
          for bl in range(nbs)]

    def first_step(bl):
        q = qs[bl]
        kc_r, vc_r, st_r, wn_r = kc_ref.at[bl], vc_ref.at[bl], st_ref.at[bl], wn_ref.at[bl]
        _nsa_sample_first(q, qi, kc_r, vc_r, bc_ref, ovl_ref, st_r, wn_r, bw_ref, bwn_ref, sel_ref.at[bl], ocmp_ref.at[bl],
                          oswa_ref.at[bl], m_ref.at[bl], l_ref.at[bl], acc_ref.at[bl], nq=nq, n_cmp=n_cmp, n_sel=n_sel, buf=buf)

    @pl.when(s_idx == 0)
    def _():
        for bl in range(nbs):
            first_step(bl)

    for bl in range(nbs):
        pages = rest[bl * g:(bl + 1) * g]
        page = pages[0].shape[1]
        kcat = jnp.concatenate([pages[u][:LANE].astype(BF16) for u in range(g)], axis=1)
        vcat = jnp.concatenate([pages[u][LANE:].astype(BF16) for u in range(g)], axis=1)
        nk = g * page
        s = _dot(qs[bl], kcat) + bs_ref[jnp.where(s_idx == last, 1, 0)]
        sb = lax.broadcasted_iota(I32, (sel_ref.shape[2], nk), 0)
        kt = lax.broadcasted_iota(I32, (sel_ref.shape[2], nk), 1)
        expand = jnp.where(sb == s_idx * (nk // SEL_BLOCK) + kt // SEL_BLOCK, 1.0, 0.0).astype(BF16)
        mask = _dot(sel_ref[bl], expand) > 0.5
        _flash_step(s, mask, vcat, m_ref.at[bl], l_ref.at[bl], acc_ref.at[bl], v_keys_on_lanes=True)

    @pl.when(s_idx == last)
    def _():
        for bl in range(nbs):
            q = qs[bl]
            m_r, l_r, acc_r = m_ref.at[bl], l_ref.at[bl], acc_ref.at[bl]
            sn = sn_ref[bl]
            s_n = _dot_nt(q, sn[:, :LANE]) + bsn_ref[...]
            mask_n = (lax.broadcasted_iota(I32, (rows, LANE), 1) <= qi) & (sel_ref[bl, :, n_sel - 1:n_sel] > 0.5)
            _flash_step(s_n, mask_n, sn[:, LANE:], m_r, l_r, acc_r)
            o_slc = _flash_out(l_r, acc_r)
            o_cmp, o_swa = ocmp_ref[bl], oswa_ref[bl]
            gates = gate_ref[bl]
            for h in range(NSA_HEADS):
                kv, gg = divmod(h, NSA_GROUP)
                r = slice(h * nq, (h + 1) * nq)
                c = kv * LANE + gg
                o = (gates[:, c:c + 1] * o_cmp[r] + gates[:, c + NSA_GROUP:c + NSA_GROUP + 1] * o_slc[r]
                     + gates[:, c + 2 * NSA_GROUP:c + 2 * NSA_GROUP + 1] * o_swa[r])
                o_ref[bl, :, h * LANE:(h + 1) * LANE] = o.astype(o_ref.dtype)


def _nsa_sample_first(q, qi, kc_ref, vc_ref, bc_ref, ovl_ref, st_ref, wn_ref, bw_ref, bwn_ref, sel_ref, ocmp_ref, oswa_ref,
                      m_ref, l_ref, acc_ref, *, nq, n_cmp, n_sel, buf):
    rows = NSA_HEADS * nq

    def compressed_and_select():
        nhp = kc_ref.shape[0]
        s = _dot_nt(q, kc_ref[...]) + bc_ref[...]
        mask_c = lax.broadcasted_iota(I32, (rows, nhp), 1) < n_cmp
        p_cmp = _softmax_masked(s, mask_c)
        ocmp_ref[...] = _dot(p_cmp.astype(BF16), vc_ref[...])
        psums = []
        for kv in range(NSA_KV_HEADS):
            ps = p_cmp[kv * NSA_GROUP * nq:kv * NSA_GROUP * nq + nq]
            for gg in range(1, NSA_GROUP):
                lo = (kv * NSA_GROUP + gg) * nq
                ps = ps + p_cmp[lo:lo + nq]
            psums.append(ps)
        n_kq = NSA_KV_HEADS * nq
        psum = jnp.concatenate(psums + [jnp.zeros((LANE - n_kq, nhp), F32)], axis=0)
        n_blk = ovl_ref.shape[0]
        n_sel_rows = -(-n_sel // 8) * 8
        imp_t = _dot_nt(ovl_ref[...], _split3(psum))[:n_sel_rows]
        blk = lax.broadcasted_iota(I32, imp_t.shape, 0)
        cur = n_sel - 1
        forced = (blk == 0) | (blk == cur) | (blk == cur - 1)
        imp_t = jnp.where(blk >= n_sel, -jnp.inf, jnp.where(forced, jnp.inf, imp_t))
        sel_t = _topk_mask(imp_t, min(SEL_TOPN, n_sel), axis=0)
        if n_sel_rows < n_blk:
            sel_t = jnp.concatenate([sel_t, jnp.zeros((n_blk - n_sel_rows, LANE), F32)], axis=0)
        sel = sel_t.T.astype(BF16)
        sel_ref[...] = jnp.concatenate(
            [sel[kv * nq:(kv + 1) * nq] for kv in range(NSA_KV_HEADS) for _ in range(NSA_GROUP)], axis=0)

    def window():
        st = st_ref[...]
        s_w = _dot(q, st[:LANE].astype(BF16)) + bw_ref[...]
        d_w = buf + qi - lax.broadcasted_iota(I32, (rows, buf), 1)
        mask_w = (d_w >= 0) & (d_w < WINDOW)
        wn = wn_ref[...]
        s_n = _dot_nt(q, wn[:, :LANE]) + bwn_ref[...]
        mask_n = lax.broadcasted_iota(I32, (rows, LANE), 1) <= qi
        s_w = jnp.where(mask_w, s_w, NEG)
        s_n = jnp.where(mask_n, s_n, NEG)
        m = jnp.maximum(_rowmax(s_w), _rowmax(s_n))
        p_w = jnp.where(mask_w, jnp.exp2(s_w - m), 0.0)
        p_n = jnp.where(mask_n, jnp.exp2(s_n - m), 0.0)
        l = _rowsum(p_w) + _rowsum(p_n)
        o = _dot_nt(p_w.astype(BF16), st[LANE:].astype(BF16)) + _dot(p_n.astype(BF16), wn[:, LANE:])
        oswa_ref[...] = o / jnp.where(l > 0.0, l, 1.0)

    compressed_and_select()
    window()
    _flash_init(m_ref, l_ref, acc_ref)


def _nsa_sample(page_table, qn, kc, vc, bias_c, ovl, bias_s, slc_new, bias_sn, state_swa, swa_new, bias_w, bias_wn,
                gates, cache_slc, *, g, nbs, n_cmp, n_sel):
    nb, nq = qn.shape[0], qn.shape[1]
    n_pages = page_table.shape[1]
    rows = NSA_HEADS * nq
    buf = state_swa.shape[2]
    hw = NSA_HEADS * LANE
    bmap = lambda b, s, pt: (b, 0, 0)
    c2 = lambda b, s, pt: (0, 0)
    per_elem = lambda a: pl.BlockSpec((nbs,) + a.shape[1:], bmap)
    gs = pltpu.PrefetchScalarGridSpec(
        num_scalar_prefetch=1, grid=(nb // nbs, n_pages // g),
        in_specs=[
            per_elem(qn), per_elem(kc), per_elem(vc),
            pl.BlockSpec(bias_c.shape, c2),
            pl.BlockSpec(ovl.shape, c2),
            pl.BlockSpec(bias_s.shape, lambda b, s, pt: (0, 0, 0)),
            per_elem(slc_new),
            pl.BlockSpec(bias_sn.shape, c2),
            per_elem(state_swa), per_elem(swa_new),
            pl.BlockSpec(bias_w.shape, c2),
            pl.BlockSpec(bias_wn.shape, c2),
            per_elem(gates),
        ] + _page_specs(g, (None,) + cache_slc.shape[1:], nbs=nbs),
        out_specs=pl.BlockSpec((nbs, nq, hw), bmap),
        scratch_shapes=[pltpu.VMEM((nbs, rows, ovl.shape[0]), BF16), pltpu.VMEM((nbs, rows, LANE), F32),
                        pltpu.VMEM((nbs, rows, LANE), F32), pltpu.VMEM((nbs, rows, 1), F32), pltpu.VMEM((nbs, rows, 1), F32),
                        pltpu.VMEM((nbs, rows, LANE), F32)],
    )
    return pl.pallas_call(
        functools.partial(_nsa_sample_kernel, g=g, nq=nq, nbs=nbs, n_cmp=n_cmp, n_sel=n_sel, buf=buf), grid_spec=gs,
        out_shape=jax.ShapeDtypeStruct((nb, nq, hw), BF16),
        compiler_params=_cparams(("arbitrary", "arbitrary")), name="nsa_sample",
    )(page_table, qn, kc, vc, bias_c, ovl, bias_s, slc_new, bias_sn, state_swa, swa_new, bias_w, bias_wn, gates,
      *([cache_slc] * (nbs * g)))


def _outproj_kernel(x_ref, om_ref, on_ref, g1_ref, sc_ref, sh_ref, wom_ref, won_ref, gn2_ref, wr_ref, br_ref,
                    x1_ref, h2_ref, ti_ref, tw_ref):
    mix = _dot(om_ref[...], wom_ref[...]) + _dot(on_ref[...], won_ref[...])
    x1 = x_ref[...] + g1_ref[...] * mix
    x1_ref[...] = x1
    h2 = _rms(x1, gn2_ref[...]) * (1.0 + sc_ref[...]) + sh_ref[...]
    h2_ref[...] = h2.astype(h2_ref.dtype)
    logits = jnp.dot(h2, wr_ref[...], precision=lax.Precision.HIGHEST, preferred_element_type=F32) + br_ref[...]
    lane = lax.broadcasted_iota(I32, logits.shape, 1)
    v = logits
    vals, idxs = [], []
    for _ in range(TOP_K):
        m = jnp.max(v, axis=-1, keepdims=True)
        idx = jnp.min(jnp.where(v == m, lane, 1 << 20), axis=-1, keepdims=True)
        vals.append(m)
        idxs.append(idx)
        v = jnp.where(lane == idx, -jnp.inf, v)
    es = [jnp.exp(m - vals[0]) for m in vals]
    tot = es[0]
    for e in es[1:]:
        tot = tot + e
    ti = jnp.zeros(logits.shape, I32)
    tw = jnp.zeros(logits.shape, F32)
    for k in range(TOP_K):
        ti = jnp.where(lane == k, idxs[k], ti)
        tw = jnp.where(lane == k, es[k] / tot, tw)
    ti_ref[...] = ti
    tw_ref[...] = tw


def _outproj(x, o_mla, o_nsa, g1, sc, sh, W, *, rows_per_mod, tm):
    t, d = x.shape
    nt = t // tm
    if rows_per_mod == 1:
        tiles_per_mod = (t // g1.shape[0]) // tm
        mod_map = lambda i: (i // tiles_per_mod, 0, 0)
    else:
        mod_map = lambda i: (i, 0, 0)
    row = lambda i: (i, 0)
    c2 = lambda i: (0, 0)
    mod = pl.BlockSpec((None, rows_per_mod, d), mod_map)
    return pl.pallas_call(
        _outproj_kernel, grid=(nt,),
        in_specs=[pl.BlockSpec((tm, d), row), pl.BlockSpec((tm, o_mla.shape[1]), row), pl.BlockSpec((tm, o_nsa.shape[1]), row),
                  mod, mod, mod, pl.BlockSpec(W["w_o_mla"].shape, c2), pl.BlockSpec(W["w_o_nsa"].shape, c2),
                  pl.BlockSpec((1, d), c2), pl.BlockSpec(W["w_router"].shape, c2), pl.BlockSpec((1, LANE), c2)],
        out_specs=[pl.BlockSpec((tm, d), row), pl.BlockSpec((tm, d), row), pl.BlockSpec((tm, LANE), row),
                   pl.BlockSpec((tm, LANE), row)],
        out_shape=[jax.ShapeDtypeStruct((t, d), F32), jax.ShapeDtypeStruct((t, d), F32),
                   jax.ShapeDtypeStruct((t, LANE), I32), jax.ShapeDtypeStruct((t, LANE), F32)],
        compiler_params=_cparams(("arbitrary",)), name="out_proj",
    )(x, o_mla, o_nsa, g1, sc, sh, W["w_o_mla"], W["w_o_nsa"], W["g_norm2"], W["w_router"], W["b_router"])


def _expert_kernel(te_ref, tv_ref, x_ref, wgu_ref, bgu_ref, wd_ref, bd_ref, ws_ref, o_ref, wgu_bf, wd_bf):
    t = pl.program_id(0)

    @pl.when((t == 0) | (te_ref[t] != te_ref[jnp.maximum(t - 1, 0)]))
    def _():
        wgu_bf[...] = wgu_ref[...].astype(BF16)
        wd_bf[...] = wd_ref[...].astype(BF16)

    @pl.when(tv_ref[t] > 0)
    def _():
        d_ff = wd_ref.shape[0]
        gu = _dot(x_ref[...].astype(BF16), wgu_bf[...]) + bgu_ref[...]
        glu = jnp.minimum(gu[:, :d_ff], SWIGLU_LIMIT)
        lin = jnp.clip(gu[:, d_ff:], -SWIGLU_LIMIT, SWIGLU_LIMIT)
        act = glu * _sigmoid(SWIGLU_ALPHA * glu) * (lin + 1.0)
        o_ref[...] = ws_ref[...] * (_dot(act.astype(BF16), wd_bf[...]) + bd_ref[...])

    @pl.when(tv_ref[t] == 0)
    def _():
        o_ref[...] = jnp.zeros_like(o_ref)


def _experts(tile_expert, tile_valid, x_sorted, w_slot, W, *, tm):
    ns, d = x_sorted.shape
    gs = pltpu.PrefetchScalarGridSpec(
        num_scalar_prefetch=2, grid=(ns // tm,),
        in_specs=[pl.BlockSpec((tm, d), lambda t, te, tv: (t, 0)),
                  pl.BlockSpec((None,) + W["w_gate_up"].shape[1:], lambda t, te, tv: (te[t], 0, 0)),
                  pl.BlockSpec((None,) + W["b_gate_up"].shape[1:], lambda t, te, tv: (te[t], 0, 0)),
                  pl.BlockSpec((None,) + W["w_down"].shape[1:], lambda t, te, tv: (te[t], 0, 0)),
                  pl.BlockSpec((None,) + W["b_down"].shape[1:], lambda t, te, tv: (te[t], 0, 0)),
                  pl.BlockSpec((tm, 1), lambda t, te, tv: (t, 0))],
        out_specs=pl.BlockSpec((tm, d), lambda t, te, tv: (t, 0)),
        scratch_shapes=[pltpu.VMEM(W["w_gate_up"].shape[1:], BF16), pltpu.VMEM(W["w_down"].shape[1:], BF16)],
    )
    return pl.pallas_call(
        _expert_kernel, grid_spec=gs, out_shape=jax.ShapeDtypeStruct((ns, d), F32),
        compiler_params=_cparams(("arbitrary",)), name="experts",
    )(tile_expert, tile_valid, x_sorted, W["w_gate_up"], W["b_gate_up"], W["w_down"], W["b_down"], w_slot)


def _final_kernel(x1_ref, *rest):
    parts, (g2_ref, gf_ref, o_ref) = rest[:TOP_K], rest[TOP_K:]
    moe = parts[0][...]
    for p_ref in parts[1:]:
        moe = moe + p_ref[...]
    o_ref[...] = _rms(x1_ref[...] + g2_ref[...] * moe, gf_ref[...])


def _final(x1, moe_parts, g2, g_final, *, rows_per_mod, tm):
    t, d = x1.shape
    if rows_per_mod == 1:
        tiles_per_mod = (t // g2.shape[0]) // tm
        mod_map = lambda i: (i // tiles_per_mod, 0, 0)
    else:
        mod_map = lambda i: (i, 0, 0)
    row = lambda i: (i, 0)
    return pl.pallas_call(
        _final_kernel, grid=(t // tm,),
        in_specs=[pl.BlockSpec((tm, d), row)] * (1 + TOP_K)
        + [pl.BlockSpec((None, rows_per_mod, d), mod_map), pl.BlockSpec((1, d), lambda i: (0, 0))],
        out_specs=pl.BlockSpec((tm, d), row), out_shape=jax.ShapeDtypeStruct((t, d), F32),
        compiler_params=_cparams(("arbitrary",)), name="final_norm")(x1, *moe_parts, g2, g_final)


def _t5_bucket(dist):
    n = jnp.maximum(dist, 0)
    max_exact = NUM_BUCKETS // 2
    nf = jnp.maximum(n, 1).astype(F32)
    large = max_exact + (jnp.log(nf / max_exact) / math.log(MAX_DISTANCE / max_exact)
                         * (NUM_BUCKETS - max_exact)).astype(I32)
    return jnp.where(n < max_exact, n, jnp.minimum(large, NUM_BUCKETS - 1))


def _bias_rows(rel_bias, dist):
    return jnp.moveaxis(rel_bias[_t5_bucket(dist)], -1, 0) * LOG2E


def _rope_table(pos):
    half = MLA_ROPE // 2
    inv_freq = 1.0 / (ROPE_THETA ** (jnp.arange(half, dtype=F32) / half))
    ang = pos.astype(F32)[:, None] * inv_freq[None, :]
    pad = jnp.zeros((pos.shape[0], LANE - MLA_ROPE), F32)
    cos, sin = jnp.cos(ang), jnp.sin(ang)
    return jnp.concatenate([cos, cos, pad, sin, sin, pad], axis=1)


def _rot_cols(w):
    half = MLA_ROPE // 2
    return jnp.concatenate([-w[..., half:], w[..., :half]], axis=-1)


def _pad_last(w, n):
    return jnp.pad(w, [(0, 0)] * (w.ndim - 1) + [(0, n - w.shape[-1])])


def _pack_weights(w_in, g_norm1, g_norm2, g_q_a, w_q_b, g_kv_a, w_kv_b, cmp_pe, cmp_w1, cmp_w2, w_o, w_router,
                  b_router, w_gate_up, b_gate_up, w_down, b_down):
    d = w_in.shape[0]
    sizes = (MLA_Q_LORA, MLA_KV_LORA, MLA_ROPE, NSA_HEADS * HEAD_DIM, KV_COLS, KV_COLS, KV_COLS, 3 * NSA_HEADS)
    offs = [0]
    for s in sizes:
        offs.append(offs[-1] + s)
    w_qa, w_ckv, w_kr, w_qn, w_cmp, w_slc, w_swa, w_g = [w_in[:, offs[k]:offs[k + 1]] for k in range(8)]
    wq = w_qn.reshape(d, NSA_KV_HEADS, NSA_GROUP, HEAD_DIM)
    wq_pad = jnp.concatenate(
        [jnp.pad(wq[:, kv], ((0, 0), (0, 0), (kv * HEAD_DIM, LANE - (kv + 1) * HEAD_DIM))).reshape(d, NSA_GROUP * LANE)
         for kv in range(NSA_KV_HEADS)], axis=1)
    wg = jnp.transpose(w_g.reshape(d, NSA_KV_HEADS, NSA_GROUP, 3), (0, 1, 3, 2)).reshape(d, NSA_KV_HEADS, 3 * NSA_GROUP)
    wg_pad = _pad_last(wg, LANE).reshape(d, NSA_KV_HEADS * LANE)
    w_in_p = jnp.concatenate([w_qa, w_ckv, wq_pad, w_cmp, w_slc, w_swa, _pad_last(w_kr, LANE),
                              _pad_last(_rot_cols(w_kr), LANE), wg_pad], axis=1).astype(BF16)
    assert w_in_p.shape[1] == _C_END
    nope = _pad_last(w_q_b[:, :, :MLA_NOPE], LANE).reshape(MLA_Q_LORA, MLA_HEADS * LANE)
    rp = w_q_b[:, :, MLA_NOPE:]
    w_qb = jnp.concatenate([nope, _pad_last(rp, LANE).reshape(MLA_Q_LORA, -1),
                            _pad_last(_rot_cols(rp), LANE).reshape(MLA_Q_LORA, -1)], axis=1).astype(BF16)
    w_kn = jnp.transpose(w_kv_b[:, :, :MLA_NOPE], (1, 2, 0))
    w_kn = jnp.pad(w_kn, ((0, 0), (0, LANE - MLA_NOPE), (0, 0))).astype(BF16)
    wv = jnp.transpose(w_kv_b[:, :, MLA_NOPE:], (1, 0, 2))
    wv_pad = jnp.stack([jnp.pad(wv[h], ((0, 0), (h * MLA_V, (MLA_HEADS - 1 - h) * MLA_V))) for h in range(MLA_HEADS)]).astype(BF16)
    n_mla = MLA_HEADS * MLA_V
    won = w_o[n_mla:].reshape(NSA_KV_HEADS, NSA_GROUP, HEAD_DIM, d)
    won_pad = jnp.concatenate(
        [jnp.pad(won[kv], ((0, 0), (kv * HEAD_DIM, LANE - (kv + 1) * HEAD_DIM), (0, 0))).reshape(NSA_GROUP * LANE, d)
         for kv in range(NSA_KV_HEADS)], axis=0).astype(BF16)
    base = jnp.concatenate([cmp_w1[:, :CMP_STRIDE], cmp_w1[:, CMP_STRIDE:]], axis=-1)
    z = jnp.zeros_like(base)
    blk = jnp.concatenate([jnp.concatenate([base, z], axis=-1), jnp.concatenate([z, base], axis=-1)], axis=2)
    w_pair = blk.reshape(2, CMP_STRIDE // 2, 2 * NSA_KV_HEADS * HEAD_DIM, NSA_KV_HEADS * 2 * CMP_HIDDEN).astype(BF16)
    pe8 = jnp.broadcast_to(cmp_pe.reshape(2, 1, CMP_LEN * HEAD_DIM), (2, 8, CMP_LEN * HEAD_DIM))
    w1r = cmp_w1.reshape(2, CMP_LEN * HEAD_DIM, CMP_HIDDEN).astype(BF16)
    w2pad = jnp.stack([jnp.stack([jnp.pad(cmp_w2[j], ((0, 0), (kv * HEAD_DIM, LANE - (kv + 1) * HEAD_DIM)))
                                  for kv in range(NSA_KV_HEADS)]) for j in range(2)]).astype(BF16)
    return dict(
        w_in=w_in_p, g_norm1=g_norm1[None], g_norm2=g_norm2[None], g_q_a=g_q_a[None], g_kv_a=g_kv_a[None],
        w_qb=w_qb, w_kn=w_kn, wv=wv_pad, wv_cat=w_kv_b[:, :, MLA_NOPE:].reshape(MLA_KV_LORA, n_mla).astype(BF16),
        w_o_mla=w_o[:n_mla].astype(BF16), w_o_nsa=won_pad,
        w_pair=w_pair, pe8=pe8, w1r=w1r, w2pad=w2pad,
        w_router=_pad_last(w_router, LANE), b_router=jnp.pad(b_router, (0, LANE - N_EXPERTS), constant_values=NEG)[None],
        w_gate_up=w_gate_up, b_gate_up=b_gate_up[:, None, :], w_down=w_down, b_down=b_down[:, None, :])


def _overlap(n_half_pad, n_sel, n_sel_pad):
    c_start = jnp.arange(n_half_pad) * CMP_STRIDE
    s_start = jnp.arange(n_sel_pad) * SEL_BLOCK
    ov = (c_start[:, None] < s_start[None, :] + SEL_BLOCK) & (c_start[:, None] + CMP_LEN > s_start[None, :])
    return (ov & (jnp.arange(n_sel_pad) < n_sel)[None, :]).astype(F32)


def _moe_dispatch(top_i, top_w, tm):
    t = top_i.shape[0]
    a = t * TOP_K
    n_tiles = -(-a // tm) + N_EXPERTS
    ns = n_tiles * tm
    e_flat = top_i[:, :TOP_K].reshape(a)
    w_flat = top_w[:, :TOP_K].reshape(a)
    onehot = (e_flat[:, None] == jnp.arange(N_EXPERTS, dtype=I32)[None, :]).astype(I32)
    csum = jnp.cumsum(onehot, axis=0)
    counts = csum[-1]
    padded = ((counts + tm - 1) // tm) * tm
    pend = jnp.cumsum(padded)
    pstart = pend - padded
    start = jnp.cumsum(counts) - counts
    slot_of_assign = jnp.sum(onehot * (pstart[None, :] + csum - 1), axis=1).reshape(t, TOP_K)
    _, order, w_sorted = lax.sort((e_flat, jnp.arange(a, dtype=I32), w_flat), num_keys=1, is_stable=True)
    tile_start = jnp.arange(n_tiles, dtype=I32) * tm
    tile_expert = jnp.minimum(jnp.sum((pend[None, :] <= tile_start[:, None]).astype(I32), axis=1), N_EXPERTS - 1)
    tile_valid = (tile_start < pend[-1]).astype(I32)
    e_hot = (tile_expert[:, None] == jnp.arange(N_EXPERTS, dtype=I32)[None, :]).astype(I32)
    lane = jnp.arange(tm, dtype=I32)[None, :]
    rank = tile_start[:, None] - jnp.sum(e_hot * pstart[None, :], axis=1, keepdims=True) + lane
    valid = (rank < jnp.sum(e_hot * counts[None, :], axis=1, keepdims=True)) & (tile_valid[:, None] > 0)
    src = jnp.clip(jnp.sum(e_hot * start[None, :], axis=1, keepdims=True) + rank, 0, a - 1).reshape(ns)
    valid = valid.reshape(ns)
    tok_of_slot = jnp.where(valid, order[src] // TOP_K, 0)
    w_of_slot = jnp.where(valid, w_sorted[src], 0.0)
    return tok_of_slot, w_of_slot[:, None], slot_of_assign, tile_expert, tile_valid


def kernel(x_prompt, x_sample, c_prompt, c_sample, cache_mla, cache_nsa_cmp, cache_nsa_slc, state_nsa_swa, page_table, rel_bias, w_ada, b_ada, g_norm1, g_norm2, w_in, g_q_a, w_q_b, g_kv_a, w_kv_b, cmp_pe, cmp_w1, cmp_w2, w_o, w_router, b_router, w_gate_up, b_gate_up, w_down, b_down, g_final):
    depth = w_in.shape[0]
    assert depth == 1, "single-layer decoder step"
    nb, seq, d = x_prompt.shape
    nbd, nq, _ = x_sample.shape
    n_pages = page_table.shape[1]
    page = cache_mla.shape[2]
    past = n_pages * page
    buf = state_nsa_swa.shape[2]
    assert page == LANE and nq <= 8 and seq % 512 == 0 and past % SEL_BLOCK == 0 and seq >= WINDOW
    tp, ts = nb * seq, nbd * nq
    W = _pack_weights(w_in[0], g_norm1[0], g_norm2[0], g_q_a[0], w_q_b[0], g_kv_a[0], w_kv_b[0], cmp_pe[0], cmp_w1[0],
                      cmp_w2[0], w_o[0], w_router[0], b_router[0], w_gate_up[0], b_gate_up[0], w_down[0], b_down[0])

    n_c = nb + nbd
    n_c_pad = -(-n_c // 8) * 8
    c_all = jnp.pad(jnp.concatenate([c_prompt, c_sample], axis=0), ((0, n_c_pad - n_c), (0, 0)))
    mod = _ada_mod(c_all, w_ada[0].astype(BF16), b_ada[0][None])
    mod_p = [m[:, None, :] for m in jnp.split(mod[:nb], 6, axis=-1)]
    tm_s = min(256, ts)
    mod_s = [jnp.repeat(m, nq, axis=0).reshape(ts // tm_s, tm_s, d) for m in jnp.split(mod[nb:n_c], 6, axis=-1)]

    tm_p = 256
    pos_p = jnp.arange(seq)
    pos_s = past + jnp.arange(nq)
    P = _proj(x_prompt.reshape(tp, d), mod_p[1], mod_p[0], _rope_table(pos_p), W, rows_per_mod=1,
              cs_period_tiles=seq // tm_p, tm=tm_p, states_t_batches=nb)
    cs_s = jnp.tile(_rope_table(pos_s), (tm_s // nq, 1))
    S_ = _proj(x_sample.reshape(ts, d), mod_s[1], mod_s[0], cs_s, W, rows_per_mod=tm_s, cs_period_tiles=1, tm=tm_s)
    p_qmla, p_kq, p_mla_t, p_qn, p_cmp, p_cmp_t, p_slc_t, p_swa_t, p_slcb, p_swab, p_gate = P
    s_qmla, s_kq, s_mla, s_qn, _, s_cmp, s_slc, s_swa, s_slcb, s_swab, s_gate = S_

    def pad_new(a):
        return jnp.pad(a.reshape(nbd, nq, a.shape[1]), ((0, 0), (0, LANE - nq), (0, 0)))

    o_mla_p = _mla_prompt(p_qmla, p_kq, W["wv"], nb=nb, seq=seq)
    g_pages = min(16, n_pages)
    cache_mla_t = jnp.swapaxes(cache_mla[0], 1, 2)
    cache_cmp_t = jnp.moveaxis(cache_nsa_cmp[0], 1, -1).reshape(-1, KV_COLS, page)
    cache_slc_t = jnp.moveaxis(cache_nsa_slc[0], 1, -1).reshape(-1, KV_COLS, page)
    state_swa_t = jnp.moveaxis(state_nsa_swa[0], 1, -1).reshape(nbd, KV_COLS, buf)
    o_mla_s = _mla_sample(page_table, s_qmla.reshape(nbd, nq, -1), pad_new(s_kq), W["wv_cat"], cache_mla_t, g=g_pages,
                          nbs=2 if nbd % 2 == 0 else 1)

    assert (past + nq) // CMP_STRIDE == past // CMP_STRIDE
    n_pp = tp // LANE
    g_cp = min(32, n_pp)
    ab_p = _cmp_first(jnp.arange(n_pp, dtype=I32), p_cmp.reshape(n_pp, LANE, KV_COLS), W["w_pair"], g=g_cp,
                      transposed=False)
    kc_p, vc_p = _cmp_finish(ab_p, W["pe8"], W["w1r"], W["w2pad"], nb=nb)
    g_cs = min(32, nbd * n_pages)
    ab_s = _cmp_first(page_table.reshape(nbd * n_pages), cache_cmp_t, W["w_pair"], g=g_cs, transposed=True)
    kc_s, vc_s = _cmp_finish(ab_s, W["pe8"], W["w1r"], W["w2pad"], nb=nbd)

    nh_p = seq // CMP_STRIDE
    n_sel_p = -(-seq // SEL_BLOCK)
    tq = 128
    far_h = _bias_rows(rel_bias, jnp.full((1, 1), MAX_DISTANCE))
    d_win = (jnp.arange(tq)[:, None] - CMP_STRIDE * (jnp.arange(LANE)[None, :] - CMP_WIN_LO) - (CMP_LEN - 1))
    assert CMP_WIN_LO * CMP_STRIDE >= MAX_DISTANCE + CMP_LEN and LANE - CMP_WIN_LO >= tq // CMP_STRIDE
    bias_c_p = jnp.where((d_win >= 0)[None], _bias_rows(rel_bias, d_win) - far_h, 0.0)
    bias_c_p = bias_c_p.reshape(NSA_HEADS * tq, LANE)
    di = jnp.arange(tq)[:, None] - jnp.arange(tq)[None, :]
    tt = jnp.stack([_bias_rows(rel_bias, di), _bias_rows(rel_bias, di + tq),
                    _bias_rows(rel_bias, jnp.full((tq, tq), MAX_DISTANCE))], axis=1)
    tt = tt.transpose(1, 0, 2, 3).reshape(3, NSA_HEADS * tq, tq)
    assert n_sel_p <= LANE
    ovl_p = jnp.tile(_overlap(nh_p, n_sel_p, LANE).T, (1, 3)).astype(BF16)
    o_nsa_p = _nsa_prompt(p_qn, kc_p, vc_p, bias_c_p, ovl_p, p_slcb.reshape(nb, seq, KV_COLS),
                          p_swab.reshape(nb, seq, KV_COLS), tt, p_gate, nb=nb, seq=seq, n_cmp=nh_p - 1, n_sel=n_sel_p)

    nh_s = past // CMP_STRIDE
    n_sel_s = -(-(past + nq) // SEL_BLOCK)
    assert n_sel_s == past // SEL_BLOCK + 1
    n_sel_pad = -(-n_sel_s // LANE) * LANE
    rows_s = NSA_HEADS * nq
    cmp_end_s = jnp.arange(nh_s) * CMP_STRIDE + CMP_LEN - 1
    bias_c_s = _bias_rows(rel_bias, pos_s[:, None] - cmp_end_s[None, :]).reshape(rows_s, nh_s)
    nk_step = g_pages * page
    far = jnp.broadcast_to(_bias_rows(rel_bias, jnp.full((nq, 1), MAX_DISTANCE)).reshape(rows_s, 1), (rows_s, nk_step))
    tail_pos = past - nk_step + jnp.arange(nk_step)
    bias_s = jnp.stack([far, _bias_rows(rel_bias, pos_s[:, None] - tail_pos[None, :]).reshape(rows_s, nk_step)])
    new_pos = past + jnp.arange(LANE)
    bias_new = _bias_rows(rel_bias, pos_s[:, None] - new_pos[None, :]).reshape(rows_s, LANE)
    swa_pos = past - buf + jnp.arange(buf)
    bias_w = _bias_rows(rel_bias, pos_s[:, None] - swa_pos[None, :]).reshape(rows_s, buf)
    ovl_s = jnp.tile(_overlap(nh_s, n_sel_s, n_sel_pad).T, (1, 3)).astype(BF16)
    o_nsa_s = _nsa_sample(page_table, s_qn.reshape(nbd, nq, -1), kc_s, vc_s, bias_c_s, ovl_s, bias_s, pad_new(s_slcb),
                          bias_new, state_swa_t, pad_new(s_swab), bias_w, bias_new,
                          s_gate.reshape(nbd, nq, -1), cache_slc_t, g=g_pages, nbs=2 if nbd % 2 == 0 else 1,
                          n_cmp=nh_s - 1, n_sel=n_sel_s)

    x1_p, h2_p, ti_p, tw_p = _outproj(x_prompt.reshape(tp, d), o_mla_p, o_nsa_p, mod_p[2], mod_p[4], mod_p[3], W,
                                      rows_per_mod=1, tm=tm_p)
    x1_s, h2_s, ti_s, tw_s = _outproj(x_sample.reshape(ts, d), o_mla_s.reshape(ts, -1), o_nsa_s.reshape(ts, -1),
                                      mod_s[2], mod_s[4], mod_s[3], W, rows_per_mod=tm_s, tm=tm_s)

    tm_e = 256
    h2 = jnp.concatenate([h2_p, h2_s], axis=0)
    tok_of_slot, w_slot, slot_of_assign, tile_expert, tile_valid = _moe_dispatch(
        jnp.concatenate([ti_p, ti_s], axis=0), jnp.concatenate([tw_p, tw_s], axis=0), tm_e)
    y_sorted = _experts(tile_expert, tile_valid, h2[tok_of_slot], w_slot, W, tm=tm_e)
    parts_p = [y_sorted[slot_of_assign[:tp, k]] for k in range(TOP_K)]
    parts_s = [y_sorted[slot_of_assign[tp:, k]] for k in range(TOP_K)]
    y_p = _final(x1_p, parts_p, mod_p[5], g_final[None], rows_per_mod=1, tm=tm_p)
    y_s = _final(x1_s, parts_s, mod_s[5], g_final[None], rows_per_mod=tm_s, tm=tm_s)

    kv_tail = (2, NSA_KV_HEADS, HEAD_DIM)
    keep_p = min(WINDOW, seq)
    swa_keys = jnp.concatenate([state_nsa_swa[0], s_swa.reshape((nbd, nq) + kv_tail)], axis=1)
    keep_s = min(WINDOW, buf + nq)

    def rows_last(a_t):
        return jnp.moveaxis(a_t.reshape((nb,) + kv_tail + (a_t.shape[-1],)), -1, 1)[None]

    return (y_p.reshape(nb, seq, d), y_s.reshape(nbd, nq, d),
            jnp.swapaxes(p_mla_t, 1, 2)[None], s_mla.reshape(1, nbd, nq, -1),
            rows_last(p_cmp_t), s_cmp.reshape((1, nbd, nq) + kv_tail),
            rows_last(p_slc_t), s_slc.reshape((1, nbd, nq) + kv_tail),
            rows_last(p_swa_t[:, :, seq - keep_p:]),
            swa_keys[None, :, buf + nq - keep_s:])
```

```python
import functools
import math

import jax
import jax.numpy as jnp
from jax import lax
from jax.experimental import pallas as pl
from jax.experimental.pallas import tpu as pltpu

F32, BF16, I32 = jnp.float32, jnp.bfloat16, jnp.int32

MLA_HEADS = 8
MLA_Q_LORA = 384
MLA_KV_LORA = 256
MLA_NOPE = 64
MLA_ROPE = 32
MLA_V = 64
NSA_HEADS = 8
NSA_KV_HEADS = 2
NSA_GROUP = NSA_HEADS // NSA_KV_HEADS
HEAD_DIM = 64
CMP_LEN = 32
CMP_STRIDE = 16
CMP_HIDDEN = 128
SEL_BLOCK = 64
SEL_TOPN = 16
WINDOW = 512
KV_COLS = 2 * NSA_KV_HEADS * HEAD_DIM
N_EXPERTS = 32
TOP_K = 4
SWIGLU_LIMIT = 7.0
SWIGLU_ALPHA = 1.702
NUM_BUCKETS = 32
MAX_DISTANCE = 128
ROPE_THETA = 10000.0
NORM_EPS = 1e-6

LANE = 128
VMEM_LIMIT = 56 * 1024 * 1024
NEG = -1e30
CMP_WIN_LO = 16

MLA_QW = MLA_KV_LORA + LANE
MLA_SCALE = (MLA_NOPE + MLA_ROPE) ** -0.5
MLA_QSCALE = MLA_SCALE * math.log2(math.e)
LOG2E = math.log2(math.e)
NSA_QSCALE = HEAD_DIM ** -0.5 * LOG2E

_C_QA = 0
_C_CKV = _C_QA + MLA_Q_LORA
_C_QN = _C_CKV + MLA_KV_LORA
_C_CMP = _C_QN + NSA_HEADS * LANE
_C_SLC = _C_CMP + KV_COLS
_C_SWA = _C_SLC + KV_COLS
_C_KR = _C_SWA + KV_COLS
_C_KRR = _C_KR + LANE
_C_G = _C_KRR + LANE
_C_END = _C_G + NSA_KV_HEADS * LANE


def _cparams(sem, vmem=VMEM_LIMIT):
    return pltpu.CompilerParams(dimension_semantics=sem, vmem_limit_bytes=vmem)


def _dot(a, b):
    return jnp.dot(a, b, preferred_element_type=F32)


def _dot_nt(a, b):
    return lax.dot_general(a, b, (((1,), (1,)), ((), ())), preferred_element_type=F32)


def _row_reduce(x, op, reduce):
    n = x.shape[1] // LANE
    if x.shape[1] % LANE or n <= 1:
        return reduce(x, axis=-1, keepdims=True)
    t = x[:, :LANE]
    for c in range(1, n):
        t = op(t, x[:, c * LANE:(c + 1) * LANE])
    return reduce(t, axis=-1, keepdims=True)


def _rowmax(x):
    return _row_reduce(x, jnp.maximum, jnp.max)


def _rowsum(x):
    return _row_reduce(x, jnp.add, jnp.sum)


def _split3(x):
    hi = x.astype(BF16)
    r1 = x - hi.astype(F32)
    mid = r1.astype(BF16)
    lo = (r1 - mid.astype(F32)).astype(BF16)
    return jnp.concatenate([hi, mid, lo], axis=-1)


def _rms(x, g):
    return x * lax.rsqrt(jnp.mean(x * x, axis=-1, keepdims=True) + NORM_EPS) * g


def _sigmoid(x):
    return 1.0 / (1.0 + jnp.exp(-x))


def _ada_kernel(c_ref, w_ref, b_ref, o_ref):
    c = c_ref[...]
    o_ref[...] = _dot((c * _sigmoid(c)).astype(BF16), w_ref[...]) + b_ref[...]


def _ada_mod(c_all, w_ada, b_ada):
    m, d = c_all.shape
    n = w_ada.shape[1]
    tn = 1536
    return pl.pallas_call(
        _ada_kernel, grid=(n // tn,),
        in_specs=[pl.BlockSpec((m, d), lambda i: (0, 0)), pl.BlockSpec((d, tn), lambda i: (0, i)),
                  pl.BlockSpec((1, tn), lambda i: (0, i))],
        out_specs=pl.BlockSpec((m, tn), lambda i: (0, i)),
        out_shape=jax.ShapeDtypeStruct((m, n), F32),
        compiler_params=_cparams(("arbitrary",)), name="ada_mod")(c_all, w_ada, b_ada)


def _proj_kernel(x_ref, sc_ref, sh_ref, g1_ref, cs_ref, win_ref, gqa_ref, wqb_ref, gkva_ref, wkn_ref,
                 qmla_ref, kq_ref, mla_ref, qn_ref, cmp_ref, cmps_ref, slc_ref, swa_ref, slcb_ref, swab_ref, gate_ref,
                 *, states_t):
    h = _rms(x_ref[...], g1_ref[...]) * (1.0 + sc_ref[...]) + sh_ref[...]
    proj = _dot(h.astype(BF16), win_ref[...])
    cs = cs_ref[...]
    cosp, sinp = cs[:, :LANE], cs[:, LANE:]
    qa = _rms(proj[:, _C_QA:_C_CKV], gqa_ref[...])
    q = _dot(qa.astype(BF16), wqb_ref[...])
    hw = MLA_HEADS * LANE
    for h_i in range(MLA_HEADS):
        lo = h_i * LANE
        qlat = _dot(q[:, lo:lo + LANE].astype(BF16), wkn_ref[h_i]) * MLA_QSCALE
        qr = (q[:, hw + lo:hw + lo + LANE] * cosp + q[:, 2 * hw + lo:2 * hw + lo + LANE] * sinp) * MLA_QSCALE
        qmla_ref[:, h_i * MLA_QW:h_i * MLA_QW + MLA_KV_LORA] = qlat.astype(BF16)
        qmla_ref[:, h_i * MLA_QW + MLA_KV_LORA:(h_i + 1) * MLA_QW] = qr.astype(BF16)
    ckv = _rms(proj[:, _C_CKV:_C_QN], gkva_ref[...])
    kr = proj[:, _C_KR:_C_KRR] * cosp + proj[:, _C_KRR:_C_G] * sinp
    kq_ref[:, :MLA_KV_LORA] = ckv.astype(BF16)
    kq_ref[:, MLA_KV_LORA:] = kr.astype(BF16)
    qn_ref[...] = (proj[:, _C_QN:_C_CMP] * NSA_QSCALE).astype(BF16)
    cmp = proj[:, _C_CMP:_C_SLC]
    slc = proj[:, _C_SLC:_C_SWA]
    swa = proj[:, _C_SWA:_C_KR]
    cmp_ref[...] = cmp
    if states_t:
        mla_ref[:MLA_KV_LORA, :] = ckv.T
        mla_ref[MLA_KV_LORA:, :] = kr.T[:MLA_ROPE]
        cmps_ref[...] = cmp.T
        slc_ref[...] = slc.T
        swa_ref[...] = swa.T
    else:
        mla_ref[:, :MLA_KV_LORA] = ckv
        mla_ref[:, MLA_KV_LORA:] = kr[:, :MLA_ROPE]
        cmps_ref[...] = cmp
        slc_ref[...] = slc
        swa_ref[...] = swa
    slcb_ref[...] = slc.astype(BF16)
    swab_ref[...] = swa.astype(BF16)
    gate_ref[...] = _sigmoid(proj[:, _C_G:_C_END])


def _proj(x, sc, sh, cs, W, *, rows_per_mod, cs_period_tiles, tm, states_t_batches=0):
    t, d = x.shape
    nt = t // tm
    if rows_per_mod == 1:
        tiles_per_mod = sc.shape[0] and (t // sc.shape[0]) // tm
        mod_map = lambda i: (i // tiles_per_mod, 0, 0)
    else:
        mod_map = lambda i: (i, 0, 0)
    cs_map = (lambda i: (i % cs_period_tiles, 0)) if cs_period_tiles > 1 else (lambda i: (0, 0))
    const2 = lambda i: (0, 0)
    row = lambda i: (i, 0)
    outs = [
        (MLA_HEADS * MLA_QW, BF16), (MLA_QW, BF16), (MLA_KV_LORA + MLA_ROPE, F32), (NSA_HEADS * LANE, BF16),
        (KV_COLS, F32), (KV_COLS, F32), (KV_COLS, F32), (KV_COLS, F32), (KV_COLS, BF16), (KV_COLS, BF16),
        (NSA_KV_HEADS * LANE, F32),
    ]
    state_outs = (2, 5, 6, 7) if states_t_batches else ()
    tiles_per_batch = (t // states_t_batches) // tm if states_t_batches else 0
    out_specs = [pl.BlockSpec((None, w, tm), lambda i: (i // tiles_per_batch, 0, i % tiles_per_batch))
                 if k in state_outs else pl.BlockSpec((tm, w), row) for k, (w, _) in enumerate(outs)]
    out_shape = [jax.ShapeDtypeStruct((states_t_batches, w, t // states_t_batches) if k in state_outs else (t, w), dt)
                 for k, (w, dt) in enumerate(outs)]
    return pl.pallas_call(
        functools.partial(_proj_kernel, states_t=bool(states_t_batches)), grid=(nt,),
        in_specs=[
            pl.BlockSpec((tm, d), row),
            pl.BlockSpec((None, rows_per_mod, d), mod_map),
            pl.BlockSpec((None, rows_per_mod, d), mod_map),
            pl.BlockSpec((1, d), const2),
            pl.BlockSpec((tm, 2 * LANE), cs_map),
            pl.BlockSpec(W["w_in"].shape, const2),
            pl.BlockSpec((1, MLA_Q_LORA), const2),
            pl.BlockSpec(W["w_qb"].shape, const2),
            pl.BlockSpec((1, MLA_KV_LORA), const2),
            pl.BlockSpec(W["w_kn"].shape, lambda i: (0, 0, 0)),
        ],
        out_specs=out_specs, out_shape=out_shape,
        compiler_params=_cparams(("arbitrary",)), name="proj_in",
    )(x, sc, sh, W["g_norm1"], cs, W["w_in"], W["g_q_a"], W["w_qb"], W["g_kv_a"], W["w_kn"])


def _mla_prompt_kernel(q_ref, k_ref, wv_ref, o_ref, m_ref, l_ref, acc_ref, *, tq, tk):
    i, j = pl.program_id(1), pl.program_id(2)
    nk = pl.num_programs(2)

    @pl.when(j == 0)
    def _():
        m_ref[...] = jnp.full_like(m_ref, NEG)
        l_ref[...] = jnp.zeros_like(l_ref)
        acc_ref[...] = jnp.zeros_like(acc_ref)

    def step(masked):
        k = k_ref[...]
        v = k[:, :MLA_KV_LORA]
        if masked:
            qpos = i * tq + lax.broadcasted_iota(I32, (tq, tk), 0)
            kpos = j * tk + lax.broadcasted_iota(I32, (tq, tk), 1)
            mask = kpos <= qpos
        ss = [_dot_nt(q_ref[:, h * MLA_QW:(h + 1) * MLA_QW], k) for h in range(MLA_HEADS)]
        ps, alphas = [], []
        for h in range(MLA_HEADS):
            r = slice(h * tq, (h + 1) * tq)
            s = jnp.where(mask, ss[h], NEG) if masked else ss[h]
            m_prev = m_ref[r]
            m_new = jnp.maximum(m_prev, _rowmax(s))
            p = jnp.exp2(s - m_new)
            if masked:
                p = jnp.where(mask, p, 0.0)
            alpha = jnp.exp2(m_prev - m_new)
            l_ref[r] = alpha * l_ref[r] + _rowsum(p)
            m_ref[r] = m_new
            ps.append(p.astype(BF16))
            alphas.append(alpha)
        for h in range(MLA_HEADS):
            r = slice(h * tq, (h + 1) * tq)
            acc_ref[r] = alphas[h] * acc_ref[r] + _dot(ps[h], v)

    first_key, last_key = j * tk, j * tk + tk - 1
    pl.when(last_key <= i * tq)(functools.partial(step, False))
    pl.when((first_key <= i * tq + tq - 1) & (last_key > i * tq))(functools.partial(step, True))

    @pl.when(j == nk - 1)
    def _():
        out = jnp.zeros(o_ref.shape, F32)
        for h in range(MLA_HEADS):
            r = slice(h * tq, (h + 1) * tq)
            l = l_ref[r]
            o_lat = acc_ref[r] / jnp.where(l > 0.0, l, 1.0)
            out = out + _dot(o_lat.astype(BF16), wv_ref[h])
        o_ref[...] = out.astype(o_ref.dtype)


def _mla_prompt(qmla, kq, wv, *, nb, seq, tq=256, tk=1024):
    tk = min(tk, seq)
    if seq % tk:
        tk = 512
    nq, nk = seq // tq, seq // tk
    ow = MLA_HEADS * MLA_V

    def k_map(b, i, j):
        return (b * nk + jnp.minimum(j, (i * tq + tq - 1) // tk), 0)

    return pl.pallas_call(
        functools.partial(_mla_prompt_kernel, tq=tq, tk=tk), grid=(nb, nq, nk),
        in_specs=[pl.BlockSpec((tq, MLA_HEADS * MLA_QW), lambda b, i, j: (b * nq + i, 0)),
                  pl.BlockSpec((tk, MLA_QW), k_map),
                  pl.BlockSpec(wv.shape, lambda b, i, j: (0, 0, 0))],
        out_specs=pl.BlockSpec((tq, ow), lambda b, i, j: (b * nq + i, 0)),
        out_shape=jax.ShapeDtypeStruct((nb * seq, ow), BF16),
        scratch_shapes=[pltpu.VMEM((MLA_HEADS * tq, 1), F32), pltpu.VMEM((MLA_HEADS * tq, 1), F32),
                        pltpu.VMEM((MLA_HEADS * tq, MLA_KV_LORA), F32)],
        compiler_params=_cparams(("arbitrary", "arbitrary", "arbitrary")), name="mla_prompt",
    )(qmla, kq, wv)


def _mla_sample_kernel(pt_ref, q_ref, kn_ref, wv_ref, *rest, g, nq, nbs):
    pages = rest[:nbs * g]
    o_ref, m_ref, l_ref, acc_ref = rest[nbs * g:]
    s_idx = pl.program_id(1)
    rows = MLA_HEADS * nq

    @pl.when(s_idx == 0)
    def _():
        m_ref[...] = jnp.full_like(m_ref, NEG)
        l_ref[...] = jnp.zeros_like(l_ref)
        acc_ref[...] = jnp.zeros_like(acc_ref)

    qs = []
    for bl in range(nbs):
        r = slice(bl * rows, (bl + 1) * rows)
        q = jnp.concatenate([q_ref[bl, :, h * MLA_QW:(h + 1) * MLA_QW] for h in range(MLA_HEADS)], axis=0)
        qs.append(q)
        qlat, qr = q[:, :MLA_KV_LORA], q[:, MLA_KV_LORA:MLA_KV_LORA + MLA_ROPE]
        pg = pages[bl * g:(bl + 1) * g]
        ckv_t = jnp.concatenate([pg[u][:MLA_KV_LORA].astype(BF16) for u in range(g)], axis=1)
        kr_t = jnp.concatenate([pg[u][MLA_KV_LORA:].astype(BF16) for u in range(g)], axis=1)
        s = _dot(qlat, ckv_t) + _dot(qr, kr_t)
        m_prev = m_ref[r]
        m_new = jnp.maximum(m_prev, _rowmax(s))
        p = jnp.exp2(s - m_new)
        alpha = jnp.exp2(m_prev - m_new)
        l_ref[r] = alpha * l_ref[r] + _rowsum(p)
        acc_ref[r] = alpha * acc_ref[r] + _dot_nt(p.astype(BF16), ckv_t)
        m_ref[r] = m_new

    @pl.when(s_idx == pl.num_programs(1) - 1)
    def _():
        for bl in range(nbs):
            r = slice(bl * rows, (bl + 1) * rows)
            kn = kn_ref[bl]
            sn = _dot_nt(qs[bl], kn)
            qi = lax.broadcasted_iota(I32, sn.shape, 0) % nq
            kt = lax.broadcasted_iota(I32, sn.shape, 1)
            mask = kt <= qi
            sn = jnp.where(mask, sn, NEG)
            m_prev = m_ref[r]
            m_new = jnp.maximum(m_prev, jnp.max(sn, axis=-1, keepdims=True))
            p = jnp.where(mask, jnp.exp2(sn - m_new), 0.0)
            alpha = jnp.exp2(m_prev - m_new)
            l = alpha * l_ref[r] + _rowsum(p)
            acc = alpha * acc_ref[r] + _dot(p.astype(BF16), kn[:, :MLA_KV_LORA])
            o_lat = (acc / jnp.where(l > 0.0, l, 1.0)).astype(BF16)
            res = _dot(o_lat, wv_ref[...])
            head_of_lane = lax.broadcasted_iota(I32, (nq, res.shape[1]), 1) // MLA_V
            out = jnp.zeros((nq, res.shape[1]), F32)
            for h in range(MLA_HEADS):
                out = out + jnp.where(head_of_lane == h, res[h * nq:(h + 1) * nq], 0.0)
            o_ref[bl] = out.astype(o_ref.dtype)


def _page_specs(g, block, n_lane_blocks=1, flat=False, nbs=1):
    def index(b, s, pt, bl, u, c):
        return ((pt[s * g + u] if flat else pt[b * nbs + bl, s * g + u]), 0, c)

    return [pl.BlockSpec(block, functools.partial(index, bl=bl, u=u, c=c))
            for bl in range(nbs) for u in range(g) for c in range(n_lane_blocks)]


def _mla_sample(page_table, qmla, kq_new_pad, wv, cache_mla, *, g, nbs):
    nb, nq = qmla.shape[0], qmla.shape[1]
    n_pages = page_table.shape[1]
    ow = MLA_HEADS * MLA_V
    rows = nbs * MLA_HEADS * nq
    bmap = lambda b, s, pt: (b, 0, 0)
    gs = pltpu.PrefetchScalarGridSpec(
        num_scalar_prefetch=1, grid=(nb // nbs, n_pages // g),
        in_specs=[pl.BlockSpec((nbs, nq, MLA_HEADS * MLA_QW), bmap),
                  pl.BlockSpec((nbs, LANE, MLA_QW), bmap),
                  pl.BlockSpec(wv.shape, lambda b, s, pt: (0, 0))]
        + _page_specs(g, (None,) + cache_mla.shape[1:], nbs=nbs),
        out_specs=pl.BlockSpec((nbs, nq, ow), bmap),
        scratch_shapes=[pltpu.VMEM((rows, 1), F32), pltpu.VMEM((rows, 1), F32), pltpu.VMEM((rows, MLA_KV_LORA), F32)],
    )
    return pl.pallas_call(
        functools.partial(_mla_sample_kernel, g=g, nq=nq, nbs=nbs), grid_spec=gs,
        out_shape=jax.ShapeDtypeStruct((nb, nq, ow), BF16),
        compiler_params=_cparams(("arbitrary", "arbitrary")), name="mla_sample",
    )(page_table, qmla, kq_new_pad, wv, *([cache_mla] * (nbs * g)))


def _cmp_first_kernel(pl_ref, w_ref, *rest, g, transposed):
    pages = rest[:g]
    o_ref, xs_ref = rest[g:]
    half_w = NSA_KV_HEADS * 2 * CMP_HIDDEN
    n_half = o_ref.shape[0]
    for u in range(g):
        x = pages[u][...]
        for j in range(2):
            if transposed:
                xs_ref[j, u * LANE:(u + 1) * LANE, :] = x[j * LANE:(j + 1) * LANE, :].T
            else:
                xs_ref[j, u * LANE:(u + 1) * LANE, :] = x[:, j * LANE:(j + 1) * LANE]
    for j in range(2):
        acc = jnp.zeros((n_half, half_w), F32)
        for pp in range(CMP_STRIDE // 2):
            xa = xs_ref[j, pl.ds(2 * pp, n_half, stride=CMP_STRIDE), :]
            xb = xs_ref[j, pl.ds(2 * pp + 1, n_half, stride=CMP_STRIDE), :]
            acc = acc + _dot(jnp.concatenate([xa, xb], axis=1).astype(BF16), w_ref[j, pp])
        o_ref[:, j * half_w:(j + 1) * half_w] = acc


def _cmp_first(page_list, pool, w_pair, *, g, transposed):
    n = page_list.shape[0]
    halves = LANE // CMP_STRIDE
    ow = 2 * NSA_KV_HEADS * 2 * CMP_HIDDEN
    gs = pltpu.PrefetchScalarGridSpec(
        num_scalar_prefetch=1, grid=(1, n // g),
        in_specs=[pl.BlockSpec(w_pair.shape, lambda b, s, pt: (0, 0, 0, 0))]
        + _page_specs(g, (None,) + pool.shape[1:], flat=True),
        out_specs=pl.BlockSpec((g * halves, ow), lambda b, s, pt: (s, 0)),
        scratch_shapes=[pltpu.VMEM((2, g * LANE, LANE), F32)],
    )
    return pl.pallas_call(
        functools.partial(_cmp_first_kernel, g=g, transposed=transposed), grid_spec=gs,
        out_shape=jax.ShapeDtypeStruct((n * halves, ow), F32),
        compiler_params=_cparams(("arbitrary", "arbitrary")), name="cmp_first",
    )(page_list, w_pair, *([pool] * g))


def _cmp_finish_kernel(ab_ref, pe_ref, w1_ref, w2_ref, kc_ref, vc_ref):
    nh = ab_ref.shape[0]
    outs = []
    for j in range(2):
        pe_term = _dot(pe_ref[j].astype(BF16), w1_ref[j])[0:1]
        acc = jnp.zeros((nh, LANE), F32)
        for kv in range(NSA_KV_HEADS):
            base = (j * NSA_KV_HEADS + kv) * 2 * CMP_HIDDEN
            first = ab_ref[:, base:base + CMP_HIDDEN]
            second = ab_ref[:, base + CMP_HIDDEN:base + 2 * CMP_HIDDEN]
            hid = first + pltpu.roll(second, nh - 1, 0) + pe_term
            hid = 0.5 * hid * (1.0 + lax.erf(hid * math.sqrt(0.5)))
            acc = acc + _dot(hid.astype(BF16), w2_ref[j, kv])
        outs.append(acc)
    kc_ref[...] = outs[0].astype(kc_ref.dtype)
    vc_ref[...] = outs[1].astype(vc_ref.dtype)


def _cmp_finish(ab, pe8, w1r, w2pad, *, nb):
    nh = ab.shape[0] // nb
    c3 = lambda b: (0, 0, 0)
    return pl.pallas_call(
        _cmp_finish_kernel, grid=(nb,),
        in_specs=[pl.BlockSpec((nh, ab.shape[1]), lambda b: (b, 0)), pl.BlockSpec(pe8.shape, c3),
                  pl.BlockSpec(w1r.shape, c3), pl.BlockSpec(w2pad.shape, lambda b: (0, 0, 0, 0))],
        out_specs=[pl.BlockSpec((None, nh, LANE), lambda b: (b, 0, 0))] * 2,
        out_shape=[jax.ShapeDtypeStruct((nb, nh, LANE), BF16)] * 2,
        compiler_params=_cparams(("arbitrary",)), name="cmp_finish")(ab, pe8, w1r, w2pad)


def _topk_mask(v, k, axis=1):
    lane = lax.broadcasted_iota(I32, v.shape, axis)
    sel = jnp.zeros(v.shape, F32)
    for _ in range(k):
        m = jnp.max(v, axis=axis, keepdims=True)
        idx = jnp.min(jnp.where(v == m, lane, 1 << 20), axis=axis, keepdims=True)
        pick = lane == idx
        sel = jnp.where(pick & (m > -jnp.inf), 1.0, sel)
        v = jnp.where(pick, -jnp.inf, v)
    return sel


def _softmax_masked(s, mask):
    s = jnp.where(mask, s, NEG)
    m = _rowmax(s)
    p = jnp.where(mask, jnp.exp2(s - m), 0.0)
    l = _rowsum(p)
    return p / jnp.where(l > 0.0, l, 1.0)


def _flash_step(s, mask, v, m_ref, l_ref, acc_ref, v_keys_on_lanes=False):
    s = jnp.where(mask, s, NEG)
    m_prev = m_ref[...]
    m_new = jnp.maximum(m_prev, _rowmax(s))
    p = jnp.where(mask, jnp.exp2(s - m_new), 0.0)
    alpha = jnp.exp2(m_prev - m_new)
    l_ref[...] = alpha * l_ref[...] + _rowsum(p)
    pv = _dot_nt(p.astype(BF16), v) if v_keys_on_lanes else _dot(p.astype(BF16), v)
    acc_ref[...] = alpha * acc_ref[...] + pv
    m_ref[...] = m_new


def _flash_init(m_ref, l_ref, acc_ref):
    m_ref[...] = jnp.full_like(m_ref, NEG)
    l_ref[...] = jnp.zeros_like(l_ref)
    acc_ref[...] = jnp.zeros_like(acc_ref)


def _flash_out(l_ref, acc_ref):
    l = l_ref[...]
    return acc_ref[...] / jnp.where(l > 0.0, l, 1.0)


def _nsa_prompt_kernel(q_ref, kc_ref, vc_ref, bc_ref, ovl_ref, slc_ref, swa_ref, tt_ref, gate_ref, o_ref,
                       m_ref, l_ref, acc_ref, *, tq, tkf, n_cmp, n_sel, seq_len):
    i = pl.program_id(1)
    rows = NSA_HEADS * tq
    n_sel_rows = -(-n_sel // 8) * 8
    q = jnp.concatenate([q_ref[:, h * LANE:(h + 1) * LANE] for h in range(NSA_HEADS)], axis=0)
    nhp = kc_ref.shape[0]
    pos_r = i * tq + lax.broadcasted_iota(I32, (rows, 1), 0) % tq

    wu = lax.broadcasted_iota(I32, (LANE, nhp), 0)
    wn = lax.broadcasted_iota(I32, (LANE, nhp), 1)
    shift = jnp.where(wn == i * (tq // CMP_STRIDE) - CMP_WIN_LO + wu, 1.0, 0.0).astype(BF16)
    far_col = tt_ref[2][:, :1]
    bias_c = _dot(_split3(bc_ref[...]), jnp.concatenate([shift] * 3, axis=0)) + far_col
    s = _dot_nt(q, kc_ref[...]) + bias_c
    n_idx = lax.broadcasted_iota(I32, (rows, nhp), 1)
    mask_c = (n_idx * CMP_STRIDE + CMP_LEN - 1 <= pos_r) & (n_idx < n_cmp)
    p_cmp = _softmax_masked(s, mask_c)
    o_cmp = _dot(p_cmp.astype(BF16), vc_ref[...])

    imps = []
    for kv in range(NSA_KV_HEADS):
        lo = kv * NSA_GROUP * tq
        psum = p_cmp[lo:lo + tq]
        for g in range(1, NSA_GROUP):
            psum = psum + p_cmp[lo + g * tq:lo + (g + 1) * tq]
        imps.append(_dot_nt(ovl_ref[...], _split3(psum))[:n_sel_rows])
    imp_t = jnp.concatenate(imps, axis=1)
    blk = lax.broadcasted_iota(I32, imp_t.shape, 0)
    cur = (i * tq + lax.broadcasted_iota(I32, imp_t.shape, 1) % tq) // SEL_BLOCK
    forced = (blk == 0) | (blk == cur) | (blk == cur - 1)
    future = (blk > cur) | (blk >= n_sel)
    imp_t = jnp.where(future, -jnp.inf, jnp.where(forced, jnp.inf, imp_t))
    sel_t = _topk_mask(imp_t, min(SEL_TOPN, n_sel), axis=0)
    if n_sel_rows < LANE:
        sel_t = jnp.concatenate([sel_t, jnp.zeros((LANE - n_sel_rows, NSA_KV_HEADS * tq), F32)], axis=0)
    sels = [sel_t[:, kv * tq:(kv + 1) * tq].T.astype(BF16) for kv in range(NSA_KV_HEADS)]

    near0 = pl.multiple_of(jnp.maximum(i - 1, 0) * tq, tq)
    near_bias = jnp.concatenate([tt_ref[jnp.where(i == 0, 0, 1)], tt_ref[0]], axis=1)
    pos_q = i * tq + lax.broadcasted_iota(I32, (tq, 1), 0)
    d_near = pos_q - (near0 + lax.broadcasted_iota(I32, (tq, 2 * tq), 1))

    def heads(pens):
        return jnp.concatenate([p_ for p_ in pens for _ in range(NSA_GROUP)], axis=0)

    def block_sel(first_key, n_keys, key_limit):
        sb = lax.broadcasted_iota(I32, (LANE, n_keys), 0)
        kt = lax.broadcasted_iota(I32, (LANE, n_keys), 1)
        hit = (sb == first_key // SEL_BLOCK + kt // SEL_BLOCK) & (first_key + kt < key_limit)
        expand = jnp.where(hit, 1.0, 0.0).astype(BF16)
        return [_dot(sel, expand) > 0.5 for sel in sels]

    n_far_w = WINDOW - tq
    far0 = pl.multiple_of(jnp.maximum(i - WINDOW // tq, 0) * tq, tq)
    kv_n = swa_ref[pl.ds(near0, 2 * tq), :]
    kv_f = swa_ref[pl.ds(far0, n_far_w), :]
    kpos_f = far0 + lax.broadcasted_iota(I32, (tq, n_far_w), 1)
    pen_n = jnp.where((d_near >= 0) & (d_near < WINDOW), 0.0, NEG)
    pen_f = jnp.where((kpos_f < near0) & (pos_q - kpos_f < WINDOW), 0.0, NEG)
    s_n = _dot_nt(q, kv_n[:, :LANE]) + near_bias + heads([pen_n] * NSA_KV_HEADS)
    s_f = _dot_nt(q, kv_f[:, :LANE]) + far_col + heads([pen_f] * NSA_KV_HEADS)
    m_w = jnp.maximum(_rowmax(s_n), _rowmax(s_f))
    p_n = jnp.exp2(s_n - m_w)
    p_f = jnp.exp2(s_f - m_w)
    l_w = _rowsum(p_n) + _rowsum(p_f)
    o_swa = (_dot(p_n.astype(BF16), kv_n[:, LANE:]) + _dot(p_f.astype(BF16), kv_f[:, LANE:])) / l_w

    kv_n = slc_ref[pl.ds(near0, 2 * tq), :]
    pens = [jnp.where(hit & (d_near >= 0), 0.0, NEG) for hit in block_sel(near0, 2 * tq, seq_len)]
    s_n = _dot_nt(q, kv_n[:, :LANE]) + near_bias + heads(pens)
    m_s = _rowmax(s_n)
    p_n = jnp.exp2(s_n - m_s)
    m_ref[...] = m_s
    l_ref[...] = _rowsum(p_n)
    acc_ref[...] = _dot(p_n.astype(BF16), kv_n[:, LANE:])

    n_far = (near0 + tkf - 1) // tkf
    half = NSA_GROUP * tq

    def far_logits(c):
        first = pl.multiple_of(jnp.minimum(c, jnp.maximum(n_far - 1, 0)) * tkf, tkf)
        k = slc_ref[pl.ds(first, tkf), :LANE]
        return tuple(_dot_nt(q[kvh * half:(kvh + 1) * half], k) for kvh in range(NSA_KV_HEADS))

    def slc_far(c, qk):
        qk_next = far_logits(c + 1)
        first = pl.multiple_of(c * tkf, tkf)
        v = slc_ref[pl.ds(first, tkf), LANE:]
        hits = block_sel(first, tkf, near0)
        for kvh in range(NSA_KV_HEADS):
            r = slice(kvh * half, (kvh + 1) * half)
            pen = jnp.where(hits[kvh], 0.0, NEG)
            s_ = qk[kvh] + far_col[r] + jnp.concatenate([pen] * NSA_GROUP, axis=0)
            m_prev = m_ref[r]
            m_new = jnp.maximum(m_prev, _rowmax(s_))
            p = jnp.exp2(s_ - m_new)
            alpha = jnp.exp2(m_prev - m_new)
            l_ref[r] = alpha * l_ref[r] + _rowsum(p)
            acc_ref[r] = alpha * acc_ref[r] + _dot(p.astype(BF16), v)
            m_ref[r] = m_new
        return qk_next

    lax.fori_loop(0, n_far, slc_far, far_logits(0))
    o_slc = acc_ref[...] / l_ref[...]

    gates = gate_ref[...]
    for h in range(NSA_HEADS):
        kv, g = divmod(h, NSA_GROUP)
        r = slice(h * tq, (h + 1) * tq)
        c = kv * LANE + g
        o = (gates[:, c:c + 1] * o_cmp[r] + gates[:, c + NSA_GROUP:c + NSA_GROUP + 1] * o_slc[r]
             + gates[:, c + 2 * NSA_GROUP:c + 2 * NSA_GROUP + 1] * o_swa[r])
        o_ref[:, h * LANE:(h + 1) * LANE] = o.astype(o_ref.dtype)


def _nsa_prompt(qn, kc, vc, bias_c, ovl, slc_b, swa_b, tt, gates, *, nb, seq, n_cmp, n_sel):
    tq = 128
    nq = seq // tq
    rows = NSA_HEADS * tq
    hw = NSA_HEADS * LANE
    nhp = kc.shape[1]
    return pl.pallas_call(
        functools.partial(_nsa_prompt_kernel, tq=tq, tkf=min(512, seq), n_cmp=n_cmp, n_sel=n_sel, seq_len=seq),
        grid=(nb, nq),
        in_specs=[
            pl.BlockSpec((tq, hw), lambda b, i: (b * nq + i, 0)),
            pl.BlockSpec((None, nhp, LANE), lambda b, i: (b, 0, 0)),
            pl.BlockSpec((None, nhp, LANE), lambda b, i: (b, 0, 0)),
            pl.BlockSpec(bias_c.shape, lambda b, i: (0, 0)),
            pl.BlockSpec(ovl.shape, lambda b, i: (0, 0)),
            pl.BlockSpec((None, seq, KV_COLS), lambda b, i: (b, 0, 0)),
            pl.BlockSpec((None, seq, KV_COLS), lambda b, i: (b, 0, 0)),
            pl.BlockSpec(tt.shape, lambda b, i: (0, 0, 0)),
            pl.BlockSpec((tq, NSA_KV_HEADS * LANE), lambda b, i: (b * nq + i, 0)),
        ],
        out_specs=pl.BlockSpec((tq, hw), lambda b, i: (b * nq + i, 0)),
        out_shape=jax.ShapeDtypeStruct((nb * seq, hw), BF16),
        scratch_shapes=[pltpu.VMEM((rows, 1), F32), pltpu.VMEM((rows, 1), F32), pltpu.VMEM((rows, LANE), F32)],
        compiler_params=_cparams(("arbitrary", "arbitrary")), name="nsa_prompt",
    )(qn, kc, vc, bias_c, ovl, slc_b, swa_b, tt, gates)


def _nsa_sample_kernel(pt_ref, q_ref, kc_ref, vc_ref, bc_ref, ovl_ref, bs_ref, sn_ref, bsn_ref, st_ref, wn_ref,
                       bw_ref, bwn_ref, gate_ref, *rest, g, nq, nbs, n_cmp, n_sel, buf):
    o_ref, sel_ref, ocmp_ref, oswa_ref, m_ref, l_ref, acc_ref = rest[nbs * g:]
    s_idx = pl.program_id(1)
    last = pl.num_programs(1) - 1
    rows = NSA_HEADS * nq
    qi = lax.broadcasted_iota(I32, (rows, 1), 0) % nq
    qs = [jnp.concatenate([q_ref[bl, :, h * LANE:(h + 1) * LANE] for h in range(NSA_HEADS)], axis=0)
          for bl in range(nbs)]

    def first_step(bl):
        q = qs[bl]
        kc_r, vc_r, st_r, wn_r = kc_ref.at[bl], vc_ref.at[bl], st_ref.at[bl], wn_ref.at[bl]
        _nsa_sample_first(q, qi, kc_r, vc_r, bc_ref, ovl_ref, st_r, wn_r, bw_ref, bwn_ref, sel_ref.at[bl], ocmp_ref.at[bl],
                          oswa_ref.at[bl], m_ref.at[bl], l_ref.at[bl], acc_ref.at[bl], nq=nq, n_cmp=n_cmp, n_sel=n_sel, buf=buf)

    @pl.when(s_idx == 0)
    def _():
        for bl in range(nbs):
            first_step(bl)

    for bl in range(nbs):
        pages = rest[bl * g:(bl + 1) * g]
        page = pages[0].shape[1]
        kcat = jnp.concatenate([pages[u][:LANE].astype(BF16) for u in range(g)], axis=1)
        vcat = jnp.concatenate([pages[u][LANE:].astype(BF16) for u in range(g)], axis=1)
        nk = g * page
        s = _dot(qs[bl], kcat) + bs_ref[jnp.where(s_idx == last, 1, 0)]
        sb = lax.broadcasted_iota(I32, (sel_ref.shape[2], nk), 0)
        kt = lax.broadcasted_iota(I32, (sel_ref.shape[2], nk), 1)
        expand = jnp.where(sb == s_idx * (nk // SEL_BLOCK) + kt // SEL_BLOCK, 1.0, 0.0).astype(BF16)
        mask = _dot(sel_ref[bl], expand) > 0.5
        _flash_step(s, mask, vcat, m_ref.at[bl], l_ref.at[bl], acc_ref.at[bl], v_keys_on_lanes=True)

    @pl.when(s_idx == last)
    def _():
        for bl in range(nbs):
            q = qs[bl]
            m_r, l_r, acc_r = m_ref.at[bl], l_ref.at[bl], acc_ref.at[bl]
            sn = sn_ref[bl]
            s_n = _dot_nt(q, sn[:, :LANE]) + bsn_ref[...]
            mask_n = (lax.broadcasted_iota(I32, (rows, LANE), 1) <= qi) & (sel_ref[bl, :, n_sel - 1:n_sel] > 0.5)
            _flash_step(s_n, mask_n, sn[:, LANE:], m_r, l_r, acc_r)
            o_slc = _flash_out(l_r, acc_r)
            o_cmp, o_swa = ocmp_ref[bl], oswa_ref[bl]
            gates = gate_ref[bl]
            for h in range(NSA_HEADS):
                kv, gg = divmod(h, NSA_GROUP)
                r = slice(h * nq, (h + 1) * nq)
                c = kv * LANE + gg
                o = (gates[:, c:c + 1] * o_cmp[r] + gates[:, c + NSA_GROUP:c + NSA_GROUP + 1] * o_slc[r]
                     + gates[:, c + 2 * NSA_GROUP:c + 2 * NSA_GROUP + 1] * o_swa[r])
                o_ref[bl, :, h * LANE:(h + 1) * LANE] = o.astype(o_ref.dtype)


def _nsa_sample_first(q, qi, kc_ref, vc_ref, bc_ref, ovl_ref, st_ref, wn_ref, bw_ref, bwn_ref, sel_ref, ocmp_ref, oswa_ref,
                      m_ref, l_ref, acc_ref, *, nq, n_cmp, n_sel, buf):
    rows = NSA_HEADS * nq

    def compressed_and_select():
        nhp = kc_ref.shape[0]
        s = _dot_nt(q, kc_ref[...]) + bc_ref[...]
        mask_c = lax.broadcasted_iota(I32, (rows, nhp), 1) < n_cmp
        p_cmp = _softmax_masked(s, mask_c)
        ocmp_ref[...] = _dot(p_cmp.astype(BF16), vc_ref[...])
        psums = []
        for kv in range(NSA_KV_HEADS):
            ps = p_cmp[kv * NSA_GROUP * nq:kv * NSA_GROUP * nq + nq]
            for gg in range(1, NSA_GROUP):
                lo = (kv * NSA_GROUP + gg) * nq
                ps = ps + p_cmp[lo:lo + nq]
            psums.append(ps)
        n_kq = NSA_KV_HEADS * nq
        psum = jnp.concatenate(psums + [jnp.zeros((LANE - n_kq, nhp), F32)], axis=0)
        n_blk = ovl_ref.shape[0]
        n_sel_rows = -(-n_sel // 8) * 8
        imp_t = _dot_nt(ovl_ref[...], _split3(psum))[:n_sel_rows]
        blk = lax.broadcasted_iota(I32, imp_t.shape, 0)
        cur = n_sel - 1
        forced = (blk == 0) | (blk == cur) | (blk == cur - 1)
        imp_t = jnp.where(blk >= n_sel, -jnp.inf, jnp.where(forced, jnp.inf, imp_t))
        sel_t = _topk_mask(imp_t, min(SEL_TOPN, n_sel), axis=0)
        if n_sel_rows < n_blk:
            sel_t = jnp.concatenate([sel_t, jnp.zeros((n_blk - n_sel_rows, LANE), F32)], axis=0)
        sel = sel_t.T.astype(BF16)
        sel_ref[...] = jnp.concatenate(
            [sel[kv * nq:(kv + 1) * nq] for kv in range(NSA_KV_HEADS) for _ in range(NSA_GROUP)], axis=0)

    def window():
        st = st_ref[...]
        s_w = _dot(q, st[:LANE].astype(BF16)) + bw_ref[...]
        d_w = buf + qi - lax.broadcasted_iota(I32, (rows, buf), 1)
        mask_w = (d_w >= 0) & (d_w < WINDOW)
        wn = wn_ref[...]
        s_n = _dot_nt(q, wn[:, :LANE]) + bwn_ref[...]
        mask_n = lax.broadcasted_iota(I32, (rows, LANE), 1) <= qi
        s_w = jnp.where(mask_w, s_w, NEG)
        s_n = jnp.where(mask_n, s_n, NEG)
        m = jnp.maximum(_rowmax(s_w), _rowmax(s_n))
        p_w = jnp.where(mask_w, jnp.exp2(s_w - m), 0.0)
        p_n = jnp.where(mask_n, jnp.exp2(s_n - m), 0.0)
        l = _rowsum(p_w) + _rowsum(p_n)
        o = _dot_nt(p_w.astype(BF16), st[LANE:].astype(BF16)) + _dot(p_n.astype(BF16), wn[:, LANE:])
        oswa_ref[...] = o / jnp.where(l > 0.0, l, 1.0)

    compressed_and_select()
    window()
    _flash_init(m_ref, l_ref, acc_ref)


def _nsa_sample(page_table, qn, kc, vc, bias_c, ovl, bias_s, slc_new, bias_sn, state_swa, swa_new, bias_w, bias_wn,
                gates, cache_slc, *, g, nbs, n_cmp, n_sel):
    nb, nq = qn.shape[0], qn.shape[1]
    n_pages = page_table.shape[1]
    rows = NSA_HEADS * nq
    buf = state_swa.shape[2]
    hw = NSA_HEADS * LANE
    bmap = lambda b, s, pt: (b, 0, 0)
    c2 = lambda b, s, pt: (0, 0)
    per_elem = lambda a: pl.BlockSpec((nbs,) + a.shape[1:], bmap)
    gs = pltpu.PrefetchScalarGridSpec(
        num_scalar_prefetch=1, grid=(nb // nbs, n_pages // g),
        in_specs=[
            per_elem(qn), per_elem(kc), per_elem(vc),
            pl.BlockSpec(bias_c.shape, c2),
            pl.BlockSpec(ovl.shape, c2),
            pl.BlockSpec(bias_s.shape, lambda b, s, pt: (0, 0, 0)),
            per_elem(slc_new),
            pl.BlockSpec(bias_sn.shape, c2),
            per_elem(state_swa), per_elem(swa_new),
            pl.BlockSpec(bias_w.shape, c2),
            pl.BlockSpec(bias_wn.shape, c2),
            per_elem(gates),
        ] + _page_specs(g, (None,) + cache_slc.shape[1:], nbs=nbs),
        out_specs=pl.BlockSpec((nbs, nq, hw), bmap),
        scratch_shapes=[pltpu.VMEM((nbs, rows, ovl.shape[0]), BF16), pltpu.VMEM((nbs, rows, LANE), F32),
                        pltpu.VMEM((nbs, rows, LANE), F32), pltpu.VMEM((nbs, rows, 1), F32), pltpu.VMEM((nbs, rows, 1), F32),
                        pltpu.VMEM((nbs, rows, LANE), F32)],
    )
    return pl.pallas_call(
        functools.partial(_nsa_sample_kernel, g=g, nq=nq, nbs=nbs, n_cmp=n_cmp, n_sel=n_sel, buf=buf), grid_spec=gs,
        out_shape=jax.ShapeDtypeStruct((nb, nq, hw), BF16),
        compiler_params=_cparams(("arbitrary", "arbitrary")), name="nsa_sample",
    )(page_table, qn, kc, vc, bias_c, ovl, bias_s, slc_new, bias_sn, state_swa, swa_new, bias_w, bias_wn, gates,
      *([cache_slc] * (nbs * g)))


def _outproj_kernel(x_ref, om_ref, on_ref, g1_ref, sc_ref, sh_ref, wom_ref, won_ref, gn2_ref, wr_ref, br_ref,
                    x1_ref, h2_ref, ti_ref, tw_ref):
    mix = _dot(om_ref[...], wom_ref[...]) + _dot(on_ref[...], won_ref[...])
    x1 = x_ref[...] + g1_ref[...] * mix
    x1_ref[...] = x1
    h2 = _rms(x1, gn2_ref[...]) * (1.0 + sc_ref[...]) + sh_ref[...]
    h2_ref[...] = h2.astype(h2_ref.dtype)
    h3, w3 = _split3(h2), _split3(wr_ref[...])
    d = h2.shape[1]
    h_cat = jnp.concatenate([h3[:, :d], h3[:, :d], h3[:, d:2 * d]], axis=1)
    w_cat = jnp.concatenate([w3[:, :d], w3[:, d:2 * d], w3[:, :d]], axis=1)
    logits = _dot_nt(h_cat, w_cat) + br_ref[...]
    lane = lax.broadcasted_iota(I32, logits.shape, 1)
    v = logits
    vals, idxs = [], []
    for _ in range(TOP_K):
        m = jnp.max(v, axis=-1, keepdims=True)
        idx = jnp.min(jnp.where(v == m, lane, 1 << 20), axis=-1, keepdims=True)
        vals.append(m)
        idxs.append(idx)
        v = jnp.where(lane == idx, -jnp.inf, v)
    es = [jnp.exp(m - vals[0]) for m in vals]
    tot = es[0]
    for e in es[1:]:
        tot = tot + e
    ti = jnp.zeros(logits.shape, I32)
    tw = jnp.zeros(logits.shape, F32)
    for k in range(TOP_K):
        ti = jnp.where(lane == k, idxs[k], ti)
        tw = jnp.where(lane == k, es[k] / tot, tw)
    ti_ref[...] = ti
    tw_ref[...] = tw


def _outproj(x, o_mla, o_nsa, g1, sc, sh, W, *, rows_per_mod, tm):
    t, d = x.shape
    nt = t // tm
    if rows_per_mod == 1:
        tiles_per_mod = (t // g1.shape[0]) // tm
        mod_map = lambda i: (i // tiles_per_mod, 0, 0)
    else:
        mod_map = lambda i: (i, 0, 0)
    row = lambda i: (i, 0)
    c2 = lambda i: (0, 0)
    mod = pl.BlockSpec((None, rows_per_mod, d), mod_map)
    return pl.pallas_call(
        _outproj_kernel, grid=(nt,),
        in_specs=[pl.BlockSpec((tm, d), row), pl.BlockSpec((tm, o_mla.shape[1]), row), pl.BlockSpec((tm, o_nsa.shape[1]), row),
                  mod, mod, mod, pl.BlockSpec(W["w_o_mla"].shape, c2), pl.BlockSpec(W["w_o_nsa"].shape, c2),
                  pl.BlockSpec((1, d), c2), pl.BlockSpec(W["w_router"].shape, c2), pl.BlockSpec((1, LANE), c2)],
        out_specs=[pl.BlockSpec((tm, d), row), pl.BlockSpec((tm, d), row), pl.BlockSpec((tm, LANE), row),
                   pl.BlockSpec((tm, LANE), row)],
        out_shape=[jax.ShapeDtypeStruct((t, d), F32), jax.ShapeDtypeStruct((t, d), F32),
                   jax.ShapeDtypeStruct((t, LANE), I32), jax.ShapeDtypeStruct((t, LANE), F32)],
        compiler_params=_cparams(("arbitrary",)), name="out_proj",
    )(x, o_mla, o_nsa, g1, sc, sh, W["w_o_mla"], W["w_o_nsa"], W["g_norm2"], W["w_router"], W["b_router"])


def _expert_kernel(te_ref, tv_ref, x_ref, wgu_ref, bgu_ref, wd_ref, bd_ref, ws_ref, o_ref, wgu_bf, wd_bf):
    t = pl.program_id(0)

    @pl.when((t == 0) | (te_ref[t] != te_ref[jnp.maximum(t - 1, 0)]))
    def _():
        wgu_bf[...] = wgu_ref[...].astype(BF16)
        wd_bf[...] = wd_ref[...].astype(BF16)

    @pl.when(tv_ref[t] > 0)
    def _():
        d_ff = wd_ref.shape[0]
        gu = _dot(x_ref[...].astype(BF16), wgu_bf[...]) + bgu_ref[...]
        glu = jnp.minimum(gu[:, :d_ff], SWIGLU_LIMIT)
        lin = jnp.clip(gu[:, d_ff:], -SWIGLU_LIMIT, SWIGLU_LIMIT)
        act = glu * _sigmoid(SWIGLU_ALPHA * glu) * (lin + 1.0)
        o_ref[...] = ws_ref[...] * (_dot(act.astype(BF16), wd_bf[...]) + bd_ref[...])

    @pl.when(tv_ref[t] == 0)
    def _():
        o_ref[...] = jnp.zeros_like(o_ref)


def _experts(tile_expert, tile_valid, x_sorted, w_slot, W, *, tm):
    ns, d = x_sorted.shape
    gs = pltpu.PrefetchScalarGridSpec(
        num_scalar_prefetch=2, grid=(ns // tm,),
        in_specs=[pl.BlockSpec((tm, d), lambda t, te, tv: (t, 0)),
                  pl.BlockSpec((None,) + W["w_gate_up"].shape[1:], lambda t, te, tv: (te[t], 0, 0)),
                  pl.BlockSpec((None,) + W["b_gate_up"].shape[1:], lambda t, te, tv: (te[t], 0, 0)),
                  pl.BlockSpec((None,) + W["w_down"].shape[1:], lambda t, te, tv: (te[t], 0, 0)),
                  pl.BlockSpec((None,) + W["b_down"].shape[1:], lambda t, te, tv: (te[t], 0, 0)),
                  pl.BlockSpec((tm, 1), lambda t, te, tv: (t, 0))],
        out_specs=pl.BlockSpec((tm, d), lambda t, te, tv: (t, 0)),
        scratch_shapes=[pltpu.VMEM(W["w_gate_up"].shape[1:], BF16), pltpu.VMEM(W["w_down"].shape[1:], BF16)],
    )
    return pl.pallas_call(
        _expert_kernel, grid_spec=gs, out_shape=jax.ShapeDtypeStruct((ns, d), F32),
        compiler_params=_cparams(("arbitrary",)), name="experts",
    )(tile_expert, tile_valid, x_sorted, W["w_gate_up"], W["b_gate_up"], W["w_down"], W["b_down"], w_slot)


def _final_kernel(x1_ref, *rest):
    parts, (g2_ref, gf_ref, o_ref) = rest[:TOP_K], rest[TOP_K:]
    moe = parts[0][...]
    for p_ref in parts[1:]:
        moe = moe + p_ref[...]
    o_ref[...] = _rms(x1_ref[...] + g2_ref[...] * moe, gf_ref[...])


def _final(x1, moe_parts, g2, g_final, *, rows_per_mod, tm):
    t, d = x1.shape
    if rows_per_mod == 1:
        tiles_per_mod = (t // g2.shape[0]) // tm
        mod_map = lambda i: (i // tiles_per_mod, 0, 0)
    else:
        mod_map = lambda i: (i, 0, 0)
    row = lambda i: (i, 0)
    return pl.pallas_call(
        _final_kernel, grid=(t // tm,),
        in_specs=[pl.BlockSpec((tm, d), row)] * (1 + TOP_K)
        + [pl.BlockSpec((None, rows_per_mod, d), mod_map), pl.BlockSpec((1, d), lambda i: (0, 0))],
        out_specs=pl.BlockSpec((tm, d), row), out_shape=jax.ShapeDtypeStruct((t, d), F32),
        compiler_params=_cparams(("arbitrary",)), name="final_norm")(x1, *moe_parts, g2, g_final)


def _t5_bucket(dist):
    n = jnp.maximum(dist, 0)
    max_exact = NUM_BUCKETS // 2
    nf = jnp.maximum(n, 1).astype(F32)
    large = max_exact + (jnp.log(nf / max_exact) / math.log(MAX_DISTANCE / max_exact)
                         * (NUM_BUCKETS - max_exact)).astype(I32)
    return jnp.where(n < max_exact, n, jnp.minimum(large, NUM_BUCKETS - 1))


def _bias_rows(rel_bias, dist):
    bucket = _t5_bucket(dist)
    out = jnp.zeros((rel_bias.shape[1],) + dist.shape, F32)
    for b in range(NUM_BUCKETS):
        out = out + jnp.where(bucket == b, 1.0, 0.0)[None] * rel_bias[b][:, None, None]
    return out * LOG2E


def _rope_table(pos):
    half = MLA_ROPE // 2
    inv_freq = 1.0 / (ROPE_THETA ** (jnp.arange(half, dtype=F32) / half))
    ang = pos.astype(F32)[:, None] * inv_freq[None, :]
    pad = jnp.zeros((pos.shape[0], LANE - MLA_ROPE), F32)
    cos, sin = jnp.cos(ang), jnp.sin(ang)
    return jnp.concatenate([cos, cos, pad, sin, sin, pad], axis=1)


def _rot_cols(w):
    half = MLA_ROPE // 2
    return jnp.concatenate([-w[..., half:], w[..., :half]], axis=-1)


def _pad_last(w, n):
    return jnp.pad(w, [(0, 0)] * (w.ndim - 1) + [(0, n - w.shape[-1])])


def _pack_weights(w_in, g_norm1, g_norm2, g_q_a, w_q_b, g_kv_a, w_kv_b, cmp_pe, cmp_w1, cmp_w2, w_o, w_router,
                  b_router, w_gate_up, b_gate_up, w_down, b_down):
    d = w_in.shape[0]
    sizes = (MLA_Q_LORA, MLA_KV_LORA, MLA_ROPE, NSA_HEADS * HEAD_DIM, KV_COLS, KV_COLS, KV_COLS, 3 * NSA_HEADS)
    offs = [0]
    for s in sizes:
        offs.append(offs[-1] + s)
    w_qa, w_ckv, w_kr, w_qn, w_cmp, w_slc, w_swa, w_g = [w_in[:, offs[k]:offs[k + 1]] for k in range(8)]
    wq = w_qn.reshape(d, NSA_KV_HEADS, NSA_GROUP, HEAD_DIM)
    wq_pad = jnp.concatenate(
        [jnp.pad(wq[:, kv], ((0, 0), (0, 0), (kv * HEAD_DIM, LANE - (kv + 1) * HEAD_DIM))).reshape(d, NSA_GROUP * LANE)
         for kv in range(NSA_KV_HEADS)], axis=1)
    wg = jnp.transpose(w_g.reshape(d, NSA_KV_HEADS, NSA_GROUP, 3), (0, 1, 3, 2)).reshape(d, NSA_KV_HEADS, 3 * NSA_GROUP)
    wg_pad = _pad_last(wg, LANE).reshape(d, NSA_KV_HEADS * LANE)
    w_in_p = jnp.concatenate([w_qa, w_ckv, wq_pad, w_cmp, w_slc, w_swa, _pad_last(w_kr, LANE),
                              _pad_last(_rot_cols(w_kr), LANE), wg_pad], axis=1).astype(BF16)
    assert w_in_p.shape[1] == _C_END
    nope = _pad_last(w_q_b[:, :, :MLA_NOPE], LANE).reshape(MLA_Q_LORA, MLA_HEADS * LANE)
    rp = w_q_b[:, :, MLA_NOPE:]
    w_qb = jnp.concatenate([nope, _pad_last(rp, LANE).reshape(MLA_Q_LORA, -1),
                            _pad_last(_rot_cols(rp), LANE).reshape(MLA_Q_LORA, -1)], axis=1).astype(BF16)
    w_kn = jnp.transpose(w_kv_b[:, :, :MLA_NOPE], (1, 2, 0))
    w_kn = jnp.pad(w_kn, ((0, 0), (0, LANE - MLA_NOPE), (0, 0))).astype(BF16)
    wv = jnp.transpose(w_kv_b[:, :, MLA_NOPE:], (1, 0, 2))
    wv_pad = jnp.stack([jnp.pad(wv[h], ((0, 0), (h * MLA_V, (MLA_HEADS - 1 - h) * MLA_V))) for h in range(MLA_HEADS)]).astype(BF16)
    n_mla = MLA_HEADS * MLA_V
    won = w_o[n_mla:].reshape(NSA_KV_HEADS, NSA_GROUP, HEAD_DIM, d)
    won_pad = jnp.concatenate(
        [jnp.pad(won[kv], ((0, 0), (kv * HEAD_DIM, LANE - (kv + 1) * HEAD_DIM), (0, 0))).reshape(NSA_GROUP * LANE, d)
         for kv in range(NSA_KV_HEADS)], axis=0).astype(BF16)
    base = jnp.concatenate([cmp_w1[:, :CMP_STRIDE], cmp_w1[:, CMP_STRIDE:]], axis=-1)
    z = jnp.zeros_like(base)
    blk = jnp.concatenate([jnp.concatenate([base, z], axis=-1), jnp.concatenate([z, base], axis=-1)], axis=2)
    w_pair = blk.reshape(2, CMP_STRIDE // 2, 2 * NSA_KV_HEADS * HEAD_DIM, NSA_KV_HEADS * 2 * CMP_HIDDEN).astype(BF16)
    pe8 = jnp.broadcast_to(cmp_pe.reshape(2, 1, CMP_LEN * HEAD_DIM), (2, 8, CMP_LEN * HEAD_DIM))
    w1r = cmp_w1.reshape(2, CMP_LEN * HEAD_DIM, CMP_HIDDEN).astype(BF16)
    w2pad = jnp.stack([jnp.stack([jnp.pad(cmp_w2[j], ((0, 0), (kv * HEAD_DIM, LANE - (kv + 1) * HEAD_DIM)))
                                  for kv in range(NSA_KV_HEADS)]) for j in range(2)]).astype(BF16)
    return dict(
        w_in=w_in_p, g_norm1=g_norm1[None], g_norm2=g_norm2[None], g_q_a=g_q_a[None], g_kv_a=g_kv_a[None],
        w_qb=w_qb, w_kn=w_kn, wv=wv_pad, wv_cat=w_kv_b[:, :, MLA_NOPE:].reshape(MLA_KV_LORA, n_mla).astype(BF16),
        w_o_mla=w_o[:n_mla].astype(BF16), w_o_nsa=won_pad,
        w_pair=w_pair, pe8=pe8, w1r=w1r, w2pad=w2pad,
        w_router=_pad_last(w_router, LANE).T, b_router=jnp.pad(b_router, (0, LANE - N_EXPERTS), constant_values=NEG)[None],
        w_gate_up=w_gate_up, b_gate_up=b_gate_up[:, None, :], w_down=w_down, b_down=b_down[:, None, :])


def _overlap(n_half_pad, n_sel, n_sel_pad):
    c_start = jnp.arange(n_half_pad) * CMP_STRIDE
    s_start = jnp.arange(n_sel_pad) * SEL_BLOCK
    ov = (c_start[:, None] < s_start[None, :] + SEL_BLOCK) & (c_start[:, None] + CMP_LEN > s_start[None, :])
    return (ov & (jnp.arange(n_sel_pad) < n_sel)[None, :]).astype(F32)


def _moe_dispatch(top_i, top_w, tm):
    t = top_i.shape[0]
    a = t * TOP_K
    n_tiles = -(-a // tm) + N_EXPERTS
    ns = n_tiles * tm
    e_flat = top_i[:, :TOP_K].reshape(a)
    w_flat = top_w[:, :TOP_K].reshape(a)
    onehot = (e_flat[:, None] == jnp.arange(N_EXPERTS, dtype=I32)[None, :]).astype(I32)
    csum = jnp.cumsum(onehot, axis=0)
    counts = csum[-1]
    padded = ((counts + tm - 1) // tm) * tm
    pend = jnp.cumsum(padded)
    pstart = pend - padded
    start = jnp.cumsum(counts) - counts
    slot_of_assign = jnp.sum(onehot * (pstart[None, :] + csum - 1), axis=1).reshape(t, TOP_K)
    _, order, w_sorted = lax.sort((e_flat, jnp.arange(a, dtype=I32), w_flat), num_keys=1, is_stable=True)
    tile_start = jnp.arange(n_tiles, dtype=I32) * tm
    tile_expert = jnp.minimum(jnp.sum((pend[None, :] <= tile_start[:, None]).astype(I32), axis=1), N_EXPERTS - 1)
    tile_valid = (tile_start < pend[-1]).astype(I32)
    e_hot = (tile_expert[:, None] == jnp.arange(N_EXPERTS, dtype=I32)[None, :]).astype(I32)
    lane = jnp.arange(tm, dtype=I32)[None, :]
    rank = tile_start[:, None] - jnp.sum(e_hot * pstart[None, :], axis=1, keepdims=True) + lane
    valid = (rank < jnp.sum(e_hot * counts[None, :], axis=1, keepdims=True)) & (tile_valid[:, None] > 0)
    src = jnp.clip(jnp.sum(e_hot * start[None, :], axis=1, keepdims=True) + rank, 0, a - 1).reshape(ns)
    valid = valid.reshape(ns)
    tok_of_slot = jnp.where(valid, order[src] // TOP_K, 0)
    w_of_slot = jnp.where(valid, w_sorted[src], 0.0)
    return tok_of_slot, w_of_slot[:, None], slot_of_assign, tile_expert, tile_valid


def kernel(x_prompt, x_sample, c_prompt, c_sample, cache_mla, cache_nsa_cmp, cache_nsa_slc, state_nsa_swa, page_table, rel_bias, w_ada, b_ada, g_norm1, g_norm2, w_in, g_q_a, w_q_b, g_kv_a, w_kv_b, cmp_pe, cmp_w1, cmp_w2, w_o, w_router, b_router, w_gate_up, b_gate_up, w_down, b_down, g_final):
    depth = w_in.shape[0]
    assert depth == 1, "single-layer decoder step"
    nb, seq, d = x_prompt.shape
    nbd, nq, _ = x_sample.shape
    n_pages = page_table.shape[1]
    page = cache_mla.shape[2]
    past = n_pages * page
    buf = state_nsa_swa.shape[2]
    assert page == LANE and nq <= 8 and seq % 512 == 0 and past % SEL_BLOCK == 0 and seq >= WINDOW
    tp, ts = nb * seq, nbd * nq
    W = _pack_weights(w_in[0], g_norm1[0], g_norm2[0], g_q_a[0], w_q_b[0], g_kv_a[0], w_kv_b[0], cmp_pe[0], cmp_w1[0],
                      cmp_w2[0], w_o[0], w_router[0], b_router[0], w_gate_up[0], b_gate_up[0], w_down[0], b_down[0])

    n_c = nb + nbd
    n_c_pad = -(-n_c // 8) * 8
    c_all = jnp.pad(jnp.concatenate([c_prompt, c_sample], axis=0), ((0, n_c_pad - n_c), (0, 0)))
    mod = _ada_mod(c_all, w_ada[0].astype(BF16), b_ada[0][None])
    mod_p = [m[:, None, :] for m in jnp.split(mod[:nb], 6, axis=-1)]
    tm_s = min(256, ts)
    mod_s = [jnp.repeat(m, nq, axis=0).reshape(ts // tm_s, tm_s, d) for m in jnp.split(mod[nb:n_c], 6, axis=-1)]

    tm_p = 256
    pos_p = jnp.arange(seq)
    pos_s = past + jnp.arange(nq)
    P = _proj(x_prompt.reshape(tp, d), mod_p[1], mod_p[0], _rope_table(pos_p), W, rows_per_mod=1,
              cs_period_tiles=seq // tm_p, tm=tm_p, states_t_batches=nb)
    cs_s = jnp.tile(_rope_table(pos_s), (tm_s // nq, 1))
    S_ = _proj(x_sample.reshape(ts, d), mod_s[1], mod_s[0], cs_s, W, rows_per_mod=tm_s, cs_period_tiles=1, tm=tm_s)
    p_qmla, p_kq, p_mla_t, p_qn, p_cmp, p_cmp_t, p_slc_t, p_swa_t, p_slcb, p_swab, p_gate = P
    s_qmla, s_kq, s_mla, s_qn, _, s_cmp, s_slc, s_swa, s_slcb, s_swab, s_gate = S_

    def pad_new(a):
        return jnp.pad(a.reshape(nbd, nq, a.shape[1]), ((0, 0), (0, LANE - nq), (0, 0)))

    o_mla_p = _mla_prompt(p_qmla, p_kq, W["wv"], nb=nb, seq=seq)
    g_pages = min(16, n_pages)
    cache_mla_t = jnp.swapaxes(cache_mla[0], 1, 2)
    cache_cmp_t = jnp.moveaxis(cache_nsa_cmp[0], 1, -1).reshape(-1, KV_COLS, page)
    cache_slc_t = jnp.moveaxis(cache_nsa_slc[0], 1, -1).reshape(-1, KV_COLS, page)
    state_swa_t = jnp.moveaxis(state_nsa_swa[0], 1, -1).reshape(nbd, KV_COLS, buf)
    o_mla_s = _mla_sample(page_table, s_qmla.reshape(nbd, nq, -1), pad_new(s_kq), W["wv_cat"], cache_mla_t, g=g_pages,
                          nbs=2 if nbd % 2 == 0 else 1)

    assert (past + nq) // CMP_STRIDE == past // CMP_STRIDE
    n_pp = tp // LANE
    g_cp = min(32, n_pp)
    ab_p = _cmp_first(jnp.arange(n_pp, dtype=I32), p_cmp.reshape(n_pp, LANE, KV_COLS), W["w_pair"], g=g_cp,
                      transposed=False)
    kc_p, vc_p = _cmp_finish(ab_p, W["pe8"], W["w1r"], W["w2pad"], nb=nb)
    g_cs = min(32, nbd * n_pages)
    ab_s = _cmp_first(page_table.reshape(nbd * n_pages), cache_cmp_t, W["w_pair"], g=g_cs, transposed=True)
    kc_s, vc_s = _cmp_finish(ab_s, W["pe8"], W["w1r"], W["w2pad"], nb=nbd)

    nh_p = seq // CMP_STRIDE
    n_sel_p = -(-seq // SEL_BLOCK)
    tq = 128
    far_h = _bias_rows(rel_bias, jnp.full((1, 1), MAX_DISTANCE))
    d_win = (jnp.arange(tq)[:, None] - CMP_STRIDE * (jnp.arange(LANE)[None, :] - CMP_WIN_LO) - (CMP_LEN - 1))
    assert CMP_WIN_LO * CMP_STRIDE >= MAX_DISTANCE + CMP_LEN and LANE - CMP_WIN_LO >= tq // CMP_STRIDE
    bias_c_p = jnp.where((d_win >= 0)[None], _bias_rows(rel_bias, d_win) - far_h, 0.0)
    bias_c_p = bias_c_p.reshape(NSA_HEADS * tq, LANE)
    di = jnp.arange(tq)[:, None] - jnp.arange(tq)[None, :]
    tt = jnp.stack([_bias_rows(rel_bias, di), _bias_rows(rel_bias, di + tq),
                    _bias_rows(rel_bias, jnp.full((tq, tq), MAX_DISTANCE))], axis=1)
    tt = tt.transpose(1, 0, 2, 3).reshape(3, NSA_HEADS * tq, tq)
    assert n_sel_p <= LANE
    ovl_p = jnp.tile(_overlap(nh_p, n_sel_p, LANE).T, (1, 3)).astype(BF16)
    o_nsa_p = _nsa_prompt(p_qn, kc_p, vc_p, bias_c_p, ovl_p, p_slcb.reshape(nb, seq, KV_COLS),
                          p_swab.reshape(nb, seq, KV_COLS), tt, p_gate, nb=nb, seq=seq, n_cmp=nh_p - 1, n_sel=n_sel_p)

    nh_s = past // CMP_STRIDE
    n_sel_s = -(-(past + nq) // SEL_BLOCK)
    assert n_sel_s == past // SEL_BLOCK + 1
    n_sel_pad = -(-n_sel_s // LANE) * LANE
    rows_s = NSA_HEADS * nq
    cmp_end_s = jnp.arange(nh_s) * CMP_STRIDE + CMP_LEN - 1
    bias_c_s = _bias_rows(rel_bias, pos_s[:, None] - cmp_end_s[None, :]).reshape(rows_s, nh_s)
    nk_step = g_pages * page
    far = jnp.broadcast_to(_bias_rows(rel_bias, jnp.full((nq, 1), MAX_DISTANCE)).reshape(rows_s, 1), (rows_s, nk_step))
    tail_pos = past - nk_step + jnp.arange(nk_step)
    bias_s = jnp.stack([far, _bias_rows(rel_bias, pos_s[:, None] - tail_pos[None, :]).reshape(rows_s, nk_step)])
    new_pos = past + jnp.arange(LANE)
    bias_new = _bias_rows(rel_bias, pos_s[:, None] - new_pos[None, :]).reshape(rows_s, LANE)
    swa_pos = past - buf + jnp.arange(buf)
    bias_w = _bias_rows(rel_bias, pos_s[:, None] - swa_pos[None, :]).reshape(rows_s, buf)
    ovl_s = jnp.tile(_overlap(nh_s, n_sel_s, n_sel_pad).T, (1, 3)).astype(BF16)
    o_nsa_s = _nsa_sample(page_table, s_qn.reshape(nbd, nq, -1), kc_s, vc_s, bias_c_s, ovl_s, bias_s, pad_new(s_slcb),
                          bias_new, state_swa_t, pad_new(s_swab), bias_w, bias_new,
                          s_gate.reshape(nbd, nq, -1), cache_slc_t, g=g_pages, nbs=2 if nbd % 2 == 0 else 1,
                          n_cmp=nh_s - 1, n_sel=n_sel_s)

    x1_p, h2_p, ti_p, tw_p = _outproj(x_prompt.reshape(tp, d), o_mla_p, o_nsa_p, mod_p[2], mod_p[4], mod_p[3], W,
                                      rows_per_mod=1, tm=2 * tm_p)
    x1_s, h2_s, ti_s, tw_s = _outproj(x_sample.reshape(ts, d), o_mla_s.reshape(ts, -1), o_nsa_s.reshape(ts, -1),
                                      mod_s[2], mod_s[4], mod_s[3], W, rows_per_mod=tm_s, tm=tm_s)

    tm_e = 256
    h2 = jnp.concatenate([h2_p, h2_s], axis=0)
    tok_of_slot, w_slot, slot_of_assign, tile_expert, tile_valid = _moe_dispatch(
        jnp.concatenate([ti_p, ti_s], axis=0), jnp.concatenate([tw_p, tw_s], axis=0), tm_e)
    y_sorted = _experts(tile_expert, tile_valid, h2[tok_of_slot], w_slot, W, tm=tm_e)
    parts_p = [y_sorted[slot_of_assign[:tp, k]] for k in range(TOP_K)]
    parts_s = [y_sorted[slot_of_assign[tp:, k]] for k in range(TOP_K)]
    y_p = _final(x1_p, parts_p, mod_p[5], g_final[None], rows_per_mod=1, tm=tm_p)
    y_s = _final(x1_s, parts_s, mod_s[5], g_final[None], rows_per_mod=tm_s, tm=tm_s)

    kv_tail = (2, NSA_KV_HEADS, HEAD_DIM)
    keep_p = min(WINDOW, seq)
    swa_keys = jnp.concatenate([state_nsa_swa[0], s_swa.reshape((nbd, nq) + kv_tail)], axis=1)
    keep_s = min(WINDOW, buf + nq)

    def rows_last(a_t):
        return jnp.moveaxis(a_t.reshape((nb,) + kv_tail + (a_t.shape[-1],)), -1, 1)[None]

    return (y_p.reshape(nb, seq, d), y_s.reshape(nbd, nq, d),
            jnp.swapaxes(p_mla_t, 1, 2)[None], s_mla.reshape(1, nbd, nq, -1),
            rows_last(p_cmp_t), s_cmp.reshape((1, nbd, nq) + kv_tail),
            rows_last(p_slc_t), s_slc.reshape((1, nbd, nq) + kv_tail),
            rows_last(p_swa_t[:, :, seq - keep_p:]),
            swa_keys[None, :, buf + nq - keep_s:])
```

```python
import functools
import math

import jax
import jax.numpy as jnp
from jax import lax
from jax.experimental import pallas as pl
from jax.experimental.pallas import tpu as pltpu

F32, BF16, I32 = jnp.float32, jnp.bfloat16, jnp.int32

MLA_HEADS = 8
MLA_Q_LORA = 384
MLA_KV_LORA = 256
MLA_NOPE = 64
MLA_ROPE = 32
MLA_V = 64
NSA_HEADS = 8
NSA_KV_HEADS = 2
NSA_GROUP = NSA_HEADS // NSA_KV_HEADS
HEAD_DIM = 64
CMP_LEN = 32
CMP_STRIDE = 16
CMP_HIDDEN = 128
SEL_BLOCK = 64
SEL_TOPN = 16
WINDOW = 512
KV_COLS = 2 * NSA_KV_HEADS * HEAD_DIM
N_EXPERTS = 32
TOP_K = 4
SWIGLU_LIMIT = 7.0
SWIGLU_ALPHA = 1.702
NUM_BUCKETS = 32
MAX_DISTANCE = 128
ROPE_THETA = 10000.0
NORM_EPS = 1e-6

LANE = 128
VMEM_LIMIT = 56 * 1024 * 1024
NEG = -1e30
CMP_WIN_LO = 16

MLA_QW = MLA_KV_LORA + LANE
MLA_SCALE = (MLA_NOPE + MLA_ROPE) ** -0.5
MLA_QSCALE = MLA_SCALE * math.log2(math.e)
LOG2E = math.log2(math.e)
NSA_QSCALE = HEAD_DIM ** -0.5 * LOG2E

_C_QA = 0
_C_CKV = _C_QA + MLA_Q_LORA
_C_QN = _C_CKV + MLA_KV_LORA
_C_CMP = _C_QN + NSA_HEADS * LANE
_C_SLC = _C_CMP + KV_COLS
_C_SWA = _C_SLC + KV_COLS
_C_KR = _C_SWA + KV_COLS
_C_KRR = _C_KR + LANE
_C_G = _C_KRR + LANE
_C_END = _C_G + NSA_KV_HEADS * LANE


def _cparams(sem, vmem=VMEM_LIMIT):
    return pltpu.CompilerParams(dimension_semantics=sem, vmem_limit_bytes=vmem)


def _dot(a, b):
    return jnp.dot(a, b, preferred_element_type=F32)


def _dot_nt(a, b):
    return lax.dot_general(a, b, (((1,), (1,)), ((), ())), preferred_element_type=F32)


def _row_reduce(x, op, reduce):
    n = x.shape[1] // LANE
    if x.shape[1] % LANE or n <= 1:
        return reduce(x, axis=-1, keepdims=True)
    t = x[:, :LANE]
    for c in range(1, n):
        t = op(t, x[:, c * LANE:(c + 1) * LANE])
    return reduce(t, axis=-1, keepdims=True)


def _rowmax(x):
    return _row_reduce(x, jnp.maximum, jnp.max)


def _rowsum(x):
    return _row_reduce(x, jnp.add, jnp.sum)


def _split3(x):
    hi = x.astype(BF16)
    r1 = x - hi.astype(F32)
    mid = r1.astype(BF16)
    lo = (r1 - mid.astype(F32)).astype(BF16)
    return jnp.concatenate([hi, mid, lo], axis=-1)


def _rms(x, g):
    return x * lax.rsqrt(jnp.mean(x * x, axis=-1, keepdims=True) + NORM_EPS) * g


def _sigmoid(x):
    return 1.0 / (1.0 + jnp.exp(-x))


def _ada_kernel(c_ref, w_ref, b_ref, o_ref):
    c = c_ref[...]
    o_ref[...] = _dot((c * _sigmoid(c)).astype(BF16), w_ref[...]) + b_ref[...]


def _ada_mod(c_all, w_ada, b_ada):
    m, d = c_all.shape
    n = w_ada.shape[1]
    tn = 1536
    return pl.pallas_call(
        _ada_kernel, grid=(n // tn,),
        in_specs=[pl.BlockSpec((m, d), lambda i: (0, 0)), pl.BlockSpec((d, tn), lambda i: (0, i)),
                  pl.BlockSpec((1, tn), lambda i: (0, i))],
        out_specs=pl.BlockSpec((m, tn), lambda i: (0, i)),
        out_shape=jax.ShapeDtypeStruct((m, n), F32),
        compiler_params=_cparams(("arbitrary",)), name="ada_mod")(c_all, w_ada, b_ada)


def _proj_kernel(x_ref, sc_ref, sh_ref, g1_ref, cs_ref, win_ref, gqa_ref, wqb_ref, gkva_ref, wkn_ref,
                 qmla_ref, kq_ref, mla_ref, qn_ref, cmp_ref, cmps_ref, slc_ref, swa_ref, slcb_ref, swab_ref, gate_ref,
                 *, states_t):
    h = _rms(x_ref[...], g1_ref[...]) * (1.0 + sc_ref[...]) + sh_ref[...]
    proj = _dot(h.astype(BF16), win_ref[...])
    cs = cs_ref[...]
    cosp, sinp = cs[:, :LANE], cs[:, LANE:]
    qa = _rms(proj[:, _C_QA:_C_CKV], gqa_ref[...])
    q = _dot(qa.astype(BF16), wqb_ref[...])
    hw = MLA_HEADS * LANE
    for h_i in range(MLA_HEADS):
        lo = h_i * LANE
        qlat = _dot(q[:, lo:lo + LANE].astype(BF16), wkn_ref[h_i]) * MLA_QSCALE
        qr = (q[:, hw + lo:hw + lo + LANE] * cosp + q[:, 2 * hw + lo:2 * hw + lo + LANE] * sinp) * MLA_QSCALE
        qmla_ref[:, h_i * MLA_QW:h_i * MLA_QW + MLA_KV_LORA] = qlat.astype(BF16)
        qmla_ref[:, h_i * MLA_QW + MLA_KV_LORA:(h_i + 1) * MLA_QW] = qr.astype(BF16)
    ckv = _rms(proj[:, _C_CKV:_C_QN], gkva_ref[...])
    kr = proj[:, _C_KR:_C_KRR] * cosp + proj[:, _C_KRR:_C_G] * sinp
    kq_ref[:, :MLA_KV_LORA] = ckv.astype(BF16)
    kq_ref[:, MLA_KV_LORA:] = kr.astype(BF16)
    qn_ref[...] = (proj[:, _C_QN:_C_CMP] * NSA_QSCALE).astype(BF16)
    cmp = proj[:, _C_CMP:_C_SLC]
    slc = proj[:, _C_SLC:_C_SWA]
    swa = proj[:, _C_SWA:_C_KR]
    cmp_ref[...] = cmp
    if states_t:
        mla_ref[:MLA_KV_LORA, :] = ckv.T
        mla_ref[MLA_KV_LORA:, :] = kr.T[:MLA_ROPE]
        cmps_ref[...] = cmp.T
        slc_ref[...] = slc.T
        swa_ref[...] = swa.T
    else:
        mla_ref[:, :MLA_KV_LORA] = ckv
        mla_ref[:, MLA_KV_LORA:] = kr[:, :MLA_ROPE]
        cmps_ref[...] = cmp
        slc_ref[...] = slc
        swa_ref[...] = swa
    slcb_ref[...] = slc.astype(BF16)
    swab_ref[...] = swa.astype(BF16)
    gate_ref[...] = _sigmoid(proj[:, _C_G:_C_END])


def _proj(x, sc, sh, cs, W, *, rows_per_mod, cs_period_tiles, tm, states_t_batches=0):
    t, d = x.shape
    nt = t // tm
    if rows_per_mod == 1:
        tiles_per_mod = sc.shape[0] and (t // sc.shape[0]) // tm
        mod_map = lambda i: (i // tiles_per_mod, 0, 0)
    else:
        mod_map = lambda i: (i, 0, 0)
    cs_map = (lambda i: (i % cs_period_tiles, 0)) if cs_period_tiles > 1 else (lambda i: (0, 0))
    const2 = lambda i: (0, 0)
    row = lambda i: (i, 0)
    outs = [
        (MLA_HEADS * MLA_QW, BF16), (MLA_QW, BF16), (MLA_KV_LORA + MLA_ROPE, F32), (NSA_HEADS * LANE, BF16),
        (KV_COLS, F32), (KV_COLS, F32), (KV_COLS, F32), (KV_COLS, F32), (KV_COLS, BF16), (KV_COLS, BF16),
        (NSA_KV_HEADS * LANE, F32),
    ]
    state_outs = (2, 5, 6, 7) if states_t_batches else ()
    tiles_per_batch = (t // states_t_batches) // tm if states_t_batches else 0
    out_specs = [pl.BlockSpec((None, w, tm), lambda i: (i // tiles_per_batch, 0, i % tiles_per_batch))
                 if k in state_outs else pl.BlockSpec((tm, w), row) for k, (w, _) in enumerate(outs)]
    out_shape = [jax.ShapeDtypeStruct((states_t_batches, w, t // states_t_batches) if k in state_outs else (t, w), dt)
                 for k, (w, dt) in enumerate(outs)]
    return pl.pallas_call(
        functools.partial(_proj_kernel, states_t=bool(states_t_batches)), grid=(nt,),
        in_specs=[
            pl.BlockSpec((tm, d), row),
            pl.BlockSpec((None, rows_per_mod, d), mod_map),
            pl.BlockSpec((None, rows_per_mod, d), mod_map),
            pl.BlockSpec((1, d), const2),
            pl.BlockSpec((tm, 2 * LANE), cs_map),
            pl.BlockSpec(W["w_in"].shape, const2),
            pl.BlockSpec((1, MLA_Q_LORA), const2),
            pl.BlockSpec(W["w_qb"].shape, const2),
            pl.BlockSpec((1, MLA_KV_LORA), const2),
            pl.BlockSpec(W["w_kn"].shape, lambda i: (0, 0, 0)),
        ],
        out_specs=out_specs, out_shape=out_shape,
        compiler_params=_cparams(("arbitrary",)), name="proj_in",
    )(x, sc, sh, W["g_norm1"], cs, W["w_in"], W["g_q_a"], W["w_qb"], W["g_kv_a"], W["w_kn"])


def _mla_prompt_kernel(q_ref, k_ref, wv_ref, o_ref, m_ref, l_ref, acc_ref, *, tq, tk):
    i, j = pl.program_id(1), pl.program_id(2)
    nk = pl.num_programs(2)

    @pl.when(j == 0)
    def _():
        m_ref[...] = jnp.full_like(m_ref, NEG)
        l_ref[...] = jnp.zeros_like(l_ref)
        acc_ref[...] = jnp.zeros_like(acc_ref)

    def step(masked):
        k = k_ref[...]
        v = k[:, :MLA_KV_LORA]
        if masked:
            qpos = i * tq + lax.broadcasted_iota(I32, (tq, tk), 0)
            kpos = j * tk + lax.broadcasted_iota(I32, (tq, tk), 1)
            mask = kpos <= qpos
        ss = [_dot_nt(q_ref[:, h * MLA_QW:(h + 1) * MLA_QW], k) for h in range(MLA_HEADS)]
        ps, alphas = [], []
        for h in range(MLA_HEADS):
            r = slice(h * tq, (h + 1) * tq)
            s = jnp.where(mask, ss[h], NEG) if masked else ss[h]
            m_prev = m_ref[r]
            m_new = jnp.maximum(m_prev, _rowmax(s))
            p = jnp.exp2(s - m_new)
            if masked:
                p = jnp.where(mask, p, 0.0)
            alpha = jnp.exp2(m_prev - m_new)
            l_ref[r] = alpha * l_ref[r] + _rowsum(p)
            m_ref[r] = m_new
            ps.append(p.astype(BF16))
            alphas.append(alpha)
        for h in range(MLA_HEADS):
            r = slice(h * tq, (h + 1) * tq)
            acc_ref[r] = alphas[h] * acc_ref[r] + _dot(ps[h], v)

    first_key, last_key = j * tk, j * tk + tk - 1
    pl.when(last_key <= i * tq)(functools.partial(step, False))
    pl.when((first_key <= i * tq + tq - 1) & (last_key > i * tq))(functools.partial(step, True))

    @pl.when(j == nk - 1)
    def _():
        out = jnp.zeros(o_ref.shape, F32)
        for h in range(MLA_HEADS):
            r = slice(h * tq, (h + 1) * tq)
            l = l_ref[r]
            o_lat = acc_ref[r] / jnp.where(l > 0.0, l, 1.0)
            out = out + _dot(o_lat.astype(BF16), wv_ref[h])
        o_ref[...] = out.astype(o_ref.dtype)


def _mla_prompt(qmla, kq, wv, *, nb, seq, tq=256, tk=1024):
    tk = min(tk, seq)
    if seq % tk:
        tk = 512
    nq, nk = seq // tq, seq // tk
    ow = MLA_HEADS * MLA_V

    def k_map(b, i, j):
        return (b * nk + jnp.minimum(j, (i * tq + tq - 1) // tk), 0)

    return pl.pallas_call(
        functools.partial(_mla_prompt_kernel, tq=tq, tk=tk), grid=(nb, nq, nk),
        in_specs=[pl.BlockSpec((tq, MLA_HEADS * MLA_QW), lambda b, i, j: (b * nq + i, 0)),
                  pl.BlockSpec((tk, MLA_QW), k_map),
                  pl.BlockSpec(wv.shape, lambda b, i, j: (0, 0, 0))],
        out_specs=pl.BlockSpec((tq, ow), lambda b, i, j: (b * nq + i, 0)),
        out_shape=jax.ShapeDtypeStruct((nb * seq, ow), BF16),
        scratch_shapes=[pltpu.VMEM((MLA_HEADS * tq, 1), F32), pltpu.VMEM((MLA_HEADS * tq, 1), F32),
                        pltpu.VMEM((MLA_HEADS * tq, MLA_KV_LORA), F32)],
        compiler_params=_cparams(("arbitrary", "arbitrary", "arbitrary")), name="mla_prompt",
    )(qmla, kq, wv)


def _mla_sample_kernel(pt_ref, q_ref, kn_ref, wv_ref, *rest, g, nq, nbs):
    pages = rest[:nbs * g]
    o_ref, m_ref, l_ref, acc_ref = rest[nbs * g:]
    s_idx = pl.program_id(1)
    rows = MLA_HEADS * nq

    @pl.when(s_idx == 0)
    def _():
        m_ref[...] = jnp.full_like(m_ref, NEG)
        l_ref[...] = jnp.zeros_like(l_ref)
        acc_ref[...] = jnp.zeros_like(acc_ref)

    qs = []
    for bl in range(nbs):
        r = slice(bl * rows, (bl + 1) * rows)
        q = jnp.concatenate([q_ref[bl, :, h * MLA_QW:(h + 1) * MLA_QW] for h in range(MLA_HEADS)], axis=0)
        qs.append(q)
        qlat, qr = q[:, :MLA_KV_LORA], q[:, MLA_KV_LORA:MLA_KV_LORA + MLA_ROPE]
        pg = pages[bl * g:(bl + 1) * g]
        ckv_t = jnp.concatenate([pg[u][:MLA_KV_LORA].astype(BF16) for u in range(g)], axis=1)
        kr_t = jnp.concatenate([pg[u][MLA_KV_LORA:].astype(BF16) for u in range(g)], axis=1)
        s = _dot(qlat, ckv_t) + _dot(qr, kr_t)
        m_prev = m_ref[r]
        m_new = jnp.maximum(m_prev, _rowmax(s))
        p = jnp.exp2(s - m_new)
        alpha = jnp.exp2(m_prev - m_new)
        l_ref[r] = alpha * l_ref[r] + _rowsum(p)
        acc_ref[r] = alpha * acc_ref[r] + _dot_nt(p.astype(BF16), ckv_t)
        m_ref[r] = m_new

    @pl.when(s_idx == pl.num_programs(1) - 1)
    def _():
        for bl in range(nbs):
            r = slice(bl * rows, (bl + 1) * rows)
            kn = kn_ref[bl]
            sn = _dot_nt(qs[bl], kn)
            qi = lax.broadcasted_iota(I32, sn.shape, 0) % nq
            kt = lax.broadcasted_iota(I32, sn.shape, 1)
            mask = kt <= qi
            sn = jnp.where(mask, sn, NEG)
            m_prev = m_ref[r]
            m_new = jnp.maximum(m_prev, jnp.max(sn, axis=-1, keepdims=True))
            p = jnp.where(mask, jnp.exp2(sn - m_new), 0.0)
            alpha = jnp.exp2(m_prev - m_new)
            l = alpha * l_ref[r] + _rowsum(p)
            acc = alpha * acc_ref[r] + _dot(p.astype(BF16), kn[:, :MLA_KV_LORA])
            o_lat = (acc / jnp.where(l > 0.0, l, 1.0)).astype(BF16)
            res = _dot(o_lat, wv_ref[...])
            head_of_lane = lax.broadcasted_iota(I32, (nq, res.shape[1]), 1) // MLA_V
            out = jnp.zeros((nq, res.shape[1]), F32)
            for h in range(MLA_HEADS):
                out = out + jnp.where(head_of_lane == h, res[h * nq:(h + 1) * nq], 0.0)
            o_ref[bl] = out.astype(o_ref.dtype)


def _page_specs(g, block, n_lane_blocks=1, flat=False, nbs=1):
    def index(b, s, pt, bl, u, c):
        return ((pt[s * g + u] if flat else pt[b * nbs + bl, s * g + u]), 0, c)

    return [pl.BlockSpec(block, functools.partial(index, bl=bl, u=u, c=c))
            for bl in range(nbs) for u in range(g) for c in range(n_lane_blocks)]


def _mla_sample(page_table, qmla, kq_new_pad, wv, cache_mla, *, g, nbs):
    nb, nq = qmla.shape[0], qmla.shape[1]
    n_pages = page_table.shape[1]
    ow = MLA_HEADS * MLA_V
    rows = nbs * MLA_HEADS * nq
    bmap = lambda b, s, pt: (b, 0, 0)
    gs = pltpu.PrefetchScalarGridSpec(
        num_scalar_prefetch=1, grid=(nb // nbs, n_pages // g),
        in_specs=[pl.BlockSpec((nbs, nq, MLA_HEADS * MLA_QW), bmap),
                  pl.BlockSpec((nbs, LANE, MLA_QW), bmap),
                  pl.BlockSpec(wv.shape, lambda b, s, pt: (0, 0))]
        + _page_specs(g, (None,) + cache_mla.shape[1:], nbs=nbs),
        out_specs=pl.BlockSpec((nbs, nq, ow), bmap),
        scratch_shapes=[pltpu.VMEM((rows, 1), F32), pltpu.VMEM((rows, 1), F32), pltpu.VMEM((rows, MLA_KV_LORA), F32)],
    )
    return pl.pallas_call(
        functools.partial(_mla_sample_kernel, g=g, nq=nq, nbs=nbs), grid_spec=gs,
        out_shape=jax.ShapeDtypeStruct((nb, nq, ow), BF16),
        compiler_params=_cparams(("arbitrary", "arbitrary")), name="mla_sample",
    )(page_table, qmla, kq_new_pad, wv, *([cache_mla] * (nbs * g)))


def _cmp_first_kernel(pl_ref, w_ref, *rest, g, transposed):
    pages = rest[:g]
    o_ref, xs_ref = rest[g:]
    half_w = NSA_KV_HEADS * 2 * CMP_HIDDEN
    n_half = o_ref.shape[0]
    for u in range(g):
        x = pages[u][...]
        for j in range(2):
            if transposed:
                xs_ref[j, u * LANE:(u + 1) * LANE, :] = x[j * LANE:(j + 1) * LANE, :].T
            else:
                xs_ref[j, u * LANE:(u + 1) * LANE, :] = x[:, j * LANE:(j + 1) * LANE]
    for j in range(2):
        acc = jnp.zeros((n_half, half_w), F32)
        for pp in range(CMP_STRIDE // 2):
            xa = xs_ref[j, pl.ds(2 * pp, n_half, stride=CMP_STRIDE), :]
            xb = xs_ref[j, pl.ds(2 * pp + 1, n_half, stride=CMP_STRIDE), :]
            acc = acc + _dot(jnp.concatenate([xa, xb], axis=1).astype(BF16), w_ref[j, pp])
        o_ref[:, j * half_w:(j + 1) * half_w] = acc


def _cmp_first(page_list, pool, w_pair, *, g, transposed):
    n = page_list.shape[0]
    halves = LANE // CMP_STRIDE
    ow = 2 * NSA_KV_HEADS * 2 * CMP_HIDDEN
    gs = pltpu.PrefetchScalarGridSpec(
        num_scalar_prefetch=1, grid=(1, n // g),
        in_specs=[pl.BlockSpec(w_pair.shape, lambda b, s, pt: (0, 0, 0, 0))]
        + _page_specs(g, (None,) + pool.shape[1:], flat=True),
        out_specs=pl.BlockSpec((g * halves, ow), lambda b, s, pt: (s, 0)),
        scratch_shapes=[pltpu.VMEM((2, g * LANE, LANE), F32)],
    )
    return pl.pallas_call(
        functools.partial(_cmp_first_kernel, g=g, transposed=transposed), grid_spec=gs,
        out_shape=jax.ShapeDtypeStruct((n * halves, ow), F32),
        compiler_params=_cparams(("arbitrary", "arbitrary")), name="cmp_first",
    )(page_list, w_pair, *([pool] * g))


def _cmp_finish_kernel(ab_ref, pe_ref, w1_ref, w2_ref, kc_ref, vc_ref):
    nh = ab_ref.shape[0]
    outs = []
    for j in range(2):
        pe_term = _dot(pe_ref[j].astype(BF16), w1_ref[j])[0:1]
        acc = jnp.zeros((nh, LANE), F32)
        for kv in range(NSA_KV_HEADS):
            base = (j * NSA_KV_HEADS + kv) * 2 * CMP_HIDDEN
            first = ab_ref[:, base:base + CMP_HIDDEN]
            second = ab_ref[:, base + CMP_HIDDEN:base + 2 * CMP_HIDDEN]
            hid = first + pltpu.roll(second, nh - 1, 0) + pe_term
            hid = 0.5 * hid * (1.0 + lax.erf(hid * math.sqrt(0.5)))
            acc = acc + _dot(hid.astype(BF16), w2_ref[j, kv])
        outs.append(acc)
    kc_ref[...] = outs[0].astype(kc_ref.dtype)
    vc_ref[...] = outs[1].astype(vc_ref.dtype)


def _cmp_finish(ab, pe8, w1r, w2pad, *, nb):
    nh = ab.shape[0] // nb
    c3 = lambda b: (0, 0, 0)
    return pl.pallas_call(
        _cmp_finish_kernel, grid=(nb,),
        in_specs=[pl.BlockSpec((nh, ab.shape[1]), lambda b: (b, 0)), pl.BlockSpec(pe8.shape, c3),
                  pl.BlockSpec(w1r.shape, c3), pl.BlockSpec(w2pad.shape, lambda b: (0, 0, 0, 0))],
        out_specs=[pl.BlockSpec((None, nh, LANE), lambda b: (b, 0, 0))] * 2,
        out_shape=[jax.ShapeDtypeStruct((nb, nh, LANE), BF16)] * 2,
        compiler_params=_cparams(("arbitrary",)), name="cmp_finish")(ab, pe8, w1r, w2pad)


def _topk_mask(v, k, axis=1):
    lane = lax.broadcasted_iota(I32, v.shape, axis)
    sel = jnp.zeros(v.shape, F32)
    for _ in range(k):
        m = jnp.max(v, axis=axis, keepdims=True)
        idx = jnp.min(jnp.where(v == m, lane, 1 << 20), axis=axis, keepdims=True)
        pick = lane == idx
        sel = jnp.where(pick & (m > -jnp.inf), 1.0, sel)
        v = jnp.where(pick, -jnp.inf, v)
    return sel


def _softmax_masked(s, mask):
    s = jnp.where(mask, s, NEG)
    m = _rowmax(s)
    p = jnp.where(mask, jnp.exp2(s - m), 0.0)
    l = _rowsum(p)
    return p / jnp.where(l > 0.0, l, 1.0)


def _flash_step(s, mask, v, m_ref, l_ref, acc_ref, v_keys_on_lanes=False):
    s = jnp.where(mask, s, NEG)
    m_prev = m_ref[...]
    m_new = jnp.maximum(m_prev, _rowmax(s))
    p = jnp.where(mask, jnp.exp2(s - m_new), 0.0)
    alpha = jnp.exp2(m_prev - m_new)
    l_ref[...] = alpha * l_ref[...] + _rowsum(p)
    pv = _dot_nt(p.astype(BF16), v) if v_keys_on_lanes else _dot(p.astype(BF16), v)
    acc_ref[...] = alpha * acc_ref[...] + pv
    m_ref[...] = m_new


def _flash_init(m_ref, l_ref, acc_ref):
    m_ref[...] = jnp.full_like(m_ref, NEG)
    l_ref[...] = jnp.zeros_like(l_ref)
    acc_ref[...] = jnp.zeros_like(acc_ref)


def _flash_out(l_ref, acc_ref):
    l = l_ref[...]
    return acc_ref[...] / jnp.where(l > 0.0, l, 1.0)


def _nsa_prompt_kernel(q_ref, kc_ref, vc_ref, bc_ref, ovl_ref, slc_ref, swa_ref, tt_ref, gate_ref, o_ref,
                       m_ref, l_ref, acc_ref, *, tq, tkf, n_cmp, n_sel, seq_len):
    i = pl.program_id(1)
    rows = NSA_HEADS * tq
    n_sel_rows = -(-n_sel // 8) * 8
    q = jnp.concatenate([q_ref[:, h * LANE:(h + 1) * LANE] for h in range(NSA_HEADS)], axis=0)
    nhp = kc_ref.shape[0]
    pos_r = i * tq + lax.broadcasted_iota(I32, (rows, 1), 0) % tq

    wu = lax.broadcasted_iota(I32, (LANE, nhp), 0)
    wn = lax.broadcasted_iota(I32, (LANE, nhp), 1)
    shift = jnp.where(wn == i * (tq // CMP_STRIDE) - CMP_WIN_LO + wu, 1.0, 0.0).astype(BF16)
    far_col = tt_ref[2][:, :1]
    bias_c = _dot(_split3(bc_ref[...]), jnp.concatenate([shift] * 3, axis=0)) + far_col
    s = _dot_nt(q, kc_ref[...]) + bias_c
    n_idx = lax.broadcasted_iota(I32, (rows, nhp), 1)
    mask_c = (n_idx * CMP_STRIDE + CMP_LEN - 1 <= pos_r) & (n_idx < n_cmp)
    p_cmp = _softmax_masked(s, mask_c)
    o_cmp = _dot(p_cmp.astype(BF16), vc_ref[...])

    imps = []
    for kv in range(NSA_KV_HEADS):
        lo = kv * NSA_GROUP * tq
        psum = p_cmp[lo:lo + tq]
        for g in range(1, NSA_GROUP):
            psum = psum + p_cmp[lo + g * tq:lo + (g + 1) * tq]
        imps.append(_dot_nt(ovl_ref[...], _split3(psum))[:n_sel_rows])
    imp_t = jnp.concatenate(imps, axis=1)
    blk = lax.broadcasted_iota(I32, imp_t.shape, 0)
    cur = (i * tq + lax.broadcasted_iota(I32, imp_t.shape, 1) % tq) // SEL_BLOCK
    forced = (blk == 0) | (blk == cur) | (blk == cur - 1)
    future = (blk > cur) | (blk >= n_sel)
    imp_t = jnp.where(future, -jnp.inf, jnp.where(forced, jnp.inf, imp_t))
    sel_t = _topk_mask(imp_t, min(SEL_TOPN, n_sel), axis=0)
    if n_sel_rows < LANE:
        sel_t = jnp.concatenate([sel_t, jnp.zeros((LANE - n_sel_rows, NSA_KV_HEADS * tq), F32)], axis=0)
    sels = [sel_t[:, kv * tq:(kv + 1) * tq].T.astype(BF16) for kv in range(NSA_KV_HEADS)]

    near0 = pl.multiple_of(jnp.maximum(i - 1, 0) * tq, tq)
    near_bias = jnp.concatenate([tt_ref[jnp.where(i == 0, 0, 1)], tt_ref[0]], axis=1)
    pos_q = i * tq + lax.broadcasted_iota(I32, (tq, 1), 0)
    d_near = pos_q - (near0 + lax.broadcasted_iota(I32, (tq, 2 * tq), 1))

    def heads(pens):
        return jnp.concatenate([p_ for p_ in pens for _ in range(NSA_GROUP)], axis=0)

    def block_sel(first_key, n_keys, key_limit):
        sb = lax.broadcasted_iota(I32, (LANE, n_keys), 0)
        kt = lax.broadcasted_iota(I32, (LANE, n_keys), 1)
        hit = (sb == first_key // SEL_BLOCK + kt // SEL_BLOCK) & (first_key + kt < key_limit)
        expand = jnp.where(hit, 1.0, 0.0).astype(BF16)
        return [_dot(sel, expand) > 0.5 for sel in sels]

    n_far_w = WINDOW - tq
    far0 = pl.multiple_of(jnp.maximum(i - WINDOW // tq, 0) * tq, tq)
    kv_n = swa_ref[pl.ds(near0, 2 * tq), :]
    kv_f = swa_ref[pl.ds(far0, n_far_w), :]
    kpos_f = far0 + lax.broadcasted_iota(I32, (tq, n_far_w), 1)
    pen_n = jnp.where((d_near >= 0) & (d_near < WINDOW), 0.0, NEG)
    pen_f = jnp.where((kpos_f < near0) & (pos_q - kpos_f < WINDOW), 0.0, NEG)
    s_n = _dot_nt(q, kv_n[:, :LANE]) + near_bias + heads([pen_n] * NSA_KV_HEADS)
    s_f = _dot_nt(q, kv_f[:, :LANE]) + far_col + heads([pen_f] * NSA_KV_HEADS)
    m_w = jnp.maximum(_rowmax(s_n), _rowmax(s_f))
    p_n = jnp.exp2(s_n - m_w)
    p_f = jnp.exp2(s_f - m_w)
    l_w = _rowsum(p_n) + _rowsum(p_f)
    o_swa = (_dot(p_n.astype(BF16), kv_n[:, LANE:]) + _dot(p_f.astype(BF16), kv_f[:, LANE:])) / l_w

    kv_n = slc_ref[pl.ds(near0, 2 * tq), :]
    pens = [jnp.where(hit & (d_near >= 0), 0.0, NEG) for hit in block_sel(near0, 2 * tq, seq_len)]
    s_n = _dot_nt(q, kv_n[:, :LANE]) + near_bias + heads(pens)
    m_s = _rowmax(s_n)
    p_n = jnp.exp2(s_n - m_s)
    m_ref[...] = m_s
    l_ref[...] = _rowsum(p_n)
    acc_ref[...] = _dot(p_n.astype(BF16), kv_n[:, LANE:])

    n_far = (near0 + tkf - 1) // tkf
    half = NSA_GROUP * tq

    def far_logits(c):
        first = pl.multiple_of(jnp.minimum(c, jnp.maximum(n_far - 1, 0)) * tkf, tkf)
        k = slc_ref[pl.ds(first, tkf), :LANE]
        return tuple(_dot_nt(q[kvh * half:(kvh + 1) * half], k) for kvh in range(NSA_KV_HEADS))

    def slc_far(c, qk):
        qk_next = far_logits(c + 1)
        first = pl.multiple_of(c * tkf, tkf)
        v = slc_ref[pl.ds(first, tkf), LANE:]
        hits = block_sel(first, tkf, near0)
        for kvh in range(NSA_KV_HEADS):
            r = slice(kvh * half, (kvh + 1) * half)
            pen = jnp.where(hits[kvh], 0.0, NEG)
            s_ = qk[kvh] + far_col[r] + jnp.concatenate([pen] * NSA_GROUP, axis=0)
            m_prev = m_ref[r]
            m_new = jnp.maximum(m_prev, _rowmax(s_))
            p = jnp.exp2(s_ - m_new)
            alpha = jnp.exp2(m_prev - m_new)
            l_ref[r] = alpha * l_ref[r] + _rowsum(p)
            acc_ref[r] = alpha * acc_ref[r] + _dot(p.astype(BF16), v)
            m_ref[r] = m_new
        return qk_next

    lax.fori_loop(0, n_far, slc_far, far_logits(0))
    o_slc = acc_ref[...] / l_ref[...]

    gates = gate_ref[...]
    for h in range(NSA_HEADS):
        kv, g = divmod(h, NSA_GROUP)
        r = slice(h * tq, (h + 1) * tq)
        c = kv * LANE + g
        o = (gates[:, c:c + 1] * o_cmp[r] + gates[:, c + NSA_GROUP:c + NSA_GROUP + 1] * o_slc[r]
             + gates[:, c + 2 * NSA_GROUP:c + 2 * NSA_GROUP + 1] * o_swa[r])
        o_ref[:, h * LANE:(h + 1) * LANE] = o.astype(o_ref.dtype)


def _nsa_prompt(qn, kc, vc, bias_c, ovl, slc_b, swa_b, tt, gates, *, nb, seq, n_cmp, n_sel):
    tq = 128
    nq = seq // tq
    rows = NSA_HEADS * tq
    hw = NSA_HEADS * LANE
    nhp = kc.shape[1]
    return pl.pallas_call(
        functools.partial(_nsa_prompt_kernel, tq=tq, tkf=min(512, seq), n_cmp=n_cmp, n_sel=n_sel, seq_len=seq),
        grid=(nb, nq),
        in_specs=[
            pl.BlockSpec((tq, hw), lambda b, i: (b * nq + i, 0)),
            pl.BlockSpec((None, nhp, LANE), lambda b, i: (b, 0, 0)),
            pl.BlockSpec((None, nhp, LANE), lambda b, i: (b, 0, 0)),
            pl.BlockSpec(bias_c.shape, lambda b, i: (0, 0)),
            pl.BlockSpec(ovl.shape, lambda b, i: (0, 0)),
            pl.BlockSpec((None, seq, KV_COLS), lambda b, i: (b, 0, 0)),
            pl.BlockSpec((None, seq, KV_COLS), lambda b, i: (b, 0, 0)),
            pl.BlockSpec(tt.shape, lambda b, i: (0, 0, 0)),
            pl.BlockSpec((tq, NSA_KV_HEADS * LANE), lambda b, i: (b * nq + i, 0)),
        ],
        out_specs=pl.BlockSpec((tq, hw), lambda b, i: (b * nq + i, 0)),
        out_shape=jax.ShapeDtypeStruct((nb * seq, hw), BF16),
        scratch_shapes=[pltpu.VMEM((rows, 1), F32), pltpu.VMEM((rows, 1), F32), pltpu.VMEM((rows, LANE), F32)],
        compiler_params=_cparams(("arbitrary", "arbitrary")), name="nsa_prompt",
    )(qn, kc, vc, bias_c, ovl, slc_b, swa_b, tt, gates)


def _nsa_sample_kernel(pt_ref, q_ref, kc_ref, vc_ref, bc_ref, ovl_ref, bs_ref, sn_ref, bsn_ref, st_ref, wn_ref,
                       bw_ref, bwn_ref, gate_ref, *rest, g, nq, nbs, n_cmp, n_sel, buf):
    o_ref, sel_ref, ocmp_ref, oswa_ref, m_ref, l_ref, acc_ref = rest[nbs * g:]
    s_idx = pl.program_id(1)
    last = pl.num_programs(1) - 1
    rows = NSA_HEADS * nq
    qi = lax.broadcasted_iota(I32, (rows, 1), 0) % nq
    qs = [jnp.concatenate([q_ref[bl, :, h * LANE:(h + 1) * LANE] for h in range(NSA_HEADS)], axis=0)
          for bl in range(nbs)]

    def first_step(bl):
        q = qs[bl]
        kc_r, vc_r, st_r, wn_r = kc_ref.at[bl], vc_ref.at[bl], st_ref.at[bl], wn_ref.at[bl]
        _nsa_sample_first(q, qi, kc_r, vc_r, bc_ref, ovl_ref, st_r, wn_r, bw_ref, bwn_ref, sel_ref.at[bl], ocmp_ref.at[bl],
                          oswa_ref.at[bl], m_ref.at[bl], l_ref.at[bl], acc_ref.at[bl], nq=nq, n_cmp=n_cmp, n_sel=n_sel, buf=buf)

    @pl.when(s_idx == 0)
    def _():
        for bl in range(nbs):
            first_step(bl)

    nk = g * rest[0].shape[1]
    sb = lax.broadcasted_iota(I32, (sel_ref.shape[2], nk), 0)
    kt = lax.broadcasted_iota(I32, (sel_ref.shape[2], nk), 1)
    expand = jnp.where(sb == s_idx * (nk // SEL_BLOCK) + kt // SEL_BLOCK, 1.0, 0.0).astype(BF16)
    for bl in range(nbs):
        pages = rest[bl * g:(bl + 1) * g]
        kcat = jnp.concatenate([pages[u][:LANE].astype(BF16) for u in range(g)], axis=1)
        vcat = jnp.concatenate([pages[u][LANE:].astype(BF16) for u in range(g)], axis=1)
        s = _dot(qs[bl], kcat) + bs_ref[jnp.where(s_idx == last, 1, 0)]
        mask = _dot(sel_ref[bl], expand) > 0.5
        _flash_step(s, mask, vcat, m_ref.at[bl], l_ref.at[bl], acc_ref.at[bl], v_keys_on_lanes=True)

    @pl.when(s_idx == last)
    def _():
        for bl in range(nbs):
            q = qs[bl]
            m_r, l_r, acc_r = m_ref.at[bl], l_ref.at[bl], acc_ref.at[bl]
            sn = sn_ref[bl]
            s_n = _dot_nt(q, sn[:, :LANE]) + bsn_ref[...]
            mask_n = (lax.broadcasted_iota(I32, (rows, LANE), 1) <= qi) & (sel_ref[bl, :, n_sel - 1:n_sel] > 0.5)
            _flash_step(s_n, mask_n, sn[:, LANE:], m_r, l_r, acc_r)
            o_slc = _flash_out(l_r, acc_r)
            o_cmp, o_swa = ocmp_ref[bl], oswa_ref[bl]
            gates = gate_ref[bl]
            for h in range(NSA_HEADS):
                kv, gg = divmod(h, NSA_GROUP)
                r = slice(h * nq, (h + 1) * nq)
                c = kv * LANE + gg
                o = (gates[:, c:c + 1] * o_cmp[r] + gates[:, c + NSA_GROUP:c + NSA_GROUP + 1] * o_slc[r]
                     + gates[:, c + 2 * NSA_GROUP:c + 2 * NSA_GROUP + 1] * o_swa[r])
                o_ref[bl, :, h * LANE:(h + 1) * LANE] = o.astype(o_ref.dtype)


def _nsa_sample_first(q, qi, kc_ref, vc_ref, bc_ref, ovl_ref, st_ref, wn_ref, bw_ref, bwn_ref, sel_ref, ocmp_ref, oswa_ref,
                      m_ref, l_ref, acc_ref, *, nq, n_cmp, n_sel, buf):
    rows = NSA_HEADS * nq

    def compressed_and_select():
        nhp = kc_ref.shape[0]
        s = _dot_nt(q, kc_ref[...]) + bc_ref[...]
        mask_c = lax.broadcasted_iota(I32, (rows, nhp), 1) < n_cmp
        p_cmp = _softmax_masked(s, mask_c)
        ocmp_ref[...] = _dot(p_cmp.astype(BF16), vc_ref[...])
        psums = []
        for kv in range(NSA_KV_HEADS):
            ps = p_cmp[kv * NSA_GROUP * nq:kv * NSA_GROUP * nq + nq]
            for gg in range(1, NSA_GROUP):
                lo = (kv * NSA_GROUP + gg) * nq
                ps = ps + p_cmp[lo:lo + nq]
            psums.append(ps)
        n_kq = NSA_KV_HEADS * nq
        psum = jnp.concatenate(psums + [jnp.zeros((LANE - n_kq, nhp), F32)], axis=0)
        n_blk = ovl_ref.shape[0]
        n_sel_rows = -(-n_sel // 8) * 8
        imp_t = _dot_nt(ovl_ref[...], _split3(psum))[:n_sel_rows]
        blk = lax.broadcasted_iota(I32, imp_t.shape, 0)
        cur = n_sel - 1
        forced = (blk == 0) | (blk == cur) | (blk == cur - 1)
        imp_t = jnp.where(blk >= n_sel, -jnp.inf, jnp.where(forced, jnp.inf, imp_t))
        sel_t = _topk_mask(imp_t, min(SEL_TOPN, n_sel), axis=0)
        if n_sel_rows < n_blk:
            sel_t = jnp.concatenate([sel_t, jnp.zeros((n_blk - n_sel_rows, LANE), F32)], axis=0)
        sel = sel_t.T.astype(BF16)
        sel_ref[...] = jnp.concatenate(
            [sel[kv * nq:(kv + 1) * nq] for kv in range(NSA_KV_HEADS) for _ in range(NSA_GROUP)], axis=0)

    def window():
        st = st_ref[...]
        s_w = _dot(q, st[:LANE].astype(BF16)) + bw_ref[...]
        d_w = buf + qi - lax.broadcasted_iota(I32, (rows, buf), 1)
        mask_w = (d_w >= 0) & (d_w < WINDOW)
        wn = wn_ref[...]
        s_n = _dot_nt(q, wn[:, :LANE]) + bwn_ref[...]
        mask_n = lax.broadcasted_iota(I32, (rows, LANE), 1) <= qi
        s_w = jnp.where(mask_w, s_w, NEG)
        s_n = jnp.where(mask_n, s_n, NEG)
        m = jnp.maximum(_rowmax(s_w), _rowmax(s_n))
        p_w = jnp.where(mask_w, jnp.exp2(s_w - m), 0.0)
        p_n = jnp.where(mask_n, jnp.exp2(s_n - m), 0.0)
        l = _rowsum(p_w) + _rowsum(p_n)
        o = _dot_nt(p_w.astype(BF16), st[LANE:].astype(BF16)) + _dot(p_n.astype(BF16), wn[:, LANE:])
        oswa_ref[...] = o / jnp.where(l > 0.0, l, 1.0)

    compressed_and_select()
    window()
    _flash_init(m_ref, l_ref, acc_ref)


def _nsa_sample(page_table, qn, kc, vc, bias_c, ovl, bias_s, slc_new, bias_sn, state_swa, swa_new, bias_w, bias_wn,
                gates, cache_slc, *, g, nbs, n_cmp, n_sel):
    nb, nq = qn.shape[0], qn.shape[1]
    n_pages = page_table.shape[1]
    rows = NSA_HEADS * nq
    buf = state_swa.shape[2]
    hw = NSA_HEADS * LANE
    bmap = lambda b, s, pt: (b, 0, 0)
    c2 = lambda b, s, pt: (0, 0)
    per_elem = lambda a: pl.BlockSpec((nbs,) + a.shape[1:], bmap)
    gs = pltpu.PrefetchScalarGridSpec(
        num_scalar_prefetch=1, grid=(nb // nbs, n_pages // g),
        in_specs=[
            per_elem(qn), per_elem(kc), per_elem(vc),
            pl.BlockSpec(bias_c.shape, c2),
            pl.BlockSpec(ovl.shape, c2),
            pl.BlockSpec(bias_s.shape, lambda b, s, pt: (0, 0, 0)),
            per_elem(slc_new),
            pl.BlockSpec(bias_sn.shape, c2),
            per_elem(state_swa), per_elem(swa_new),
            pl.BlockSpec(bias_w.shape, c2),
            pl.BlockSpec(bias_wn.shape, c2),
            per_elem(gates),
        ] + _page_specs(g, (None,) + cache_slc.shape[1:], nbs=nbs),
        out_specs=pl.BlockSpec((nbs, nq, hw), bmap),
        scratch_shapes=[pltpu.VMEM((nbs, rows, ovl.shape[0]), BF16), pltpu.VMEM((nbs, rows, LANE), F32),
                        pltpu.VMEM((nbs, rows, LANE), F32), pltpu.VMEM((nbs, rows, 1), F32), pltpu.VMEM((nbs, rows, 1), F32),
                        pltpu.VMEM((nbs, rows, LANE), F32)],
    )
    return pl.pallas_call(
        functools.partial(_nsa_sample_kernel, g=g, nq=nq, nbs=nbs, n_cmp=n_cmp, n_sel=n_sel, buf=buf), grid_spec=gs,
        out_shape=jax.ShapeDtypeStruct((nb, nq, hw), BF16),
        compiler_params=_cparams(("arbitrary", "arbitrary")), name="nsa_sample",
    )(page_table, qn, kc, vc, bias_c, ovl, bias_s, slc_new, bias_sn, state_swa, swa_new, bias_w, bias_wn, gates,
      *([cache_slc] * (nbs * g)))


def _outproj_kernel(x_ref, om_ref, on_ref, g1_ref, sc_ref, sh_ref, wom_ref, won_ref, gn2_ref, wr_ref, br_ref,
                    x1_ref, h2_ref, ti_ref, tw_ref):
    mix = _dot(om_ref[...], wom_ref[...]) + _dot(on_ref[...], won_ref[...])
    x1 = x_ref[...] + g1_ref[...] * mix
    x1_ref[...] = x1
    h2 = _rms(x1, gn2_ref[...]) * (1.0 + sc_ref[...]) + sh_ref[...]
    h2_ref[...] = h2.astype(h2_ref.dtype)
    h3, w3 = _split3(h2), _split3(wr_ref[...])
    d = h2.shape[1]
    h_cat = jnp.concatenate([h3[:, :d], h3[:, :d], h3[:, d:2 * d]], axis=1)
    w_cat = jnp.concatenate([w3[:, :d], w3[:, d:2 * d], w3[:, :d]], axis=1)
    logits = _dot_nt(h_cat, w_cat) + br_ref[...]
    lane = lax.broadcasted_iota(I32, logits.shape, 1)
    v = logits
    vals, idxs = [], []
    for _ in range(TOP_K):
        m = jnp.max(v, axis=-1, keepdims=True)
        idx = jnp.min(jnp.where(v == m, lane, 1 << 20), axis=-1, keepdims=True)
        vals.append(m)
        idxs.append(idx)
        v = jnp.where(lane == idx, -jnp.inf, v)
    es = [jnp.exp(m - vals[0]) for m in vals]
    tot = es[0]
    for e in es[1:]:
        tot = tot + e
    ti = jnp.zeros(logits.shape, I32)
    tw = jnp.zeros(logits.shape, F32)
    for k in range(TOP_K):
        ti = jnp.where(lane == k, idxs[k], ti)
        tw = jnp.where(lane == k, es[k] / tot, tw)
    ti_ref[...] = ti
    tw_ref[...] = tw


def _outproj(x, o_mla, o_nsa, g1, sc, sh, W, *, rows_per_mod, tm):
    t, d = x.shape
    nt = t // tm
    if rows_per_mod == 1:
        tiles_per_mod = (t // g1.shape[0]) // tm
        mod_map = lambda i: (i // tiles_per_mod, 0, 0)
    else:
        mod_map = lambda i: (i, 0, 0)
    row = lambda i: (i, 0)
    c2 = lambda i: (0, 0)
    mod = pl.BlockSpec((None, rows_per_mod, d), mod_map)
    return pl.pallas_call(
        _outproj_kernel, grid=(nt,),
        in_specs=[pl.BlockSpec((tm, d), row), pl.BlockSpec((tm, o_mla.shape[1]), row), pl.BlockSpec((tm, o_nsa.shape[1]), row),
                  mod, mod, mod, pl.BlockSpec(W["w_o_mla"].shape, c2), pl.BlockSpec(W["w_o_nsa"].shape, c2),
                  pl.BlockSpec((1, d), c2), pl.BlockSpec(W["w_router"].shape, c2), pl.BlockSpec((1, LANE), c2)],
        out_specs=[pl.BlockSpec((tm, d), row), pl.BlockSpec((tm, d), row), pl.BlockSpec((tm, LANE), row),
                   pl.BlockSpec((tm, LANE), row)],
        out_shape=[jax.ShapeDtypeStruct((t, d), F32), jax.ShapeDtypeStruct((t, d), F32),
                   jax.ShapeDtypeStruct((t, LANE), I32), jax.ShapeDtypeStruct((t, LANE), F32)],
        compiler_params=_cparams(("arbitrary",)), name="out_proj",
    )(x, o_mla, o_nsa, g1, sc, sh, W["w_o_mla"], W["w_o_nsa"], W["g_norm2"], W["w_router"], W["b_router"])


def _expert_kernel(te_ref, tv_ref, x_ref, wgu_ref, bgu_ref, wd_ref, bd_ref, o_ref, wgu_bf, wd_bf):
    t = pl.program_id(0)

    @pl.when((t == 0) | (te_ref[t] != te_ref[jnp.maximum(t - 1, 0)]))
    def _():
        wgu_bf[...] = wgu_ref[...].astype(BF16)
        wd_bf[...] = wd_ref[...].astype(BF16)

    @pl.when(tv_ref[t] > 0)
    def _():
        d_ff = wd_ref.shape[0]
        gu = _dot(x_ref[...].astype(BF16), wgu_bf[...]) + bgu_ref[...]
        glu = jnp.minimum(gu[:, :d_ff], SWIGLU_LIMIT)
        lin = jnp.clip(gu[:, d_ff:], -SWIGLU_LIMIT, SWIGLU_LIMIT)
        act = glu * _sigmoid(SWIGLU_ALPHA * glu) * (lin + 1.0)
        o_ref[...] = _dot(act.astype(BF16), wd_bf[...]) + bd_ref[...]

    @pl.when(tv_ref[t] == 0)
    def _():
        o_ref[...] = jnp.zeros_like(o_ref)


def _experts(tile_expert, tile_valid, x_sorted, W, *, tm):
    ns, d = x_sorted.shape
    gs = pltpu.PrefetchScalarGridSpec(
        num_scalar_prefetch=2, grid=(ns // tm,),
        in_specs=[pl.BlockSpec((tm, d), lambda t, te, tv: (t, 0)),
                  pl.BlockSpec((None,) + W["w_gate_up"].shape[1:], lambda t, te, tv: (te[t], 0, 0)),
                  pl.BlockSpec((None,) + W["b_gate_up"].shape[1:], lambda t, te, tv: (te[t], 0, 0)),
                  pl.BlockSpec((None,) + W["w_down"].shape[1:], lambda t, te, tv: (te[t], 0, 0)),
                  pl.BlockSpec((None,) + W["b_down"].shape[1:], lambda t, te, tv: (te[t], 0, 0))],
        out_specs=pl.BlockSpec((tm, d), lambda t, te, tv: (t, 0)),
        scratch_shapes=[pltpu.VMEM(W["w_gate_up"].shape[1:], BF16), pltpu.VMEM(W["w_down"].shape[1:], BF16)],
    )
    return pl.pallas_call(
        _expert_kernel, grid_spec=gs, out_shape=jax.ShapeDtypeStruct((ns, d), F32),
        compiler_params=_cparams(("arbitrary",)), name="experts",
    )(tile_expert, tile_valid, x_sorted, W["w_gate_up"], W["b_gate_up"], W["w_down"], W["b_down"])


def _final_kernel(x1_ref, tw_ref, *rest):
    parts, (g2_ref, gf_ref, o_ref) = rest[:TOP_K], rest[TOP_K:]
    tw = tw_ref[...]
    moe = tw[:, 0:1] * parts[0][...]
    for k in range(1, TOP_K):
        moe = moe + tw[:, k:k + 1] * parts[k][...]
    o_ref[...] = _rms(x1_ref[...] + g2_ref[...] * moe, gf_ref[...])


def _final(x1, top_w, moe_parts, g2, g_final, *, rows_per_mod, tm):
    t, d = x1.shape
    if rows_per_mod == 1:
        tiles_per_mod = (t // g2.shape[0]) // tm
        mod_map = lambda i: (i // tiles_per_mod, 0, 0)
    else:
        mod_map = lambda i: (i, 0, 0)
    row = lambda i: (i, 0)
    return pl.pallas_call(
        _final_kernel, grid=(t // tm,),
        in_specs=[pl.BlockSpec((tm, d), row), pl.BlockSpec((tm, LANE), row)] + [pl.BlockSpec((tm, d), row)] * TOP_K
        + [pl.BlockSpec((None, rows_per_mod, d), mod_map), pl.BlockSpec((1, d), lambda i: (0, 0))],
        out_specs=pl.BlockSpec((tm, d), row), out_shape=jax.ShapeDtypeStruct((t, d), F32),
        compiler_params=_cparams(("arbitrary",)), name="final_norm")(x1, top_w, *moe_parts, g2, g_final)


def _t5_bucket(dist):
    n = jnp.maximum(dist, 0)
    max_exact = NUM_BUCKETS // 2
    nf = jnp.maximum(n, 1).astype(F32)
    large = max_exact + (jnp.log(nf / max_exact) / math.log(MAX_DISTANCE / max_exact)
                         * (NUM_BUCKETS - max_exact)).astype(I32)
    return jnp.where(n < max_exact, n, jnp.minimum(large, NUM_BUCKETS - 1))


def _bias_rows(rel_bias, dist):
    bucket = _t5_bucket(dist)
    out = jnp.zeros((rel_bias.shape[1],) + dist.shape, F32)
    for b in range(NUM_BUCKETS):
        out = out + jnp.where(bucket == b, 1.0, 0.0)[None] * rel_bias[b][:, None, None]
    return out * LOG2E


def _rope_table(pos):
    half = MLA_ROPE // 2
    inv_freq = 1.0 / (ROPE_THETA ** (jnp.arange(half, dtype=F32) / half))
    ang = pos.astype(F32)[:, None] * inv_freq[None, :]
    pad = jnp.zeros((pos.shape[0], LANE - MLA_ROPE), F32)
    cos, sin = jnp.cos(ang), jnp.sin(ang)
    return jnp.concatenate([cos, cos, pad, sin, sin, pad], axis=1)


def _rot_cols(w):
    half = MLA_ROPE // 2
    return jnp.concatenate([-w[..., half:], w[..., :half]], axis=-1)


def _pad_last(w, n):
    return jnp.pad(w, [(0, 0)] * (w.ndim - 1) + [(0, n - w.shape[-1])])


def _pack_weights(w_in, g_norm1, g_norm2, g_q_a, w_q_b, g_kv_a, w_kv_b, cmp_pe, cmp_w1, cmp_w2, w_o, w_router,
                  b_router, w_gate_up, b_gate_up, w_down, b_down):
    d = w_in.shape[0]
    sizes = (MLA_Q_LORA, MLA_KV_LORA, MLA_ROPE, NSA_HEADS * HEAD_DIM, KV_COLS, KV_COLS, KV_COLS, 3 * NSA_HEADS)
    offs = [0]
    for s in sizes:
        offs.append(offs[-1] + s)
    w_qa, w_ckv, w_kr, w_qn, w_cmp, w_slc, w_swa, w_g = [w_in[:, offs[k]:offs[k + 1]] for k in range(8)]
    wq = w_qn.reshape(d, NSA_KV_HEADS, NSA_GROUP, HEAD_DIM)
    wq_pad = jnp.concatenate(
        [jnp.pad(wq[:, kv], ((0, 0), (0, 0), (kv * HEAD_DIM, LANE - (kv + 1) * HEAD_DIM))).reshape(d, NSA_GROUP * LANE)
         for kv in range(NSA_KV_HEADS)], axis=1)
    wg = jnp.transpose(w_g.reshape(d, NSA_KV_HEADS, NSA_GROUP, 3), (0, 1, 3, 2)).reshape(d, NSA_KV_HEADS, 3 * NSA_GROUP)
    wg_pad = _pad_last(wg, LANE).reshape(d, NSA_KV_HEADS * LANE)
    w_in_p = jnp.concatenate([w_qa, w_ckv, wq_pad, w_cmp, w_slc, w_swa, _pad_last(w_kr, LANE),
                              _pad_last(_rot_cols(w_kr), LANE), wg_pad], axis=1).astype(BF16)
    assert w_in_p.shape[1] == _C_END
    nope = _pad_last(w_q_b[:, :, :MLA_NOPE], LANE).reshape(MLA_Q_LORA, MLA_HEADS * LANE)
    rp = w_q_b[:, :, MLA_NOPE:]
    w_qb = jnp.concatenate([nope, _pad_last(rp, LANE).reshape(MLA_Q_LORA, -1),
                            _pad_last(_rot_cols(rp), LANE).reshape(MLA_Q_LORA, -1)], axis=1).astype(BF16)
    w_kn = jnp.transpose(w_kv_b[:, :, :MLA_NOPE], (1, 2, 0))
    w_kn = jnp.pad(w_kn, ((0, 0), (0, LANE - MLA_NOPE), (0, 0))).astype(BF16)
    wv = jnp.transpose(w_kv_b[:, :, MLA_NOPE:], (1, 0, 2))
    wv_pad = jnp.stack([jnp.pad(wv[h], ((0, 0), (h * MLA_V, (MLA_HEADS - 1 - h) * MLA_V))) for h in range(MLA_HEADS)]).astype(BF16)
    n_mla = MLA_HEADS * MLA_V
    won = w_o[n_mla:].reshape(NSA_KV_HEADS, NSA_GROUP, HEAD_DIM, d)
    won_pad = jnp.concatenate(
        [jnp.pad(won[kv], ((0, 0), (kv * HEAD_DIM, LANE - (kv + 1) * HEAD_DIM), (0, 0))).reshape(NSA_GROUP * LANE, d)
         for kv in range(NSA_KV_HEADS)], axis=0).astype(BF16)
    base = jnp.concatenate([cmp_w1[:, :CMP_STRIDE], cmp_w1[:, CMP_STRIDE:]], axis=-1)
    z = jnp.zeros_like(base)
    blk = jnp.concatenate([jnp.concatenate([base, z], axis=-1), jnp.concatenate([z, base], axis=-1)], axis=2)
    w_pair = blk.reshape(2, CMP_STRIDE // 2, 2 * NSA_KV_HEADS * HEAD_DIM, NSA_KV_HEADS * 2 * CMP_HIDDEN).astype(BF16)
    pe8 = jnp.broadcast_to(cmp_pe.reshape(2, 1, CMP_LEN * HEAD_DIM), (2, 8, CMP_LEN * HEAD_DIM))
    w1r = cmp_w1.reshape(2, CMP_LEN * HEAD_DIM, CMP_HIDDEN).astype(BF16)
    w2pad = jnp.stack([jnp.stack([jnp.pad(cmp_w2[j], ((0, 0), (kv * HEAD_DIM, LANE - (kv + 1) * HEAD_DIM)))
                                  for kv in range(NSA_KV_HEADS)]) for j in range(2)]).astype(BF16)
    return dict(
        w_in=w_in_p, g_norm1=g_norm1[None], g_norm2=g_norm2[None], g_q_a=g_q_a[None], g_kv_a=g_kv_a[None],
        w_qb=w_qb, w_kn=w_kn, wv=wv_pad, wv_cat=w_kv_b[:, :, MLA_NOPE:].reshape(MLA_KV_LORA, n_mla).astype(BF16),
        w_o_mla=w_o[:n_mla].astype(BF16), w_o_nsa=won_pad,
        w_pair=w_pair, pe8=pe8, w1r=w1r, w2pad=w2pad,
        w_router=_pad_last(w_router, LANE).T, b_router=jnp.pad(b_router, (0, LANE - N_EXPERTS), constant_values=NEG)[None],
        w_gate_up=w_gate_up, b_gate_up=b_gate_up[:, None, :], w_down=w_down, b_down=b_down[:, None, :])


def _overlap(n_half_pad, n_sel, n_sel_pad):
    c_start = jnp.arange(n_half_pad) * CMP_STRIDE
    s_start = jnp.arange(n_sel_pad) * SEL_BLOCK
    ov = (c_start[:, None] < s_start[None, :] + SEL_BLOCK) & (c_start[:, None] + CMP_LEN > s_start[None, :])
    return (ov & (jnp.arange(n_sel_pad) < n_sel)[None, :]).astype(F32)


def _moe_dispatch(top_i, tm):
    t = top_i.shape[0]
    a = t * TOP_K
    n_tiles = -(-a // tm) + N_EXPERTS
    ns = n_tiles * tm
    e_flat = top_i[:, :TOP_K].reshape(a)
    onehot = (e_flat[:, None] == jnp.arange(N_EXPERTS, dtype=I32)[None, :]).astype(I32)
    csum = jnp.cumsum(onehot, axis=0)
    counts = csum[-1]
    padded = ((counts + tm - 1) // tm) * tm
    pend = jnp.cumsum(padded)
    pstart = pend - padded
    start = jnp.cumsum(counts) - counts
    slot_of_assign = jnp.sum(onehot * (pstart[None, :] + csum - 1), axis=1).reshape(t, TOP_K)
    _, order = lax.sort((e_flat, jnp.arange(a, dtype=I32)), num_keys=1, is_stable=True)
    tile_start = jnp.arange(n_tiles, dtype=I32) * tm
    tile_expert = jnp.minimum(jnp.sum((pend[None, :] <= tile_start[:, None]).astype(I32), axis=1), N_EXPERTS - 1)
    tile_valid = (tile_start < pend[-1]).astype(I32)
    e_hot = (tile_expert[:, None] == jnp.arange(N_EXPERTS, dtype=I32)[None, :]).astype(I32)
    rank0 = tile_start - jnp.sum(e_hot * pstart[None, :], axis=1)
    base = jnp.clip(jnp.sum(e_hot * start[None, :], axis=1) + rank0, 0, a)
    order_pad = jnp.pad(order, (0, tm))
    run = jax.vmap(lambda b: lax.dynamic_slice(order_pad, (b,), (tm,)))(base)
    rank = rank0[:, None] + jnp.arange(tm, dtype=I32)[None, :]
    valid = (rank < jnp.sum(e_hot * counts[None, :], axis=1, keepdims=True)) & (tile_valid[:, None] > 0)
    tok_of_slot = jnp.where(valid, run // TOP_K, 0).reshape(ns)
    return tok_of_slot, slot_of_assign, tile_expert, tile_valid


def kernel(x_prompt, x_sample, c_prompt, c_sample, cache_mla, cache_nsa_cmp, cache_nsa_slc, state_nsa_swa, page_table, rel_bias, w_ada, b_ada, g_norm1, g_norm2, w_in, g_q_a, w_q_b, g_kv_a, w_kv_b, cmp_pe, cmp_w1, cmp_w2, w_o, w_router, b_router, w_gate_up, b_gate_up, w_down, b_down, g_final):
    depth = w_in.shape[0]
    assert depth == 1, "single-layer decoder step"
    nb, seq, d = x_prompt.shape
    nbd, nq, _ = x_sample.shape
    n_pages = page_table.shape[1]
    page = cache_mla.shape[2]
    past = n_pages * page
    buf = state_nsa_swa.shape[2]
    assert page == LANE and nq <= 8 and seq % 512 == 0 and past % SEL_BLOCK == 0 and seq >= WINDOW
    tp, ts = nb * seq, nbd * nq
    W = _pack_weights(w_in[0], g_norm1[0], g_norm2[0], g_q_a[0], w_q_b[0], g_kv_a[0], w_kv_b[0], cmp_pe[0], cmp_w1[0],
                      cmp_w2[0], w_o[0], w_router[0], b_router[0], w_gate_up[0], b_gate_up[0], w_down[0], b_down[0])

    n_c = nb + nbd
    n_c_pad = -(-n_c // 8) * 8
    c_all = jnp.pad(jnp.concatenate([c_prompt, c_sample], axis=0), ((0, n_c_pad - n_c), (0, 0)))
    mod = _ada_mod(c_all, w_ada[0].astype(BF16), b_ada[0][None])
    mod_p = [m[:, None, :] for m in jnp.split(mod[:nb], 6, axis=-1)]
    tm_s = min(256, ts)
    mod_s = [jnp.repeat(m, nq, axis=0).reshape(ts // tm_s, tm_s, d) for m in jnp.split(mod[nb:n_c], 6, axis=-1)]

    tm_p = 256
    pos_p = jnp.arange(seq)
    pos_s = past + jnp.arange(nq)
    P = _proj(x_prompt.reshape(tp, d), mod_p[1], mod_p[0], _rope_table(pos_p), W, rows_per_mod=1,
              cs_period_tiles=seq // tm_p, tm=tm_p, states_t_batches=nb)
    cs_s = jnp.tile(_rope_table(pos_s), (tm_s // nq, 1))
    S_ = _proj(x_sample.reshape(ts, d), mod_s[1], mod_s[0], cs_s, W, rows_per_mod=tm_s, cs_period_tiles=1, tm=tm_s)
    p_qmla, p_kq, p_mla_t, p_qn, p_cmp, p_cmp_t, p_slc_t, p_swa_t, p_slcb, p_swab, p_gate = P
    s_qmla, s_kq, s_mla, s_qn, _, s_cmp, s_slc, s_swa, s_slcb, s_swab, s_gate = S_

    def pad_new(a):
        return jnp.pad(a.reshape(nbd, nq, a.shape[1]), ((0, 0), (0, LANE - nq), (0, 0)))

    o_mla_p = _mla_prompt(p_qmla, p_kq, W["wv"], nb=nb, seq=seq)
    g_pages = min(16, n_pages)
    cache_mla_t = jnp.swapaxes(cache_mla[0], 1, 2)
    cache_cmp_t = jnp.moveaxis(cache_nsa_cmp[0], 1, -1).reshape(-1, KV_COLS, page)
    cache_slc_t = jnp.moveaxis(cache_nsa_slc[0], 1, -1).reshape(-1, KV_COLS, page)
    state_swa_t = jnp.moveaxis(state_nsa_swa[0], 1, -1).reshape(nbd, KV_COLS, buf)
    o_mla_s = _mla_sample(page_table, s_qmla.reshape(nbd, nq, -1), pad_new(s_kq), W["wv_cat"], cache_mla_t, g=g_pages,
                          nbs=2 if nbd % 2 == 0 else 1)

    assert (past + nq) // CMP_STRIDE == past // CMP_STRIDE
    n_pp = tp // LANE
    g_cp = min(32, n_pp)
    ab_p = _cmp_first(jnp.arange(n_pp, dtype=I32), p_cmp.reshape(n_pp, LANE, KV_COLS), W["w_pair"], g=g_cp,
                      transposed=False)
    kc_p, vc_p = _cmp_finish(ab_p, W["pe8"], W["w1r"], W["w2pad"], nb=nb)
    g_cs = min(32, nbd * n_pages)
    ab_s = _cmp_first(page_table.reshape(nbd * n_pages), cache_cmp_t, W["w_pair"], g=g_cs, transposed=True)
    kc_s, vc_s = _cmp_finish(ab_s, W["pe8"], W["w1r"], W["w2pad"], nb=nbd)

    nh_p = seq // CMP_STRIDE
    n_sel_p = -(-seq // SEL_BLOCK)
    tq = 128
    far_h = _bias_rows(rel_bias, jnp.full((1, 1), MAX_DISTANCE))
    d_win = (jnp.arange(tq)[:, None] - CMP_STRIDE * (jnp.arange(LANE)[None, :] - CMP_WIN_LO) - (CMP_LEN - 1))
    assert CMP_WIN_LO * CMP_STRIDE >= MAX_DISTANCE + CMP_LEN and LANE - CMP_WIN_LO >= tq // CMP_STRIDE
    bias_c_p = jnp.where((d_win >= 0)[None], _bias_rows(rel_bias, d_win) - far_h, 0.0)
    bias_c_p = bias_c_p.reshape(NSA_HEADS * tq, LANE)
    di = jnp.arange(tq)[:, None] - jnp.arange(tq)[None, :]
    tt = jnp.stack([_bias_rows(rel_bias, di), _bias_rows(rel_bias, di + tq),
                    _bias_rows(rel_bias, jnp.full((tq, tq), MAX_DISTANCE))], axis=1)
    tt = tt.transpose(1, 0, 2, 3).reshape(3, NSA_HEADS * tq, tq)
    assert n_sel_p <= LANE
    ovl_p = jnp.tile(_overlap(nh_p, n_sel_p, LANE).T, (1, 3)).astype(BF16)
    o_nsa_p = _nsa_prompt(p_qn, kc_p, vc_p, bias_c_p, ovl_p, p_slcb.reshape(nb, seq, KV_COLS),
                          p_swab.reshape(nb, seq, KV_COLS), tt, p_gate, nb=nb, seq=seq, n_cmp=nh_p - 1, n_sel=n_sel_p)

    nh_s = past // CMP_STRIDE
    n_sel_s = -(-(past + nq) // SEL_BLOCK)
    assert n_sel_s == past // SEL_BLOCK + 1
    n_sel_pad = -(-n_sel_s // LANE) * LANE
    rows_s = NSA_HEADS * nq
    cmp_end_s = jnp.arange(nh_s) * CMP_STRIDE + CMP_LEN - 1
    bias_c_s = _bias_rows(rel_bias, pos_s[:, None] - cmp_end_s[None, :]).reshape(rows_s, nh_s)
    nk_step = g_pages * page
    far = jnp.broadcast_to(_bias_rows(rel_bias, jnp.full((nq, 1), MAX_DISTANCE)).reshape(rows_s, 1), (rows_s, nk_step))
    tail_pos = past - nk_step + jnp.arange(nk_step)
    bias_s = jnp.stack([far, _bias_rows(rel_bias, pos_s[:, None] - tail_pos[None, :]).reshape(rows_s, nk_step)])
    new_pos = past + jnp.arange(LANE)
    bias_new = _bias_rows(rel_bias, pos_s[:, None] - new_pos[None, :]).reshape(rows_s, LANE)
    swa_pos = past - buf + jnp.arange(buf)
    bias_w = _bias_rows(rel_bias, pos_s[:, None] - swa_pos[None, :]).reshape(rows_s, buf)
    ovl_s = jnp.tile(_overlap(nh_s, n_sel_s, n_sel_pad).T, (1, 3)).astype(BF16)
    o_nsa_s = _nsa_sample(page_table, s_qn.reshape(nbd, nq, -1), kc_s, vc_s, bias_c_s, ovl_s, bias_s, pad_new(s_slcb),
                          bias_new, state_swa_t, pad_new(s_swab), bias_w, bias_new,
                          s_gate.reshape(nbd, nq, -1), cache_slc_t, g=g_pages, nbs=2 if nbd % 2 == 0 else 1,
                          n_cmp=nh_s - 1, n_sel=n_sel_s)

    x1_p, h2_p, ti_p, tw_p = _outproj(x_prompt.reshape(tp, d), o_mla_p, o_nsa_p, mod_p[2], mod_p[4], mod_p[3], W,
                                      rows_per_mod=1, tm=2 * tm_p)
    x1_s, h2_s, ti_s, tw_s = _outproj(x_sample.reshape(ts, d), o_mla_s.reshape(ts, -1), o_nsa_s.reshape(ts, -1),
                                      mod_s[2], mod_s[4], mod_s[3], W, rows_per_mod=tm_s, tm=tm_s)

    tm_e = 256
    h2 = jnp.concatenate([h2_p, h2_s], axis=0)
    tok_of_slot, slot_of_assign, tile_expert, tile_valid = _moe_dispatch(jnp.concatenate([ti_p, ti_s], axis=0), tm_e)
    y_sorted = _experts(tile_expert, tile_valid, h2[tok_of_slot], W, tm=tm_e)
    parts_p = [y_sorted[slot_of_assign[:tp, k]] for k in range(TOP_K)]
    parts_s = [y_sorted[slot_of_assign[tp:, k]] for k in range(TOP_K)]
    y_p = _final(x1_p, tw_p, parts_p, mod_p[5], g_final[None], rows_per_mod=1, tm=tm_p)
    y_s = _final(x1_s, tw_s, parts_s, mod_s[5], g_final[None], rows_per_mod=tm_s, tm=tm_s)

    kv_tail = (2, NSA_KV_HEADS, HEAD_DIM)
    keep_p = min(WINDOW, seq)
    swa_keys = jnp.concatenate([state_nsa_swa[0], s_swa.reshape((nbd, nq) + kv_tail)], axis=1)
    keep_s = min(WINDOW, buf + nq)

    def rows_last(a_t):
        return jnp.moveaxis(a_t.reshape((nb,) + kv_tail + (a_t.shape[-1],)), -1, 1)[None]

    return (y_p.reshape(nb, seq, d), y_s.reshape(nbd, nq, d),
            jnp.swapaxes(p_mla_t, 1, 2)[None], s_mla.reshape(1, nbd, nq, -1),
            rows_last(p_cmp_t), s_cmp.reshape((1, nbd, nq) + kv_tail),
            rows_last(p_slc_t), s_slc.reshape((1, nbd, nq) + kv_tail),
            rows_last(p_swa_t[:, :, seq - keep_p:]),
            swa_keys[None, :, buf + nq - keep_s:])
```

```python
import functools
import math

import jax
import jax.numpy as jnp
from jax import lax
from jax.experimental import pallas as pl
from jax.experimental.pallas import tpu as pltpu

F32, BF16, I32 = jnp.float32, jnp.bfloat16, jnp.int32

MLA_HEADS = 8
MLA_Q_LORA = 384
MLA_KV_LORA = 256
MLA_NOPE = 64
MLA_ROPE = 32
MLA_V = 64
NSA_HEADS = 8
NSA_KV_HEADS = 2
NSA_GROUP = NSA_HEADS // NSA_KV_HEADS
HEAD_DIM = 64
CMP_LEN = 32
CMP_STRIDE = 16
CMP_HIDDEN = 128
SEL_BLOCK = 64
SEL_TOPN = 16
WINDOW = 512
KV_COLS = 2 * NSA_KV_HEADS * HEAD_DIM
N_EXPERTS = 32
TOP_K = 4
SWIGLU_LIMIT = 7.0
SWIGLU_ALPHA = 1.702
NUM_BUCKETS = 32
MAX_DISTANCE = 128
ROPE_THETA = 10000.0
NORM_EPS = 1e-6

LANE = 128
VMEM_LIMIT = 56 * 1024 * 1024
NEG = -1e30
CMP_WIN_LO = 16

MLA_QW = MLA_KV_LORA + LANE
MLA_SCALE = (MLA_NOPE + MLA_ROPE) ** -0.5
MLA_QSCALE = MLA_SCALE * math.log2(math.e)
LOG2E = math.log2(math.e)
NSA_QSCALE = HEAD_DIM ** -0.5 * LOG2E

_C_QA = 0
_C_CKV = _C_QA + MLA_Q_LORA
_C_QN = _C_CKV + MLA_KV_LORA
_C_CMP = _C_QN + NSA_HEADS * LANE
_C_SLC = _C_CMP + KV_COLS
_C_SWA = _C_SLC + KV_COLS
_C_KR = _C_SWA + KV_COLS
_C_KRR = _C_KR + LANE
_C_G = _C_KRR + LANE
_C_END = _C_G + NSA_KV_HEADS * LANE


def _cparams(sem, vmem=VMEM_LIMIT):
    return pltpu.CompilerParams(dimension_semantics=sem, vmem_limit_bytes=vmem)


def _dot(a, b):
    return jnp.dot(a, b, preferred_element_type=F32)


def _dot_nt(a, b):
    return lax.dot_general(a, b, (((1,), (1,)), ((), ())), preferred_element_type=F32)


def _row_reduce(x, op, reduce):
    n = x.shape[1] // LANE
    if x.shape[1] % LANE or n <= 1:
        return reduce(x, axis=-1, keepdims=True)
    t = x[:, :LANE]
    for c in range(1, n):
        t = op(t, x[:, c * LANE:(c + 1) * LANE])
    return reduce(t, axis=-1, keepdims=True)


def _rowmax(x):
    return _row_reduce(x, jnp.maximum, jnp.max)


def _rowsum(x):
    return _row_reduce(x, jnp.add, jnp.sum)


def _split3(x):
    hi = x.astype(BF16)
    r1 = x - hi.astype(F32)
    mid = r1.astype(BF16)
    lo = (r1 - mid.astype(F32)).astype(BF16)
    return jnp.concatenate([hi, mid, lo], axis=-1)


def _rms(x, g):
    return x * lax.rsqrt(jnp.mean(x * x, axis=-1, keepdims=True) + NORM_EPS) * g


def _sigmoid(x):
    return 1.0 / (1.0 + jnp.exp(-x))


def _ada_kernel(c_ref, w_ref, b_ref, o_ref):
    c = c_ref[...]
    o_ref[...] = _dot((c * _sigmoid(c)).astype(BF16), w_ref[...]) + b_ref[...]


def _ada_mod(c_all, w_ada, b_ada):
    m, d = c_all.shape
    n = w_ada.shape[1]
    tn = 1536
    return pl.pallas_call(
        _ada_kernel, grid=(n // tn,),
        in_specs=[pl.BlockSpec((m, d), lambda i: (0, 0)), pl.BlockSpec((d, tn), lambda i: (0, i)),
                  pl.BlockSpec((1, tn), lambda i: (0, i))],
        out_specs=pl.BlockSpec((m, tn), lambda i: (0, i)),
        out_shape=jax.ShapeDtypeStruct((m, n), F32),
        compiler_params=_cparams(("arbitrary",)), name="ada_mod")(c_all, w_ada, b_ada)


def _proj_kernel(x_ref, sc_ref, sh_ref, g1_ref, cs_ref, win_ref, gqa_ref, wqb_ref, gkva_ref, wkn_ref,
                 qmla_ref, kq_ref, mla_ref, qn_ref, cmp_ref, cmps_ref, slc_ref, swa_ref, slcb_ref, swab_ref, gate_ref,
                 *, states_t):
    h = _rms(x_ref[...], g1_ref[...]) * (1.0 + sc_ref[...]) + sh_ref[...]
    proj = _dot(h.astype(BF16), win_ref[...])
    cs = cs_ref[...]
    cosp, sinp = cs[:, :LANE], cs[:, LANE:]
    qa = _rms(proj[:, _C_QA:_C_CKV], gqa_ref[...])
    q = _dot(qa.astype(BF16), wqb_ref[...])
    hw = MLA_HEADS * LANE
    for h_i in range(MLA_HEADS):
        lo = h_i * LANE
        qlat = _dot(q[:, lo:lo + LANE].astype(BF16), wkn_ref[h_i]) * MLA_QSCALE
        qr = (q[:, hw + lo:hw + lo + LANE] * cosp + q[:, 2 * hw + lo:2 * hw + lo + LANE] * sinp) * MLA_QSCALE
        qmla_ref[:, h_i * MLA_QW:h_i * MLA_QW + MLA_KV_LORA] = qlat.astype(BF16)
        qmla_ref[:, h_i * MLA_QW + MLA_KV_LORA:(h_i + 1) * MLA_QW] = qr.astype(BF16)
    ckv = _rms(proj[:, _C_CKV:_C_QN], gkva_ref[...])
    kr = proj[:, _C_KR:_C_KRR] * cosp + proj[:, _C_KRR:_C_G] * sinp
    kq_ref[:, :MLA_KV_LORA] = ckv.astype(BF16)
    kq_ref[:, MLA_KV_LORA:] = kr.astype(BF16)
    qn_ref[...] = (proj[:, _C_QN:_C_CMP] * NSA_QSCALE).astype(BF16)
    cmp = proj[:, _C_CMP:_C_SLC]
    slc = proj[:, _C_SLC:_C_SWA]
    swa = proj[:, _C_SWA:_C_KR]
    cmp_ref[...] = cmp
    if states_t:
        mla_ref[:MLA_KV_LORA, :] = ckv.T
        mla_ref[MLA_KV_LORA:, :] = kr.T[:MLA_ROPE]
        cmps_ref[...] = cmp.T
        slc_ref[...] = slc.T
        swa_ref[...] = swa.T
    else:
        mla_ref[:, :MLA_KV_LORA] = ckv
        mla_ref[:, MLA_KV_LORA:] = kr[:, :MLA_ROPE]
        cmps_ref[...] = cmp
        slc_ref[...] = slc
        swa_ref[...] = swa
    slcb_ref[...] = slc.astype(BF16)
    swab_ref[...] = swa.astype(BF16)
    gate_ref[...] = _sigmoid(proj[:, _C_G:_C_END])


def _proj(x, sc, sh, cs, W, *, rows_per_mod, cs_period_tiles, tm, states_t_batches=0):
    t, d = x.shape
    nt = t // tm
    if rows_per_mod == 1:
        tiles_per_mod = sc.shape[0] and (t // sc.shape[0]) // tm
        mod_map = lambda i: (i // tiles_per_mod, 0, 0)
    else:
        mod_map = lambda i: (i, 0, 0)
    cs_map = (lambda i: (i % cs_period_tiles, 0)) if cs_period_tiles > 1 else (lambda i: (0, 0))
    const2 = lambda i: (0, 0)
    row = lambda i: (i, 0)
    outs = [
        (MLA_HEADS * MLA_QW, BF16), (MLA_QW, BF16), (MLA_KV_LORA + MLA_ROPE, F32), (NSA_HEADS * LANE, BF16),
        (KV_COLS, F32), (KV_COLS, F32), (KV_COLS, F32), (KV_COLS, F32), (KV_COLS, BF16), (KV_COLS, BF16),
        (NSA_KV_HEADS * LANE, F32),
    ]
    state_outs = (2, 5, 6, 7) if states_t_batches else ()
    tiles_per_batch = (t // states_t_batches) // tm if states_t_batches else 0
    out_specs = [pl.BlockSpec((None, w, tm), lambda i: (i // tiles_per_batch, 0, i % tiles_per_batch))
                 if k in state_outs else pl.BlockSpec((tm, w), row) for k, (w, _) in enumerate(outs)]
    out_shape = [jax.ShapeDtypeStruct((states_t_batches, w, t // states_t_batches) if k in state_outs else (t, w), dt)
                 for k, (w, dt) in enumerate(outs)]
    return pl.pallas_call(
        functools.partial(_proj_kernel, states_t=bool(states_t_batches)), grid=(nt,),
        in_specs=[
            pl.BlockSpec((tm, d), row),
            pl.BlockSpec((None, rows_per_mod, d), mod_map),
            pl.BlockSpec((None, rows_per_mod, d), mod_map),
            pl.BlockSpec((1, d), const2),
            pl.BlockSpec((tm, 2 * LANE), cs_map),
            pl.BlockSpec(W["w_in"].shape, const2),
            pl.BlockSpec((1, MLA_Q_LORA), const2),
            pl.BlockSpec(W["w_qb"].shape, const2),
            pl.BlockSpec((1, MLA_KV_LORA), const2),
            pl.BlockSpec(W["w_kn"].shape, lambda i: (0, 0, 0)),
        ],
        out_specs=out_specs, out_shape=out_shape,
        compiler_params=_cparams(("arbitrary",)), name="proj_in",
    )(x, sc, sh, W["g_norm1"], cs, W["w_in"], W["g_q_a"], W["w_qb"], W["g_kv_a"], W["w_kn"])


def _mla_prompt_kernel(q_ref, k_ref, wv_ref, o_ref, m_ref, l_ref, acc_ref, *, tq, tk):
    i, j = pl.program_id(1), pl.program_id(2)
    nk = pl.num_programs(2)

    @pl.when(j == 0)
    def _():
        m_ref[...] = jnp.full_like(m_ref, NEG)
        l_ref[...] = jnp.zeros_like(l_ref)
        acc_ref[...] = jnp.zeros_like(acc_ref)

    def step(masked):
        k = k_ref[...]
        v = k[:, :MLA_KV_LORA]
        if masked:
            qpos = i * tq + lax.broadcasted_iota(I32, (tq, tk), 0)
            kpos = j * tk + lax.broadcasted_iota(I32, (tq, tk), 1)
            mask = kpos <= qpos
        ss = [_dot_nt(q_ref[:, h * MLA_QW:(h + 1) * MLA_QW], k) for h in range(MLA_HEADS)]
        ps, alphas = [], []
        for h in range(MLA_HEADS):
            r = slice(h * tq, (h + 1) * tq)
            s = jnp.where(mask, ss[h], NEG) if masked else ss[h]
            m_prev = m_ref[r]
            m_new = jnp.maximum(m_prev, _rowmax(s))
            p = jnp.exp2(s - m_new)
            if masked:
                p = jnp.where(mask, p, 0.0)
            alpha = jnp.exp2(m_prev - m_new)
            l_ref[r] = alpha * l_ref[r] + _rowsum(p)
            m_ref[r] = m_new
            ps.append(p.astype(BF16))
            alphas.append(alpha)
        for h in range(MLA_HEADS):
            r = slice(h * tq, (h + 1) * tq)
            acc_ref[r] = alphas[h] * acc_ref[r] + _dot(ps[h], v)

    first_key, last_key = j * tk, j * tk + tk - 1
    pl.when(last_key <= i * tq)(functools.partial(step, False))
    pl.when((first_key <= i * tq + tq - 1) & (last_key > i * tq))(functools.partial(step, True))

    @pl.when(j == nk - 1)
    def _():
        out = jnp.zeros(o_ref.shape, F32)
        for h in range(MLA_HEADS):
            r = slice(h * tq, (h + 1) * tq)
            l = l_ref[r]
            o_lat = acc_ref[r] / jnp.where(l > 0.0, l, 1.0)
            out = out + _dot(o_lat.astype(BF16), wv_ref[h])
        o_ref[...] = out.astype(o_ref.dtype)


def _mla_prompt(qmla, kq, wv, *, nb, seq, tq=256, tk=1024):
    tk = min(tk, seq)
    if seq % tk:
        tk = 512
    nq, nk = seq // tq, seq // tk
    ow = MLA_HEADS * MLA_V

    def k_map(b, i, j):
        return (b * nk + jnp.minimum(j, (i * tq + tq - 1) // tk), 0)

    return pl.pallas_call(
        functools.partial(_mla_prompt_kernel, tq=tq, tk=tk), grid=(nb, nq, nk),
        in_specs=[pl.BlockSpec((tq, MLA_HEADS * MLA_QW), lambda b, i, j: (b * nq + i, 0)),
                  pl.BlockSpec((tk, MLA_QW), k_map),
                  pl.BlockSpec(wv.shape, lambda b, i, j: (0, 0, 0))],
        out_specs=pl.BlockSpec((tq, ow), lambda b, i, j: (b * nq + i, 0)),
        out_shape=jax.ShapeDtypeStruct((nb * seq, ow), BF16),
        scratch_shapes=[pltpu.VMEM((MLA_HEADS * tq, 1), F32), pltpu.VMEM((MLA_HEADS * tq, 1), F32),
                        pltpu.VMEM((MLA_HEADS * tq, MLA_KV_LORA), F32)],
        compiler_params=_cparams(("arbitrary", "arbitrary", "arbitrary")), name="mla_prompt",
    )(qmla, kq, wv)


def _mla_sample_kernel(pt_ref, q_ref, kn_ref, wv_ref, *rest, g, nq, nbs):
    pages = rest[:nbs * g]
    o_ref, m_ref, l_ref, acc_ref = rest[nbs * g:]
    s_idx = pl.program_id(1)
    rows = MLA_HEADS * nq

    @pl.when(s_idx == 0)
    def _():
        m_ref[...] = jnp.full_like(m_ref, NEG)
        l_ref[...] = jnp.zeros_like(l_ref)
        acc_ref[...] = jnp.zeros_like(acc_ref)

    qs = []
    for bl in range(nbs):
        r = slice(bl * rows, (bl + 1) * rows)
        q = jnp.concatenate([q_ref[bl, :, h * MLA_QW:(h + 1) * MLA_QW] for h in range(MLA_HEADS)], axis=0)
        qs.append(q)
        qlat, qr = q[:, :MLA_KV_LORA], q[:, MLA_KV_LORA:MLA_KV_LORA + MLA_ROPE]
        pg = pages[bl * g:(bl + 1) * g]
        ckv_t = jnp.concatenate([pg[u][:MLA_KV_LORA].astype(BF16) for u in range(g)], axis=1)
        kr_t = jnp.concatenate([pg[u][MLA_KV_LORA:].astype(BF16) for u in range(g)], axis=1)
        s = _dot(qlat, ckv_t) + _dot(qr, kr_t)
        m_prev = m_ref[r]
        m_new = jnp.maximum(m_prev, _rowmax(s))
        p = jnp.exp2(s - m_new)
        alpha = jnp.exp2(m_prev - m_new)
        l_ref[r] = alpha * l_ref[r] + _rowsum(p)
        acc_ref[r] = alpha * acc_ref[r] + _dot_nt(p.astype(BF16), ckv_t)
        m_ref[r] = m_new

    @pl.when(s_idx == pl.num_programs(1) - 1)
    def _():
        for bl in range(nbs):
            r = slice(bl * rows, (bl + 1) * rows)
            kn = kn_ref[bl]
            sn = _dot_nt(qs[bl], kn)
            qi = lax.broadcasted_iota(I32, sn.shape, 0) % nq
            kt = lax.broadcasted_iota(I32, sn.shape, 1)
            mask = kt <= qi
            sn = jnp.where(mask, sn, NEG)
            m_prev = m_ref[r]
            m_new = jnp.maximum(m_prev, jnp.max(sn, axis=-1, keepdims=True))
            p = jnp.where(mask, jnp.exp2(sn - m_new), 0.0)
            alpha = jnp.exp2(m_prev - m_new)
            l = alpha * l_ref[r] + _rowsum(p)
            acc = alpha * acc_ref[r] + _dot(p.astype(BF16), kn[:, :MLA_KV_LORA])
            o_lat = (acc / jnp.where(l > 0.0, l, 1.0)).astype(BF16)
            res = _dot(o_lat, wv_ref[...])
            head_of_lane = lax.broadcasted_iota(I32, (nq, res.shape[1]), 1) // MLA_V
            out = jnp.zeros((nq, res.shape[1]), F32)
            for h in range(MLA_HEADS):
                out = out + jnp.where(head_of_lane == h, res[h * nq:(h + 1) * nq], 0.0)
            o_ref[bl] = out.astype(o_ref.dtype)


def _page_specs(g, block, n_lane_blocks=1, flat=False, nbs=1):
    def index(b, s, pt, bl, u, c):
        return ((pt[s * g + u] if flat else pt[b * nbs + bl, s * g + u]), 0, c)

    return [pl.BlockSpec(block, functools.partial(index, bl=bl, u=u, c=c))
            for bl in range(nbs) for u in range(g) for c in range(n_lane_blocks)]


def _mla_sample(page_table, qmla, kq_new_pad, wv, cache_mla, *, g, nbs):
    nb, nq = qmla.shape[0], qmla.shape[1]
    n_pages = page_table.shape[1]
    ow = MLA_HEADS * MLA_V
    rows = nbs * MLA_HEADS * nq
    bmap = lambda b, s, pt: (b, 0, 0)
    gs = pltpu.PrefetchScalarGridSpec(
        num_scalar_prefetch=1, grid=(nb // nbs, n_pages // g),
        in_specs=[pl.BlockSpec((nbs, nq, MLA_HEADS * MLA_QW), bmap),
                  pl.BlockSpec((nbs, LANE, MLA_QW), bmap),
                  pl.BlockSpec(wv.shape, lambda b, s, pt: (0, 0))]
        + _page_specs(g, (None,) + cache_mla.shape[1:], nbs=nbs),
        out_specs=pl.BlockSpec((nbs, nq, ow), bmap),
        scratch_shapes=[pltpu.VMEM((rows, 1), F32), pltpu.VMEM((rows, 1), F32), pltpu.VMEM((rows, MLA_KV_LORA), F32)],
    )
    return pl.pallas_call(
        functools.partial(_mla_sample_kernel, g=g, nq=nq, nbs=nbs), grid_spec=gs,
        out_shape=jax.ShapeDtypeStruct((nb, nq, ow), BF16),
        compiler_params=_cparams(("arbitrary", "arbitrary")), name="mla_sample",
    )(page_table, qmla, kq_new_pad, wv, *([cache_mla] * (nbs * g)))


def _compress_kernel(pl_ref, w_ref, pe_ref, w1_ref, w2_ref, *rest, g, transposed):
    pages = rest[:g]
    kc_ref, vc_ref, xs_ref = rest[g:]
    half_w = NSA_KV_HEADS * 2 * CMP_HIDDEN
    nh = kc_ref.shape[0]
    for u in range(g):
        x = pages[u][...]
        for j in range(2):
            if transposed:
                xs_ref[j, u * LANE:(u + 1) * LANE, :] = x[j * LANE:(j + 1) * LANE, :].T
            else:
                xs_ref[j, u * LANE:(u + 1) * LANE, :] = x[:, j * LANE:(j + 1) * LANE]
    for j, out_ref in enumerate((kc_ref, vc_ref)):
        ab = jnp.zeros((nh, half_w), F32)
        for pp in range(CMP_STRIDE // 2):
            xa = xs_ref[j, pl.ds(2 * pp, nh, stride=CMP_STRIDE), :]
            xb = xs_ref[j, pl.ds(2 * pp + 1, nh, stride=CMP_STRIDE), :]
            ab = ab + _dot(jnp.concatenate([xa, xb], axis=1).astype(BF16), w_ref[j, pp])
        pe_term = _dot(pe_ref[j].astype(BF16), w1_ref[j])[0:1]
        acc = jnp.zeros((nh, LANE), F32)
        for kv in range(NSA_KV_HEADS):
            base = kv * 2 * CMP_HIDDEN
            first = ab[:, base:base + CMP_HIDDEN]
            second = ab[:, base + CMP_HIDDEN:base + 2 * CMP_HIDDEN]
            hid = first + pltpu.roll(second, nh - 1, 0) + pe_term
            hid = 0.5 * hid * (1.0 + lax.erf(hid * math.sqrt(0.5)))
            acc = acc + _dot(hid.astype(BF16), w2_ref[j, kv])
        out_ref[...] = acc.astype(out_ref.dtype)


def _compress(page_list, pool, w_pair, pe8, w1r, w2pad, *, nb, transposed):
    g = page_list.shape[0] // nb
    nh = g * (LANE // CMP_STRIDE)
    c3 = lambda b, s, pt: (0, 0, 0)
    gs = pltpu.PrefetchScalarGridSpec(
        num_scalar_prefetch=1, grid=(1, nb),
        in_specs=[pl.BlockSpec(w_pair.shape, lambda b, s, pt: (0, 0, 0, 0)), pl.BlockSpec(pe8.shape, c3),
                  pl.BlockSpec(w1r.shape, c3), pl.BlockSpec(w2pad.shape, lambda b, s, pt: (0, 0, 0, 0))]
        + _page_specs(g, (None,) + pool.shape[1:], flat=True),
        out_specs=[pl.BlockSpec((None, nh, LANE), lambda b, s, pt: (s, 0, 0))] * 2,
        scratch_shapes=[pltpu.VMEM((2, g * LANE, LANE), F32)],
    )
    return pl.pallas_call(
        functools.partial(_compress_kernel, g=g, transposed=transposed), grid_spec=gs,
        out_shape=[jax.ShapeDtypeStruct((nb, nh, LANE), BF16)] * 2,
        compiler_params=_cparams(("arbitrary", "arbitrary")), name="compress",
    )(page_list, w_pair, pe8, w1r, w2pad, *([pool] * g))


def _topk_mask(v, k, axis=1):
    lane = lax.broadcasted_iota(I32, v.shape, axis)
    sel = jnp.zeros(v.shape, F32)
    for _ in range(k):
        m = jnp.max(v, axis=axis, keepdims=True)
        idx = jnp.min(jnp.where(v == m, lane, 1 << 20), axis=axis, keepdims=True)
        pick = lane == idx
        sel = jnp.where(pick & (m > -jnp.inf), 1.0, sel)
        v = jnp.where(pick, -jnp.inf, v)
    return sel


def _softmax_masked(s, mask):
    s = jnp.where(mask, s, NEG)
    m = _rowmax(s)
    p = jnp.where(mask, jnp.exp2(s - m), 0.0)
    l = _rowsum(p)
    return p / jnp.where(l > 0.0, l, 1.0)


def _flash_step(s, mask, v, m_ref, l_ref, acc_ref, v_keys_on_lanes=False):
    s = jnp.where(mask, s, NEG)
    m_prev = m_ref[...]
    m_new = jnp.maximum(m_prev, _rowmax(s))
    p = jnp.where(mask, jnp.exp2(s - m_new), 0.0)
    alpha = jnp.exp2(m_prev - m_new)
    l_ref[...] = alpha * l_ref[...] + _rowsum(p)
    pv = _dot_nt(p.astype(BF16), v) if v_keys_on_lanes else _dot(p.astype(BF16), v)
    acc_ref[...] = alpha * acc_ref[...] + pv
    m_ref[...] = m_new


def _flash_init(m_ref, l_ref, acc_ref):
    m_ref[...] = jnp.full_like(m_ref, NEG)
    l_ref[...] = jnp.zeros_like(l_ref)
    acc_ref[...] = jnp.zeros_like(acc_ref)


def _flash_out(l_ref, acc_ref):
    l = l_ref[...]
    return acc_ref[...] / jnp.where(l > 0.0, l, 1.0)


def _nsa_prompt_kernel(q_ref, kc_ref, vc_ref, bc_ref, ovl_ref, slc_ref, swa_ref, tt_ref, gate_ref, o_ref,
                       m_ref, l_ref, acc_ref, *, tq, tkf, n_cmp, n_sel, seq_len):
    i = pl.program_id(1)
    rows = NSA_HEADS * tq
    n_sel_rows = -(-n_sel // 8) * 8
    q = jnp.concatenate([q_ref[:, h * LANE:(h + 1) * LANE] for h in range(NSA_HEADS)], axis=0)
    nhp = kc_ref.shape[0]
    pos_r = i * tq + lax.broadcasted_iota(I32, (rows, 1), 0) % tq

    wu = lax.broadcasted_iota(I32, (LANE, nhp), 0)
    wn = lax.broadcasted_iota(I32, (LANE, nhp), 1)
    shift = jnp.where(wn == i * (tq // CMP_STRIDE) - CMP_WIN_LO + wu, 1.0, 0.0).astype(BF16)
    far_col = tt_ref[2][:, :1]
    bias_c = _dot(_split3(bc_ref[...]), jnp.concatenate([shift] * 3, axis=0)) + far_col
    s = _dot_nt(q, kc_ref[...]) + bias_c
    n_idx = lax.broadcasted_iota(I32, (rows, nhp), 1)
    mask_c = (n_idx * CMP_STRIDE + CMP_LEN - 1 <= pos_r) & (n_idx < n_cmp)
    p_cmp = _softmax_masked(s, mask_c)
    o_cmp = _dot(p_cmp.astype(BF16), vc_ref[...])

    imps = []
    for kv in range(NSA_KV_HEADS):
        lo = kv * NSA_GROUP * tq
        psum = p_cmp[lo:lo + tq]
        for g in range(1, NSA_GROUP):
            psum = psum + p_cmp[lo + g * tq:lo + (g + 1) * tq]
        imps.append(_dot_nt(ovl_ref[...], _split3(psum))[:n_sel_rows])
    imp_t = jnp.concatenate(imps, axis=1)
    blk = lax.broadcasted_iota(I32, imp_t.shape, 0)
    cur = (i * tq + lax.broadcasted_iota(I32, imp_t.shape, 1) % tq) // SEL_BLOCK
    forced = (blk == 0) | (blk == cur) | (blk == cur - 1)
    future = (blk > cur) | (blk >= n_sel)
    imp_t = jnp.where(future, -jnp.inf, jnp.where(forced, jnp.inf, imp_t))
    sel_t = _topk_mask(imp_t, min(SEL_TOPN, n_sel), axis=0)
    if n_sel_rows < LANE:
        sel_t = jnp.concatenate([sel_t, jnp.zeros((LANE - n_sel_rows, NSA_KV_HEADS * tq), F32)], axis=0)
    sels = [sel_t[:, kv * tq:(kv + 1) * tq].T.astype(BF16) for kv in range(NSA_KV_HEADS)]

    near0 = pl.multiple_of(jnp.maximum(i - 1, 0) * tq, tq)
    near_bias = jnp.concatenate([tt_ref[jnp.where(i == 0, 0, 1)], tt_ref[0]], axis=1)
    pos_q = i * tq + lax.broadcasted_iota(I32, (tq, 1), 0)
    d_near = pos_q - (near0 + lax.broadcasted_iota(I32, (tq, 2 * tq), 1))

    def heads(pens):
        return jnp.concatenate([p_ for p_ in pens for _ in range(NSA_GROUP)], axis=0)

    def block_sel(first_key, n_keys, key_limit):
        sb = lax.broadcasted_iota(I32, (LANE, n_keys), 0)
        kt = lax.broadcasted_iota(I32, (LANE, n_keys), 1)
        hit = (sb == first_key // SEL_BLOCK + kt // SEL_BLOCK) & (first_key + kt < key_limit)
        expand = jnp.where(hit, 1.0, 0.0).astype(BF16)
        return [_dot(sel, expand) > 0.5 for sel in sels]

    n_far_w = WINDOW - tq
    far0 = pl.multiple_of(jnp.maximum(i - WINDOW // tq, 0) * tq, tq)
    kv_n = swa_ref[pl.ds(near0, 2 * tq), :]
    kv_f = swa_ref[pl.ds(far0, n_far_w), :]
    kpos_f = far0 + lax.broadcasted_iota(I32, (tq, n_far_w), 1)
    pen_n = jnp.where((d_near >= 0) & (d_near < WINDOW), 0.0, NEG)
    pen_f = jnp.where((kpos_f < near0) & (pos_q - kpos_f < WINDOW), 0.0, NEG)
    s_n = _dot_nt(q, kv_n[:, :LANE]) + near_bias + heads([pen_n] * NSA_KV_HEADS)
    s_f = _dot_nt(q, kv_f[:, :LANE]) + far_col + heads([pen_f] * NSA_KV_HEADS)
    m_w = jnp.maximum(_rowmax(s_n), _rowmax(s_f))
    p_n = jnp.exp2(s_n - m_w)
    p_f = jnp.exp2(s_f - m_w)
    l_w = _rowsum(p_n) + _rowsum(p_f)
    o_swa = (_dot(p_n.astype(BF16), kv_n[:, LANE:]) + _dot(p_f.astype(BF16), kv_f[:, LANE:])) / l_w

    kv_n = slc_ref[pl.ds(near0, 2 * tq), :]
    pens = [jnp.where(hit & (d_near >= 0), 0.0, NEG) for hit in block_sel(near0, 2 * tq, seq_len)]
    s_n = _dot_nt(q, kv_n[:, :LANE]) + near_bias + heads(pens)
    m_s = _rowmax(s_n)
    p_n = jnp.exp2(s_n - m_s)
    m_ref[...] = m_s
    l_ref[...] = _rowsum(p_n)
    acc_ref[...] = _dot(p_n.astype(BF16), kv_n[:, LANE:])

    n_far = (near0 + tkf - 1) // tkf
    half = NSA_GROUP * tq

    def far_logits(c):
        first = pl.multiple_of(jnp.minimum(c, jnp.maximum(n_far - 1, 0)) * tkf, tkf)
        k = slc_ref[pl.ds(first, tkf), :LANE]
        return tuple(_dot_nt(q[kvh * half:(kvh + 1) * half], k) for kvh in range(NSA_KV_HEADS))

    def slc_far(c, qk):
        qk_next = far_logits(c + 1)
        first = pl.multiple_of(c * tkf, tkf)
        v = slc_ref[pl.ds(first, tkf), LANE:]
        hits = block_sel(first, tkf, near0)
        for kvh in range(NSA_KV_HEADS):
            r = slice(kvh * half, (kvh + 1) * half)
            pen = jnp.where(hits[kvh], 0.0, NEG)
            s_ = qk[kvh] + far_col[r] + jnp.concatenate([pen] * NSA_GROUP, axis=0)
            m_prev = m_ref[r]
            m_new = jnp.maximum(m_prev, _rowmax(s_))
            p = jnp.exp2(s_ - m_new)
            alpha = jnp.exp2(m_prev - m_new)
            l_ref[r] = alpha * l_ref[r] + _rowsum(p)
            acc_ref[r] = alpha * acc_ref[r] + _dot(p.astype(BF16), v)
            m_ref[r] = m_new
        return qk_next

    lax.fori_loop(0, n_far, slc_far, far_logits(0))
    o_slc = acc_ref[...] / l_ref[...]

    gates = gate_ref[...]
    for h in range(NSA_HEADS):
        kv, g = divmod(h, NSA_GROUP)
        r = slice(h * tq, (h + 1) * tq)
        c = kv * LANE + g
        o = (gates[:, c:c + 1] * o_cmp[r] + gates[:, c + NSA_GROUP:c + NSA_GROUP + 1] * o_slc[r]
             + gates[:, c + 2 * NSA_GROUP:c + 2 * NSA_GROUP + 1] * o_swa[r])
        o_ref[:, h * LANE:(h + 1) * LANE] = o.astype(o_ref.dtype)


def _nsa_prompt(qn, kc, vc, bias_c, ovl, slc_b, swa_b, tt, gates, *, nb, seq, n_cmp, n_sel):
    tq = tt.shape[2]
    nq = seq // tq
    rows = NSA_HEADS * tq
    hw = NSA_HEADS * LANE
    nhp = kc.shape[1]
    return pl.pallas_call(
        functools.partial(_nsa_prompt_kernel, tq=tq, tkf=min(512, seq), n_cmp=n_cmp, n_sel=n_sel, seq_len=seq),
        grid=(nb, nq),
        in_specs=[
            pl.BlockSpec((tq, hw), lambda b, i: (b * nq + i, 0)),
            pl.BlockSpec((None, nhp, LANE), lambda b, i: (b, 0, 0)),
            pl.BlockSpec((None, nhp, LANE), lambda b, i: (b, 0, 0)),
            pl.BlockSpec(bias_c.shape, lambda b, i: (0, 0)),
            pl.BlockSpec(ovl.shape, lambda b, i: (0, 0)),
            pl.BlockSpec((None, seq, KV_COLS), lambda b, i: (b, 0, 0)),
            pl.BlockSpec((None, seq, KV_COLS), lambda b, i: (b, 0, 0)),
            pl.BlockSpec(tt.shape, lambda b, i: (0, 0, 0)),
            pl.BlockSpec((tq, NSA_KV_HEADS * LANE), lambda b, i: (b * nq + i, 0)),
        ],
        out_specs=pl.BlockSpec((tq, hw), lambda b, i: (b * nq + i, 0)),
        out_shape=jax.ShapeDtypeStruct((nb * seq, hw), BF16),
        scratch_shapes=[pltpu.VMEM((rows, 1), F32), pltpu.VMEM((rows, 1), F32), pltpu.VMEM((rows, LANE), F32)],
        compiler_params=_cparams(("arbitrary", "arbitrary")), name="nsa_prompt",
    )(qn, kc, vc, bias_c, ovl, slc_b, swa_b, tt, gates)


def _nsa_sample_kernel(pt_ref, q_ref, kc_ref, vc_ref, bc_ref, ovl_ref, bs_ref, sn_ref, bsn_ref, st_ref, wn_ref,
                       bw_ref, bwn_ref, gate_ref, *rest, g, nq, nbs, n_cmp, n_sel, buf):
    o_ref, sel_ref, ocmp_ref, oswa_ref, m_ref, l_ref, acc_ref = rest[nbs * g:]
    s_idx = pl.program_id(1)
    last = pl.num_programs(1) - 1
    rows = NSA_HEADS * nq
    qi = lax.broadcasted_iota(I32, (rows, 1), 0) % nq
    qs = [jnp.concatenate([q_ref[bl, :, h * LANE:(h + 1) * LANE] for h in range(NSA_HEADS)], axis=0)
          for bl in range(nbs)]

    def first_step(bl):
        q = qs[bl]
        kc_r, vc_r, st_r, wn_r = kc_ref.at[bl], vc_ref.at[bl], st_ref.at[bl], wn_ref.at[bl]
        _nsa_sample_first(q, qi, kc_r, vc_r, bc_ref, ovl_ref, st_r, wn_r, bw_ref, bwn_ref, sel_ref.at[bl], ocmp_ref.at[bl],
                          oswa_ref.at[bl], m_ref.at[bl], l_ref.at[bl], acc_ref.at[bl], nq=nq, n_cmp=n_cmp, n_sel=n_sel, buf=buf)

    @pl.when(s_idx == 0)
    def _():
        for bl in range(nbs):
            first_step(bl)

    nk = g * rest[0].shape[1]
    sb = lax.broadcasted_iota(I32, (sel_ref.shape[2], nk), 0)
    kt = lax.broadcasted_iota(I32, (sel_ref.shape[2], nk), 1)
    expand = jnp.where(sb == s_idx * (nk // SEL_BLOCK) + kt // SEL_BLOCK, 1.0, 0.0).astype(BF16)
    for bl in range(nbs):
        pages = rest[bl * g:(bl + 1) * g]
        kcat = jnp.concatenate([pages[u][:LANE].astype(BF16) for u in range(g)], axis=1)
        vcat = jnp.concatenate([pages[u][LANE:].astype(BF16) for u in range(g)], axis=1)
        s = _dot(qs[bl], kcat) + bs_ref[jnp.where(s_idx == last, 1, 0)]
        mask = _dot(sel_ref[bl], expand) > 0.5
        _flash_step(s, mask, vcat, m_ref.at[bl], l_ref.at[bl], acc_ref.at[bl], v_keys_on_lanes=True)

    @pl.when(s_idx == last)
    def _():
        for bl in range(nbs):
            q = qs[bl]
            m_r, l_r, acc_r = m_ref.at[bl], l_ref.at[bl], acc_ref.at[bl]
            sn = sn_ref[bl]
            s_n = _dot_nt(q, sn[:, :LANE]) + bsn_ref[...]
            mask_n = (lax.broadcasted_iota(I32, (rows, LANE), 1) <= qi) & (sel_ref[bl, :, n_sel - 1:n_sel] > 0.5)
            _flash_step(s_n, mask_n, sn[:, LANE:], m_r, l_r, acc_r)
            o_slc = _flash_out(l_r, acc_r)
            o_cmp, o_swa = ocmp_ref[bl], oswa_ref[bl]
            gates = gate_ref[bl]
            for h in range(NSA_HEADS):
                kv, gg = divmod(h, NSA_GROUP)
                r = slice(h * nq, (h + 1) * nq)
                c = kv * LANE + gg
                o = (gates[:, c:c + 1] * o_cmp[r] + gates[:, c + NSA_GROUP:c + NSA_GROUP + 1] * o_slc[r]
                     + gates[:, c + 2 * NSA_GROUP:c + 2 * NSA_GROUP + 1] * o_swa[r])
                o_ref[bl, :, h * LANE:(h + 1) * LANE] = o.astype(o_ref.dtype)


def _nsa_sample_first(q, qi, kc_ref, vc_ref, bc_ref, ovl_ref, st_ref, wn_ref, bw_ref, bwn_ref, sel_ref, ocmp_ref, oswa_ref,
                      m_ref, l_ref, acc_ref, *, nq, n_cmp, n_sel, buf):
    rows = NSA_HEADS * nq

    def compressed_and_select():
        nhp = kc_ref.shape[0]
        s = _dot_nt(q, kc_ref[...]) + bc_ref[...]
        mask_c = lax.broadcasted_iota(I32, (rows, nhp), 1) < n_cmp
        p_cmp = _softmax_masked(s, mask_c)
        ocmp_ref[...] = _dot(p_cmp.astype(BF16), vc_ref[...])
        psums = []
        for kv in range(NSA_KV_HEADS):
            ps = p_cmp[kv * NSA_GROUP * nq:kv * NSA_GROUP * nq + nq]
            for gg in range(1, NSA_GROUP):
                lo = (kv * NSA_GROUP + gg) * nq
                ps = ps + p_cmp[lo:lo + nq]
            psums.append(ps)
        n_kq = NSA_KV_HEADS * nq
        psum = jnp.concatenate(psums + [jnp.zeros((LANE - n_kq, nhp), F32)], axis=0)
        n_blk = ovl_ref.shape[0]
        n_sel_rows = -(-n_sel // 8) * 8
        imp_t = _dot_nt(ovl_ref[...], _split3(psum))[:n_sel_rows]
        blk = lax.broadcasted_iota(I32, imp_t.shape, 0)
        cur = n_sel - 1
        forced = (blk == 0) | (blk == cur) | (blk == cur - 1)
        imp_t = jnp.where(blk >= n_sel, -jnp.inf, jnp.where(forced, jnp.inf, imp_t))
        sel_t = _topk_mask(imp_t, min(SEL_TOPN, n_sel), axis=0)
        if n_sel_rows < n_blk:
            sel_t = jnp.concatenate([sel_t, jnp.zeros((n_blk - n_sel_rows, LANE), F32)], axis=0)
        sel = sel_t.T.astype(BF16)
        sel_ref[...] = jnp.concatenate(
            [sel[kv * nq:(kv + 1) * nq] for kv in range(NSA_KV_HEADS) for _ in range(NSA_GROUP)], axis=0)

    def window():
        st = st_ref[...]
        s_w = _dot(q, st[:LANE].astype(BF16)) + bw_ref[...]
        d_w = buf + qi - lax.broadcasted_iota(I32, (rows, buf), 1)
        mask_w = (d_w >= 0) & (d_w < WINDOW)
        wn = wn_ref[...]
        s_n = _dot_nt(q, wn[:, :LANE]) + bwn_ref[...]
        mask_n = lax.broadcasted_iota(I32, (rows, LANE), 1) <= qi
        s_w = jnp.where(mask_w, s_w, NEG)
        s_n = jnp.where(mask_n, s_n, NEG)
        m = jnp.maximum(_rowmax(s_w), _rowmax(s_n))
        p_w = jnp.where(mask_w, jnp.exp2(s_w - m), 0.0)
        p_n = jnp.where(mask_n, jnp.exp2(s_n - m), 0.0)
        l = _rowsum(p_w) + _rowsum(p_n)
        o = _dot_nt(p_w.astype(BF16), st[LANE:].astype(BF16)) + _dot(p_n.astype(BF16), wn[:, LANE:])
        oswa_ref[...] = o / jnp.where(l > 0.0, l, 1.0)

    compressed_and_select()
    window()
    _flash_init(m_ref, l_ref, acc_ref)


def _nsa_sample(page_table, qn, kc, vc, bias_c, ovl, bias_s, slc_new, bias_sn, state_swa, swa_new, bias_w, bias_wn,
                gates, cache_slc, *, g, nbs, n_cmp, n_sel):
    nb, nq = qn.shape[0], qn.shape[1]
    n_pages = page_table.shape[1]
    rows = NSA_HEADS * nq
    buf = state_swa.shape[2]
    hw = NSA_HEADS * LANE
    bmap = lambda b, s, pt: (b, 0, 0)
    c2 = lambda b, s, pt: (0, 0)
    per_elem = lambda a: pl.BlockSpec((nbs,) + a.shape[1:], bmap)
    gs = pltpu.PrefetchScalarGridSpec(
        num_scalar_prefetch=1, grid=(nb // nbs, n_pages // g),
        in_specs=[
            per_elem(qn), per_elem(kc), per_elem(vc),
            pl.BlockSpec(bias_c.shape, c2),
            pl.BlockSpec(ovl.shape, c2),
            pl.BlockSpec(bias_s.shape, lambda b, s, pt: (0, 0, 0)),
            per_elem(slc_new),
            pl.BlockSpec(bias_sn.shape, c2),
            per_elem(state_swa), per_elem(swa_new),
            pl.BlockSpec(bias_w.shape, c2),
            pl.BlockSpec(bias_wn.shape, c2),
            per_elem(gates),
        ] + _page_specs(g, (None,) + cache_slc.shape[1:], nbs=nbs),
        out_specs=pl.BlockSpec((nbs, nq, hw), bmap),
        scratch_shapes=[pltpu.VMEM((nbs, rows, ovl.shape[0]), BF16), pltpu.VMEM((nbs, rows, LANE), F32),
                        pltpu.VMEM((nbs, rows, LANE), F32), pltpu.VMEM((nbs, rows, 1), F32), pltpu.VMEM((nbs, rows, 1), F32),
                        pltpu.VMEM((nbs, rows, LANE), F32)],
    )
    return pl.pallas_call(
        functools.partial(_nsa_sample_kernel, g=g, nq=nq, nbs=nbs, n_cmp=n_cmp, n_sel=n_sel, buf=buf), grid_spec=gs,
        out_shape=jax.ShapeDtypeStruct((nb, nq, hw), BF16),
        compiler_params=_cparams(("arbitrary", "arbitrary")), name="nsa_sample",
    )(page_table, qn, kc, vc, bias_c, ovl, bias_s, slc_new, bias_sn, state_swa, swa_new, bias_w, bias_wn, gates,
      *([cache_slc] * (nbs * g)))


def _outproj_kernel(x_ref, om_ref, on_ref, g1_ref, sc_ref, sh_ref, wom_ref, won_ref, gn2_ref, wr_ref, br_ref,
                    x1_ref, h2_ref, ti_ref, tw_ref):
    mix = _dot(om_ref[...], wom_ref[...]) + _dot(on_ref[...], won_ref[...])
    x1 = x_ref[...] + g1_ref[...] * mix
    x1_ref[...] = x1
    h2 = _rms(x1, gn2_ref[...]) * (1.0 + sc_ref[...]) + sh_ref[...]
    h2_ref[...] = h2.astype(h2_ref.dtype)
    h3, w3 = _split3(h2), _split3(wr_ref[...])
    d = h2.shape[1]
    h_cat = jnp.concatenate([h3[:, :d], h3[:, :d], h3[:, d:2 * d]], axis=1)
    w_cat = jnp.concatenate([w3[:, :d], w3[:, d:2 * d], w3[:, :d]], axis=1)
    logits = _dot_nt(h_cat, w_cat) + br_ref[...]
    lane = lax.broadcasted_iota(I32, logits.shape, 1)
    v = logits
    vals, idxs = [], []
    for _ in range(TOP_K):
        m = jnp.max(v, axis=-1, keepdims=True)
        idx = jnp.min(jnp.where(v == m, lane, 1 << 20), axis=-1, keepdims=True)
        vals.append(m)
        idxs.append(idx)
        v = jnp.where(lane == idx, -jnp.inf, v)
    es = [jnp.exp(m - vals[0]) for m in vals]
    tot = es[0]
    for e in es[1:]:
        tot = tot + e
    ti = jnp.zeros(logits.shape, I32)
    tw = jnp.zeros(logits.shape, F32)
    for k in range(TOP_K):
        ti = jnp.where(lane == k, idxs[k], ti)
        tw = jnp.where(lane == k, es[k] / tot, tw)
    ti_ref[...] = ti
    tw_ref[...] = tw


def _outproj(x, o_mla, o_nsa, g1, sc, sh, W, *, rows_per_mod, tm):
    t, d = x.shape
    nt = t // tm
    if rows_per_mod == 1:
        tiles_per_mod = (t // g1.shape[0]) // tm
        mod_map = lambda i: (i // tiles_per_mod, 0, 0)
    else:
        mod_map = lambda i: (i, 0, 0)
    row = lambda i: (i, 0)
    c2 = lambda i: (0, 0)
    mod = pl.BlockSpec((None, rows_per_mod, d), mod_map)
    return pl.pallas_call(
        _outproj_kernel, grid=(nt,),
        in_specs=[pl.BlockSpec((tm, d), row), pl.BlockSpec((tm, o_mla.shape[1]), row), pl.BlockSpec((tm, o_nsa.shape[1]), row),
                  mod, mod, mod, pl.BlockSpec(W["w_o_mla"].shape, c2), pl.BlockSpec(W["w_o_nsa"].shape, c2),
                  pl.BlockSpec((1, d), c2), pl.BlockSpec(W["w_router"].shape, c2), pl.BlockSpec((1, LANE), c2)],
        out_specs=[pl.BlockSpec((tm, d), row), pl.BlockSpec((tm, d), row), pl.BlockSpec((tm, LANE), row),
                   pl.BlockSpec((tm, LANE), row)],
        out_shape=[jax.ShapeDtypeStruct((t, d), F32), jax.ShapeDtypeStruct((t, d), F32),
                   jax.ShapeDtypeStruct((t, LANE), I32), jax.ShapeDtypeStruct((t, LANE), F32)],
        compiler_params=_cparams(("arbitrary",)), name="out_proj",
    )(x, o_mla, o_nsa, g1, sc, sh, W["w_o_mla"], W["w_o_nsa"], W["g_norm2"], W["w_router"], W["b_router"])


def _expert_kernel(te_ref, tv_ref, x_ref, wgu_ref, bgu_ref, wd_ref, bd_ref, o_ref, wgu_bf, wd_bf):
    t = pl.program_id(0)

    @pl.when((t == 0) | (te_ref[t] != te_ref[jnp.maximum(t - 1, 0)]))
    def _():
        wgu_bf[...] = wgu_ref[...].astype(BF16)
        wd_bf[...] = wd_ref[...].astype(BF16)

    @pl.when(tv_ref[t] > 0)
    def _():
        d_ff = wd_ref.shape[0]
        gu = _dot(x_ref[...].astype(BF16), wgu_bf[...]) + bgu_ref[...]
        glu = jnp.minimum(gu[:, :d_ff], SWIGLU_LIMIT)
        lin = jnp.clip(gu[:, d_ff:], -SWIGLU_LIMIT, SWIGLU_LIMIT)
        act = glu * _sigmoid(SWIGLU_ALPHA * glu) * (lin + 1.0)
        o_ref[...] = _dot(act.astype(BF16), wd_bf[...]) + bd_ref[...]

    @pl.when(tv_ref[t] == 0)
    def _():
        o_ref[...] = jnp.zeros_like(o_ref)


def _experts(tile_expert, tile_valid, x_sorted, W, *, tm):
    ns, d = x_sorted.shape
    gs = pltpu.PrefetchScalarGridSpec(
        num_scalar_prefetch=2, grid=(ns // tm,),
        in_specs=[pl.BlockSpec((tm, d), lambda t, te, tv: (t, 0)),
                  pl.BlockSpec((None,) + W["w_gate_up"].shape[1:], lambda t, te, tv: (te[t], 0, 0)),
                  pl.BlockSpec((None,) + W["b_gate_up"].shape[1:], lambda t, te, tv: (te[t], 0, 0)),
                  pl.BlockSpec((None,) + W["w_down"].shape[1:], lambda t, te, tv: (te[t], 0, 0)),
                  pl.BlockSpec((None,) + W["b_down"].shape[1:], lambda t, te, tv: (te[t], 0, 0))],
        out_specs=pl.BlockSpec((tm, d), lambda t, te, tv: (t, 0)),
        scratch_shapes=[pltpu.VMEM(W["w_gate_up"].shape[1:], BF16), pltpu.VMEM(W["w_down"].shape[1:], BF16)],
    )
    return pl.pallas_call(
        _expert_kernel, grid_spec=gs, out_shape=jax.ShapeDtypeStruct((ns, d), F32),
        compiler_params=_cparams(("arbitrary",)), name="experts",
    )(tile_expert, tile_valid, x_sorted, W["w_gate_up"], W["b_gate_up"], W["w_down"], W["b_down"])


def _final_kernel(x1_ref, tw_ref, *rest):
    parts, (g2_ref, gf_ref, o_ref) = rest[:TOP_K], rest[TOP_K:]
    tw = tw_ref[...]
    moe = tw[:, 0:1] * parts[0][...]
    for k in range(1, TOP_K):
        moe = moe + tw[:, k:k + 1] * parts[k][...]
    o_ref[...] = _rms(x1_ref[...] + g2_ref[...] * moe, gf_ref[...])


def _final(x1, top_w, moe_parts, g2, g_final, *, rows_per_mod, tm):
    t, d = x1.shape
    if rows_per_mod == 1:
        tiles_per_mod = (t // g2.shape[0]) // tm
        mod_map = lambda i: (i // tiles_per_mod, 0, 0)
    else:
        mod_map = lambda i: (i, 0, 0)
    row = lambda i: (i, 0)
    return pl.pallas_call(
        _final_kernel, grid=(t // tm,),
        in_specs=[pl.BlockSpec((tm, d), row), pl.BlockSpec((tm, LANE), row)] + [pl.BlockSpec((tm, d), row)] * TOP_K
        + [pl.BlockSpec((None, rows_per_mod, d), mod_map), pl.BlockSpec((1, d), lambda i: (0, 0))],
        out_specs=pl.BlockSpec((tm, d), row), out_shape=jax.ShapeDtypeStruct((t, d), F32),
        compiler_params=_cparams(("arbitrary",)), name="final_norm")(x1, top_w, *moe_parts, g2, g_final)


def _t5_bucket(dist):
    n = jnp.maximum(dist, 0)
    max_exact = NUM_BUCKETS // 2
    nf = jnp.maximum(n, 1).astype(F32)
    large = max_exact + (jnp.log(nf / max_exact) / math.log(MAX_DISTANCE / max_exact)
                         * (NUM_BUCKETS - max_exact)).astype(I32)
    return jnp.where(n < max_exact, n, jnp.minimum(large, NUM_BUCKETS - 1))


def _bias_rows(rel_bias, dist):
    bucket = _t5_bucket(dist)
    out = jnp.zeros((rel_bias.shape[1],) + dist.shape, F32)
    for b in range(NUM_BUCKETS):
        out = out + jnp.where(bucket == b, 1.0, 0.0)[None] * rel_bias[b][:, None, None]
    return out * LOG2E


def _rope_table(pos):
    half = MLA_ROPE // 2
    inv_freq = 1.0 / (ROPE_THETA ** (jnp.arange(half, dtype=F32) / half))
    ang = pos.astype(F32)[:, None] * inv_freq[None, :]
    pad = jnp.zeros((pos.shape[0], LANE - MLA_ROPE), F32)
    cos, sin = jnp.cos(ang), jnp.sin(ang)
    return jnp.concatenate([cos, cos, pad, sin, sin, pad], axis=1)


def _rot_cols(w):
    half = MLA_ROPE // 2
    return jnp.concatenate([-w[..., half:], w[..., :half]], axis=-1)


def _pad_last(w, n):
    return jnp.pad(w, [(0, 0)] * (w.ndim - 1) + [(0, n - w.shape[-1])])


def _pack_weights(w_in, g_norm1, g_norm2, g_q_a, w_q_b, g_kv_a, w_kv_b, cmp_pe, cmp_w1, cmp_w2, w_o, w_router,
                  b_router, w_gate_up, b_gate_up, w_down, b_down):
    d = w_in.shape[0]
    sizes = (MLA_Q_LORA, MLA_KV_LORA, MLA_ROPE, NSA_HEADS * HEAD_DIM, KV_COLS, KV_COLS, KV_COLS, 3 * NSA_HEADS)
    offs = [0]
    for s in sizes:
        offs.append(offs[-1] + s)
    w_qa, w_ckv, w_kr, w_qn, w_cmp, w_slc, w_swa, w_g = [w_in[:, offs[k]:offs[k + 1]] for k in range(8)]
    wq = w_qn.reshape(d, NSA_KV_HEADS, NSA_GROUP, HEAD_DIM)
    wq_pad = jnp.concatenate(
        [jnp.pad(wq[:, kv], ((0, 0), (0, 0), (kv * HEAD_DIM, LANE - (kv + 1) * HEAD_DIM))).reshape(d, NSA_GROUP * LANE)
         for kv in range(NSA_KV_HEADS)], axis=1)
    wg = jnp.transpose(w_g.reshape(d, NSA_KV_HEADS, NSA_GROUP, 3), (0, 1, 3, 2)).reshape(d, NSA_KV_HEADS, 3 * NSA_GROUP)
    wg_pad = _pad_last(wg, LANE).reshape(d, NSA_KV_HEADS * LANE)
    w_in_p = jnp.concatenate([w_qa, w_ckv, wq_pad, w_cmp, w_slc, w_swa, _pad_last(w_kr, LANE),
                              _pad_last(_rot_cols(w_kr), LANE), wg_pad], axis=1).astype(BF16)
    assert w_in_p.shape[1] == _C_END
    nope = _pad_last(w_q_b[:, :, :MLA_NOPE], LANE).reshape(MLA_Q_LORA, MLA_HEADS * LANE)
    rp = w_q_b[:, :, MLA_NOPE:]
    w_qb = jnp.concatenate([nope, _pad_last(rp, LANE).reshape(MLA_Q_LORA, -1),
                            _pad_last(_rot_cols(rp), LANE).reshape(MLA_Q_LORA, -1)], axis=1).astype(BF16)
    w_kn = jnp.transpose(w_kv_b[:, :, :MLA_NOPE], (1, 2, 0))
    w_kn = jnp.pad(w_kn, ((0, 0), (0, LANE - MLA_NOPE), (0, 0))).astype(BF16)
    wv = jnp.transpose(w_kv_b[:, :, MLA_NOPE:], (1, 0, 2))
    wv_pad = jnp.stack([jnp.pad(wv[h], ((0, 0), (h * MLA_V, (MLA_HEADS - 1 - h) * MLA_V))) for h in range(MLA_HEADS)]).astype(BF16)
    n_mla = MLA_HEADS * MLA_V
    won = w_o[n_mla:].reshape(NSA_KV_HEADS, NSA_GROUP, HEAD_DIM, d)
    won_pad = jnp.concatenate(
        [jnp.pad(won[kv], ((0, 0), (kv * HEAD_DIM, LANE - (kv + 1) * HEAD_DIM), (0, 0))).reshape(NSA_GROUP * LANE, d)
         for kv in range(NSA_KV_HEADS)], axis=0).astype(BF16)
    base = jnp.concatenate([cmp_w1[:, :CMP_STRIDE], cmp_w1[:, CMP_STRIDE:]], axis=-1)
    z = jnp.zeros_like(base)
    blk = jnp.concatenate([jnp.concatenate([base, z], axis=-1), jnp.concatenate([z, base], axis=-1)], axis=2)
    w_pair = blk.reshape(2, CMP_STRIDE // 2, 2 * NSA_KV_HEADS * HEAD_DIM, NSA_KV_HEADS * 2 * CMP_HIDDEN).astype(BF16)
    pe8 = jnp.broadcast_to(cmp_pe.reshape(2, 1, CMP_LEN * HEAD_DIM), (2, 8, CMP_LEN * HEAD_DIM))
    w1r = cmp_w1.reshape(2, CMP_LEN * HEAD_DIM, CMP_HIDDEN).astype(BF16)
    w2pad = jnp.stack([jnp.stack([jnp.pad(cmp_w2[j], ((0, 0), (kv * HEAD_DIM, LANE - (kv + 1) * HEAD_DIM)))
                                  for kv in range(NSA_KV_HEADS)]) for j in range(2)]).astype(BF16)
    return dict(
        w_in=w_in_p, g_norm1=g_norm1[None], g_norm2=g_norm2[None], g_q_a=g_q_a[None], g_kv_a=g_kv_a[None],
        w_qb=w_qb, w_kn=w_kn, wv=wv_pad, wv_cat=w_kv_b[:, :, MLA_NOPE:].reshape(MLA_KV_LORA, n_mla).astype(BF16),
        w_o_mla=w_o[:n_mla].astype(BF16), w_o_nsa=won_pad,
        w_pair=w_pair, pe8=pe8, w1r=w1r, w2pad=w2pad,
        w_router=_pad_last(w_router, LANE).T, b_router=jnp.pad(b_router, (0, LANE - N_EXPERTS), constant_values=NEG)[None],
        w_gate_up=w_gate_up, b_gate_up=b_gate_up[:, None, :], w_down=w_down, b_down=b_down[:, None, :])


def _overlap(n_half_pad, n_sel, n_sel_pad):
    c_start = jnp.arange(n_half_pad) * CMP_STRIDE
    s_start = jnp.arange(n_sel_pad) * SEL_BLOCK
    ov = (c_start[:, None] < s_start[None, :] + SEL_BLOCK) & (c_start[:, None] + CMP_LEN > s_start[None, :])
    return (ov & (jnp.arange(n_sel_pad) < n_sel)[None, :]).astype(F32)


def _moe_dispatch(top_i, tm):
    t = top_i.shape[0]
    a = t * TOP_K
    n_tiles = -(-a // tm) + N_EXPERTS
    ns = n_tiles * tm
    e_flat = top_i[:, :TOP_K].reshape(a)
    onehot = (e_flat[:, None] == jnp.arange(N_EXPERTS, dtype=I32)[None, :]).astype(I32)
    csum = jnp.cumsum(onehot, axis=0)
    counts = csum[-1]
    padded = ((counts + tm - 1) // tm) * tm
    pend = jnp.cumsum(padded)
    pstart = pend - padded
    start = jnp.cumsum(counts) - counts
    slot_of_assign = jnp.sum(onehot * (pstart[None, :] + csum - 1), axis=1).reshape(t, TOP_K)
    _, order = lax.sort((e_flat, jnp.arange(a, dtype=I32)), num_keys=1, is_stable=True)
    tile_start = jnp.arange(n_tiles, dtype=I32) * tm
    tile_expert = jnp.minimum(jnp.sum((pend[None, :] <= tile_start[:, None]).astype(I32), axis=1), N_EXPERTS - 1)
    tile_valid = (tile_start < pend[-1]).astype(I32)
    e_hot = (tile_expert[:, None] == jnp.arange(N_EXPERTS, dtype=I32)[None, :]).astype(I32)
    rank0 = tile_start - jnp.sum(e_hot * pstart[None, :], axis=1)
    lane = jnp.arange(tm, dtype=I32)[None, :]
    base = jnp.sum(e_hot * start[None, :], axis=1) + rank0
    run = order[jnp.clip(base[:, None] + lane, 0, a - 1)]
    rank = rank0[:, None] + lane
    valid = (rank < jnp.sum(e_hot * counts[None, :], axis=1, keepdims=True)) & (tile_valid[:, None] > 0)
    tok_of_slot = jnp.where(valid, run // TOP_K, 0).reshape(ns)
    return tok_of_slot, slot_of_assign, tile_expert, tile_valid


def kernel(x_prompt, x_sample, c_prompt, c_sample, cache_mla, cache_nsa_cmp, cache_nsa_slc, state_nsa_swa, page_table, rel_bias, w_ada, b_ada, g_norm1, g_norm2, w_in, g_q_a, w_q_b, g_kv_a, w_kv_b, cmp_pe, cmp_w1, cmp_w2, w_o, w_router, b_router, w_gate_up, b_gate_up, w_down, b_down, g_final):
    depth = w_in.shape[0]
    assert depth == 1, "single-layer decoder step"
    nb, seq, d = x_prompt.shape
    nbd, nq, _ = x_sample.shape
    n_pages = page_table.shape[1]
    page = cache_mla.shape[2]
    past = n_pages * page
    buf = state_nsa_swa.shape[2]
    assert page == LANE and nq <= 8 and seq % 512 == 0 and past % SEL_BLOCK == 0 and seq >= WINDOW
    tp, ts = nb * seq, nbd * nq
    W = _pack_weights(w_in[0], g_norm1[0], g_norm2[0], g_q_a[0], w_q_b[0], g_kv_a[0], w_kv_b[0], cmp_pe[0], cmp_w1[0],
                      cmp_w2[0], w_o[0], w_router[0], b_router[0], w_gate_up[0], b_gate_up[0], w_down[0], b_down[0])

    n_c = nb + nbd
    n_c_pad = -(-n_c // 8) * 8
    c_all = jnp.pad(jnp.concatenate([c_prompt, c_sample], axis=0), ((0, n_c_pad - n_c), (0, 0)))
    mod = _ada_mod(c_all, w_ada[0].astype(BF16), b_ada[0][None])
    mod_p = [m[:, None, :] for m in jnp.split(mod[:nb], 6, axis=-1)]
    tm_s = min(256, ts)
    mod_s = [jnp.repeat(m, nq, axis=0).reshape(ts // tm_s, tm_s, d) for m in jnp.split(mod[nb:n_c], 6, axis=-1)]

    tm_p = 256
    pos_p = jnp.arange(seq)
    pos_s = past + jnp.arange(nq)
    P = _proj(x_prompt.reshape(tp, d), mod_p[1], mod_p[0], _rope_table(pos_p), W, rows_per_mod=1,
              cs_period_tiles=seq // tm_p, tm=tm_p, states_t_batches=nb)
    cs_s = jnp.tile(_rope_table(pos_s), (tm_s // nq, 1))
    S_ = _proj(x_sample.reshape(ts, d), mod_s[1], mod_s[0], cs_s, W, rows_per_mod=tm_s, cs_period_tiles=1, tm=tm_s)
    p_qmla, p_kq, p_mla_t, p_qn, p_cmp, p_cmp_t, p_slc_t, p_swa_t, p_slcb, p_swab, p_gate = P
    s_qmla, s_kq, s_mla, s_qn, _, s_cmp, s_slc, s_swa, s_slcb, s_swab, s_gate = S_

    def pad_new(a):
        return jnp.pad(a.reshape(nbd, nq, a.shape[1]), ((0, 0), (0, LANE - nq), (0, 0)))

    o_mla_p = _mla_prompt(p_qmla, p_kq, W["wv"], nb=nb, seq=seq)
    g_pages = min(16, n_pages)
    cache_mla_t = jnp.swapaxes(cache_mla[0], 1, 2)
    cache_cmp_t = jnp.moveaxis(cache_nsa_cmp[0], 1, -1).reshape(-1, KV_COLS, page)
    cache_slc_t = jnp.moveaxis(cache_nsa_slc[0], 1, -1).reshape(-1, KV_COLS, page)
    state_swa_t = jnp.moveaxis(state_nsa_swa[0], 1, -1).reshape(nbd, KV_COLS, buf)
    o_mla_s = _mla_sample(page_table, s_qmla.reshape(nbd, nq, -1), pad_new(s_kq), W["wv_cat"], cache_mla_t, g=g_pages,
                          nbs=2 if nbd % 2 == 0 else 1)

    assert (past + nq) // CMP_STRIDE == past // CMP_STRIDE
    n_pp = tp // LANE
    kc_p, vc_p = _compress(jnp.arange(n_pp, dtype=I32), p_cmp.reshape(n_pp, LANE, KV_COLS), W["w_pair"], W["pe8"],
                           W["w1r"], W["w2pad"], nb=nb, transposed=False)
    kc_s, vc_s = _compress(page_table.reshape(nbd * n_pages), cache_cmp_t, W["w_pair"], W["pe8"], W["w1r"], W["w2pad"],
                           nb=nbd, transposed=True)

    nh_p = seq // CMP_STRIDE
    n_sel_p = -(-seq // SEL_BLOCK)
    tq = 128
    far_h = _bias_rows(rel_bias, jnp.full((1, 1), MAX_DISTANCE))
    d_win = (jnp.arange(tq)[:, None] - CMP_STRIDE * (jnp.arange(LANE)[None, :] - CMP_WIN_LO) - (CMP_LEN - 1))
    assert CMP_WIN_LO * CMP_STRIDE >= MAX_DISTANCE + CMP_LEN and LANE - CMP_WIN_LO >= tq // CMP_STRIDE
    bias_c_p = jnp.where((d_win >= 0)[None], _bias_rows(rel_bias, d_win) - far_h, 0.0)
    bias_c_p = bias_c_p.reshape(NSA_HEADS * tq, LANE)
    di = jnp.arange(tq)[:, None] - jnp.arange(tq)[None, :]
    tt = jnp.stack([_bias_rows(rel_bias, di), _bias_rows(rel_bias, di + tq),
                    _bias_rows(rel_bias, jnp.full((tq, tq), MAX_DISTANCE))], axis=1)
    tt = tt.transpose(1, 0, 2, 3).reshape(3, NSA_HEADS * tq, tq)
    assert n_sel_p <= LANE
    ovl_p = jnp.tile(_overlap(nh_p, n_sel_p, LANE).T, (1, 3)).astype(BF16)
    o_nsa_p = _nsa_prompt(p_qn, kc_p, vc_p, bias_c_p, ovl_p, p_slcb.reshape(nb, seq, KV_COLS),
                          p_swab.reshape(nb, seq, KV_COLS), tt, p_gate, nb=nb, seq=seq, n_cmp=nh_p - 1, n_sel=n_sel_p)

    nh_s = past // CMP_STRIDE
    n_sel_s = -(-(past + nq) // SEL_BLOCK)
    assert n_sel_s == past // SEL_BLOCK + 1
    n_sel_pad = -(-n_sel_s // LANE) * LANE
    rows_s = NSA_HEADS * nq
    cmp_end_s = jnp.arange(nh_s) * CMP_STRIDE + CMP_LEN - 1
    bias_c_s = _bias_rows(rel_bias, pos_s[:, None] - cmp_end_s[None, :]).reshape(rows_s, nh_s)
    nk_step = g_pages * page
    far = jnp.broadcast_to(_bias_rows(rel_bias, jnp.full((nq, 1), MAX_DISTANCE)).reshape(rows_s, 1), (rows_s, nk_step))
    tail_pos = past - nk_step + jnp.arange(nk_step)
    bias_s = jnp.stack([far, _bias_rows(rel_bias, pos_s[:, None] - tail_pos[None, :]).reshape(rows_s, nk_step)])
    new_pos = past + jnp.arange(LANE)
    bias_new = _bias_rows(rel_bias, pos_s[:, None] - new_pos[None, :]).reshape(rows_s, LANE)
    swa_pos = past - buf + jnp.arange(buf)
    bias_w = _bias_rows(rel_bias, pos_s[:, None] - swa_pos[None, :]).reshape(rows_s, buf)
    ovl_s = jnp.tile(_overlap(nh_s, n_sel_s, n_sel_pad).T, (1, 3)).astype(BF16)
    o_nsa_s = _nsa_sample(page_table, s_qn.reshape(nbd, nq, -1), kc_s, vc_s, bias_c_s, ovl_s, bias_s, pad_new(s_slcb),
                          bias_new, state_swa_t, pad_new(s_swab), bias_w, bias_new,
                          s_gate.reshape(nbd, nq, -1), cache_slc_t, g=g_pages, nbs=2 if nbd % 2 == 0 else 1,
                          n_cmp=nh_s - 1, n_sel=n_sel_s)

    x1_p, h2_p, ti_p, tw_p = _outproj(x_prompt.reshape(tp, d), o_mla_p, o_nsa_p, mod_p[2], mod_p[4], mod_p[3], W,
                                      rows_per_mod=1, tm=2 * tm_p)
    x1_s, h2_s, ti_s, tw_s = _outproj(x_sample.reshape(ts, d), o_mla_s.reshape(ts, -1), o_nsa_s.reshape(ts, -1),
                                      mod_s[2], mod_s[4], mod_s[3], W, rows_per_mod=tm_s, tm=tm_s)

    tm_e = 256
    h2 = jnp.concatenate([h2_p, h2_s], axis=0)
    tok_of_slot, slot_of_assign, tile_expert, tile_valid = _moe_dispatch(jnp.concatenate([ti_p, ti_s], axis=0), tm_e)
    y_sorted = _experts(tile_expert, tile_valid, h2[tok_of_slot], W, tm=tm_e)
    parts_p = [y_sorted[slot_of_assign[:tp, k]] for k in range(TOP_K)]
    parts_s = [y_sorted[slot_of_assign[tp:, k]] for k in range(TOP_K)]
    y_p = _final(x1_p, tw_p, parts_p, mod_p[5], g_final[None], rows_per_mod=1, tm=tm_p)
    y_s = _final(x1_s, tw_s, parts_s, mod_s[5], g_final[None], rows_per_mod=tm_s, tm=tm_s)

    kv_tail = (2, NSA_KV_HEADS, HEAD_DIM)
    keep_p = min(WINDOW, seq)
    swa_keys = jnp.concatenate([state_nsa_swa[0], s_swa.reshape((nbd, nq) + kv_tail)], axis=1)
    keep_s = min(WINDOW, buf + nq)

    def rows_last(a_t):
        return jnp.moveaxis(a_t.reshape((nb,) + kv_tail + (a_t.shape[-1],)), -1, 1)[None]

    return (y_p.reshape(nb, seq, d), y_s.reshape(nbd, nq, d),
            jnp.swapaxes(p_mla_t, 1, 2)[None], s_mla.reshape(1, nbd, nq, -1),
            rows_last(p_cmp_t), s_cmp.reshape((1, nbd, nq) + kv_tail),
            rows_last(p_slc_t), s_slc.reshape((1, nbd, nq) + kv_tail),
            rows_last(p_swa_t[:, :, seq - keep_p:]),
            swa_keys[None, :, buf + nq - keep_s:])
```

```python
import functools
import math

import jax
import jax.numpy as jnp
from jax import lax
from jax.experimental import pallas as pl
from jax.experimental.pallas import tpu as pltpu

F32, BF16, I32 = jnp.float32, jnp.bfloat16, jnp.int32

MLA_HEADS = 8
MLA_Q_LORA = 384
MLA_KV_LORA = 256
MLA_NOPE = 64
MLA_ROPE = 32
MLA_V = 64
NSA_HEADS = 8
NSA_KV_HEADS = 2
NSA_GROUP = NSA_HEADS // NSA_KV_HEADS
HEAD_DIM = 64
CMP_LEN = 32
CMP_STRIDE = 16
CMP_HIDDEN = 128
SEL_BLOCK = 64
SEL_TOPN = 16
WINDOW = 512
KV_COLS = 2 * NSA_KV_HEADS * HEAD_DIM
N_EXPERTS = 32
TOP_K = 4
SWIGLU_LIMIT = 7.0
SWIGLU_ALPHA = 1.702
NUM_BUCKETS = 32
MAX_DISTANCE = 128
ROPE_THETA = 10000.0
NORM_EPS = 1e-6

LANE = 128
VMEM_LIMIT = 56 * 1024 * 1024
NEG = -1e30
CMP_WIN_LO = 16
TOKEN_TILE = 256
QUERY_TILE = 128
EXPERT_TILE = 256
PAGES_PER_STEP = 16

MLA_QW = MLA_KV_LORA + LANE
MLA_SCALE = (MLA_NOPE + MLA_ROPE) ** -0.5
MLA_QSCALE = MLA_SCALE * math.log2(math.e)
LOG2E = math.log2(math.e)
NSA_QSCALE = HEAD_DIM ** -0.5 * LOG2E

_C_QA = 0
_C_CKV = _C_QA + MLA_Q_LORA
_C_QN = _C_CKV + MLA_KV_LORA
_C_CMP = _C_QN + NSA_HEADS * LANE
_C_SLC = _C_CMP + KV_COLS
_C_SWA = _C_SLC + KV_COLS
_C_KR = _C_SWA + KV_COLS
_C_KRR = _C_KR + LANE
_C_G = _C_KRR + LANE
_C_END = _C_G + NSA_KV_HEADS * LANE


def _cparams(sem, vmem=VMEM_LIMIT):
    return pltpu.CompilerParams(dimension_semantics=sem, vmem_limit_bytes=vmem)


def _dot(a, b):
    return jnp.dot(a, b, preferred_element_type=F32)


def _dot_nt(a, b):
    return lax.dot_general(a, b, (((1,), (1,)), ((), ())), preferred_element_type=F32)


def _row_reduce(x, op, reduce):
    n = x.shape[1] // LANE
    if x.shape[1] % LANE or n <= 1:
        return reduce(x, axis=-1, keepdims=True)
    t = x[:, :LANE]
    for c in range(1, n):
        t = op(t, x[:, c * LANE:(c + 1) * LANE])
    return reduce(t, axis=-1, keepdims=True)


def _rowmax(x):
    return _row_reduce(x, jnp.maximum, jnp.max)


def _rowsum(x):
    return _row_reduce(x, jnp.add, jnp.sum)


def _split3(x):
    hi = x.astype(BF16)
    r1 = x - hi.astype(F32)
    mid = r1.astype(BF16)
    lo = (r1 - mid.astype(F32)).astype(BF16)
    return jnp.concatenate([hi, mid, lo], axis=-1)


def _rms(x, g):
    return x * lax.rsqrt(jnp.mean(x * x, axis=-1, keepdims=True) + NORM_EPS) * g


def _sigmoid(x):
    return 1.0 / (1.0 + jnp.exp(-x))


def _ada_kernel(c_ref, w_ref, b_ref, o_ref):
    c = c_ref[...]
    o_ref[...] = _dot((c * _sigmoid(c)).astype(BF16), w_ref[...]) + b_ref[...]


def _ada_mod(c_all, w_ada, b_ada):
    m, d = c_all.shape
    n = w_ada.shape[1]
    tn = 1536
    return pl.pallas_call(
        _ada_kernel, grid=(n // tn,),
        in_specs=[pl.BlockSpec((m, d), lambda i: (0, 0)), pl.BlockSpec((d, tn), lambda i: (0, i)),
                  pl.BlockSpec((1, tn), lambda i: (0, i))],
        out_specs=pl.BlockSpec((m, tn), lambda i: (0, i)),
        out_shape=jax.ShapeDtypeStruct((m, n), F32),
        compiler_params=_cparams(("arbitrary",)), name="ada_mod")(c_all, w_ada, b_ada)


def _proj_kernel(x_ref, sc_ref, sh_ref, g1_ref, cs_ref, win_ref, gqa_ref, wqb_ref, gkva_ref, wkn_ref,
                 qmla_ref, kq_ref, mla_ref, qn_ref, cmp_ref, cmps_ref, slc_ref, swa_ref, slcb_ref, swab_ref, gate_ref,
                 *, states_t):
    h = _rms(x_ref[...], g1_ref[...]) * (1.0 + sc_ref[...]) + sh_ref[...]
    proj = _dot(h.astype(BF16), win_ref[...])
    cs = cs_ref[...]
    cosp, sinp = cs[:, :LANE], cs[:, LANE:]
    qa = _rms(proj[:, _C_QA:_C_CKV], gqa_ref[...])
    q = _dot(qa.astype(BF16), wqb_ref[...])
    hw = MLA_HEADS * LANE
    for h_i in range(MLA_HEADS):
        lo = h_i * LANE
        qlat = _dot(q[:, lo:lo + LANE].astype(BF16), wkn_ref[h_i]) * MLA_QSCALE
        qr = (q[:, hw + lo:hw + lo + LANE] * cosp + q[:, 2 * hw + lo:2 * hw + lo + LANE] * sinp) * MLA_QSCALE
        qmla_ref[:, h_i * MLA_QW:h_i * MLA_QW + MLA_KV_LORA] = qlat.astype(BF16)
        qmla_ref[:, h_i * MLA_QW + MLA_KV_LORA:(h_i + 1) * MLA_QW] = qr.astype(BF16)
    ckv = _rms(proj[:, _C_CKV:_C_QN], gkva_ref[...])
    kr = proj[:, _C_KR:_C_KRR] * cosp + proj[:, _C_KRR:_C_G] * sinp
    kq_ref[:, :MLA_KV_LORA] = ckv.astype(BF16)
    kq_ref[:, MLA_KV_LORA:] = kr.astype(BF16)
    qn_ref[...] = (proj[:, _C_QN:_C_CMP] * NSA_QSCALE).astype(BF16)
    cmp = proj[:, _C_CMP:_C_SLC]
    slc = proj[:, _C_SLC:_C_SWA]
    swa = proj[:, _C_SWA:_C_KR]
    cmp_ref[...] = cmp
    if states_t:
        mla_ref[:MLA_KV_LORA, :] = ckv.T
        mla_ref[MLA_KV_LORA:, :] = kr.T[:MLA_ROPE]
        cmps_ref[...] = cmp.T
        slc_ref[...] = slc.T
        swa_ref[...] = swa.T
    else:
        mla_ref[:, :MLA_KV_LORA] = ckv
        mla_ref[:, MLA_KV_LORA:] = kr[:, :MLA_ROPE]
        cmps_ref[...] = cmp
        slc_ref[...] = slc
        swa_ref[...] = swa
    slcb_ref[...] = slc.astype(BF16)
    swab_ref[...] = swa.astype(BF16)
    gate_ref[...] = _sigmoid(proj[:, _C_G:_C_END])


def _proj(x, sc, sh, cs, W, *, rows_per_mod, cs_period_tiles, tm, states_t_batches=0):
    t, d = x.shape
    nt = t // tm
    if rows_per_mod == 1:
        tiles_per_mod = sc.shape[0] and (t // sc.shape[0]) // tm
        mod_map = lambda i: (i // tiles_per_mod, 0, 0)
    else:
        mod_map = lambda i: (i, 0, 0)
    cs_map = (lambda i: (i % cs_period_tiles, 0)) if cs_period_tiles > 1 else (lambda i: (0, 0))
    const2 = lambda i: (0, 0)
    row = lambda i: (i, 0)
    outs = [
        (MLA_HEADS * MLA_QW, BF16), (MLA_QW, BF16), (MLA_KV_LORA + MLA_ROPE, F32), (NSA_HEADS * LANE, BF16),
        (KV_COLS, F32), (KV_COLS, F32), (KV_COLS, F32), (KV_COLS, F32), (KV_COLS, BF16), (KV_COLS, BF16),
        (NSA_KV_HEADS * LANE, F32),
    ]
    state_outs = (2, 5, 6, 7) if states_t_batches else ()
    tiles_per_batch = (t // states_t_batches) // tm if states_t_batches else 0
    out_specs = [pl.BlockSpec((None, w, tm), lambda i: (i // tiles_per_batch, 0, i % tiles_per_batch))
                 if k in state_outs else pl.BlockSpec((tm, w), row) for k, (w, _) in enumerate(outs)]
    out_shape = [jax.ShapeDtypeStruct((states_t_batches, w, t // states_t_batches) if k in state_outs else (t, w), dt)
                 for k, (w, dt) in enumerate(outs)]
    return pl.pallas_call(
        functools.partial(_proj_kernel, states_t=bool(states_t_batches)), grid=(nt,),
        in_specs=[
            pl.BlockSpec((tm, d), row),
            pl.BlockSpec((None, rows_per_mod, d), mod_map),
            pl.BlockSpec((None, rows_per_mod, d), mod_map),
            pl.BlockSpec((1, d), const2),
            pl.BlockSpec((tm, 2 * LANE), cs_map),
            pl.BlockSpec(W["w_in"].shape, const2),
            pl.BlockSpec((1, MLA_Q_LORA), const2),
            pl.BlockSpec(W["w_qb"].shape, const2),
            pl.BlockSpec((1, MLA_KV_LORA), const2),
            pl.BlockSpec(W["w_kn"].shape, lambda i: (0, 0, 0)),
        ],
        out_specs=out_specs, out_shape=out_shape,
        compiler_params=_cparams(("arbitrary",)), name="proj_in",
    )(x, sc, sh, W["g_norm1"], cs, W["w_in"], W["g_q_a"], W["w_qb"], W["g_kv_a"], W["w_kn"])


def _mla_prompt_kernel(q_ref, k_ref, wv_ref, o_ref, m_ref, l_ref, acc_ref, *, tq, tk):
    i, j = pl.program_id(1), pl.program_id(2)
    nk = pl.num_programs(2)

    @pl.when(j == 0)
    def _():
        m_ref[...] = jnp.full_like(m_ref, NEG)
        l_ref[...] = jnp.zeros_like(l_ref)
        acc_ref[...] = jnp.zeros_like(acc_ref)

    def step(masked):
        k = k_ref[...]
        v = k[:, :MLA_KV_LORA]
        if masked:
            qpos = i * tq + lax.broadcasted_iota(I32, (tq, tk), 0)
            kpos = j * tk + lax.broadcasted_iota(I32, (tq, tk), 1)
            mask = kpos <= qpos
        ss = [_dot_nt(q_ref[:, h * MLA_QW:(h + 1) * MLA_QW], k) for h in range(MLA_HEADS)]
        ps, alphas = [], []
        for h in range(MLA_HEADS):
            r = slice(h * tq, (h + 1) * tq)
            s = jnp.where(mask, ss[h], NEG) if masked else ss[h]
            m_prev = m_ref[r]
            m_new = jnp.maximum(m_prev, _rowmax(s))
            p = jnp.exp2(s - m_new)
            if masked:
                p = jnp.where(mask, p, 0.0)
            alpha = jnp.exp2(m_prev - m_new)
            l_ref[r] = alpha * l_ref[r] + _rowsum(p)
            m_ref[r] = m_new
            ps.append(p.astype(BF16))
            alphas.append(alpha)
        for h in range(MLA_HEADS):
            r = slice(h * tq, (h + 1) * tq)
            acc_ref[r] = alphas[h] * acc_ref[r] + _dot(ps[h], v)

    first_key, last_key = j * tk, j * tk + tk - 1
    pl.when(last_key <= i * tq)(functools.partial(step, False))
    pl.when((first_key <= i * tq + tq - 1) & (last_key > i * tq))(functools.partial(step, True))

    @pl.when(j == nk - 1)
    def _():
        out = jnp.zeros(o_ref.shape, F32)
        for h in range(MLA_HEADS):
            r = slice(h * tq, (h + 1) * tq)
            l = l_ref[r]
            o_lat = acc_ref[r] / jnp.where(l > 0.0, l, 1.0)
            out = out + _dot(o_lat.astype(BF16), wv_ref[h])
        o_ref[...] = out.astype(o_ref.dtype)


def _mla_prompt(qmla, kq, wv, *, nb, seq, tq=256, tk=1024):
    tk = min(tk, seq)
    if seq % tk:
        tk = 512
    nq, nk = seq // tq, seq // tk
    ow = MLA_HEADS * MLA_V

    def k_map(b, i, j):
        return (b * nk + jnp.minimum(j, (i * tq + tq - 1) // tk), 0)

    return pl.pallas_call(
        functools.partial(_mla_prompt_kernel, tq=tq, tk=tk), grid=(nb, nq, nk),
        in_specs=[pl.BlockSpec((tq, MLA_HEADS * MLA_QW), lambda b, i, j: (b * nq + i, 0)),
                  pl.BlockSpec((tk, MLA_QW), k_map),
                  pl.BlockSpec(wv.shape, lambda b, i, j: (0, 0, 0))],
        out_specs=pl.BlockSpec((tq, ow), lambda b, i, j: (b * nq + i, 0)),
        out_shape=jax.ShapeDtypeStruct((nb * seq, ow), BF16),
        scratch_shapes=[pltpu.VMEM((MLA_HEADS * tq, 1), F32), pltpu.VMEM((MLA_HEADS * tq, 1), F32),
                        pltpu.VMEM((MLA_HEADS * tq, MLA_KV_LORA), F32)],
        compiler_params=_cparams(("arbitrary", "arbitrary", "arbitrary")), name="mla_prompt",
    )(qmla, kq, wv)


def _mla_sample_kernel(pt_ref, q_ref, kn_ref, wv_ref, *rest, g, nq, nbs):
    pages = rest[:nbs * g]
    o_ref, m_ref, l_ref, acc_ref = rest[nbs * g:]
    s_idx = pl.program_id(1)
    rows = MLA_HEADS * nq

    @pl.when(s_idx == 0)
    def _():
        m_ref[...] = jnp.full_like(m_ref, NEG)
        l_ref[...] = jnp.zeros_like(l_ref)
        acc_ref[...] = jnp.zeros_like(acc_ref)

    qs = []
    for bl in range(nbs):
        r = slice(bl * rows, (bl + 1) * rows)
        q = jnp.concatenate([q_ref[bl, :, h * MLA_QW:(h + 1) * MLA_QW] for h in range(MLA_HEADS)], axis=0)
        qs.append(q)
        qlat, qr = q[:, :MLA_KV_LORA], q[:, MLA_KV_LORA:MLA_KV_LORA + MLA_ROPE]
        pg = pages[bl * g:(bl + 1) * g]
        ckv_t = jnp.concatenate([pg[u][:MLA_KV_LORA].astype(BF16) for u in range(g)], axis=1)
        kr_t = jnp.concatenate([pg[u][MLA_KV_LORA:].astype(BF16) for u in range(g)], axis=1)
        s = _dot(qlat, ckv_t) + _dot(qr, kr_t)
        m_prev = m_ref[r]
        m_new = jnp.maximum(m_prev, _rowmax(s))
        p = jnp.exp2(s - m_new)
        alpha = jnp.exp2(m_prev - m_new)
        l_ref[r] = alpha * l_ref[r] + _rowsum(p)
        acc_ref[r] = alpha * acc_ref[r] + _dot_nt(p.astype(BF16), ckv_t)
        m_ref[r] = m_new

    @pl.when(s_idx == pl.num_programs(1) - 1)
    def _():
        for bl in range(nbs):
            r = slice(bl * rows, (bl + 1) * rows)
            kn = kn_ref[bl]
            sn = _dot_nt(qs[bl], kn)
            qi = lax.broadcasted_iota(I32, sn.shape, 0) % nq
            kt = lax.broadcasted_iota(I32, sn.shape, 1)
            mask = kt <= qi
            sn = jnp.where(mask, sn, NEG)
            m_prev = m_ref[r]
            m_new = jnp.maximum(m_prev, jnp.max(sn, axis=-1, keepdims=True))
            p = jnp.where(mask, jnp.exp2(sn - m_new), 0.0)
            alpha = jnp.exp2(m_prev - m_new)
            l = alpha * l_ref[r] + _rowsum(p)
            acc = alpha * acc_ref[r] + _dot(p.astype(BF16), kn[:, :MLA_KV_LORA])
            o_lat = (acc / jnp.where(l > 0.0, l, 1.0)).astype(BF16)
            res = _dot(o_lat, wv_ref[...])
            head_of_lane = lax.broadcasted_iota(I32, (nq, res.shape[1]), 1) // MLA_V
            out = jnp.zeros((nq, res.shape[1]), F32)
            for h in range(MLA_HEADS):
                out = out + jnp.where(head_of_lane == h, res[h * nq:(h + 1) * nq], 0.0)
            o_ref[bl] = out.astype(o_ref.dtype)


def _page_specs(g, block, n_lane_blocks=1, flat=False, nbs=1):
    def index(b, s, pt, bl, u, c):
        return ((pt[s * g + u] if flat else pt[b * nbs + bl, s * g + u]), 0, c)

    return [pl.BlockSpec(block, functools.partial(index, bl=bl, u=u, c=c))
            for bl in range(nbs) for u in range(g) for c in range(n_lane_blocks)]


def _mla_sample(page_table, qmla, kq_new_pad, wv, cache_mla, *, g, nbs):
    nb, nq = qmla.shape[0], qmla.shape[1]
    n_pages = page_table.shape[1]
    ow = MLA_HEADS * MLA_V
    rows = nbs * MLA_HEADS * nq
    bmap = lambda b, s, pt: (b, 0, 0)
    gs = pltpu.PrefetchScalarGridSpec(
        num_scalar_prefetch=1, grid=(nb // nbs, n_pages // g),
        in_specs=[pl.BlockSpec((nbs, nq, MLA_HEADS * MLA_QW), bmap),
                  pl.BlockSpec((nbs, LANE, MLA_QW), bmap),
                  pl.BlockSpec(wv.shape, lambda b, s, pt: (0, 0))]
        + _page_specs(g, (None,) + cache_mla.shape[1:], nbs=nbs),
        out_specs=pl.BlockSpec((nbs, nq, ow), bmap),
        scratch_shapes=[pltpu.VMEM((rows, 1), F32), pltpu.VMEM((rows, 1), F32), pltpu.VMEM((rows, MLA_KV_LORA), F32)],
    )
    return pl.pallas_call(
        functools.partial(_mla_sample_kernel, g=g, nq=nq, nbs=nbs), grid_spec=gs,
        out_shape=jax.ShapeDtypeStruct((nb, nq, ow), BF16),
        compiler_params=_cparams(("arbitrary", "arbitrary")), name="mla_sample",
    )(page_table, qmla, kq_new_pad, wv, *([cache_mla] * (nbs * g)))


def _compress_kernel(pl_ref, w_ref, pe_ref, w1_ref, w2_ref, *rest, g, transposed):
    pages = rest[:g]
    kc_ref, vc_ref, xs_ref = rest[g:]
    half_w = NSA_KV_HEADS * 2 * CMP_HIDDEN
    nh = kc_ref.shape[0]
    for u in range(g):
        x = pages[u][...]
        for j in range(2):
            if transposed:
                xs_ref[j, u * LANE:(u + 1) * LANE, :] = x[j * LANE:(j + 1) * LANE, :].T
            else:
                xs_ref[j, u * LANE:(u + 1) * LANE, :] = x[:, j * LANE:(j + 1) * LANE]
    for j, out_ref in enumerate((kc_ref, vc_ref)):
        ab = jnp.zeros((nh, half_w), F32)
        for pp in range(CMP_STRIDE // 2):
            xa = xs_ref[j, pl.ds(2 * pp, nh, stride=CMP_STRIDE), :]
            xb = xs_ref[j, pl.ds(2 * pp + 1, nh, stride=CMP_STRIDE), :]
            ab = ab + _dot(jnp.concatenate([xa, xb], axis=1).astype(BF16), w_ref[j, pp])
        pe_term = _dot(pe_ref[j].astype(BF16), w1_ref[j])[0:1]
        acc = jnp.zeros((nh, LANE), F32)
        for kv in range(NSA_KV_HEADS):
            base = kv * 2 * CMP_HIDDEN
            first = ab[:, base:base + CMP_HIDDEN]
            second = ab[:, base + CMP_HIDDEN:base + 2 * CMP_HIDDEN]
            hid = first + pltpu.roll(second, nh - 1, 0) + pe_term
            hid = 0.5 * hid * (1.0 + lax.erf(hid * math.sqrt(0.5)))
            acc = acc + _dot(hid.astype(BF16), w2_ref[j, kv])
        out_ref[...] = acc.astype(out_ref.dtype)


def _compress(page_list, pool, w_pair, pe8, w1r, w2pad, *, nb, transposed):
    g = page_list.shape[0] // nb
    nh = g * (LANE // CMP_STRIDE)
    c3 = lambda b, s, pt: (0, 0, 0)
    gs = pltpu.PrefetchScalarGridSpec(
        num_scalar_prefetch=1, grid=(1, nb),
        in_specs=[pl.BlockSpec(w_pair.shape, lambda b, s, pt: (0, 0, 0, 0)), pl.BlockSpec(pe8.shape, c3),
                  pl.BlockSpec(w1r.shape, c3), pl.BlockSpec(w2pad.shape, lambda b, s, pt: (0, 0, 0, 0))]
        + _page_specs(g, (None,) + pool.shape[1:], flat=True),
        out_specs=[pl.BlockSpec((None, nh, LANE), lambda b, s, pt: (s, 0, 0))] * 2,
        scratch_shapes=[pltpu.VMEM((2, g * LANE, LANE), F32)],
    )
    return pl.pallas_call(
        functools.partial(_compress_kernel, g=g, transposed=transposed), grid_spec=gs,
        out_shape=[jax.ShapeDtypeStruct((nb, nh, LANE), BF16)] * 2,
        compiler_params=_cparams(("arbitrary", "arbitrary")), name="compress",
    )(page_list, w_pair, pe8, w1r, w2pad, *([pool] * g))


def _topk_mask(v, k, axis=1):
    lane = lax.broadcasted_iota(I32, v.shape, axis)
    sel = jnp.zeros(v.shape, F32)
    for _ in range(k):
        m = jnp.max(v, axis=axis, keepdims=True)
        idx = jnp.min(jnp.where(v == m, lane, 1 << 20), axis=axis, keepdims=True)
        pick = lane == idx
        sel = jnp.where(pick & (m > -jnp.inf), 1.0, sel)
        v = jnp.where(pick, -jnp.inf, v)
    return sel


def _softmax_masked(s, mask):
    s = jnp.where(mask, s, NEG)
    m = _rowmax(s)
    p = jnp.where(mask, jnp.exp2(s - m), 0.0)
    l = _rowsum(p)
    return p / jnp.where(l > 0.0, l, 1.0)


def _flash_step(s, mask, v, m_ref, l_ref, acc_ref, v_keys_on_lanes=False):
    s = jnp.where(mask, s, NEG)
    m_prev = m_ref[...]
    m_new = jnp.maximum(m_prev, _rowmax(s))
    p = jnp.where(mask, jnp.exp2(s - m_new), 0.0)
    alpha = jnp.exp2(m_prev - m_new)
    l_ref[...] = alpha * l_ref[...] + _rowsum(p)
    pv = _dot_nt(p.astype(BF16), v) if v_keys_on_lanes else _dot(p.astype(BF16), v)
    acc_ref[...] = alpha * acc_ref[...] + pv
    m_ref[...] = m_new


def _flash_init(m_ref, l_ref, acc_ref):
    m_ref[...] = jnp.full_like(m_ref, NEG)
    l_ref[...] = jnp.zeros_like(l_ref)
    acc_ref[...] = jnp.zeros_like(acc_ref)


def _flash_out(l_ref, acc_ref):
    l = l_ref[...]
    return acc_ref[...] / jnp.where(l > 0.0, l, 1.0)


def _nsa_prompt_kernel(q_ref, kc_ref, vc_ref, bc_ref, ovl_ref, slc_ref, swa_ref, tt_ref, gate_ref, o_ref,
                       m_ref, l_ref, acc_ref, *, tq, tkf, n_cmp, n_sel, seq_len):
    i = pl.program_id(1)
    rows = NSA_HEADS * tq
    n_sel_rows = -(-n_sel // 8) * 8
    q = jnp.concatenate([q_ref[:, h * LANE:(h + 1) * LANE] for h in range(NSA_HEADS)], axis=0)
    nhp = kc_ref.shape[0]
    pos_r = i * tq + lax.broadcasted_iota(I32, (rows, 1), 0) % tq

    wu = lax.broadcasted_iota(I32, (LANE, nhp), 0)
    wn = lax.broadcasted_iota(I32, (LANE, nhp), 1)
    shift = jnp.where(wn == i * (tq // CMP_STRIDE) - CMP_WIN_LO + wu, 1.0, 0.0).astype(BF16)
    far_col = tt_ref[2][:, :1]
    bias_c = _dot(_split3(bc_ref[...]), jnp.concatenate([shift] * 3, axis=0)) + far_col
    s = _dot_nt(q, kc_ref[...]) + bias_c
    n_idx = lax.broadcasted_iota(I32, (rows, nhp), 1)
    mask_c = (n_idx * CMP_STRIDE + CMP_LEN - 1 <= pos_r) & (n_idx < n_cmp)
    p_cmp = _softmax_masked(s, mask_c)
    o_cmp = _dot(p_cmp.astype(BF16), vc_ref[...])

    imps = []
    for kv in range(NSA_KV_HEADS):
        lo = kv * NSA_GROUP * tq
        psum = p_cmp[lo:lo + tq]
        for g in range(1, NSA_GROUP):
            psum = psum + p_cmp[lo + g * tq:lo + (g + 1) * tq]
        imps.append(_dot_nt(ovl_ref[...], _split3(psum))[:n_sel_rows])
    imp_t = jnp.concatenate(imps, axis=1)
    blk = lax.broadcasted_iota(I32, imp_t.shape, 0)
    cur = (i * tq + lax.broadcasted_iota(I32, imp_t.shape, 1) % tq) // SEL_BLOCK
    forced = (blk == 0) | (blk == cur) | (blk == cur - 1)
    future = (blk > cur) | (blk >= n_sel)
    imp_t = jnp.where(future, -jnp.inf, jnp.where(forced, jnp.inf, imp_t))
    sel_t = _topk_mask(imp_t, min(SEL_TOPN, n_sel), axis=0)
    if n_sel_rows < LANE:
        sel_t = jnp.concatenate([sel_t, jnp.zeros((LANE - n_sel_rows, NSA_KV_HEADS * tq), F32)], axis=0)
    sels = [sel_t[:, kv * tq:(kv + 1) * tq].T.astype(BF16) for kv in range(NSA_KV_HEADS)]

    near0 = pl.multiple_of(jnp.maximum(i - 1, 0) * tq, tq)
    near_bias = jnp.concatenate([tt_ref[jnp.where(i == 0, 0, 1)], tt_ref[0]], axis=1)
    pos_q = i * tq + lax.broadcasted_iota(I32, (tq, 1), 0)
    d_near = pos_q - (near0 + lax.broadcasted_iota(I32, (tq, 2 * tq), 1))

    def heads(pens):
        return jnp.concatenate([p_ for p_ in pens for _ in range(NSA_GROUP)], axis=0)

    def block_sel(first_key, n_keys, key_limit):
        sb = lax.broadcasted_iota(I32, (LANE, n_keys), 0)
        kt = lax.broadcasted_iota(I32, (LANE, n_keys), 1)
        hit = (sb == first_key // SEL_BLOCK + kt // SEL_BLOCK) & (first_key + kt < key_limit)
        expand = jnp.where(hit, 1.0, 0.0).astype(BF16)
        return [_dot(sel, expand) > 0.5 for sel in sels]

    n_far_w = WINDOW - tq
    far0 = pl.multiple_of(jnp.maximum(i - WINDOW // tq, 0) * tq, tq)
    kv_n = swa_ref[pl.ds(near0, 2 * tq), :]
    kv_f = swa_ref[pl.ds(far0, n_far_w), :]
    kpos_f = far0 + lax.broadcasted_iota(I32, (tq, n_far_w), 1)
    pen_n = jnp.where((d_near >= 0) & (d_near < WINDOW), 0.0, NEG)
    pen_f = jnp.where((kpos_f < near0) & (pos_q - kpos_f < WINDOW), 0.0, NEG)
    s_n = _dot_nt(q, kv_n[:, :LANE]) + near_bias + heads([pen_n] * NSA_KV_HEADS)
    s_f = _dot_nt(q, kv_f[:, :LANE]) + far_col + heads([pen_f] * NSA_KV_HEADS)
    m_w = jnp.maximum(_rowmax(s_n), _rowmax(s_f))
    p_n = jnp.exp2(s_n - m_w)
    p_f = jnp.exp2(s_f - m_w)
    l_w = _rowsum(p_n) + _rowsum(p_f)
    o_swa = (_dot(p_n.astype(BF16), kv_n[:, LANE:]) + _dot(p_f.astype(BF16), kv_f[:, LANE:])) / l_w

    kv_n = slc_ref[pl.ds(near0, 2 * tq), :]
    pens = [jnp.where(hit & (d_near >= 0), 0.0, NEG) for hit in block_sel(near0, 2 * tq, seq_len)]
    s_n = _dot_nt(q, kv_n[:, :LANE]) + near_bias + heads(pens)
    m_s = _rowmax(s_n)
    p_n = jnp.exp2(s_n - m_s)
    m_ref[...] = m_s
    l_ref[...] = _rowsum(p_n)
    acc_ref[...] = _dot(p_n.astype(BF16), kv_n[:, LANE:])

    n_far = (near0 + tkf - 1) // tkf
    half = NSA_GROUP * tq

    def far_logits(c):
        first = pl.multiple_of(jnp.minimum(c, jnp.maximum(n_far - 1, 0)) * tkf, tkf)
        k = slc_ref[pl.ds(first, tkf), :LANE]
        return tuple(_dot_nt(q[kvh * half:(kvh + 1) * half], k) for kvh in range(NSA_KV_HEADS))

    def slc_far(c, qk):
        qk_next = far_logits(c + 1)
        first = pl.multiple_of(c * tkf, tkf)
        v = slc_ref[pl.ds(first, tkf), LANE:]
        hits = block_sel(first, tkf, near0)
        for kvh in range(NSA_KV_HEADS):
            r = slice(kvh * half, (kvh + 1) * half)
            pen = jnp.where(hits[kvh], 0.0, NEG)
            s_ = qk[kvh] + far_col[r] + jnp.concatenate([pen] * NSA_GROUP, axis=0)
            m_prev = m_ref[r]
            m_new = jnp.maximum(m_prev, _rowmax(s_))
            p = jnp.exp2(s_ - m_new)
            alpha = jnp.exp2(m_prev - m_new)
            l_ref[r] = alpha * l_ref[r] + _rowsum(p)
            acc_ref[r] = alpha * acc_ref[r] + _dot(p.astype(BF16), v)
            m_ref[r] = m_new
        return qk_next

    lax.fori_loop(0, n_far, slc_far, far_logits(0))
    o_slc = acc_ref[...] / l_ref[...]

    gates = gate_ref[...]
    for h in range(NSA_HEADS):
        kv, g = divmod(h, NSA_GROUP)
        r = slice(h * tq, (h + 1) * tq)
        c = kv * LANE + g
        o = (gates[:, c:c + 1] * o_cmp[r] + gates[:, c + NSA_GROUP:c + NSA_GROUP + 1] * o_slc[r]
             + gates[:, c + 2 * NSA_GROUP:c + 2 * NSA_GROUP + 1] * o_swa[r])
        o_ref[:, h * LANE:(h + 1) * LANE] = o.astype(o_ref.dtype)


def _nsa_prompt(qn, kc, vc, bias_c, ovl, slc_b, swa_b, tt, gates, *, nb, seq, n_cmp, n_sel):
    tq = tt.shape[2]
    nq = seq // tq
    rows = NSA_HEADS * tq
    hw = NSA_HEADS * LANE
    nhp = kc.shape[1]
    return pl.pallas_call(
        functools.partial(_nsa_prompt_kernel, tq=tq, tkf=min(512, seq), n_cmp=n_cmp, n_sel=n_sel, seq_len=seq),
        grid=(nb, nq),
        in_specs=[
            pl.BlockSpec((tq, hw), lambda b, i: (b * nq + i, 0)),
            pl.BlockSpec((None, nhp, LANE), lambda b, i: (b, 0, 0)),
            pl.BlockSpec((None, nhp, LANE), lambda b, i: (b, 0, 0)),
            pl.BlockSpec(bias_c.shape, lambda b, i: (0, 0)),
            pl.BlockSpec(ovl.shape, lambda b, i: (0, 0)),
            pl.BlockSpec((None, seq, KV_COLS), lambda b, i: (b, 0, 0)),
            pl.BlockSpec((None, seq, KV_COLS), lambda b, i: (b, 0, 0)),
            pl.BlockSpec(tt.shape, lambda b, i: (0, 0, 0)),
            pl.BlockSpec((tq, NSA_KV_HEADS * LANE), lambda b, i: (b * nq + i, 0)),
        ],
        out_specs=pl.BlockSpec((tq, hw), lambda b, i: (b * nq + i, 0)),
        out_shape=jax.ShapeDtypeStruct((nb * seq, hw), BF16),
        scratch_shapes=[pltpu.VMEM((rows, 1), F32), pltpu.VMEM((rows, 1), F32), pltpu.VMEM((rows, LANE), F32)],
        compiler_params=_cparams(("arbitrary", "arbitrary")), name="nsa_prompt",
    )(qn, kc, vc, bias_c, ovl, slc_b, swa_b, tt, gates)


def _nsa_sample_kernel(pt_ref, q_ref, kc_ref, vc_ref, bc_ref, ovl_ref, bs_ref, sn_ref, bsn_ref, st_ref, wn_ref,
                       bw_ref, bwn_ref, gate_ref, *rest, g, nq, nbs, n_cmp, n_sel, buf):
    o_ref, sel_ref, ocmp_ref, oswa_ref, m_ref, l_ref, acc_ref = rest[nbs * g:]
    s_idx = pl.program_id(1)
    last = pl.num_programs(1) - 1
    rows = NSA_HEADS * nq
    qi = lax.broadcasted_iota(I32, (rows, 1), 0) % nq
    qs = [jnp.concatenate([q_ref[bl, :, h * LANE:(h + 1) * LANE] for h in range(NSA_HEADS)], axis=0)
          for bl in range(nbs)]

    def first_step(bl):
        q = qs[bl]
        kc_r, vc_r, st_r, wn_r = kc_ref.at[bl], vc_ref.at[bl], st_ref.at[bl], wn_ref.at[bl]
        _nsa_sample_first(q, qi, kc_r, vc_r, bc_ref, ovl_ref, st_r, wn_r, bw_ref, bwn_ref, sel_ref.at[bl], ocmp_ref.at[bl],
                          oswa_ref.at[bl], m_ref.at[bl], l_ref.at[bl], acc_ref.at[bl], nq=nq, n_cmp=n_cmp, n_sel=n_sel, buf=buf)

    @pl.when(s_idx == 0)
    def _():
        for bl in range(nbs):
            first_step(bl)

    nk = g * rest[0].shape[1]
    sb = lax.broadcasted_iota(I32, (sel_ref.shape[2], nk), 0)
    kt = lax.broadcasted_iota(I32, (sel_ref.shape[2], nk), 1)
    expand = jnp.where(sb == s_idx * (nk // SEL_BLOCK) + kt // SEL_BLOCK, 1.0, 0.0).astype(BF16)
    for bl in range(nbs):
        pages = rest[bl * g:(bl + 1) * g]
        kcat = jnp.concatenate([pages[u][:LANE].astype(BF16) for u in range(g)], axis=1)
        vcat = jnp.concatenate([pages[u][LANE:].astype(BF16) for u in range(g)], axis=1)
        s = _dot(qs[bl], kcat) + bs_ref[jnp.where(s_idx == last, 1, 0)]
        mask = _dot(sel_ref[bl], expand) > 0.5
        _flash_step(s, mask, vcat, m_ref.at[bl], l_ref.at[bl], acc_ref.at[bl], v_keys_on_lanes=True)

    @pl.when(s_idx == last)
    def _():
        for bl in range(nbs):
            q = qs[bl]
            m_r, l_r, acc_r = m_ref.at[bl], l_ref.at[bl], acc_ref.at[bl]
            sn = sn_ref[bl]
            s_n = _dot_nt(q, sn[:, :LANE]) + bsn_ref[...]
            mask_n = (lax.broadcasted_iota(I32, (rows, LANE), 1) <= qi) & (sel_ref[bl, :, n_sel - 1:n_sel] > 0.5)
            _flash_step(s_n, mask_n, sn[:, LANE:], m_r, l_r, acc_r)
            o_slc = _flash_out(l_r, acc_r)
            o_cmp, o_swa = ocmp_ref[bl], oswa_ref[bl]
            gates = gate_ref[bl]
            for h in range(NSA_HEADS):
                kv, gg = divmod(h, NSA_GROUP)
                r = slice(h * nq, (h + 1) * nq)
                c = kv * LANE + gg
                o = (gates[:, c:c + 1] * o_cmp[r] + gates[:, c + NSA_GROUP:c + NSA_GROUP + 1] * o_slc[r]
                     + gates[:, c + 2 * NSA_GROUP:c + 2 * NSA_GROUP + 1] * o_swa[r])
                o_ref[bl, :, h * LANE:(h + 1) * LANE] = o.astype(o_ref.dtype)


def _nsa_sample_first(q, qi, kc_ref, vc_ref, bc_ref, ovl_ref, st_ref, wn_ref, bw_ref, bwn_ref, sel_ref, ocmp_ref, oswa_ref,
                      m_ref, l_ref, acc_ref, *, nq, n_cmp, n_sel, buf):
    rows = NSA_HEADS * nq

    def compressed_and_select():
        nhp = kc_ref.shape[0]
        s = _dot_nt(q, kc_ref[...]) + bc_ref[...]
        mask_c = lax.broadcasted_iota(I32, (rows, nhp), 1) < n_cmp
        p_cmp = _softmax_masked(s, mask_c)
        ocmp_ref[...] = _dot(p_cmp.astype(BF16), vc_ref[...])
        psums = []
        for kv in range(NSA_KV_HEADS):
            ps = p_cmp[kv * NSA_GROUP * nq:kv * NSA_GROUP * nq + nq]
            for gg in range(1, NSA_GROUP):
                lo = (kv * NSA_GROUP + gg) * nq
                ps = ps + p_cmp[lo:lo + nq]
            psums.append(ps)
        n_kq = NSA_KV_HEADS * nq
        psum = jnp.concatenate(psums + [jnp.zeros((LANE - n_kq, nhp), F32)], axis=0)
        n_blk = ovl_ref.shape[0]
        n_sel_rows = -(-n_sel // 8) * 8
        imp_t = _dot_nt(ovl_ref[...], _split3(psum))[:n_sel_rows]
        blk = lax.broadcasted_iota(I32, imp_t.shape, 0)
        cur = n_sel - 1
        forced = (blk == 0) | (blk == cur) | (blk == cur - 1)
        imp_t = jnp.where(blk >= n_sel, -jnp.inf, jnp.where(forced, jnp.inf, imp_t))
        sel_t = _topk_mask(imp_t, min(SEL_TOPN, n_sel), axis=0)
        if n_sel_rows < n_blk:
            sel_t = jnp.concatenate([sel_t, jnp.zeros((n_blk - n_sel_rows, LANE), F32)], axis=0)
        sel = sel_t.T.astype(BF16)
        sel_ref[...] = jnp.concatenate(
            [sel[kv * nq:(kv + 1) * nq] for kv in range(NSA_KV_HEADS) for _ in range(NSA_GROUP)], axis=0)

    def window():
        st = st_ref[...]
        s_w = _dot(q, st[:LANE].astype(BF16)) + bw_ref[...]
        d_w = buf + qi - lax.broadcasted_iota(I32, (rows, buf), 1)
        mask_w = (d_w >= 0) & (d_w < WINDOW)
        wn = wn_ref[...]
        s_n = _dot_nt(q, wn[:, :LANE]) + bwn_ref[...]
        mask_n = lax.broadcasted_iota(I32, (rows, LANE), 1) <= qi
        s_w = jnp.where(mask_w, s_w, NEG)
        s_n = jnp.where(mask_n, s_n, NEG)
        m = jnp.maximum(_rowmax(s_w), _rowmax(s_n))
        p_w = jnp.where(mask_w, jnp.exp2(s_w - m), 0.0)
        p_n = jnp.where(mask_n, jnp.exp2(s_n - m), 0.0)
        l = _rowsum(p_w) + _rowsum(p_n)
        o = _dot_nt(p_w.astype(BF16), st[LANE:].astype(BF16)) + _dot(p_n.astype(BF16), wn[:, LANE:])
        oswa_ref[...] = o / jnp.where(l > 0.0, l, 1.0)

    compressed_and_select()
    window()
    _flash_init(m_ref, l_ref, acc_ref)


def _nsa_sample(page_table, qn, kc, vc, bias_c, ovl, bias_s, slc_new, bias_sn, state_swa, swa_new, bias_w, bias_wn,
                gates, cache_slc, *, g, nbs, n_cmp, n_sel):
    nb, nq = qn.shape[0], qn.shape[1]
    n_pages = page_table.shape[1]
    rows = NSA_HEADS * nq
    buf = state_swa.shape[2]
    hw = NSA_HEADS * LANE
    bmap = lambda b, s, pt: (b, 0, 0)
    c2 = lambda b, s, pt: (0, 0)
    per_elem = lambda a: pl.BlockSpec((nbs,) + a.shape[1:], bmap)
    gs = pltpu.PrefetchScalarGridSpec(
        num_scalar_prefetch=1, grid=(nb // nbs, n_pages // g),
        in_specs=[
            per_elem(qn), per_elem(kc), per_elem(vc),
            pl.BlockSpec(bias_c.shape, c2),
            pl.BlockSpec(ovl.shape, c2),
            pl.BlockSpec(bias_s.shape, lambda b, s, pt: (0, 0, 0)),
            per_elem(slc_new),
            pl.BlockSpec(bias_sn.shape, c2),
            per_elem(state_swa), per_elem(swa_new),
            pl.BlockSpec(bias_w.shape, c2),
            pl.BlockSpec(bias_wn.shape, c2),
            per_elem(gates),
        ] + _page_specs(g, (None,) + cache_slc.shape[1:], nbs=nbs),
        out_specs=pl.BlockSpec((nbs, nq, hw), bmap),
        scratch_shapes=[pltpu.VMEM((nbs, rows, ovl.shape[0]), BF16), pltpu.VMEM((nbs, rows, LANE), F32),
                        pltpu.VMEM((nbs, rows, LANE), F32), pltpu.VMEM((nbs, rows, 1), F32), pltpu.VMEM((nbs, rows, 1), F32),
                        pltpu.VMEM((nbs, rows, LANE), F32)],
    )
    return pl.pallas_call(
        functools.partial(_nsa_sample_kernel, g=g, nq=nq, nbs=nbs, n_cmp=n_cmp, n_sel=n_sel, buf=buf), grid_spec=gs,
        out_shape=jax.ShapeDtypeStruct((nb, nq, hw), BF16),
        compiler_params=_cparams(("arbitrary", "arbitrary")), name="nsa_sample",
    )(page_table, qn, kc, vc, bias_c, ovl, bias_s, slc_new, bias_sn, state_swa, swa_new, bias_w, bias_wn, gates,
      *([cache_slc] * (nbs * g)))


def _outproj_kernel(x_ref, om_ref, on_ref, g1_ref, sc_ref, sh_ref, wom_ref, won_ref, gn2_ref, wr_ref, br_ref,
                    x1_ref, h2_ref, ti_ref, tw_ref):
    mix = _dot(om_ref[...], wom_ref[...]) + _dot(on_ref[...], won_ref[...])
    x1 = x_ref[...] + g1_ref[...] * mix
    x1_ref[...] = x1
    h2 = _rms(x1, gn2_ref[...]) * (1.0 + sc_ref[...]) + sh_ref[...]
    h2_ref[...] = h2.astype(h2_ref.dtype)
    h3, w3 = _split3(h2), _split3(wr_ref[...])
    d = h2.shape[1]
    h_cat = jnp.concatenate([h3[:, :d], h3[:, :d], h3[:, d:2 * d]], axis=1)
    w_cat = jnp.concatenate([w3[:, :d], w3[:, d:2 * d], w3[:, :d]], axis=1)
    logits = _dot_nt(h_cat, w_cat) + br_ref[...]
    lane = lax.broadcasted_iota(I32, logits.shape, 1)
    v = logits
    vals, idxs = [], []
    for _ in range(TOP_K):
        m = jnp.max(v, axis=-1, keepdims=True)
        idx = jnp.min(jnp.where(v == m, lane, 1 << 20), axis=-1, keepdims=True)
        vals.append(m)
        idxs.append(idx)
        v = jnp.where(lane == idx, -jnp.inf, v)
    es = [jnp.exp(m - vals[0]) for m in vals]
    tot = es[0]
    for e in es[1:]:
        tot = tot + e
    ti = jnp.zeros(logits.shape, I32)
    tw = jnp.zeros(logits.shape, F32)
    for k in range(TOP_K):
        ti = jnp.where(lane == k, idxs[k], ti)
        tw = jnp.where(lane == k, es[k] / tot, tw)
    ti_ref[...] = ti
    tw_ref[...] = tw


def _outproj(x, o_mla, o_nsa, g1, sc, sh, W, *, rows_per_mod, tm):
    t, d = x.shape
    nt = t // tm
    if rows_per_mod == 1:
        tiles_per_mod = (t // g1.shape[0]) // tm
        mod_map = lambda i: (i // tiles_per_mod, 0, 0)
    else:
        mod_map = lambda i: (i, 0, 0)
    row = lambda i: (i, 0)
    c2 = lambda i: (0, 0)
    mod = pl.BlockSpec((None, rows_per_mod, d), mod_map)
    return pl.pallas_call(
        _outproj_kernel, grid=(nt,),
        in_specs=[pl.BlockSpec((tm, d), row), pl.BlockSpec((tm, o_mla.shape[1]), row), pl.BlockSpec((tm, o_nsa.shape[1]), row),
                  mod, mod, mod, pl.BlockSpec(W["w_o_mla"].shape, c2), pl.BlockSpec(W["w_o_nsa"].shape, c2),
                  pl.BlockSpec((1, d), c2), pl.BlockSpec(W["w_router"].shape, c2), pl.BlockSpec((1, LANE), c2)],
        out_specs=[pl.BlockSpec((tm, d), row), pl.BlockSpec((tm, d), row), pl.BlockSpec((tm, LANE), row),
                   pl.BlockSpec((tm, LANE), row)],
        out_shape=[jax.ShapeDtypeStruct((t, d), F32), jax.ShapeDtypeStruct((t, d), F32),
                   jax.ShapeDtypeStruct((t, LANE), I32), jax.ShapeDtypeStruct((t, LANE), F32)],
        compiler_params=_cparams(("arbitrary",)), name="out_proj",
    )(x, o_mla, o_nsa, g1, sc, sh, W["w_o_mla"], W["w_o_nsa"], W["g_norm2"], W["w_router"], W["b_router"])


def _expert_kernel(te_ref, tv_ref, x_ref, wgu_ref, bgu_ref, wd_ref, bd_ref, o_ref, wgu_bf, wd_bf):
    t = pl.program_id(0)

    @pl.when((t == 0) | (te_ref[t] != te_ref[jnp.maximum(t - 1, 0)]))
    def _():
        wgu_bf[...] = wgu_ref[...].astype(BF16)
        wd_bf[...] = wd_ref[...].astype(BF16)

    @pl.when(tv_ref[t] > 0)
    def _():
        d_ff = wd_ref.shape[0]
        gu = _dot(x_ref[...].astype(BF16), wgu_bf[...]) + bgu_ref[...]
        glu = jnp.minimum(gu[:, :d_ff], SWIGLU_LIMIT)
        lin = jnp.clip(gu[:, d_ff:], -SWIGLU_LIMIT, SWIGLU_LIMIT)
        act = glu * _sigmoid(SWIGLU_ALPHA * glu) * (lin + 1.0)
        o_ref[...] = _dot(act.astype(BF16), wd_bf[...]) + bd_ref[...]

    @pl.when(tv_ref[t] == 0)
    def _():
        o_ref[...] = jnp.zeros_like(o_ref)


def _experts(tile_expert, tile_valid, x_sorted, W, *, tm):
    ns, d = x_sorted.shape
    gs = pltpu.PrefetchScalarGridSpec(
        num_scalar_prefetch=2, grid=(ns // tm,),
        in_specs=[pl.BlockSpec((tm, d), lambda t, te, tv: (t, 0)),
                  pl.BlockSpec((None,) + W["w_gate_up"].shape[1:], lambda t, te, tv: (te[t], 0, 0)),
                  pl.BlockSpec((None,) + W["b_gate_up"].shape[1:], lambda t, te, tv: (te[t], 0, 0)),
                  pl.BlockSpec((None,) + W["w_down"].shape[1:], lambda t, te, tv: (te[t], 0, 0)),
                  pl.BlockSpec((None,) + W["b_down"].shape[1:], lambda t, te, tv: (te[t], 0, 0))],
        out_specs=pl.BlockSpec((tm, d), lambda t, te, tv: (t, 0)),
        scratch_shapes=[pltpu.VMEM(W["w_gate_up"].shape[1:], BF16), pltpu.VMEM(W["w_down"].shape[1:], BF16)],
    )
    return pl.pallas_call(
        _expert_kernel, grid_spec=gs, out_shape=jax.ShapeDtypeStruct((ns, d), F32),
        compiler_params=_cparams(("arbitrary",)), name="experts",
    )(tile_expert, tile_valid, x_sorted, W["w_gate_up"], W["b_gate_up"], W["w_down"], W["b_down"])


def _final_kernel(x1_ref, tw_ref, *rest):
    parts, (g2_ref, gf_ref, o_ref) = rest[:TOP_K], rest[TOP_K:]
    tw = tw_ref[...]
    moe = tw[:, 0:1] * parts[0][...]
    for k in range(1, TOP_K):
        moe = moe + tw[:, k:k + 1] * parts[k][...]
    o_ref[...] = _rms(x1_ref[...] + g2_ref[...] * moe, gf_ref[...])


def _final(x1, top_w, moe_parts, g2, g_final, *, rows_per_mod, tm):
    t, d = x1.shape
    if rows_per_mod == 1:
        tiles_per_mod = (t // g2.shape[0]) // tm
        mod_map = lambda i: (i // tiles_per_mod, 0, 0)
    else:
        mod_map = lambda i: (i, 0, 0)
    row = lambda i: (i, 0)
    return pl.pallas_call(
        _final_kernel, grid=(t // tm,),
        in_specs=[pl.BlockSpec((tm, d), row), pl.BlockSpec((tm, LANE), row)]
        + [pl.BlockSpec((tm, d), functools.partial(lambda i, k: (k * (t // tm) + i, 0), k=k)) for k in range(TOP_K)]
        + [pl.BlockSpec((None, rows_per_mod, d), mod_map), pl.BlockSpec((1, d), lambda i: (0, 0))],
        out_specs=pl.BlockSpec((tm, d), row), out_shape=jax.ShapeDtypeStruct((t, d), F32),
        compiler_params=_cparams(("arbitrary",)), name="final_norm")(x1, top_w, *([moe_parts] * TOP_K), g2, g_final)


def _t5_bucket(dist):
    n = jnp.maximum(dist, 0)
    max_exact = NUM_BUCKETS // 2
    nf = jnp.maximum(n, 1).astype(F32)
    large = max_exact + (jnp.log(nf / max_exact) / math.log(MAX_DISTANCE / max_exact)
                         * (NUM_BUCKETS - max_exact)).astype(I32)
    return jnp.where(n < max_exact, n, jnp.minimum(large, NUM_BUCKETS - 1))


def _bias_rows(rel_bias, dist):
    bucket = _t5_bucket(dist)
    out = jnp.zeros((rel_bias.shape[1],) + dist.shape, F32)
    for b in range(NUM_BUCKETS):
        out = out + jnp.where(bucket == b, 1.0, 0.0)[None] * rel_bias[b][:, None, None]
    return out * LOG2E


def _rope_table(pos):
    half = MLA_ROPE // 2
    inv_freq = 1.0 / (ROPE_THETA ** (jnp.arange(half, dtype=F32) / half))
    ang = pos.astype(F32)[:, None] * inv_freq[None, :]
    pad = jnp.zeros((pos.shape[0], LANE - MLA_ROPE), F32)
    cos, sin = jnp.cos(ang), jnp.sin(ang)
    return jnp.concatenate([cos, cos, pad, sin, sin, pad], axis=1)


def _rot_cols(w):
    half = MLA_ROPE // 2
    return jnp.concatenate([-w[..., half:], w[..., :half]], axis=-1)


def _pad_last(w, n):
    return jnp.pad(w, [(0, 0)] * (w.ndim - 1) + [(0, n - w.shape[-1])])


def _pack_weights(w_in, g_norm1, g_norm2, g_q_a, w_q_b, g_kv_a, w_kv_b, cmp_pe, cmp_w1, cmp_w2, w_o, w_router,
                  b_router, w_gate_up, b_gate_up, w_down, b_down):
    d = w_in.shape[0]
    sizes = (MLA_Q_LORA, MLA_KV_LORA, MLA_ROPE, NSA_HEADS * HEAD_DIM, KV_COLS, KV_COLS, KV_COLS, 3 * NSA_HEADS)
    offs = [0]
    for s in sizes:
        offs.append(offs[-1] + s)
    w_qa, w_ckv, w_kr, w_qn, w_cmp, w_slc, w_swa, w_g = [w_in[:, offs[k]:offs[k + 1]] for k in range(8)]
    wq = w_qn.reshape(d, NSA_KV_HEADS, NSA_GROUP, HEAD_DIM)
    wq_pad = jnp.concatenate(
        [jnp.pad(wq[:, kv], ((0, 0), (0, 0), (kv * HEAD_DIM, LANE - (kv + 1) * HEAD_DIM))).reshape(d, NSA_GROUP * LANE)
         for kv in range(NSA_KV_HEADS)], axis=1)
    wg = jnp.transpose(w_g.reshape(d, NSA_KV_HEADS, NSA_GROUP, 3), (0, 1, 3, 2)).reshape(d, NSA_KV_HEADS, 3 * NSA_GROUP)
    wg_pad = _pad_last(wg, LANE).reshape(d, NSA_KV_HEADS * LANE)
    w_in_p = jnp.concatenate([w_qa, w_ckv, wq_pad, w_cmp, w_slc, w_swa, _pad_last(w_kr, LANE),
                              _pad_last(_rot_cols(w_kr), LANE), wg_pad], axis=1).astype(BF16)
    assert w_in_p.shape[1] == _C_END
    nope = _pad_last(w_q_b[:, :, :MLA_NOPE], LANE).reshape(MLA_Q_LORA, MLA_HEADS * LANE)
    rp = w_q_b[:, :, MLA_NOPE:]
    w_qb = jnp.concatenate([nope, _pad_last(rp, LANE).reshape(MLA_Q_LORA, -1),
                            _pad_last(_rot_cols(rp), LANE).reshape(MLA_Q_LORA, -1)], axis=1).astype(BF16)
    w_kn = jnp.transpose(w_kv_b[:, :, :MLA_NOPE], (1, 2, 0))
    w_kn = jnp.pad(w_kn, ((0, 0), (0, LANE - MLA_NOPE), (0, 0))).astype(BF16)
    wv = jnp.transpose(w_kv_b[:, :, MLA_NOPE:], (1, 0, 2))
    wv_pad = jnp.stack([jnp.pad(wv[h], ((0, 0), (h * MLA_V, (MLA_HEADS - 1 - h) * MLA_V))) for h in range(MLA_HEADS)]).astype(BF16)
    n_mla = MLA_HEADS * MLA_V
    won = w_o[n_mla:].reshape(NSA_KV_HEADS, NSA_GROUP, HEAD_DIM, d)
    won_pad = jnp.concatenate(
        [jnp.pad(won[kv], ((0, 0), (kv * HEAD_DIM, LANE - (kv + 1) * HEAD_DIM), (0, 0))).reshape(NSA_GROUP * LANE, d)
         for kv in range(NSA_KV_HEADS)], axis=0).astype(BF16)
    base = jnp.concatenate([cmp_w1[:, :CMP_STRIDE], cmp_w1[:, CMP_STRIDE:]], axis=-1)
    z = jnp.zeros_like(base)
    blk = jnp.concatenate([jnp.concatenate([base, z], axis=-1), jnp.concatenate([z, base], axis=-1)], axis=2)
    w_pair = blk.reshape(2, CMP_STRIDE // 2, 2 * NSA_KV_HEADS * HEAD_DIM, NSA_KV_HEADS * 2 * CMP_HIDDEN).astype(BF16)
    pe8 = jnp.broadcast_to(cmp_pe.reshape(2, 1, CMP_LEN * HEAD_DIM), (2, 8, CMP_LEN * HEAD_DIM))
    w1r = cmp_w1.reshape(2, CMP_LEN * HEAD_DIM, CMP_HIDDEN).astype(BF16)
    w2pad = jnp.stack([jnp.stack([jnp.pad(cmp_w2[j], ((0, 0), (kv * HEAD_DIM, LANE - (kv + 1) * HEAD_DIM)))
                                  for kv in range(NSA_KV_HEADS)]) for j in range(2)]).astype(BF16)
    return dict(
        w_in=w_in_p, g_norm1=g_norm1[None], g_norm2=g_norm2[None], g_q_a=g_q_a[None], g_kv_a=g_kv_a[None],
        w_qb=w_qb, w_kn=w_kn, wv=wv_pad, wv_cat=w_kv_b[:, :, MLA_NOPE:].reshape(MLA_KV_LORA, n_mla).astype(BF16),
        w_o_mla=w_o[:n_mla].astype(BF16), w_o_nsa=won_pad,
        w_pair=w_pair, pe8=pe8, w1r=w1r, w2pad=w2pad,
        w_router=_pad_last(w_router, LANE).T, b_router=jnp.pad(b_router, (0, LANE - N_EXPERTS), constant_values=NEG)[None],
        w_gate_up=w_gate_up, b_gate_up=b_gate_up[:, None, :], w_down=w_down, b_down=b_down[:, None, :])


def _overlap(n_half_pad, n_sel, n_sel_pad):
    c_start = jnp.arange(n_half_pad) * CMP_STRIDE
    s_start = jnp.arange(n_sel_pad) * SEL_BLOCK
    ov = (c_start[:, None] < s_start[None, :] + SEL_BLOCK) & (c_start[:, None] + CMP_LEN > s_start[None, :])
    return (ov & (jnp.arange(n_sel_pad) < n_sel)[None, :]).astype(F32)


def _moe_dispatch(top_i, tm):
    t = top_i.shape[0]
    a = t * TOP_K
    n_tiles = -(-a // tm) + N_EXPERTS
    ns = n_tiles * tm
    e_flat = top_i[:, :TOP_K].reshape(a)
    onehot = (e_flat[:, None] == jnp.arange(N_EXPERTS, dtype=I32)[None, :]).astype(I32)
    csum = jnp.cumsum(onehot, axis=0)
    counts = csum[-1]
    padded = ((counts + tm - 1) // tm) * tm
    pend = jnp.cumsum(padded)
    pstart = pend - padded
    start = jnp.cumsum(counts) - counts
    slot_of_assign = jnp.sum(onehot * (pstart[None, :] + csum - 1), axis=1).reshape(t, TOP_K)
    _, order = lax.sort((e_flat, jnp.arange(a, dtype=I32)), num_keys=1, is_stable=True)
    tile_start = jnp.arange(n_tiles, dtype=I32) * tm
    tile_expert = jnp.minimum(jnp.sum((pend[None, :] <= tile_start[:, None]).astype(I32), axis=1), N_EXPERTS - 1)
    tile_valid = (tile_start < pend[-1]).astype(I32)
    e_hot = (tile_expert[:, None] == jnp.arange(N_EXPERTS, dtype=I32)[None, :]).astype(I32)
    rank0 = tile_start - jnp.sum(e_hot * pstart[None, :], axis=1)
    lane = jnp.arange(tm, dtype=I32)[None, :]
    base = jnp.sum(e_hot * start[None, :], axis=1) + rank0
    run = order[jnp.clip(base[:, None] + lane, 0, a - 1)]
    rank = rank0[:, None] + lane
    valid = (rank < jnp.sum(e_hot * counts[None, :], axis=1, keepdims=True)) & (tile_valid[:, None] > 0)
    tok_of_slot = jnp.where(valid, run // TOP_K, 0).reshape(ns)
    return tok_of_slot, slot_of_assign, tile_expert, tile_valid


def kernel(x_prompt, x_sample, c_prompt, c_sample, cache_mla, cache_nsa_cmp, cache_nsa_slc, state_nsa_swa, page_table, rel_bias, w_ada, b_ada, g_norm1, g_norm2, w_in, g_q_a, w_q_b, g_kv_a, w_kv_b, cmp_pe, cmp_w1, cmp_w2, w_o, w_router, b_router, w_gate_up, b_gate_up, w_down, b_down, g_final):
    depth = w_in.shape[0]
    assert depth == 1, "single-layer decoder step"
    nb, seq, d = x_prompt.shape
    nbd, nq, _ = x_sample.shape
    n_pages = page_table.shape[1]
    page = cache_mla.shape[2]
    past = n_pages * page
    buf = state_nsa_swa.shape[2]
    assert page == LANE and nq <= 8 and seq % 512 == 0 and past % SEL_BLOCK == 0 and seq >= WINDOW
    tp, ts = nb * seq, nbd * nq
    W = _pack_weights(w_in[0], g_norm1[0], g_norm2[0], g_q_a[0], w_q_b[0], g_kv_a[0], w_kv_b[0], cmp_pe[0], cmp_w1[0],
                      cmp_w2[0], w_o[0], w_router[0], b_router[0], w_gate_up[0], b_gate_up[0], w_down[0], b_down[0])

    n_c = nb + nbd
    n_c_pad = -(-n_c // 8) * 8
    c_all = jnp.pad(jnp.concatenate([c_prompt, c_sample], axis=0), ((0, n_c_pad - n_c), (0, 0)))
    mod = _ada_mod(c_all, w_ada[0].astype(BF16), b_ada[0][None])
    mod_p = [m[:, None, :] for m in jnp.split(mod[:nb], 6, axis=-1)]
    tm_s = min(TOKEN_TILE, ts)
    mod_s = [jnp.repeat(m, nq, axis=0).reshape(ts // tm_s, tm_s, d) for m in jnp.split(mod[nb:n_c], 6, axis=-1)]

    tm_p = TOKEN_TILE
    pos_p = jnp.arange(seq)
    pos_s = past + jnp.arange(nq)
    P = _proj(x_prompt.reshape(tp, d), mod_p[1], mod_p[0], _rope_table(pos_p), W, rows_per_mod=1,
              cs_period_tiles=seq // tm_p, tm=tm_p, states_t_batches=nb)
    cs_s = jnp.tile(_rope_table(pos_s), (tm_s // nq, 1))
    S_ = _proj(x_sample.reshape(ts, d), mod_s[1], mod_s[0], cs_s, W, rows_per_mod=tm_s, cs_period_tiles=1, tm=tm_s)
    p_qmla, p_kq, p_mla_t, p_qn, p_cmp, p_cmp_t, p_slc_t, p_swa_t, p_slcb, p_swab, p_gate = P
    s_qmla, s_kq, s_mla, s_qn, _, s_cmp, s_slc, s_swa, s_slcb, s_swab, s_gate = S_

    def pad_new(a):
        return jnp.pad(a.reshape(nbd, nq, a.shape[1]), ((0, 0), (0, LANE - nq), (0, 0)))

    o_mla_p = _mla_prompt(p_qmla, p_kq, W["wv"], nb=nb, seq=seq)
    g_pages = min(PAGES_PER_STEP, n_pages)
    cache_mla_t = jnp.swapaxes(cache_mla[0], 1, 2)
    cache_cmp_t = jnp.moveaxis(cache_nsa_cmp[0], 1, -1).reshape(-1, KV_COLS, page)
    cache_slc_t = jnp.moveaxis(cache_nsa_slc[0], 1, -1).reshape(-1, KV_COLS, page)
    state_swa_t = jnp.moveaxis(state_nsa_swa[0], 1, -1).reshape(nbd, KV_COLS, buf)
    o_mla_s = _mla_sample(page_table, s_qmla.reshape(nbd, nq, -1), pad_new(s_kq), W["wv_cat"], cache_mla_t, g=g_pages,
                          nbs=2 if nbd % 2 == 0 else 1)

    assert (past + nq) // CMP_STRIDE == past // CMP_STRIDE
    n_pp = tp // LANE
    kc_p, vc_p = _compress(jnp.arange(n_pp, dtype=I32), p_cmp.reshape(n_pp, LANE, KV_COLS), W["w_pair"], W["pe8"],
                           W["w1r"], W["w2pad"], nb=nb, transposed=False)
    kc_s, vc_s = _compress(page_table.reshape(nbd * n_pages), cache_cmp_t, W["w_pair"], W["pe8"], W["w1r"], W["w2pad"],
                           nb=nbd, transposed=True)

    nh_p = seq // CMP_STRIDE
    n_sel_p = -(-seq // SEL_BLOCK)
    tq = QUERY_TILE
    far_h = _bias_rows(rel_bias, jnp.full((1, 1), MAX_DISTANCE))
    d_win = (jnp.arange(tq)[:, None] - CMP_STRIDE * (jnp.arange(LANE)[None, :] - CMP_WIN_LO) - (CMP_LEN - 1))
    assert CMP_WIN_LO * CMP_STRIDE >= MAX_DISTANCE + CMP_LEN and LANE - CMP_WIN_LO >= tq // CMP_STRIDE
    bias_c_p = jnp.where((d_win >= 0)[None], _bias_rows(rel_bias, d_win) - far_h, 0.0)
    bias_c_p = bias_c_p.reshape(NSA_HEADS * tq, LANE)
    di = jnp.arange(tq)[:, None] - jnp.arange(tq)[None, :]
    tt = jnp.stack([_bias_rows(rel_bias, di), _bias_rows(rel_bias, di + tq),
                    _bias_rows(rel_bias, jnp.full((tq, tq), MAX_DISTANCE))], axis=1)
    tt = tt.transpose(1, 0, 2, 3).reshape(3, NSA_HEADS * tq, tq)
    assert n_sel_p <= LANE
    ovl_p = jnp.tile(_overlap(nh_p, n_sel_p, LANE).T, (1, 3)).astype(BF16)
    o_nsa_p = _nsa_prompt(p_qn, kc_p, vc_p, bias_c_p, ovl_p, p_slcb.reshape(nb, seq, KV_COLS),
                          p_swab.reshape(nb, seq, KV_COLS), tt, p_gate, nb=nb, seq=seq, n_cmp=nh_p - 1, n_sel=n_sel_p)

    nh_s = past // CMP_STRIDE
    n_sel_s = -(-(past + nq) // SEL_BLOCK)
    assert n_sel_s == past // SEL_BLOCK + 1
    n_sel_pad = -(-n_sel_s // LANE) * LANE
    rows_s = NSA_HEADS * nq
    cmp_end_s = jnp.arange(nh_s) * CMP_STRIDE + CMP_LEN - 1
    bias_c_s = _bias_rows(rel_bias, pos_s[:, None] - cmp_end_s[None, :]).reshape(rows_s, nh_s)
    nk_step = g_pages * page
    far = jnp.broadcast_to(_bias_rows(rel_bias, jnp.full((nq, 1), MAX_DISTANCE)).reshape(rows_s, 1), (rows_s, nk_step))
    tail_pos = past - nk_step + jnp.arange(nk_step)
    bias_s = jnp.stack([far, _bias_rows(rel_bias, pos_s[:, None] - tail_pos[None, :]).reshape(rows_s, nk_step)])
    new_pos = past + jnp.arange(LANE)
    bias_new = _bias_rows(rel_bias, pos_s[:, None] - new_pos[None, :]).reshape(rows_s, LANE)
    swa_pos = past - buf + jnp.arange(buf)
    bias_w = _bias_rows(rel_bias, pos_s[:, None] - swa_pos[None, :]).reshape(rows_s, buf)
    ovl_s = jnp.tile(_overlap(nh_s, n_sel_s, n_sel_pad).T, (1, 3)).astype(BF16)
    o_nsa_s = _nsa_sample(page_table, s_qn.reshape(nbd, nq, -1), kc_s, vc_s, bias_c_s, ovl_s, bias_s, pad_new(s_slcb),
                          bias_new, state_swa_t, pad_new(s_swab), bias_w, bias_new,
                          s_gate.reshape(nbd, nq, -1), cache_slc_t, g=g_pages, nbs=2 if nbd % 2 == 0 else 1,
                          n_cmp=nh_s - 1, n_sel=n_sel_s)

    x1_p, h2_p, ti_p, tw_p = _outproj(x_prompt.reshape(tp, d), o_mla_p, o_nsa_p, mod_p[2], mod_p[4], mod_p[3], W,
                                      rows_per_mod=1, tm=2 * tm_p)
    x1_s, h2_s, ti_s, tw_s = _outproj(x_sample.reshape(ts, d), o_mla_s.reshape(ts, -1), o_nsa_s.reshape(ts, -1),
                                      mod_s[2], mod_s[4], mod_s[3], W, rows_per_mod=tm_s, tm=tm_s)

    tm_e = EXPERT_TILE
    h2 = jnp.concatenate([h2_p, h2_s], axis=0)
    tok_of_slot, slot_of_assign, tile_expert, tile_valid = _moe_dispatch(jnp.concatenate([ti_p, ti_s], axis=0), tm_e)
    y_sorted = _experts(tile_expert, tile_valid, h2[tok_of_slot], W, tm=tm_e)
    parts_p = y_sorted[slot_of_assign[:tp].T.reshape(-1)]
    parts_s = y_sorted[slot_of_assign[tp:].T.reshape(-1)]
    y_p = _final(x1_p, tw_p, parts_p, mod_p[5], g_final[None], rows_per_mod=1, tm=tm_p)
    y_s = _final(x1_s, tw_s, parts_s, mod_s[5], g_final[None], rows_per_mod=tm_s, tm=tm_s)

    kv_tail = (2, NSA_KV_HEADS, HEAD_DIM)
    keep_p = min(WINDOW, seq)
    swa_keys = jnp.concatenate([state_nsa_swa[0], s_swa.reshape((nbd, nq) + kv_tail)], axis=1)
    keep_s = min(WINDOW, buf + nq)

    def rows_last(a_t):
        return jnp.moveaxis(a_t.reshape((nb,) + kv_tail + (a_t.shape[-1],)), -1, 1)[None]

    return (y_p.reshape(nb, seq, d), y_s.reshape(nbd, nq, d),
            jnp.swapaxes(p_mla_t, 1, 2)[None], s_mla.reshape(1, nbd, nq, -1),
            rows_last(p_cmp_t), s_cmp.reshape((1, nbd, nq) + kv_tail),
            rows_last(p_slc_t), s_slc.reshape((1, nbd, nq) + kv_tail),
            rows_last(p_swa_t[:, :, seq - keep_p:]),
            swa_keys[None, :, buf + nq - keep_s:])
```

```python
import functools
import math

import jax
import jax.numpy as jnp
from jax import lax
from jax.experimental import pallas as pl
from jax.experimental.pallas import tpu as pltpu

F32, BF16, I32 = jnp.float32, jnp.bfloat16, jnp.int32

MLA_HEADS = 8
MLA_Q_LORA = 384
MLA_KV_LORA = 256
MLA_NOPE = 64
MLA_ROPE = 32
MLA_V = 64
NSA_HEADS = 8
NSA_KV_HEADS = 2
NSA_GROUP = NSA_HEADS // NSA_KV_HEADS
HEAD_DIM = 64
CMP_LEN = 32
CMP_STRIDE = 16
CMP_HIDDEN = 128
SEL_BLOCK = 64
SEL_TOPN = 16
WINDOW = 512
KV_COLS = 2 * NSA_KV_HEADS * HEAD_DIM
N_EXPERTS = 32
TOP_K = 4
SWIGLU_LIMIT = 7.0
SWIGLU_ALPHA = 1.702
NUM_BUCKETS = 32
MAX_DISTANCE = 128
ROPE_THETA = 10000.0
NORM_EPS = 1e-6

LANE = 128
VMEM_LIMIT = 56 * 1024 * 1024
NEG = -1e30
CMP_WIN_LO = 16
TOKEN_TILE = 256
QUERY_TILE = 128
EXPERT_TILE = 256
PAGES_PER_STEP = 64

MLA_QW = MLA_KV_LORA + LANE
MLA_SCALE = (MLA_NOPE + MLA_ROPE) ** -0.5
MLA_QSCALE = MLA_SCALE * math.log2(math.e)
LOG2E = math.log2(math.e)
NSA_QSCALE = HEAD_DIM ** -0.5 * LOG2E

_C_QA = 0
_C_CKV = _C_QA + MLA_Q_LORA
_C_QN = _C_CKV + MLA_KV_LORA
_C_CMP = _C_QN + NSA_HEADS * LANE
_C_SLC = _C_CMP + KV_COLS
_C_SWA = _C_SLC + KV_COLS
_C_KR = _C_SWA + KV_COLS
_C_KRR = _C_KR + LANE
_C_G = _C_KRR + LANE
_C_END = _C_G + NSA_KV_HEADS * LANE


def _cparams(sem, vmem=VMEM_LIMIT):
    return pltpu.CompilerParams(dimension_semantics=sem, vmem_limit_bytes=vmem)


def _dot(a, b):
    return jnp.dot(a, b, preferred_element_type=F32)


def _dot_nt(a, b):
    return lax.dot_general(a, b, (((1,), (1,)), ((), ())), preferred_element_type=F32)


def _row_reduce(x, op, reduce):
    n = x.shape[1] // LANE
    if x.shape[1] % LANE or n <= 1:
        return reduce(x, axis=-1, keepdims=True)
    t = x[:, :LANE]
    for c in range(1, n):
        t = op(t, x[:, c * LANE:(c + 1) * LANE])
    return reduce(t, axis=-1, keepdims=True)


def _rowmax(x):
    return _row_reduce(x, jnp.maximum, jnp.max)


def _rowsum(x):
    return _row_reduce(x, jnp.add, jnp.sum)


def _split3(x):
    hi = x.astype(BF16)
    r1 = x - hi.astype(F32)
    mid = r1.astype(BF16)
    lo = (r1 - mid.astype(F32)).astype(BF16)
    return jnp.concatenate([hi, mid, lo], axis=-1)


def _rms(x, g):
    return x * lax.rsqrt(jnp.mean(x * x, axis=-1, keepdims=True) + NORM_EPS) * g


def _sigmoid(x):
    return 1.0 / (1.0 + jnp.exp(-x))


def _ada_kernel(c_ref, w_ref, b_ref, o_ref):
    c = c_ref[...]
    o_ref[...] = _dot((c * _sigmoid(c)).astype(BF16), w_ref[...]) + b_ref[...]


def _ada_mod(c_all, w_ada, b_ada):
    m, d = c_all.shape
    n = w_ada.shape[1]
    tn = 1536
    return pl.pallas_call(
        _ada_kernel, grid=(n // tn,),
        in_specs=[pl.BlockSpec((m, d), lambda i: (0, 0)), pl.BlockSpec((d, tn), lambda i: (0, i)),
                  pl.BlockSpec((1, tn), lambda i: (0, i))],
        out_specs=pl.BlockSpec((m, tn), lambda i: (0, i)),
        out_shape=jax.ShapeDtypeStruct((m, n), F32),
        compiler_params=_cparams(("arbitrary",)), name="ada_mod")(c_all, w_ada, b_ada)


def _proj_kernel(x_ref, sc_ref, sh_ref, g1_ref, cs_ref, win_ref, gqa_ref, wqb_ref, gkva_ref, wkn_ref,
                 qmla_ref, kq_ref, mla_ref, qn_ref, cmp_ref, cmps_ref, slc_ref, swa_ref, slcb_ref, swab_ref, gate_ref,
                 *, states_t):
    h = _rms(x_ref[...], g1_ref[...]) * (1.0 + sc_ref[...]) + sh_ref[...]
    proj = _dot(h.astype(BF16), win_ref[...])
    cs = cs_ref[...]
    cosp, sinp = cs[:, :LANE], cs[:, LANE:]
    qa = _rms(proj[:, _C_QA:_C_CKV], gqa_ref[...])
    q = _dot(qa.astype(BF16), wqb_ref[...])
    hw = MLA_HEADS * LANE
    for h_i in range(MLA_HEADS):
        lo = h_i * LANE
        qlat = _dot(q[:, lo:lo + LANE].astype(BF16), wkn_ref[h_i]) * MLA_QSCALE
        qr = (q[:, hw + lo:hw + lo + LANE] * cosp + q[:, 2 * hw + lo:2 * hw + lo + LANE] * sinp) * MLA_QSCALE
        qmla_ref[:, h_i * MLA_QW:h_i * MLA_QW + MLA_KV_LORA] = qlat.astype(BF16)
        qmla_ref[:, h_i * MLA_QW + MLA_KV_LORA:(h_i + 1) * MLA_QW] = qr.astype(BF16)
    ckv = _rms(proj[:, _C_CKV:_C_QN], gkva_ref[...])
    kr = proj[:, _C_KR:_C_KRR] * cosp + proj[:, _C_KRR:_C_G] * sinp
    kq_ref[:, :MLA_KV_LORA] = ckv.astype(BF16)
    kq_ref[:, MLA_KV_LORA:] = kr.astype(BF16)
    qn_ref[...] = (proj[:, _C_QN:_C_CMP] * NSA_QSCALE).astype(BF16)
    cmp = proj[:, _C_CMP:_C_SLC]
    slc = proj[:, _C_SLC:_C_SWA]
    swa = proj[:, _C_SWA:_C_KR]
    cmp_ref[...] = cmp
    if states_t:
        mla_ref[:MLA_KV_LORA, :] = ckv.T
        mla_ref[MLA_KV_LORA:, :] = kr.T[:MLA_ROPE]
        cmps_ref[...] = cmp.T
        slc_ref[...] = slc.T
        swa_ref[...] = swa.T
    else:
        mla_ref[:, :MLA_KV_LORA] = ckv
        mla_ref[:, MLA_KV_LORA:] = kr[:, :MLA_ROPE]
        cmps_ref[...] = cmp
        slc_ref[...] = slc
        swa_ref[...] = swa
    slcb_ref[...] = slc.astype(BF16)
    swab_ref[...] = swa.astype(BF16)
    gate_ref[...] = _sigmoid(proj[:, _C_G:_C_END])


def _proj(x, sc, sh, cs, W, *, rows_per_mod, cs_period_tiles, tm, states_t_batches=0):
    t, d = x.shape
    nt = t // tm
    if rows_per_mod == 1:
        tiles_per_mod = sc.shape[0] and (t // sc.shape[0]) // tm
        mod_map = lambda i: (i // tiles_per_mod, 0, 0)
    else:
        mod_map = lambda i: (i, 0, 0)
    cs_map = (lambda i: (i % cs_period_tiles, 0)) if cs_period_tiles > 1 else (lambda i: (0, 0))
    const2 = lambda i: (0, 0)
    row = lambda i: (i, 0)
    outs = [
        (MLA_HEADS * MLA_QW, BF16), (MLA_QW, BF16), (MLA_KV_LORA + MLA_ROPE, F32), (NSA_HEADS * LANE, BF16),
        (KV_COLS, F32), (KV_COLS, F32), (KV_COLS, F32), (KV_COLS, F32), (KV_COLS, BF16), (KV_COLS, BF16),
        (NSA_KV_HEADS * LANE, F32),
    ]
    state_outs = (2, 5, 6, 7) if states_t_batches else ()
    tiles_per_batch = (t // states_t_batches) // tm if states_t_batches else 0
    out_specs = [pl.BlockSpec((None, w, tm), lambda i: (i // tiles_per_batch, 0, i % tiles_per_batch))
                 if k in state_outs else pl.BlockSpec((tm, w), row) for k, (w, _) in enumerate(outs)]
    out_shape = [jax.ShapeDtypeStruct((states_t_batches, w, t // states_t_batches) if k in state_outs else (t, w), dt)
                 for k, (w, dt) in enumerate(outs)]
    return pl.pallas_call(
        functools.partial(_proj_kernel, states_t=bool(states_t_batches)), grid=(nt,),
        in_specs=[
            pl.BlockSpec((tm, d), row),
            pl.BlockSpec((None, rows_per_mod, d), mod_map),
            pl.BlockSpec((None, rows_per_mod, d), mod_map),
            pl.BlockSpec((1, d), const2),
            pl.BlockSpec((tm, 2 * LANE), cs_map),
            pl.BlockSpec(W["w_in"].shape, const2),
            pl.BlockSpec((1, MLA_Q_LORA), const2),
            pl.BlockSpec(W["w_qb"].shape, const2),
            pl.BlockSpec((1, MLA_KV_LORA), const2),
            pl.BlockSpec(W["w_kn"].shape, lambda i: (0, 0, 0)),
        ],
        out_specs=out_specs, out_shape=out_shape,
        compiler_params=_cparams(("arbitrary",)), name="proj_in",
    )(x, sc, sh, W["g_norm1"], cs, W["w_in"], W["g_q_a"], W["w_qb"], W["g_kv_a"], W["w_kn"])


def _mla_prompt_kernel(q_ref, k_ref, wv_ref, o_ref, m_ref, l_ref, acc_ref, *, tq, tk):
    i, j = pl.program_id(1), pl.program_id(2)
    nk = pl.num_programs(2)

    @pl.when(j == 0)
    def _():
        m_ref[...] = jnp.full_like(m_ref, NEG)
        l_ref[...] = jnp.zeros_like(l_ref)
        acc_ref[...] = jnp.zeros_like(acc_ref)

    def step(masked):
        k = k_ref[...]
        v = k[:, :MLA_KV_LORA]
        if masked:
            qpos = i * tq + lax.broadcasted_iota(I32, (tq, tk), 0)
            kpos = j * tk + lax.broadcasted_iota(I32, (tq, tk), 1)
            mask = kpos <= qpos
        ss = [_dot_nt(q_ref[:, h * MLA_QW:(h + 1) * MLA_QW], k) for h in range(MLA_HEADS)]
        ps, alphas = [], []
        for h in range(MLA_HEADS):
            r = slice(h * tq, (h + 1) * tq)
            s = jnp.where(mask, ss[h], NEG) if masked else ss[h]
            m_prev = m_ref[r]
            m_new = jnp.maximum(m_prev, _rowmax(s))
            p = jnp.exp2(s - m_new)
            if masked:
                p = jnp.where(mask, p, 0.0)
            alpha = jnp.exp2(m_prev - m_new)
            l_ref[r] = alpha * l_ref[r] + _rowsum(p)
            m_ref[r] = m_new
            ps.append(p.astype(BF16))
            alphas.append(alpha)
        for h in range(MLA_HEADS):
            r = slice(h * tq, (h + 1) * tq)
            acc_ref[r] = alphas[h] * acc_ref[r] + _dot(ps[h], v)

    first_key, last_key = j * tk, j * tk + tk - 1
    pl.when(last_key <= i * tq)(functools.partial(step, False))
    pl.when((first_key <= i * tq + tq - 1) & (last_key > i * tq))(functools.partial(step, True))

    @pl.when(j == nk - 1)
    def _():
        out = jnp.zeros(o_ref.shape, F32)
        for h in range(MLA_HEADS):
            r = slice(h * tq, (h + 1) * tq)
            l = l_ref[r]
            o_lat = acc_ref[r] / jnp.where(l > 0.0, l, 1.0)
            out = out + _dot(o_lat.astype(BF16), wv_ref[h])
        o_ref[...] = out.astype(o_ref.dtype)


def _mla_prompt(qmla, kq, wv, *, nb, seq, tq=256, tk=1024):
    tk = min(tk, seq)
    if seq % tk:
        tk = 512
    nq, nk = seq // tq, seq // tk
    ow = MLA_HEADS * MLA_V

    def k_map(b, i, j):
        return (b * nk + jnp.minimum(j, (i * tq + tq - 1) // tk), 0)

    return pl.pallas_call(
        functools.partial(_mla_prompt_kernel, tq=tq, tk=tk), grid=(nb, nq, nk),
        in_specs=[pl.BlockSpec((tq, MLA_HEADS * MLA_QW), lambda b, i, j: (b * nq + i, 0)),
                  pl.BlockSpec((tk, MLA_QW), k_map),
                  pl.BlockSpec(wv.shape, lambda b, i, j: (0, 0, 0))],
        out_specs=pl.BlockSpec((tq, ow), lambda b, i, j: (b * nq + i, 0)),
        out_shape=jax.ShapeDtypeStruct((nb * seq, ow), BF16),
        scratch_shapes=[pltpu.VMEM((MLA_HEADS * tq, 1), F32), pltpu.VMEM((MLA_HEADS * tq, 1), F32),
                        pltpu.VMEM((MLA_HEADS * tq, MLA_KV_LORA), F32)],
        compiler_params=_cparams(("arbitrary", "arbitrary", "arbitrary")), name="mla_prompt",
    )(qmla, kq, wv)


def _mla_sample_kernel(pt_ref, q_ref, kn_ref, wv_ref, *rest, g, nq, nbs):
    pages = rest[:nbs * g]
    o_ref, m_ref, l_ref, acc_ref = rest[nbs * g:]
    s_idx = pl.program_id(1)
    rows = MLA_HEADS * nq

    @pl.when(s_idx == 0)
    def _():
        m_ref[...] = jnp.full_like(m_ref, NEG)
        l_ref[...] = jnp.zeros_like(l_ref)
        acc_ref[...] = jnp.zeros_like(acc_ref)

    qs = []
    for bl in range(nbs):
        r = slice(bl * rows, (bl + 1) * rows)
        q = jnp.concatenate([q_ref[bl, :, h * MLA_QW:(h + 1) * MLA_QW] for h in range(MLA_HEADS)], axis=0)
        qs.append(q)
        qlat, qr = q[:, :MLA_KV_LORA], q[:, MLA_KV_LORA:MLA_KV_LORA + MLA_ROPE]
        pg = pages[bl * g:(bl + 1) * g]
        ckv_t = jnp.concatenate([pg[u][:MLA_KV_LORA].astype(BF16) for u in range(g)], axis=1)
        kr_t = jnp.concatenate([pg[u][MLA_KV_LORA:].astype(BF16) for u in range(g)], axis=1)
        s = _dot(qlat, ckv_t) + _dot(qr, kr_t)
        m_prev = m_ref[r]
        m_new = jnp.maximum(m_prev, _rowmax(s))
        p = jnp.exp2(s - m_new)
        alpha = jnp.exp2(m_prev - m_new)
        l_ref[r] = alpha * l_ref[r] + _rowsum(p)
        acc_ref[r] = alpha * acc_ref[r] + _dot_nt(p.astype(BF16), ckv_t)
        m_ref[r] = m_new

    @pl.when(s_idx == pl.num_programs(1) - 1)
    def _():
        for bl in range(nbs):
            r = slice(bl * rows, (bl + 1) * rows)
            kn = kn_ref[bl]
            sn = _dot_nt(qs[bl], kn)
            qi = lax.broadcasted_iota(I32, sn.shape, 0) % nq
            kt = lax.broadcasted_iota(I32, sn.shape, 1)
            mask = kt <= qi
            sn = jnp.where(mask, sn, NEG)
            m_prev = m_ref[r]
            m_new = jnp.maximum(m_prev, jnp.max(sn, axis=-1, keepdims=True))
            p = jnp.where(mask, jnp.exp2(sn - m_new), 0.0)
            alpha = jnp.exp2(m_prev - m_new)
            l = alpha * l_ref[r] + _rowsum(p)
            acc = alpha * acc_ref[r] + _dot(p.astype(BF16), kn[:, :MLA_KV_LORA])
            o_lat = (acc / jnp.where(l > 0.0, l, 1.0)).astype(BF16)
            res = _dot(o_lat, wv_ref[...])
            head_of_lane = lax.broadcasted_iota(I32, (nq, res.shape[1]), 1) // MLA_V
            out = jnp.zeros((nq, res.shape[1]), F32)
            for h in range(MLA_HEADS):
                out = out + jnp.where(head_of_lane == h, res[h * nq:(h + 1) * nq], 0.0)
            o_ref[bl] = out.astype(o_ref.dtype)


def _page_specs(g, block, n_lane_blocks=1, flat=False, nbs=1):
    def index(b, s, pt, bl, u, c):
        return ((pt[s * g + u] if flat else pt[b * nbs + bl, s * g + u]), 0, c)

    return [pl.BlockSpec(block, functools.partial(index, bl=bl, u=u, c=c))
            for bl in range(nbs) for u in range(g) for c in range(n_lane_blocks)]


def _mla_sample(page_table, qmla, kq_new_pad, wv, cache_mla, *, g, nbs):
    nb, nq = qmla.shape[0], qmla.shape[1]
    n_pages = page_table.shape[1]
    ow = MLA_HEADS * MLA_V
    rows = nbs * MLA_HEADS * nq
    bmap = lambda b, s, pt: (b, 0, 0)
    gs = pltpu.PrefetchScalarGridSpec(
        num_scalar_prefetch=1, grid=(nb // nbs, n_pages // g),
        in_specs=[pl.BlockSpec((nbs, nq, MLA_HEADS * MLA_QW), bmap),
                  pl.BlockSpec((nbs, LANE, MLA_QW), bmap),
                  pl.BlockSpec(wv.shape, lambda b, s, pt: (0, 0))]
        + _page_specs(g, (None,) + cache_mla.shape[1:], nbs=nbs),
        out_specs=pl.BlockSpec((nbs, nq, ow), bmap),
        scratch_shapes=[pltpu.VMEM((rows, 1), F32), pltpu.VMEM((rows, 1), F32), pltpu.VMEM((rows, MLA_KV_LORA), F32)],
    )
    return pl.pallas_call(
        functools.partial(_mla_sample_kernel, g=g, nq=nq, nbs=nbs), grid_spec=gs,
        out_shape=jax.ShapeDtypeStruct((nb, nq, ow), BF16),
        compiler_params=_cparams(("arbitrary", "arbitrary")), name="mla_sample",
    )(page_table, qmla, kq_new_pad, wv, *([cache_mla] * (nbs * g)))


def _compress_kernel(pl_ref, w_ref, pe_ref, w1_ref, w2_ref, *rest, g, transposed):
    pages = rest[:g]
    kc_ref, vc_ref, xs_ref = rest[g:]
    half_w = NSA_KV_HEADS * 2 * CMP_HIDDEN
    nh = kc_ref.shape[0]
    for u in range(g):
        x = pages[u][...]
        for j in range(2):
            if transposed:
                xs_ref[j, u * LANE:(u + 1) * LANE, :] = x[j * LANE:(j + 1) * LANE, :].T
            else:
                xs_ref[j, u * LANE:(u + 1) * LANE, :] = x[:, j * LANE:(j + 1) * LANE]
    for j, out_ref in enumerate((kc_ref, vc_ref)):
        ab = jnp.zeros((nh, half_w), F32)
        for pp in range(CMP_STRIDE // 2):
            xa = xs_ref[j, pl.ds(2 * pp, nh, stride=CMP_STRIDE), :]
            xb = xs_ref[j, pl.ds(2 * pp + 1, nh, stride=CMP_STRIDE), :]
            ab = ab + _dot(jnp.concatenate([xa, xb], axis=1).astype(BF16), w_ref[j, pp])
        pe_term = _dot(pe_ref[j].astype(BF16), w1_ref[j])[0:1]
        acc = jnp.zeros((nh, LANE), F32)
        for kv in range(NSA_KV_HEADS):
            base = kv * 2 * CMP_HIDDEN
            first = ab[:, base:base + CMP_HIDDEN]
            second = ab[:, base + CMP_HIDDEN:base + 2 * CMP_HIDDEN]
            hid = first + pltpu.roll(second, nh - 1, 0) + pe_term
            hid = 0.5 * hid * (1.0 + lax.erf(hid * math.sqrt(0.5)))
            acc = acc + _dot(hid.astype(BF16), w2_ref[j, kv])
        out_ref[...] = acc.astype(out_ref.dtype)


def _compress(page_list, pool, w_pair, pe8, w1r, w2pad, *, nb, transposed):
    g = page_list.shape[0] // nb
    nh = g * (LANE // CMP_STRIDE)
    c3 = lambda b, s, pt: (0, 0, 0)
    gs = pltpu.PrefetchScalarGridSpec(
        num_scalar_prefetch=1, grid=(1, nb),
        in_specs=[pl.BlockSpec(w_pair.shape, lambda b, s, pt: (0, 0, 0, 0)), pl.BlockSpec(pe8.shape, c3),
                  pl.BlockSpec(w1r.shape, c3), pl.BlockSpec(w2pad.shape, lambda b, s, pt: (0, 0, 0, 0))]
        + _page_specs(g, (None,) + pool.shape[1:], flat=True),
        out_specs=[pl.BlockSpec((None, nh, LANE), lambda b, s, pt: (s, 0, 0))] * 2,
        scratch_shapes=[pltpu.VMEM((2, g * LANE, LANE), F32)],
    )
    return pl.pallas_call(
        functools.partial(_compress_kernel, g=g, transposed=transposed), grid_spec=gs,
        out_shape=[jax.ShapeDtypeStruct((nb, nh, LANE), BF16)] * 2,
        compiler_params=_cparams(("arbitrary", "arbitrary")), name="compress",
    )(page_list, w_pair, pe8, w1r, w2pad, *([pool] * g))


def _topk_mask(v, k, axis=1):
    lane = lax.broadcasted_iota(I32, v.shape, axis)
    sel = jnp.zeros(v.shape, F32)
    for _ in range(k):
        m = jnp.max(v, axis=axis, keepdims=True)
        idx = jnp.min(jnp.where(v == m, lane, 1 << 20), axis=axis, keepdims=True)
        pick = lane == idx
        sel = jnp.where(pick & (m > -jnp.inf), 1.0, sel)
        v = jnp.where(pick, -jnp.inf, v)
    return sel


def _softmax_masked(s, mask):
    s = jnp.where(mask, s, NEG)
    m = _rowmax(s)
    p = jnp.where(mask, jnp.exp2(s - m), 0.0)
    l = _rowsum(p)
    return p / jnp.where(l > 0.0, l, 1.0)


def _flash_step(s, mask, v, m_ref, l_ref, acc_ref, v_keys_on_lanes=False):
    s = jnp.where(mask, s, NEG)
    m_prev = m_ref[...]
    m_new = jnp.maximum(m_prev, _rowmax(s))
    p = jnp.where(mask, jnp.exp2(s - m_new), 0.0)
    alpha = jnp.exp2(m_prev - m_new)
    l_ref[...] = alpha * l_ref[...] + _rowsum(p)
    pv = _dot_nt(p.astype(BF16), v) if v_keys_on_lanes else _dot(p.astype(BF16), v)
    acc_ref[...] = alpha * acc_ref[...] + pv
    m_ref[...] = m_new


def _flash_init(m_ref, l_ref, acc_ref):
    m_ref[...] = jnp.full_like(m_ref, NEG)
    l_ref[...] = jnp.zeros_like(l_ref)
    acc_ref[...] = jnp.zeros_like(acc_ref)


def _flash_out(l_ref, acc_ref):
    l = l_ref[...]
    return acc_ref[...] / jnp.where(l > 0.0, l, 1.0)


def _nsa_prompt_kernel(q_ref, kc_ref, vc_ref, bc_ref, ovl_ref, slc_ref, swa_ref, tt_ref, gate_ref, o_ref,
                       m_ref, l_ref, acc_ref, *, tq, tkf, n_cmp, n_sel, seq_len):
    i = pl.program_id(1)
    rows = NSA_HEADS * tq
    n_sel_rows = -(-n_sel // 8) * 8
    q = jnp.concatenate([q_ref[:, h * LANE:(h + 1) * LANE] for h in range(NSA_HEADS)], axis=0)
    nhp = kc_ref.shape[0]
    pos_r = i * tq + lax.broadcasted_iota(I32, (rows, 1), 0) % tq

    wu = lax.broadcasted_iota(I32, (LANE, nhp), 0)
    wn = lax.broadcasted_iota(I32, (LANE, nhp), 1)
    shift = jnp.where(wn == i * (tq // CMP_STRIDE) - CMP_WIN_LO + wu, 1.0, 0.0).astype(BF16)
    far_col = tt_ref[2][:, :1]
    bias_c = _dot(_split3(bc_ref[...]), jnp.concatenate([shift] * 3, axis=0)) + far_col
    s = _dot_nt(q, kc_ref[...]) + bias_c
    n_idx = lax.broadcasted_iota(I32, (rows, nhp), 1)
    mask_c = (n_idx * CMP_STRIDE + CMP_LEN - 1 <= pos_r) & (n_idx < n_cmp)
    p_cmp = _softmax_masked(s, mask_c)
    o_cmp = _dot(p_cmp.astype(BF16), vc_ref[...])

    imps = []
    for kv in range(NSA_KV_HEADS):
        lo = kv * NSA_GROUP * tq
        psum = p_cmp[lo:lo + tq]
        for g in range(1, NSA_GROUP):
            psum = psum + p_cmp[lo + g * tq:lo + (g + 1) * tq]
        imps.append(_dot_nt(ovl_ref[...], _split3(psum))[:n_sel_rows])
    imp_t = jnp.concatenate(imps, axis=1)
    blk = lax.broadcasted_iota(I32, imp_t.shape, 0)
    cur = (i * tq + lax.broadcasted_iota(I32, imp_t.shape, 1) % tq) // SEL_BLOCK
    forced = (blk == 0) | (blk == cur) | (blk == cur - 1)
    future = (blk > cur) | (blk >= n_sel)
    imp_t = jnp.where(future, -jnp.inf, jnp.where(forced, jnp.inf, imp_t))
    sel_t = _topk_mask(imp_t, min(SEL_TOPN, n_sel), axis=0)
    if n_sel_rows < LANE:
        sel_t = jnp.concatenate([sel_t, jnp.zeros((LANE - n_sel_rows, NSA_KV_HEADS * tq), F32)], axis=0)
    sels = [sel_t[:, kv * tq:(kv + 1) * tq].T.astype(BF16) for kv in range(NSA_KV_HEADS)]

    near0 = pl.multiple_of(jnp.maximum(i - 1, 0) * tq, tq)
    near_bias = jnp.concatenate([tt_ref[jnp.where(i == 0, 0, 1)], tt_ref[0]], axis=1)
    pos_q = i * tq + lax.broadcasted_iota(I32, (tq, 1), 0)
    d_near = pos_q - (near0 + lax.broadcasted_iota(I32, (tq, 2 * tq), 1))

    def heads(pens):
        return jnp.concatenate([p_ for p_ in pens for _ in range(NSA_GROUP)], axis=0)

    def block_sel(first_key, n_keys, key_limit):
        sb = lax.broadcasted_iota(I32, (LANE, n_keys), 0)
        kt = lax.broadcasted_iota(I32, (LANE, n_keys), 1)
        hit = (sb == first_key // SEL_BLOCK + kt // SEL_BLOCK) & (first_key + kt < key_limit)
        expand = jnp.where(hit, 1.0, 0.0).astype(BF16)
        return [_dot(sel, expand) > 0.5 for sel in sels]

    n_far_w = WINDOW - tq
    far0 = pl.multiple_of(jnp.maximum(i - WINDOW // tq, 0) * tq, tq)
    kv_n = swa_ref[pl.ds(near0, 2 * tq), :]
    kv_f = swa_ref[pl.ds(far0, n_far_w), :]
    kpos_f = far0 + lax.broadcasted_iota(I32, (tq, n_far_w), 1)
    pen_n = jnp.where((d_near >= 0) & (d_near < WINDOW), 0.0, NEG)
    pen_f = jnp.where((kpos_f < near0) & (pos_q - kpos_f < WINDOW), 0.0, NEG)
    s_n = _dot_nt(q, kv_n[:, :LANE]) + near_bias + heads([pen_n] * NSA_KV_HEADS)
    s_f = _dot_nt(q, kv_f[:, :LANE]) + far_col + heads([pen_f] * NSA_KV_HEADS)
    m_w = jnp.maximum(_rowmax(s_n), _rowmax(s_f))
    p_n = jnp.exp2(s_n - m_w)
    p_f = jnp.exp2(s_f - m_w)
    l_w = _rowsum(p_n) + _rowsum(p_f)
    o_swa = (_dot(p_n.astype(BF16), kv_n[:, LANE:]) + _dot(p_f.astype(BF16), kv_f[:, LANE:])) / l_w

    kv_n = slc_ref[pl.ds(near0, 2 * tq), :]
    pens = [jnp.where(hit & (d_near >= 0), 0.0, NEG) for hit in block_sel(near0, 2 * tq, seq_len)]
    s_n = _dot_nt(q, kv_n[:, :LANE]) + near_bias + heads(pens)
    m_s = _rowmax(s_n)
    p_n = jnp.exp2(s_n - m_s)
    m_ref[...] = m_s
    l_ref[...] = _rowsum(p_n)
    acc_ref[...] = _dot(p_n.astype(BF16), kv_n[:, LANE:])

    n_far = (near0 + tkf - 1) // tkf
    half = NSA_GROUP * tq

    def far_logits(c):
        first = pl.multiple_of(jnp.minimum(c, jnp.maximum(n_far - 1, 0)) * tkf, tkf)
        k = slc_ref[pl.ds(first, tkf), :LANE]
        return tuple(_dot_nt(q[kvh * half:(kvh + 1) * half], k) for kvh in range(NSA_KV_HEADS))

    def slc_far(c, qk):
        qk_next = far_logits(c + 1)
        first = pl.multiple_of(c * tkf, tkf)
        v = slc_ref[pl.ds(first, tkf), LANE:]
        hits = block_sel(first, tkf, near0)
        for kvh in range(NSA_KV_HEADS):
            r = slice(kvh * half, (kvh + 1) * half)
            pen = jnp.where(hits[kvh], 0.0, NEG)
            s_ = qk[kvh] + far_col[r] + jnp.concatenate([pen] * NSA_GROUP, axis=0)
            m_prev = m_ref[r]
            m_new = jnp.maximum(m_prev, _rowmax(s_))
            p = jnp.exp2(s_ - m_new)
            alpha = jnp.exp2(m_prev - m_new)
            l_ref[r] = alpha * l_ref[r] + _rowsum(p)
            acc_ref[r] = alpha * acc_ref[r] + _dot(p.astype(BF16), v)
            m_ref[r] = m_new
        return qk_next

    lax.fori_loop(0, n_far, slc_far, far_logits(0))
    o_slc = acc_ref[...] / l_ref[...]

    gates = gate_ref[...]
    for h in range(NSA_HEADS):
        kv, g = divmod(h, NSA_GROUP)
        r = slice(h * tq, (h + 1) * tq)
        c = kv * LANE + g
        o = (gates[:, c:c + 1] * o_cmp[r] + gates[:, c + NSA_GROUP:c + NSA_GROUP + 1] * o_slc[r]
             + gates[:, c + 2 * NSA_GROUP:c + 2 * NSA_GROUP + 1] * o_swa[r])
        o_ref[:, h * LANE:(h + 1) * LANE] = o.astype(o_ref.dtype)


def _nsa_prompt(qn, kc, vc, bias_c, ovl, slc_b, swa_b, tt, gates, *, nb, seq, n_cmp, n_sel):
    tq = tt.shape[2]
    nq = seq // tq
    rows = NSA_HEADS * tq
    hw = NSA_HEADS * LANE
    nhp = kc.shape[1]
    return pl.pallas_call(
        functools.partial(_nsa_prompt_kernel, tq=tq, tkf=min(512, seq), n_cmp=n_cmp, n_sel=n_sel, seq_len=seq),
        grid=(nb, nq),
        in_specs=[
            pl.BlockSpec((tq, hw), lambda b, i: (b * nq + i, 0)),
            pl.BlockSpec((None, nhp, LANE), lambda b, i: (b, 0, 0)),
            pl.BlockSpec((None, nhp, LANE), lambda b, i: (b, 0, 0)),
            pl.BlockSpec(bias_c.shape, lambda b, i: (0, 0)),
            pl.BlockSpec(ovl.shape, lambda b, i: (0, 0)),
            pl.BlockSpec((None, seq, KV_COLS), lambda b, i: (b, 0, 0)),
            pl.BlockSpec((None, seq, KV_COLS), lambda b, i: (b, 0, 0)),
            pl.BlockSpec(tt.shape, lambda b, i: (0, 0, 0)),
            pl.BlockSpec((tq, NSA_KV_HEADS * LANE), lambda b, i: (b * nq + i, 0)),
        ],
        out_specs=pl.BlockSpec((tq, hw), lambda b, i: (b * nq + i, 0)),
        out_shape=jax.ShapeDtypeStruct((nb * seq, hw), BF16),
        scratch_shapes=[pltpu.VMEM((rows, 1), F32), pltpu.VMEM((rows, 1), F32), pltpu.VMEM((rows, LANE), F32)],
        compiler_params=_cparams(("arbitrary", "arbitrary")), name="nsa_prompt",
    )(qn, kc, vc, bias_c, ovl, slc_b, swa_b, tt, gates)


def _nsa_sample_kernel(pt_ref, q_ref, kc_ref, vc_ref, bc_ref, ovl_ref, bs_ref, sn_ref, bsn_ref, st_ref, wn_ref,
                       bw_ref, bwn_ref, gate_ref, *rest, g, nq, nbs, n_cmp, n_sel, buf):
    o_ref, sel_ref, ocmp_ref, oswa_ref, m_ref, l_ref, acc_ref = rest[nbs * g:]
    s_idx = pl.program_id(1)
    last = pl.num_programs(1) - 1
    rows = NSA_HEADS * nq
    qi = lax.broadcasted_iota(I32, (rows, 1), 0) % nq
    qs = [jnp.concatenate([q_ref[bl, :, h * LANE:(h + 1) * LANE] for h in range(NSA_HEADS)], axis=0)
          for bl in range(nbs)]

    def first_step(bl):
        q = qs[bl]
        kc_r, vc_r, st_r, wn_r = kc_ref.at[bl], vc_ref.at[bl], st_ref.at[bl], wn_ref.at[bl]
        _nsa_sample_first(q, qi, kc_r, vc_r, bc_ref, ovl_ref, st_r, wn_r, bw_ref, bwn_ref, sel_ref.at[bl], ocmp_ref.at[bl],
                          oswa_ref.at[bl], m_ref.at[bl], l_ref.at[bl], acc_ref.at[bl], nq=nq, n_cmp=n_cmp, n_sel=n_sel, buf=buf)

    @pl.when(s_idx == 0)
    def _():
        for bl in range(nbs):
            first_step(bl)

    nk = g * rest[0].shape[1]
    sb = lax.broadcasted_iota(I32, (sel_ref.shape[2], nk), 0)
    kt = lax.broadcasted_iota(I32, (sel_ref.shape[2], nk), 1)
    expand = jnp.where(sb == s_idx * (nk // SEL_BLOCK) + kt // SEL_BLOCK, 1.0, 0.0).astype(BF16)
    for bl in range(nbs):
        pages = rest[bl * g:(bl + 1) * g]
        kcat = jnp.concatenate([pages[u][:LANE].astype(BF16) for u in range(g)], axis=1)
        vcat = jnp.concatenate([pages[u][LANE:].astype(BF16) for u in range(g)], axis=1)
        s = _dot(qs[bl], kcat) + bs_ref[jnp.where(s_idx == last, 1, 0)]
        mask = _dot(sel_ref[bl], expand) > 0.5
        _flash_step(s, mask, vcat, m_ref.at[bl], l_ref.at[bl], acc_ref.at[bl], v_keys_on_lanes=True)

    @pl.when(s_idx == last)
    def _():
        for bl in range(nbs):
            q = qs[bl]
            m_r, l_r, acc_r = m_ref.at[bl], l_ref.at[bl], acc_ref.at[bl]
            sn = sn_ref[bl]
            s_n = _dot_nt(q, sn[:, :LANE]) + bsn_ref[...]
            mask_n = (lax.broadcasted_iota(I32, (rows, LANE), 1) <= qi) & (sel_ref[bl, :, n_sel - 1:n_sel] > 0.5)
            _flash_step(s_n, mask_n, sn[:, LANE:], m_r, l_r, acc_r)
            o_slc = _flash_out(l_r, acc_r)
            o_cmp, o_swa = ocmp_ref[bl], oswa_ref[bl]
            gates = gate_ref[bl]
            for h in range(NSA_HEADS):
                kv, gg = divmod(h, NSA_GROUP)
                r = slice(h * nq, (h + 1) * nq)
                c = kv * LANE + gg
                o = (gates[:, c:c + 1] * o_cmp[r] + gates[:, c + NSA_GROUP:c + NSA_GROUP + 1] * o_slc[r]
                     + gates[:, c + 2 * NSA_GROUP:c + 2 * NSA_GROUP + 1] * o_swa[r])
                o_ref[bl, :, h * LANE:(h + 1) * LANE] = o.astype(o_ref.dtype)


def _nsa_sample_first(q, qi, kc_ref, vc_ref, bc_ref, ovl_ref, st_ref, wn_ref, bw_ref, bwn_ref, sel_ref, ocmp_ref, oswa_ref,
                      m_ref, l_ref, acc_ref, *, nq, n_cmp, n_sel, buf):
    rows = NSA_HEADS * nq

    def compressed_and_select():
        nhp = kc_ref.shape[0]
        s = _dot_nt(q, kc_ref[...]) + bc_ref[...]
        mask_c = lax.broadcasted_iota(I32, (rows, nhp), 1) < n_cmp
        p_cmp = _softmax_masked(s, mask_c)
        ocmp_ref[...] = _dot(p_cmp.astype(BF16), vc_ref[...])
        psums = []
        for kv in range(NSA_KV_HEADS):
            ps = p_cmp[kv * NSA_GROUP * nq:kv * NSA_GROUP * nq + nq]
            for gg in range(1, NSA_GROUP):
                lo = (kv * NSA_GROUP + gg) * nq
                ps = ps + p_cmp[lo:lo + nq]
            psums.append(ps)
        n_kq = NSA_KV_HEADS * nq
        psum = jnp.concatenate(psums + [jnp.zeros((LANE - n_kq, nhp), F32)], axis=0)
        n_blk = ovl_ref.shape[0]
        n_sel_rows = -(-n_sel // 8) * 8
        imp_t = _dot_nt(ovl_ref[...], _split3(psum))[:n_sel_rows]
        blk = lax.broadcasted_iota(I32, imp_t.shape, 0)
        cur = n_sel - 1
        forced = (blk == 0) | (blk == cur) | (blk == cur - 1)
        imp_t = jnp.where(blk >= n_sel, -jnp.inf, jnp.where(forced, jnp.inf, imp_t))
        sel_t = _topk_mask(imp_t, min(SEL_TOPN, n_sel), axis=0)
        if n_sel_rows < n_blk:
            sel_t = jnp.concatenate([sel_t, jnp.zeros((n_blk - n_sel_rows, LANE), F32)], axis=0)
        sel = sel_t.T.astype(BF16)
        sel_ref[...] = jnp.concatenate(
            [sel[kv * nq:(kv + 1) * nq] for kv in range(NSA_KV_HEADS) for _ in range(NSA_GROUP)], axis=0)

    def window():
        st = st_ref[...]
        s_w = _dot(q, st[:LANE].astype(BF16)) + bw_ref[...]
        d_w = buf + qi - lax.broadcasted_iota(I32, (rows, buf), 1)
        mask_w = (d_w >= 0) & (d_w < WINDOW)
        wn = wn_ref[...]
        s_n = _dot_nt(q, wn[:, :LANE]) + bwn_ref[...]
        mask_n = lax.broadcasted_iota(I32, (rows, LANE), 1) <= qi
        s_w = jnp.where(mask_w, s_w, NEG)
        s_n = jnp.where(mask_n, s_n, NEG)
        m = jnp.maximum(_rowmax(s_w), _rowmax(s_n))
        p_w = jnp.where(mask_w, jnp.exp2(s_w - m), 0.0)
        p_n = jnp.where(mask_n, jnp.exp2(s_n - m), 0.0)
        l = _rowsum(p_w) + _rowsum(p_n)
        o = _dot_nt(p_w.astype(BF16), st[LANE:].astype(BF16)) + _dot(p_n.astype(BF16), wn[:, LANE:])
        oswa_ref[...] = o / jnp.where(l > 0.0, l, 1.0)

    compressed_and_select()
    window()
    _flash_init(m_ref, l_ref, acc_ref)


def _nsa_sample(page_table, qn, kc, vc, bias_c, ovl, bias_s, slc_new, bias_sn, state_swa, swa_new, bias_w, bias_wn,
                gates, cache_slc, *, g, nbs, n_cmp, n_sel):
    nb, nq = qn.shape[0], qn.shape[1]
    n_pages = page_table.shape[1]
    rows = NSA_HEADS * nq
    buf = state_swa.shape[2]
    hw = NSA_HEADS * LANE
    bmap = lambda b, s, pt: (b, 0, 0)
    c2 = lambda b, s, pt: (0, 0)
    per_elem = lambda a: pl.BlockSpec((nbs,) + a.shape[1:], bmap)
    gs = pltpu.PrefetchScalarGridSpec(
        num_scalar_prefetch=1, grid=(nb // nbs, n_pages // g),
        in_specs=[
            per_elem(qn), per_elem(kc), per_elem(vc),
            pl.BlockSpec(bias_c.shape, c2),
            pl.BlockSpec(ovl.shape, c2),
            pl.BlockSpec(bias_s.shape, lambda b, s, pt: (0, 0, 0)),
            per_elem(slc_new),
            pl.BlockSpec(bias_sn.shape, c2),
            per_elem(state_swa), per_elem(swa_new),
            pl.BlockSpec(bias_w.shape, c2),
            pl.BlockSpec(bias_wn.shape, c2),
            per_elem(gates),
        ] + _page_specs(g, (None,) + cache_slc.shape[1:], nbs=nbs),
        out_specs=pl.BlockSpec((nbs, nq, hw), bmap),
        scratch_shapes=[pltpu.VMEM((nbs, rows, ovl.shape[0]), BF16), pltpu.VMEM((nbs, rows, LANE), F32),
                        pltpu.VMEM((nbs, rows, LANE), F32), pltpu.VMEM((nbs, rows, 1), F32), pltpu.VMEM((nbs, rows, 1), F32),
                        pltpu.VMEM((nbs, rows, LANE), F32)],
    )
    return pl.pallas_call(
        functools.partial(_nsa_sample_kernel, g=g, nq=nq, nbs=nbs, n_cmp=n_cmp, n_sel=n_sel, buf=buf), grid_spec=gs,
        out_shape=jax.ShapeDtypeStruct((nb, nq, hw), BF16),
        compiler_params=_cparams(("arbitrary", "arbitrary")), name="nsa_sample",
    )(page_table, qn, kc, vc, bias_c, ovl, bias_s, slc_new, bias_sn, state_swa, swa_new, bias_w, bias_wn, gates,
      *([cache_slc] * (nbs * g)))


def _outproj_kernel(x_ref, om_ref, on_ref, g1_ref, sc_ref, sh_ref, wom_ref, won_ref, gn2_ref, wr_ref, br_ref,
                    x1_ref, h2_ref, ti_ref, tw_ref):
    mix = _dot(om_ref[...], wom_ref[...]) + _dot(on_ref[...], won_ref[...])
    x1 = x_ref[...] + g1_ref[...] * mix
    x1_ref[...] = x1
    h2 = _rms(x1, gn2_ref[...]) * (1.0 + sc_ref[...]) + sh_ref[...]
    h2_ref[...] = h2.astype(h2_ref.dtype)
    h3, w3 = _split3(h2), _split3(wr_ref[...])
    d = h2.shape[1]
    h_cat = jnp.concatenate([h3[:, :d], h3[:, :d], h3[:, d:2 * d]], axis=1)
    w_cat = jnp.concatenate([w3[:, :d], w3[:, d:2 * d], w3[:, :d]], axis=1)
    logits = _dot_nt(h_cat, w_cat) + br_ref[...]
    lane = lax.broadcasted_iota(I32, logits.shape, 1)
    v = logits
    vals, idxs = [], []
    for _ in range(TOP_K):
        m = jnp.max(v, axis=-1, keepdims=True)
        idx = jnp.min(jnp.where(v == m, lane, 1 << 20), axis=-1, keepdims=True)
        vals.append(m)
        idxs.append(idx)
        v = jnp.where(lane == idx, -jnp.inf, v)
    es = [jnp.exp(m - vals[0]) for m in vals]
    tot = es[0]
    for e in es[1:]:
        tot = tot + e
    ti = jnp.zeros(logits.shape, I32)
    tw = jnp.zeros(logits.shape, F32)
    for k in range(TOP_K):
        ti = jnp.where(lane == k, idxs[k], ti)
        tw = jnp.where(lane == k, es[k] / tot, tw)
    ti_ref[...] = ti
    tw_ref[...] = tw


def _outproj(x, o_mla, o_nsa, g1, sc, sh, W, *, rows_per_mod, tm):
    t, d = x.shape
    nt = t // tm
    if rows_per_mod == 1:
        tiles_per_mod = (t // g1.shape[0]) // tm
        mod_map = lambda i: (i // tiles_per_mod, 0, 0)
    else:
        mod_map = lambda i: (i, 0, 0)
    row = lambda i: (i, 0)
    c2 = lambda i: (0, 0)
    mod = pl.BlockSpec((None, rows_per_mod, d), mod_map)
    return pl.pallas_call(
        _outproj_kernel, grid=(nt,),
        in_specs=[pl.BlockSpec((tm, d), row), pl.BlockSpec((tm, o_mla.shape[1]), row), pl.BlockSpec((tm, o_nsa.shape[1]), row),
                  mod, mod, mod, pl.BlockSpec(W["w_o_mla"].shape, c2), pl.BlockSpec(W["w_o_nsa"].shape, c2),
                  pl.BlockSpec((1, d), c2), pl.BlockSpec(W["w_router"].shape, c2), pl.BlockSpec((1, LANE), c2)],
        out_specs=[pl.BlockSpec((tm, d), row), pl.BlockSpec((tm, d), row), pl.BlockSpec((tm, LANE), row),
                   pl.BlockSpec((tm, LANE), row)],
        out_shape=[jax.ShapeDtypeStruct((t, d), F32), jax.ShapeDtypeStruct((t, d), F32),
                   jax.ShapeDtypeStruct((t, LANE), I32), jax.ShapeDtypeStruct((t, LANE), F32)],
        compiler_params=_cparams(("arbitrary",)), name="out_proj",
    )(x, o_mla, o_nsa, g1, sc, sh, W["w_o_mla"], W["w_o_nsa"], W["g_norm2"], W["w_router"], W["b_router"])


def _expert_kernel(te_ref, tv_ref, x_ref, wgu_ref, bgu_ref, wd_ref, bd_ref, o_ref, wgu_bf, wd_bf):
    t = pl.program_id(0)

    @pl.when((t == 0) | (te_ref[t] != te_ref[jnp.maximum(t - 1, 0)]))
    def _():
        wgu_bf[...] = wgu_ref[...].astype(BF16)
        wd_bf[...] = wd_ref[...].astype(BF16)

    @pl.when(tv_ref[t] > 0)
    def _():
        d_ff = wd_ref.shape[0]
        gu = _dot(x_ref[...].astype(BF16), wgu_bf[...]) + bgu_ref[...]
        glu = jnp.minimum(gu[:, :d_ff], SWIGLU_LIMIT)
        lin = jnp.clip(gu[:, d_ff:], -SWIGLU_LIMIT, SWIGLU_LIMIT)
        act = glu * _sigmoid(SWIGLU_ALPHA * glu) * (lin + 1.0)
        o_ref[...] = _dot(act.astype(BF16), wd_bf[...]) + bd_ref[...]

    @pl.when(tv_ref[t] == 0)
    def _():
        o_ref[...] = jnp.zeros_like(o_ref)


def _experts(tile_expert, tile_valid, x_sorted, W, *, tm):
    ns, d = x_sorted.shape
    gs = pltpu.PrefetchScalarGridSpec(
        num_scalar_prefetch=2, grid=(ns // tm,),
        in_specs=[pl.BlockSpec((tm, d), lambda t, te, tv: (t, 0)),
                  pl.BlockSpec((None,) + W["w_gate_up"].shape[1:], lambda t, te, tv: (te[t], 0, 0)),
                  pl.BlockSpec((None,) + W["b_gate_up"].shape[1:], lambda t, te, tv: (te[t], 0, 0)),
                  pl.BlockSpec((None,) + W["w_down"].shape[1:], lambda t, te, tv: (te[t], 0, 0)),
                  pl.BlockSpec((None,) + W["b_down"].shape[1:], lambda t, te, tv: (te[t], 0, 0))],
        out_specs=pl.BlockSpec((tm, d), lambda t, te, tv: (t, 0)),
        scratch_shapes=[pltpu.VMEM(W["w_gate_up"].shape[1:], BF16), pltpu.VMEM(W["w_down"].shape[1:], BF16)],
    )
    return pl.pallas_call(
        _expert_kernel, grid_spec=gs, out_shape=jax.ShapeDtypeStruct((ns, d), F32),
        compiler_params=_cparams(("arbitrary",)), name="experts",
    )(tile_expert, tile_valid, x_sorted, W["w_gate_up"], W["b_gate_up"], W["w_down"], W["b_down"])


def _final_kernel(x1_ref, tw_ref, *rest):
    parts, (g2_ref, gf_ref, o_ref) = rest[:TOP_K], rest[TOP_K:]
    tw = tw_ref[...]
    moe = tw[:, 0:1] * parts[0][...]
    for k in range(1, TOP_K):
        moe = moe + tw[:, k:k + 1] * parts[k][...]
    o_ref[...] = _rms(x1_ref[...] + g2_ref[...] * moe, gf_ref[...])


def _final(x1, top_w, moe_parts, g2, g_final, *, rows_per_mod, tm):
    t, d = x1.shape
    if rows_per_mod == 1:
        tiles_per_mod = (t // g2.shape[0]) // tm
        mod_map = lambda i: (i // tiles_per_mod, 0, 0)
    else:
        mod_map = lambda i: (i, 0, 0)
    row = lambda i: (i, 0)
    return pl.pallas_call(
        _final_kernel, grid=(t // tm,),
        in_specs=[pl.BlockSpec((tm, d), row), pl.BlockSpec((tm, LANE), row)]
        + [pl.BlockSpec((tm, d), functools.partial(lambda i, k: (k * (t // tm) + i, 0), k=k)) for k in range(TOP_K)]
        + [pl.BlockSpec((None, rows_per_mod, d), mod_map), pl.BlockSpec((1, d), lambda i: (0, 0))],
        out_specs=pl.BlockSpec((tm, d), row), out_shape=jax.ShapeDtypeStruct((t, d), F32),
        compiler_params=_cparams(("arbitrary",)), name="final_norm")(x1, top_w, *([moe_parts] * TOP_K), g2, g_final)


def _t5_bucket(dist):
    n = jnp.maximum(dist, 0)
    max_exact = NUM_BUCKETS // 2
    nf = jnp.maximum(n, 1).astype(F32)
    large = max_exact + (jnp.log(nf / max_exact) / math.log(MAX_DISTANCE / max_exact)
                         * (NUM_BUCKETS - max_exact)).astype(I32)
    return jnp.where(n < max_exact, n, jnp.minimum(large, NUM_BUCKETS - 1))


def _bias_rows(rel_bias, dist):
    bucket = _t5_bucket(dist)
    out = jnp.zeros((rel_bias.shape[1],) + dist.shape, F32)
    for b in range(NUM_BUCKETS):
        out = out + jnp.where(bucket == b, 1.0, 0.0)[None] * rel_bias[b][:, None, None]
    return out * LOG2E


def _rope_table(pos):
    half = MLA_ROPE // 2
    inv_freq = 1.0 / (ROPE_THETA ** (jnp.arange(half, dtype=F32) / half))
    ang = pos.astype(F32)[:, None] * inv_freq[None, :]
    pad = jnp.zeros((pos.shape[0], LANE - MLA_ROPE), F32)
    cos, sin = jnp.cos(ang), jnp.sin(ang)
    return jnp.concatenate([cos, cos, pad, sin, sin, pad], axis=1)


def _rot_cols(w):
    half = MLA_ROPE // 2
    return jnp.concatenate([-w[..., half:], w[..., :half]], axis=-1)


def _pad_last(w, n):
    return jnp.pad(w, [(0, 0)] * (w.ndim - 1) + [(0, n - w.shape[-1])])


def _pack_weights(w_in, g_norm1, g_norm2, g_q_a, w_q_b, g_kv_a, w_kv_b, cmp_pe, cmp_w1, cmp_w2, w_o, w_router,
                  b_router, w_gate_up, b_gate_up, w_down, b_down):
    d = w_in.shape[0]
    sizes = (MLA_Q_LORA, MLA_KV_LORA, MLA_ROPE, NSA_HEADS * HEAD_DIM, KV_COLS, KV_COLS, KV_COLS, 3 * NSA_HEADS)
    offs = [0]
    for s in sizes:
        offs.append(offs[-1] + s)
    w_qa, w_ckv, w_kr, w_qn, w_cmp, w_slc, w_swa, w_g = [w_in[:, offs[k]:offs[k + 1]] for k in range(8)]
    wq = w_qn.reshape(d, NSA_KV_HEADS, NSA_GROUP, HEAD_DIM)
    wq_pad = jnp.concatenate(
        [jnp.pad(wq[:, kv], ((0, 0), (0, 0), (kv * HEAD_DIM, LANE - (kv + 1) * HEAD_DIM))).reshape(d, NSA_GROUP * LANE)
         for kv in range(NSA_KV_HEADS)], axis=1)
    wg = jnp.transpose(w_g.reshape(d, NSA_KV_HEADS, NSA_GROUP, 3), (0, 1, 3, 2)).reshape(d, NSA_KV_HEADS, 3 * NSA_GROUP)
    wg_pad = _pad_last(wg, LANE).reshape(d, NSA_KV_HEADS * LANE)
    w_in_p = jnp.concatenate([w_qa, w_ckv, wq_pad, w_cmp, w_slc, w_swa, _pad_last(w_kr, LANE),
                              _pad_last(_rot_cols(w_kr), LANE), wg_pad], axis=1).astype(BF16)
    assert w_in_p.shape[1] == _C_END
    nope = _pad_last(w_q_b[:, :, :MLA_NOPE], LANE).reshape(MLA_Q_LORA, MLA_HEADS * LANE)
    rp = w_q_b[:, :, MLA_NOPE:]
    w_qb = jnp.concatenate([nope, _pad_last(rp, LANE).reshape(MLA_Q_LORA, -1),
                            _pad_last(_rot_cols(rp), LANE).reshape(MLA_Q_LORA, -1)], axis=1).astype(BF16)
    w_kn = jnp.transpose(w_kv_b[:, :, :MLA_NOPE], (1, 2, 0))
    w_kn = jnp.pad(w_kn, ((0, 0), (0, LANE - MLA_NOPE), (0, 0))).astype(BF16)
    wv = jnp.transpose(w_kv_b[:, :, MLA_NOPE:], (1, 0, 2))
    wv_pad = jnp.stack([jnp.pad(wv[h], ((0, 0), (h * MLA_V, (MLA_HEADS - 1 - h) * MLA_V))) for h in range(MLA_HEADS)]).astype(BF16)
    n_mla = MLA_HEADS * MLA_V
    won = w_o[n_mla:].reshape(NSA_KV_HEADS, NSA_GROUP, HEAD_DIM, d)
    won_pad = jnp.concatenate(
        [jnp.pad(won[kv], ((0, 0), (kv * HEAD_DIM, LANE - (kv + 1) * HEAD_DIM), (0, 0))).reshape(NSA_GROUP * LANE, d)
         for kv in range(NSA_KV_HEADS)], axis=0).astype(BF16)
    base = jnp.concatenate([cmp_w1[:, :CMP_STRIDE], cmp_w1[:, CMP_STRIDE:]], axis=-1)
    z = jnp.zeros_like(base)
    blk = jnp.concatenate([jnp.concatenate([base, z], axis=-1), jnp.concatenate([z, base], axis=-1)], axis=2)
    w_pair = blk.reshape(2, CMP_STRIDE // 2, 2 * NSA_KV_HEADS * HEAD_DIM, NSA_KV_HEADS * 2 * CMP_HIDDEN).astype(BF16)
    pe8 = jnp.broadcast_to(cmp_pe.reshape(2, 1, CMP_LEN * HEAD_DIM), (2, 8, CMP_LEN * HEAD_DIM))
    w1r = cmp_w1.reshape(2, CMP_LEN * HEAD_DIM, CMP_HIDDEN).astype(BF16)
    w2pad = jnp.stack([jnp.stack([jnp.pad(cmp_w2[j], ((0, 0), (kv * HEAD_DIM, LANE - (kv + 1) * HEAD_DIM)))
                                  for kv in range(NSA_KV_HEADS)]) for j in range(2)]).astype(BF16)
    return dict(
        w_in=w_in_p, g_norm1=g_norm1[None], g_norm2=g_norm2[None], g_q_a=g_q_a[None], g_kv_a=g_kv_a[None],
        w_qb=w_qb, w_kn=w_kn, wv=wv_pad, wv_cat=w_kv_b[:, :, MLA_NOPE:].reshape(MLA_KV_LORA, n_mla).astype(BF16),
        w_o_mla=w_o[:n_mla].astype(BF16), w_o_nsa=won_pad,
        w_pair=w_pair, pe8=pe8, w1r=w1r, w2pad=w2pad,
        w_router=_pad_last(w_router, LANE).T, b_router=jnp.pad(b_router, (0, LANE - N_EXPERTS), constant_values=NEG)[None],
        w_gate_up=w_gate_up, b_gate_up=b_gate_up[:, None, :], w_down=w_down, b_down=b_down[:, None, :])


def _overlap(n_half_pad, n_sel, n_sel_pad):
    c_start = jnp.arange(n_half_pad) * CMP_STRIDE
    s_start = jnp.arange(n_sel_pad) * SEL_BLOCK
    ov = (c_start[:, None] < s_start[None, :] + SEL_BLOCK) & (c_start[:, None] + CMP_LEN > s_start[None, :])
    return (ov & (jnp.arange(n_sel_pad) < n_sel)[None, :]).astype(F32)


def _moe_dispatch(top_i, tm):
    t = top_i.shape[0]
    a = t * TOP_K
    n_tiles = -(-a // tm) + N_EXPERTS
    ns = n_tiles * tm
    e_flat = top_i[:, :TOP_K].reshape(a)
    onehot = (e_flat[:, None] == jnp.arange(N_EXPERTS, dtype=I32)[None, :]).astype(I32)
    csum = jnp.cumsum(onehot, axis=0)
    counts = csum[-1]
    padded = ((counts + tm - 1) // tm) * tm
    pend = jnp.cumsum(padded)
    pstart = pend - padded
    start = jnp.cumsum(counts) - counts
    slot_of_assign = jnp.sum(onehot * (pstart[None, :] + csum - 1), axis=1).reshape(t, TOP_K)
    _, order = lax.sort((e_flat, jnp.arange(a, dtype=I32)), num_keys=1, is_stable=True)
    tile_start = jnp.arange(n_tiles, dtype=I32) * tm
    tile_expert = jnp.minimum(jnp.sum((pend[None, :] <= tile_start[:, None]).astype(I32), axis=1), N_EXPERTS - 1)
    tile_valid = (tile_start < pend[-1]).astype(I32)
    e_hot = (tile_expert[:, None] == jnp.arange(N_EXPERTS, dtype=I32)[None, :]).astype(I32)
    rank0 = tile_start - jnp.sum(e_hot * pstart[None, :], axis=1)
    lane = jnp.arange(tm, dtype=I32)[None, :]
    base = jnp.sum(e_hot * start[None, :], axis=1) + rank0
    run = order[jnp.clip(base[:, None] + lane, 0, a - 1)]
    rank = rank0[:, None] + lane
    valid = (rank < jnp.sum(e_hot * counts[None, :], axis=1, keepdims=True)) & (tile_valid[:, None] > 0)
    tok_of_slot = jnp.where(valid, run // TOP_K, 0).reshape(ns)
    return tok_of_slot, slot_of_assign, tile_expert, tile_valid


def kernel(x_prompt, x_sample, c_prompt, c_sample, cache_mla, cache_nsa_cmp, cache_nsa_slc, state_nsa_swa, page_table, rel_bias, w_ada, b_ada, g_norm1, g_norm2, w_in, g_q_a, w_q_b, g_kv_a, w_kv_b, cmp_pe, cmp_w1, cmp_w2, w_o, w_router, b_router, w_gate_up, b_gate_up, w_down, b_down, g_final):
    depth = w_in.shape[0]
    assert depth == 1, "single-layer decoder step"
    nb, seq, d = x_prompt.shape
    nbd, nq, _ = x_sample.shape
    n_pages = page_table.shape[1]
    page = cache_mla.shape[2]
    past = n_pages * page
    buf = state_nsa_swa.shape[2]
    assert page == LANE and nq <= 8 and seq % 512 == 0 and past % SEL_BLOCK == 0 and seq >= WINDOW
    tp, ts = nb * seq, nbd * nq
    W = _pack_weights(w_in[0], g_norm1[0], g_norm2[0], g_q_a[0], w_q_b[0], g_kv_a[0], w_kv_b[0], cmp_pe[0], cmp_w1[0],
                      cmp_w2[0], w_o[0], w_router[0], b_router[0], w_gate_up[0], b_gate_up[0], w_down[0], b_down[0])

    n_c = nb + nbd
    n_c_pad = -(-n_c // 8) * 8
    c_all = jnp.pad(jnp.concatenate([c_prompt, c_sample], axis=0), ((0, n_c_pad - n_c), (0, 0)))
    mod = _ada_mod(c_all, w_ada[0].astype(BF16), b_ada[0][None])
    mod_p = [m[:, None, :] for m in jnp.split(mod[:nb], 6, axis=-1)]
    tm_s = min(TOKEN_TILE, ts)
    mod_s = [jnp.repeat(m, nq, axis=0).reshape(ts // tm_s, tm_s, d) for m in jnp.split(mod[nb:n_c], 6, axis=-1)]

    tm_p = TOKEN_TILE
    pos_p = jnp.arange(seq)
    pos_s = past + jnp.arange(nq)
    P = _proj(x_prompt.reshape(tp, d), mod_p[1], mod_p[0], _rope_table(pos_p), W, rows_per_mod=1,
              cs_period_tiles=seq // tm_p, tm=tm_p, states_t_batches=nb)
    cs_s = jnp.tile(_rope_table(pos_s), (tm_s // nq, 1))
    S_ = _proj(x_sample.reshape(ts, d), mod_s[1], mod_s[0], cs_s, W, rows_per_mod=tm_s, cs_period_tiles=1, tm=tm_s)
    p_qmla, p_kq, p_mla_t, p_qn, p_cmp, p_cmp_t, p_slc_t, p_swa_t, p_slcb, p_swab, p_gate = P
    s_qmla, s_kq, s_mla, s_qn, _, s_cmp, s_slc, s_swa, s_slcb, s_swab, s_gate = S_

    def pad_new(a):
        return jnp.pad(a.reshape(nbd, nq, a.shape[1]), ((0, 0), (0, LANE - nq), (0, 0)))

    o_mla_p = _mla_prompt(p_qmla, p_kq, W["wv"], nb=nb, seq=seq)
    g_pages = min(PAGES_PER_STEP, n_pages)
    cache_mla_t = jnp.swapaxes(cache_mla[0], 1, 2)
    cache_cmp_t = jnp.moveaxis(cache_nsa_cmp[0], 1, -1).reshape(-1, KV_COLS, page)
    cache_slc_t = jnp.moveaxis(cache_nsa_slc[0], 1, -1).reshape(-1, KV_COLS, page)
    state_swa_t = jnp.moveaxis(state_nsa_swa[0], 1, -1).reshape(nbd, KV_COLS, buf)
    o_mla_s = _mla_sample(page_table, s_qmla.reshape(nbd, nq, -1), pad_new(s_kq), W["wv_cat"], cache_mla_t, g=g_pages,
                          nbs=2 if nbd % 2 == 0 else 1)

    assert (past + nq) // CMP_STRIDE == past // CMP_STRIDE
    n_pp = tp // LANE
    kc_p, vc_p = _compress(jnp.arange(n_pp, dtype=I32), p_cmp.reshape(n_pp, LANE, KV_COLS), W["w_pair"], W["pe8"],
                           W["w1r"], W["w2pad"], nb=nb, transposed=False)
    kc_s, vc_s = _compress(page_table.reshape(nbd * n_pages), cache_cmp_t, W["w_pair"], W["pe8"], W["w1r"], W["w2pad"],
                           nb=nbd, transposed=True)

    nh_p = seq // CMP_STRIDE
    n_sel_p = -(-seq // SEL_BLOCK)
    tq = QUERY_TILE
    far_h = _bias_rows(rel_bias, jnp.full((1, 1), MAX_DISTANCE))
    d_win = (jnp.arange(tq)[:, None] - CMP_STRIDE * (jnp.arange(LANE)[None, :] - CMP_WIN_LO) - (CMP_LEN - 1))
    assert CMP_WIN_LO * CMP_STRIDE >= MAX_DISTANCE + CMP_LEN and LANE - CMP_WIN_LO >= tq // CMP_STRIDE
    bias_c_p = jnp.where((d_win >= 0)[None], _bias_rows(rel_bias, d_win) - far_h, 0.0)
    bias_c_p = bias_c_p.reshape(NSA_HEADS * tq, LANE)
    di = jnp.arange(tq)[:, None] - jnp.arange(tq)[None, :]
    tt = jnp.stack([_bias_rows(rel_bias, di), _bias_rows(rel_bias, di + tq),
                    _bias_rows(rel_bias, jnp.full((tq, tq), MAX_DISTANCE))], axis=1)
    tt = tt.transpose(1, 0, 2, 3).reshape(3, NSA_HEADS * tq, tq)
    assert n_sel_p <= LANE
    ovl_p = jnp.tile(_overlap(nh_p, n_sel_p, LANE).T, (1, 3)).astype(BF16)
    o_nsa_p = _nsa_prompt(p_qn, kc_p, vc_p, bias_c_p, ovl_p, p_slcb.reshape(nb, seq, KV_COLS),
                          p_swab.reshape(nb, seq, KV_COLS), tt, p_gate, nb=nb, seq=seq, n_cmp=nh_p - 1, n_sel=n_sel_p)

    nh_s = past // CMP_STRIDE
    n_sel_s = -(-(past + nq) // SEL_BLOCK)
    assert n_sel_s == past // SEL_BLOCK + 1
    n_sel_pad = -(-n_sel_s // LANE) * LANE
    rows_s = NSA_HEADS * nq
    cmp_end_s = jnp.arange(nh_s) * CMP_STRIDE + CMP_LEN - 1
    bias_c_s = _bias_rows(rel_bias, pos_s[:, None] - cmp_end_s[None, :]).reshape(rows_s, nh_s)
    nk_step = g_pages * page
    far = jnp.broadcast_to(_bias_rows(rel_bias, jnp.full((nq, 1), MAX_DISTANCE)).reshape(rows_s, 1), (rows_s, nk_step))
    tail_pos = past - nk_step + jnp.arange(nk_step)
    bias_s = jnp.stack([far, _bias_rows(rel_bias, pos_s[:, None] - tail_pos[None, :]).reshape(rows_s, nk_step)])
    new_pos = past + jnp.arange(LANE)
    bias_new = _bias_rows(rel_bias, pos_s[:, None] - new_pos[None, :]).reshape(rows_s, LANE)
    swa_pos = past - buf + jnp.arange(buf)
    bias_w = _bias_rows(rel_bias, pos_s[:, None] - swa_pos[None, :]).reshape(rows_s, buf)
    ovl_s = jnp.tile(_overlap(nh_s, n_sel_s, n_sel_pad).T, (1, 3)).astype(BF16)
    o_nsa_s = _nsa_sample(page_table, s_qn.reshape(nbd, nq, -1), kc_s, vc_s, bias_c_s, ovl_s, bias_s, pad_new(s_slcb),
                          bias_new, state_swa_t, pad_new(s_swab), bias_w, bias_new,
                          s_gate.reshape(nbd, nq, -1), cache_slc_t, g=g_pages, nbs=2 if nbd % 2 == 0 else 1,
                          n_cmp=nh_s - 1, n_sel=n_sel_s)

    x1_p, h2_p, ti_p, tw_p = _outproj(x_prompt.reshape(tp, d), o_mla_p, o_nsa_p, mod_p[2], mod_p[4], mod_p[3], W,
                                      rows_per_mod=1, tm=2 * tm_p)
    x1_s, h2_s, ti_s, tw_s = _outproj(x_sample.reshape(ts, d), o_mla_s.reshape(ts, -1), o_nsa_s.reshape(ts, -1),
                                      mod_s[2], mod_s[4], mod_s[3], W, rows_per_mod=tm_s, tm=tm_s)

    tm_e = EXPERT_TILE
    h2 = jnp.concatenate([h2_p, h2_s], axis=0)
    tok_of_slot, slot_of_assign, tile_expert, tile_valid = _moe_dispatch(jnp.concatenate([ti_p, ti_s], axis=0), tm_e)
    y_sorted = _experts(tile_expert, tile_valid, h2[tok_of_slot], W, tm=tm_e)
    parts_p = y_sorted[slot_of_assign[:tp].T.reshape(-1)]
    parts_s = y_sorted[slot_of_assign[tp:].T.reshape(-1)]
    y_p = _final(x1_p, tw_p, parts_p, mod_p[5], g_final[None], rows_per_mod=1, tm=tm_p)
    y_s = _final(x1_s, tw_s, parts_s, mod_s[5], g_final[None], rows_per_mod=tm_s, tm=tm_s)

    kv_tail = (2, NSA_KV_HEADS, HEAD_DIM)
    keep_p = min(WINDOW, seq)
    swa_keys = jnp.concatenate([state_nsa_swa[0], s_swa.reshape((nbd, nq) + kv_tail)], axis=1)
    keep_s = min(WINDOW, buf + nq)

    def rows_last(a_t):
        return jnp.moveaxis(a_t.reshape((nb,) + kv_tail + (a_t.shape[-1],)), -1, 1)[None]

    return (y_p.reshape(nb, seq, d), y_s.reshape(nbd, nq, d),
            jnp.swapaxes(p_mla_t, 1, 2)[None], s_mla.reshape(1, nbd, nq, -1),
            rows_last(p_cmp_t), s_cmp.reshape((1, nbd, nq) + kv_tail),
            rows_last(p_slc_t), s_slc.reshape((1, nbd, nq) + kv_tail),
            rows_last(p_swa_t[:, :, seq - keep_p:]),
            swa_keys[None, :, buf + nq - keep_s:])
```

```python
import functools
import math

import jax
import jax.numpy as jnp
from jax import lax
from jax.experimental import pallas as pl
from jax.experimental.pallas import tpu as pltpu

F32, BF16, I32 = jnp.float32, jnp.bfloat16, jnp.int32

MLA_HEADS = 8
MLA_Q_LORA = 384
MLA_KV_LORA = 256
MLA_NOPE = 64
MLA_ROPE = 32
MLA_V = 64
NSA_HEADS = 8
NSA_KV_HEADS = 2
NSA_GROUP = NSA_HEADS // NSA_KV_HEADS
HEAD_DIM = 64
CMP_LEN = 32
CMP_STRIDE = 16
CMP_HIDDEN = 128
SEL_BLOCK = 64
SEL_TOPN = 16
WINDOW = 512
KV_COLS = 2 * NSA_KV_HEADS * HEAD_DIM
N_EXPERTS = 32
TOP_K = 4
SWIGLU_LIMIT = 7.0
SWIGLU_ALPHA = 1.702
NUM_BUCKETS = 32
MAX_DISTANCE = 128
ROPE_THETA = 10000.0
NORM_EPS = 1e-6

LANE = 128
VMEM_LIMIT = 56 * 1024 * 1024
NEG = -1e30
CMP_WIN_LO = 16
TOKEN_TILE = 256
QUERY_TILE = 128
EXPERT_TILE = 256
PAGES_PER_STEP = 64

MLA_QW = MLA_KV_LORA + LANE
MLA_SCALE = (MLA_NOPE + MLA_ROPE) ** -0.5
MLA_QSCALE = MLA_SCALE * math.log2(math.e)
LOG2E = math.log2(math.e)
NSA_QSCALE = HEAD_DIM ** -0.5 * LOG2E

_C_QA = 0
_C_CKV = _C_QA + MLA_Q_LORA
_C_QN = _C_CKV + MLA_KV_LORA
_C_CMP = _C_QN + NSA_HEADS * LANE
_C_SLC = _C_CMP + KV_COLS
_C_SWA = _C_SLC + KV_COLS
_C_KR = _C_SWA + KV_COLS
_C_KRR = _C_KR + LANE
_C_G = _C_KRR + LANE
_C_END = _C_G + NSA_KV_HEADS * LANE


def _cparams(sem, vmem=VMEM_LIMIT):
    return pltpu.CompilerParams(dimension_semantics=sem, vmem_limit_bytes=vmem)


def _dot(a, b):
    return jnp.dot(a, b, preferred_element_type=F32)


def _dot_nt(a, b):
    return lax.dot_general(a, b, (((1,), (1,)), ((), ())), preferred_element_type=F32)


def _row_reduce(x, op, reduce):
    n = x.shape[1] // LANE
    if x.shape[1] % LANE or n <= 1:
        return reduce(x, axis=-1, keepdims=True)
    t = x[:, :LANE]
    for c in range(1, n):
        t = op(t, x[:, c * LANE:(c + 1) * LANE])
    return reduce(t, axis=-1, keepdims=True)


def _rowmax(x):
    return _row_reduce(x, jnp.maximum, jnp.max)


def _rowsum(x):
    return _row_reduce(x, jnp.add, jnp.sum)


def _split3(x):
    hi = x.astype(BF16)
    r1 = x - hi.astype(F32)
    mid = r1.astype(BF16)
    lo = (r1 - mid.astype(F32)).astype(BF16)
    return jnp.concatenate([hi, mid, lo], axis=-1)


def _rms(x, g):
    return x * lax.rsqrt(jnp.mean(x * x, axis=-1, keepdims=True) + NORM_EPS) * g


def _sigmoid(x):
    return 1.0 / (1.0 + jnp.exp(-x))


def _ada_kernel(c_ref, w_ref, b_ref, o_ref):
    c = c_ref[...]
    o_ref[...] = _dot((c * _sigmoid(c)).astype(BF16), w_ref[...]) + b_ref[...]


def _ada_mod(c_all, w_ada, b_ada):
    m, d = c_all.shape
    n = w_ada.shape[1]
    tn = 1536
    return pl.pallas_call(
        _ada_kernel, grid=(n // tn,),
        in_specs=[pl.BlockSpec((m, d), lambda i: (0, 0)), pl.BlockSpec((d, tn), lambda i: (0, i)),
                  pl.BlockSpec((1, tn), lambda i: (0, i))],
        out_specs=pl.BlockSpec((m, tn), lambda i: (0, i)),
        out_shape=jax.ShapeDtypeStruct((m, n), F32),
        compiler_params=_cparams(("arbitrary",)), name="ada_mod")(c_all, w_ada, b_ada)


def _proj_kernel(x_ref, sc_ref, sh_ref, g1_ref, cs_ref, win_ref, gqa_ref, wqb_ref, gkva_ref, wkn_ref,
                 qmla_ref, kq_ref, mla_ref, qn_ref, cmp_ref, cmps_ref, slc_ref, swa_ref, slcb_ref, swab_ref, gate_ref,
                 *, states_t):
    h = _rms(x_ref[...], g1_ref[...]) * (1.0 + sc_ref[...]) + sh_ref[...]
    proj = _dot(h.astype(BF16), win_ref[...])
    cs = cs_ref[...]
    cosp, sinp = cs[:, :LANE], cs[:, LANE:]
    qa = _rms(proj[:, _C_QA:_C_CKV], gqa_ref[...])
    q = _dot(qa.astype(BF16), wqb_ref[...])
    hw = MLA_HEADS * LANE
    for h_i in range(MLA_HEADS):
        lo = h_i * LANE
        qlat = _dot(q[:, lo:lo + LANE].astype(BF16), wkn_ref[h_i]) * MLA_QSCALE
        qr = (q[:, hw + lo:hw + lo + LANE] * cosp + q[:, 2 * hw + lo:2 * hw + lo + LANE] * sinp) * MLA_QSCALE
        qmla_ref[:, h_i * MLA_QW:h_i * MLA_QW + MLA_KV_LORA] = qlat.astype(BF16)
        qmla_ref[:, h_i * MLA_QW + MLA_KV_LORA:(h_i + 1) * MLA_QW] = qr.astype(BF16)
    ckv = _rms(proj[:, _C_CKV:_C_QN], gkva_ref[...])
    kr = proj[:, _C_KR:_C_KRR] * cosp + proj[:, _C_KRR:_C_G] * sinp
    kq_ref[:, :MLA_KV_LORA] = ckv.astype(BF16)
    kq_ref[:, MLA_KV_LORA:] = kr.astype(BF16)
    qn_ref[...] = (proj[:, _C_QN:_C_CMP] * NSA_QSCALE).astype(BF16)
    cmp = proj[:, _C_CMP:_C_SLC]
    slc = proj[:, _C_SLC:_C_SWA]
    swa = proj[:, _C_SWA:_C_KR]
    cmp_ref[...] = cmp
    if states_t:
        mla_ref[:MLA_KV_LORA, :] = ckv.T
        mla_ref[MLA_KV_LORA:, :] = kr.T[:MLA_ROPE]
        cmps_ref[...] = cmp.T
        slc_ref[...] = slc.T
        swa_ref[...] = swa.T
    else:
        mla_ref[:, :MLA_KV_LORA] = ckv
        mla_ref[:, MLA_KV_LORA:] = kr[:, :MLA_ROPE]
        cmps_ref[...] = cmp
        slc_ref[...] = slc
        swa_ref[...] = swa
    slcb_ref[...] = slc.astype(BF16)
    swab_ref[...] = swa.astype(BF16)
    gate_ref[...] = _sigmoid(proj[:, _C_G:_C_END])


def _proj(x, sc, sh, cs, W, *, rows_per_mod, cs_period_tiles, tm, states_t_batches=0):
    t, d = x.shape
    nt = t // tm
    if rows_per_mod == 1:
        tiles_per_mod = sc.shape[0] and (t // sc.shape[0]) // tm
        mod_map = lambda i: (i // tiles_per_mod, 0, 0)
    else:
        mod_map = lambda i: (i, 0, 0)
    cs_map = (lambda i: (i % cs_period_tiles, 0)) if cs_period_tiles > 1 else (lambda i: (0, 0))
    const2 = lambda i: (0, 0)
    row = lambda i: (i, 0)
    outs = [
        (MLA_HEADS * MLA_QW, BF16), (MLA_QW, BF16), (MLA_KV_LORA + MLA_ROPE, F32), (NSA_HEADS * LANE, BF16),
        (KV_COLS, F32), (KV_COLS, F32), (KV_COLS, F32), (KV_COLS, F32), (KV_COLS, BF16), (KV_COLS, BF16),
        (NSA_KV_HEADS * LANE, F32),
    ]
    state_outs = (2, 5, 6, 7) if states_t_batches else ()
    tiles_per_batch = (t // states_t_batches) // tm if states_t_batches else 0
    out_specs = [pl.BlockSpec((None, w, tm), lambda i: (i // tiles_per_batch, 0, i % tiles_per_batch))
                 if k in state_outs else pl.BlockSpec((tm, w), row) for k, (w, _) in enumerate(outs)]
    out_shape = [jax.ShapeDtypeStruct((states_t_batches, w, t // states_t_batches) if k in state_outs else (t, w), dt)
                 for k, (w, dt) in enumerate(outs)]
    return pl.pallas_call(
        functools.partial(_proj_kernel, states_t=bool(states_t_batches)), grid=(nt,),
        in_specs=[
            pl.BlockSpec((tm, d), row),
            pl.BlockSpec((None, rows_per_mod, d), mod_map),
            pl.BlockSpec((None, rows_per_mod, d), mod_map),
            pl.BlockSpec((1, d), const2),
            pl.BlockSpec((tm, 2 * LANE), cs_map),
            pl.BlockSpec(W["w_in"].shape, const2),
            pl.BlockSpec((1, MLA_Q_LORA), const2),
            pl.BlockSpec(W["w_qb"].shape, const2),
            pl.BlockSpec((1, MLA_KV_LORA), const2),
            pl.BlockSpec(W["w_kn"].shape, lambda i: (0, 0, 0)),
        ],
        out_specs=out_specs, out_shape=out_shape,
        compiler_params=_cparams(("arbitrary",)), name="proj_in",
    )(x, sc, sh, W["g_norm1"], cs, W["w_in"], W["g_q_a"], W["w_qb"], W["g_kv_a"], W["w_kn"])


def _mla_prompt_kernel(q_ref, k_ref, wv_ref, o_ref, m_ref, l_ref, acc_ref, *, tq, tk):
    i, j = pl.program_id(1), pl.program_id(2)
    nk = pl.num_programs(2)

    @pl.when(j == 0)
    def _():
        m_ref[...] = jnp.full_like(m_ref, NEG)
        l_ref[...] = jnp.zeros_like(l_ref)
        acc_ref[...] = jnp.zeros_like(acc_ref)

    def step(masked):
        k = k_ref[...]
        v = k[:, :MLA_KV_LORA]
        if masked:
            qpos = i * tq + lax.broadcasted_iota(I32, (tq, tk), 0)
            kpos = j * tk + lax.broadcasted_iota(I32, (tq, tk), 1)
            mask = kpos <= qpos
        ss = [_dot_nt(q_ref[:, h * MLA_QW:(h + 1) * MLA_QW], k) for h in range(MLA_HEADS)]
        ps, alphas = [], []
        for h in range(MLA_HEADS):
            r = slice(h * tq, (h + 1) * tq)
            s = jnp.where(mask, ss[h], NEG) if masked else ss[h]
            m_prev = m_ref[r]
            m_new = jnp.maximum(m_prev, _rowmax(s))
            p = jnp.exp2(s - m_new)
            if masked:
                p = jnp.where(mask, p, 0.0)
            alpha = jnp.exp2(m_prev - m_new)
            l_ref[r] = alpha * l_ref[r] + _rowsum(p)
            m_ref[r] = m_new
            ps.append(p.astype(BF16))
            alphas.append(alpha)
        for h in range(MLA_HEADS):
            r = slice(h * tq, (h + 1) * tq)
            acc_ref[r] = alphas[h] * acc_ref[r] + _dot(ps[h], v)

    first_key, last_key = j * tk, j * tk + tk - 1
    pl.when(last_key <= i * tq)(functools.partial(step, False))
    pl.when((first_key <= i * tq + tq - 1) & (last_key > i * tq))(functools.partial(step, True))

    @pl.when(j == nk - 1)
    def _():
        out = jnp.zeros(o_ref.shape, F32)
        for h in range(MLA_HEADS):
            r = slice(h * tq, (h + 1) * tq)
            l = l_ref[r]
            o_lat = acc_ref[r] / jnp.where(l > 0.0, l, 1.0)
            out = out + _dot(o_lat.astype(BF16), wv_ref[h])
        o_ref[...] = out.astype(o_ref.dtype)


def _mla_prompt(qmla, kq, wv, *, nb, seq, tq=256, tk=1024):
    tk = min(tk, seq)
    if seq % tk:
        tk = 512
    nq, nk = seq // tq, seq // tk
    ow = MLA_HEADS * MLA_V

    def k_map(b, i, j):
        return (b * nk + jnp.minimum(j, (i * tq + tq - 1) // tk), 0)

    return pl.pallas_call(
        functools.partial(_mla_prompt_kernel, tq=tq, tk=tk), grid=(nb, nq, nk),
        in_specs=[pl.BlockSpec((tq, MLA_HEADS * MLA_QW), lambda b, i, j: (b * nq + i, 0)),
                  pl.BlockSpec((tk, MLA_QW), k_map),
                  pl.BlockSpec(wv.shape, lambda b, i, j: (0, 0, 0))],
        out_specs=pl.BlockSpec((tq, ow), lambda b, i, j: (b * nq + i, 0)),
        out_shape=jax.ShapeDtypeStruct((nb * seq, ow), BF16),
        scratch_shapes=[pltpu.VMEM((MLA_HEADS * tq, 1), F32), pltpu.VMEM((MLA_HEADS * tq, 1), F32),
                        pltpu.VMEM((MLA_HEADS * tq, MLA_KV_LORA), F32)],
        compiler_params=_cparams(("arbitrary", "arbitrary", "arbitrary")), name="mla_prompt",
    )(qmla, kq, wv)


def _mla_sample_kernel(pt_ref, q_ref, kn_ref, wv_ref, *rest, g, nq, nbs):
    pages = rest[:nbs * g]
    o_ref, m_ref, l_ref, acc_ref = rest[nbs * g:]
    s_idx = pl.program_id(1)
    rows = MLA_HEADS * nq

    @pl.when(s_idx == 0)
    def _():
        m_ref[...] = jnp.full_like(m_ref, NEG)
        l_ref[...] = jnp.zeros_like(l_ref)
        acc_ref[...] = jnp.zeros_like(acc_ref)

    qs = []
    for bl in range(nbs):
        r = slice(bl * rows, (bl + 1) * rows)
        q = jnp.concatenate([q_ref[bl, :, h * MLA_QW:(h + 1) * MLA_QW] for h in range(MLA_HEADS)], axis=0)
        qs.append(q)
        qlat, qr = q[:, :MLA_KV_LORA], q[:, MLA_KV_LORA:MLA_KV_LORA + MLA_ROPE]
        pg = pages[bl * g:(bl + 1) * g]
        ckv_t = jnp.concatenate([pg[u][:MLA_KV_LORA].astype(BF16) for u in range(g)], axis=1)
        kr_t = jnp.concatenate([pg[u][MLA_KV_LORA:].astype(BF16) for u in range(g)], axis=1)
        s = _dot(qlat, ckv_t) + _dot(qr, kr_t)
        m_prev = m_ref[r]
        m_new = jnp.maximum(m_prev, _rowmax(s))
        p = jnp.exp2(s - m_new)
        alpha = jnp.exp2(m_prev - m_new)
        l_ref[r] = alpha * l_ref[r] + _rowsum(p)
        acc_ref[r] = alpha * acc_ref[r] + _dot_nt(p.astype(BF16), ckv_t)
        m_ref[r] = m_new

    @pl.when(s_idx == pl.num_programs(1) - 1)
    def _():
        for bl in range(nbs):
            r = slice(bl * rows, (bl + 1) * rows)
            kn = kn_ref[bl]
            sn = _dot_nt(qs[bl], kn)
            qi = lax.broadcasted_iota(I32, sn.shape, 0) % nq
            kt = lax.broadcasted_iota(I32, sn.shape, 1)
            mask = kt <= qi
            sn = jnp.where(mask, sn, NEG)
            m_prev = m_ref[r]
            m_new = jnp.maximum(m_prev, jnp.max(sn, axis=-1, keepdims=True))
            p = jnp.where(mask, jnp.exp2(sn - m_new), 0.0)
            alpha = jnp.exp2(m_prev - m_new)
            l = alpha * l_ref[r] + _rowsum(p)
            acc = alpha * acc_ref[r] + _dot(p.astype(BF16), kn[:, :MLA_KV_LORA])
            o_lat = (acc / jnp.where(l > 0.0, l, 1.0)).astype(BF16)
            res = _dot(o_lat, wv_ref[...])
            head_of_lane = lax.broadcasted_iota(I32, (nq, res.shape[1]), 1) // MLA_V
            out = jnp.zeros((nq, res.shape[1]), F32)
            for h in range(MLA_HEADS):
                out = out + jnp.where(head_of_lane == h, res[h * nq:(h + 1) * nq], 0.0)
            o_ref[bl] = out.astype(o_ref.dtype)


def _page_specs(g, block, n_lane_blocks=1, flat=False, nbs=1):
    def index(b, s, pt, bl, u, c):
        return ((pt[s * g + u] if flat else pt[b * nbs + bl, s * g + u]), 0, c)

    return [pl.BlockSpec(block, functools.partial(index, bl=bl, u=u, c=c))
            for bl in range(nbs) for u in range(g) for c in range(n_lane_blocks)]


def _mla_sample(page_table, qmla, kq_new_pad, wv, cache_mla, *, g, nbs):
    nb, nq = qmla.shape[0], qmla.shape[1]
    n_pages = page_table.shape[1]
    ow = MLA_HEADS * MLA_V
    rows = nbs * MLA_HEADS * nq
    bmap = lambda b, s, pt: (b, 0, 0)
    gs = pltpu.PrefetchScalarGridSpec(
        num_scalar_prefetch=1, grid=(nb // nbs, n_pages // g),
        in_specs=[pl.BlockSpec((nbs, nq, MLA_HEADS * MLA_QW), bmap),
                  pl.BlockSpec((nbs, LANE, MLA_QW), bmap),
                  pl.BlockSpec(wv.shape, lambda b, s, pt: (0, 0))]
        + _page_specs(g, (None,) + cache_mla.shape[1:], nbs=nbs),
        out_specs=pl.BlockSpec((nbs, nq, ow), bmap),
        scratch_shapes=[pltpu.VMEM((rows, 1), F32), pltpu.VMEM((rows, 1), F32), pltpu.VMEM((rows, MLA_KV_LORA), F32)],
    )
    return pl.pallas_call(
        functools.partial(_mla_sample_kernel, g=g, nq=nq, nbs=nbs), grid_spec=gs,
        out_shape=jax.ShapeDtypeStruct((nb, nq, ow), BF16),
        compiler_params=_cparams(("arbitrary", "arbitrary")), name="mla_sample",
    )(page_table, qmla, kq_new_pad, wv, *([cache_mla] * (nbs * g)))


def _compress_kernel(pl_ref, w_ref, pe_ref, w1_ref, w2_ref, *rest, g, transposed):
    pages = rest[:g]
    kc_ref, vc_ref, xs_ref = rest[g:]
    half_w = NSA_KV_HEADS * 2 * CMP_HIDDEN
    nh = kc_ref.shape[0]
    for u in range(g):
        x = pages[u][...]
        for j in range(2):
            if transposed:
                xs_ref[j, u * LANE:(u + 1) * LANE, :] = x[j * LANE:(j + 1) * LANE, :].T
            else:
                xs_ref[j, u * LANE:(u + 1) * LANE, :] = x[:, j * LANE:(j + 1) * LANE]
    for j, out_ref in enumerate((kc_ref, vc_ref)):
        ab = jnp.zeros((nh, half_w), F32)
        for pp in range(CMP_STRIDE // 2):
            xa = xs_ref[j, pl.ds(2 * pp, nh, stride=CMP_STRIDE), :]
            xb = xs_ref[j, pl.ds(2 * pp + 1, nh, stride=CMP_STRIDE), :]
            ab = ab + _dot(jnp.concatenate([xa, xb], axis=1).astype(BF16), w_ref[j, pp])
        pe_term = _dot(pe_ref[j].astype(BF16), w1_ref[j])[0:1]
        acc = jnp.zeros((nh, LANE), F32)
        for kv in range(NSA_KV_HEADS):
            base = kv * 2 * CMP_HIDDEN
            first = ab[:, base:base + CMP_HIDDEN]
            second = ab[:, base + CMP_HIDDEN:base + 2 * CMP_HIDDEN]
            hid = first + pltpu.roll(second, nh - 1, 0) + pe_term
            hid = 0.5 * hid * (1.0 + lax.erf(hid * math.sqrt(0.5)))
            acc = acc + _dot(hid.astype(BF16), w2_ref[j, kv])
        out_ref[...] = acc.astype(out_ref.dtype)


def _compress(page_list, pool, w_pair, pe8, w1r, w2pad, *, nb, transposed):
    g = page_list.shape[0] // nb
    nh = g * (LANE // CMP_STRIDE)
    c3 = lambda b, s, pt: (0, 0, 0)
    gs = pltpu.PrefetchScalarGridSpec(
        num_scalar_prefetch=1, grid=(1, nb),
        in_specs=[pl.BlockSpec(w_pair.shape, lambda b, s, pt: (0, 0, 0, 0)), pl.BlockSpec(pe8.shape, c3),
                  pl.BlockSpec(w1r.shape, c3), pl.BlockSpec(w2pad.shape, lambda b, s, pt: (0, 0, 0, 0))]
        + _page_specs(g, (None,) + pool.shape[1:], flat=True),
        out_specs=[pl.BlockSpec((None, nh, LANE), lambda b, s, pt: (s, 0, 0))] * 2,
        scratch_shapes=[pltpu.VMEM((2, g * LANE, LANE), F32)],
    )
    return pl.pallas_call(
        functools.partial(_compress_kernel, g=g, transposed=transposed), grid_spec=gs,
        out_shape=[jax.ShapeDtypeStruct((nb, nh, LANE), BF16)] * 2,
        compiler_params=_cparams(("arbitrary", "arbitrary")), name="compress",
    )(page_list, w_pair, pe8, w1r, w2pad, *([pool] * g))


def _topk_mask(v, k, axis=1):
    lane = lax.broadcasted_iota(I32, v.shape, axis)
    sel = jnp.zeros(v.shape, F32)
    for _ in range(k):
        m = jnp.max(v, axis=axis, keepdims=True)
        idx = jnp.min(jnp.where(v == m, lane, 1 << 20), axis=axis, keepdims=True)
        pick = lane == idx
        sel = jnp.where(pick & (m > -jnp.inf), 1.0, sel)
        v = jnp.where(pick, -jnp.inf, v)
    return sel


def _softmax_masked(s, mask):
    s = jnp.where(mask, s, NEG)
    m = _rowmax(s)
    p = jnp.where(mask, jnp.exp2(s - m), 0.0)
    l = _rowsum(p)
    return p / jnp.where(l > 0.0, l, 1.0)


def _flash_step(s, mask, v, m_ref, l_ref, acc_ref, v_keys_on_lanes=False):
    s = jnp.where(mask, s, NEG)
    m_prev = m_ref[...]
    m_new = jnp.maximum(m_prev, _rowmax(s))
    p = jnp.where(mask, jnp.exp2(s - m_new), 0.0)
    alpha = jnp.exp2(m_prev - m_new)
    l_ref[...] = alpha * l_ref[...] + _rowsum(p)
    pv = _dot_nt(p.astype(BF16), v) if v_keys_on_lanes else _dot(p.astype(BF16), v)
    acc_ref[...] = alpha * acc_ref[...] + pv
    m_ref[...] = m_new


def _flash_init(m_ref, l_ref, acc_ref):
    m_ref[...] = jnp.full_like(m_ref, NEG)
    l_ref[...] = jnp.zeros_like(l_ref)
    acc_ref[...] = jnp.zeros_like(acc_ref)


def _flash_out(l_ref, acc_ref):
    l = l_ref[...]
    return acc_ref[...] / jnp.where(l > 0.0, l, 1.0)


def _nsa_prompt_kernel(q_ref, kc_ref, vc_ref, bc_ref, ovl_ref, slc_ref, swa_ref, tt_ref, gate_ref, o_ref,
                       m_ref, l_ref, acc_ref, *, tq, tkf, n_cmp, n_sel, seq_len):
    i = pl.program_id(1)
    rows = NSA_HEADS * tq
    n_sel_rows = -(-n_sel // 8) * 8
    q = jnp.concatenate([q_ref[:, h * LANE:(h + 1) * LANE] for h in range(NSA_HEADS)], axis=0)
    nhp = kc_ref.shape[0]
    pos_r = i * tq + lax.broadcasted_iota(I32, (rows, 1), 0) % tq

    wu = lax.broadcasted_iota(I32, (LANE, nhp), 0)
    wn = lax.broadcasted_iota(I32, (LANE, nhp), 1)
    shift = jnp.where(wn == i * (tq // CMP_STRIDE) - CMP_WIN_LO + wu, 1.0, 0.0).astype(BF16)
    far_col = tt_ref[2][:, :1]
    bias_c = _dot(_split3(bc_ref[...]), jnp.concatenate([shift] * 3, axis=0)) + far_col
    s = _dot_nt(q, kc_ref[...]) + bias_c
    n_idx = lax.broadcasted_iota(I32, (rows, nhp), 1)
    mask_c = (n_idx * CMP_STRIDE + CMP_LEN - 1 <= pos_r) & (n_idx < n_cmp)
    p_cmp = _softmax_masked(s, mask_c)
    o_cmp = _dot(p_cmp.astype(BF16), vc_ref[...])

    imps = []
    for kv in range(NSA_KV_HEADS):
        lo = kv * NSA_GROUP * tq
        psum = p_cmp[lo:lo + tq]
        for g in range(1, NSA_GROUP):
            psum = psum + p_cmp[lo + g * tq:lo + (g + 1) * tq]
        imps.append(_dot_nt(ovl_ref[...], _split3(psum))[:n_sel_rows])
    imp_t = jnp.concatenate(imps, axis=1)
    blk = lax.broadcasted_iota(I32, imp_t.shape, 0)
    cur = (i * tq + lax.broadcasted_iota(I32, imp_t.shape, 1) % tq) // SEL_BLOCK
    forced = (blk == 0) | (blk == cur) | (blk == cur - 1)
    future = (blk > cur) | (blk >= n_sel)
    imp_t = jnp.where(future, -jnp.inf, jnp.where(forced, jnp.inf, imp_t))
    sel_t = _topk_mask(imp_t, min(SEL_TOPN, n_sel), axis=0)
    if n_sel_rows < LANE:
        sel_t = jnp.concatenate([sel_t, jnp.zeros((LANE - n_sel_rows, NSA_KV_HEADS * tq), F32)], axis=0)
    sels = [sel_t[:, kv * tq:(kv + 1) * tq].T.astype(BF16) for kv in range(NSA_KV_HEADS)]

    near0 = pl.multiple_of(jnp.maximum(i - 1, 0) * tq, tq)
    near_bias = jnp.concatenate([tt_ref[jnp.where(i == 0, 0, 1)], tt_ref[0]], axis=1)
    pos_q = i * tq + lax.broadcasted_iota(I32, (tq, 1), 0)
    d_near = pos_q - (near0 + lax.broadcasted_iota(I32, (tq, 2 * tq), 1))

    def heads(pens):
        return jnp.concatenate([p_ for p_ in pens for _ in range(NSA_GROUP)], axis=0)

    def block_sel(first_key, n_keys, key_limit):
        sb = lax.broadcasted_iota(I32, (LANE, n_keys), 0)
        kt = lax.broadcasted_iota(I32, (LANE, n_keys), 1)
        hit = (sb == first_key // SEL_BLOCK + kt // SEL_BLOCK) & (first_key + kt < key_limit)
        expand = jnp.where(hit, 1.0, 0.0).astype(BF16)
        return [_dot(sel, expand) > 0.5 for sel in sels]

    n_far_w = WINDOW - tq
    far0 = pl.multiple_of(jnp.maximum(i - WINDOW // tq, 0) * tq, tq)
    kv_n = swa_ref[pl.ds(near0, 2 * tq), :]
    kv_f = swa_ref[pl.ds(far0, n_far_w), :]
    kpos_f = far0 + lax.broadcasted_iota(I32, (tq, n_far_w), 1)
    pen_n = jnp.where((d_near >= 0) & (d_near < WINDOW), 0.0, NEG)
    pen_f = jnp.where((kpos_f < near0) & (pos_q - kpos_f < WINDOW), 0.0, NEG)
    s_n = _dot_nt(q, kv_n[:, :LANE]) + near_bias + heads([pen_n] * NSA_KV_HEADS)
    s_f = _dot_nt(q, kv_f[:, :LANE]) + far_col + heads([pen_f] * NSA_KV_HEADS)
    m_w = jnp.maximum(_rowmax(s_n), _rowmax(s_f))
    p_n = jnp.exp2(s_n - m_w)
    p_f = jnp.exp2(s_f - m_w)
    l_w = _rowsum(p_n) + _rowsum(p_f)
    o_swa = (_dot(p_n.astype(BF16), kv_n[:, LANE:]) + _dot(p_f.astype(BF16), kv_f[:, LANE:])) / l_w

    kv_n = slc_ref[pl.ds(near0, 2 * tq), :]
    pens = [jnp.where(hit & (d_near >= 0), 0.0, NEG) for hit in block_sel(near0, 2 * tq, seq_len)]
    s_n = _dot_nt(q, kv_n[:, :LANE]) + near_bias + heads(pens)
    m_s = _rowmax(s_n)
    p_n = jnp.exp2(s_n - m_s)
    m_ref[...] = m_s
    l_ref[...] = _rowsum(p_n)
    acc_ref[...] = _dot(p_n.astype(BF16), kv_n[:, LANE:])

    n_far = (near0 + tkf - 1) // tkf
    half = NSA_GROUP * tq

    def far_logits(c):
        first = pl.multiple_of(jnp.minimum(c, jnp.maximum(n_far - 1, 0)) * tkf, tkf)
        k = slc_ref[pl.ds(first, tkf), :LANE]
        return tuple(_dot_nt(q[kvh * half:(kvh + 1) * half], k) for kvh in range(NSA_KV_HEADS))

    def slc_far(c, qk):
        qk_next = far_logits(c + 1)
        first = pl.multiple_of(c * tkf, tkf)
        v = slc_ref[pl.ds(first, tkf), LANE:]
        hits = block_sel(first, tkf, near0)
        for kvh in range(NSA_KV_HEADS):
            r = slice(kvh * half, (kvh + 1) * half)
            pen = jnp.where(hits[kvh], 0.0, NEG)
            s_ = qk[kvh] + far_col[r] + jnp.concatenate([pen] * NSA_GROUP, axis=0)
            m_prev = m_ref[r]
            m_new = jnp.maximum(m_prev, _rowmax(s_))
            p = jnp.exp2(s_ - m_new)
            alpha = jnp.exp2(m_prev - m_new)
            l_ref[r] = alpha * l_ref[r] + _rowsum(p)
            acc_ref[r] = alpha * acc_ref[r] + _dot(p.astype(BF16), v)
            m_ref[r] = m_new
        return qk_next

    lax.fori_loop(0, n_far, slc_far, far_logits(0))
    o_slc = acc_ref[...] / l_ref[...]

    gates = gate_ref[...]
    for h in range(NSA_HEADS):
        kv, g = divmod(h, NSA_GROUP)
        r = slice(h * tq, (h + 1) * tq)
        c = kv * LANE + g
        o = (gates[:, c:c + 1] * o_cmp[r] + gates[:, c + NSA_GROUP:c + NSA_GROUP + 1] * o_slc[r]
             + gates[:, c + 2 * NSA_GROUP:c + 2 * NSA_GROUP + 1] * o_swa[r])
        o_ref[:, h * LANE:(h + 1) * LANE] = o.astype(o_ref.dtype)


def _nsa_prompt(qn, kc, vc, bias_c, ovl, slc_b, swa_b, tt, gates, *, nb, seq, n_cmp, n_sel):
    tq = tt.shape[2]
    nq = seq // tq
    rows = NSA_HEADS * tq
    hw = NSA_HEADS * LANE
    nhp = kc.shape[1]
    return pl.pallas_call(
        functools.partial(_nsa_prompt_kernel, tq=tq, tkf=min(512, seq), n_cmp=n_cmp, n_sel=n_sel, seq_len=seq),
        grid=(nb, nq),
        in_specs=[
            pl.BlockSpec((tq, hw), lambda b, i: (b * nq + i, 0)),
            pl.BlockSpec((None, nhp, LANE), lambda b, i: (b, 0, 0)),
            pl.BlockSpec((None, nhp, LANE), lambda b, i: (b, 0, 0)),
            pl.BlockSpec(bias_c.shape, lambda b, i: (0, 0)),
            pl.BlockSpec(ovl.shape, lambda b, i: (0, 0)),
            pl.BlockSpec((None, seq, KV_COLS), lambda b, i: (b, 0, 0)),
            pl.BlockSpec((None, seq, KV_COLS), lambda b, i: (b, 0, 0)),
            pl.BlockSpec(tt.shape, lambda b, i: (0, 0, 0)),
            pl.BlockSpec((tq, NSA_KV_HEADS * LANE), lambda b, i: (b * nq + i, 0)),
        ],
        out_specs=pl.BlockSpec((tq, hw), lambda b, i: (b * nq + i, 0)),
        out_shape=jax.ShapeDtypeStruct((nb * seq, hw), BF16),
        scratch_shapes=[pltpu.VMEM((rows, 1), F32), pltpu.VMEM((rows, 1), F32), pltpu.VMEM((rows, LANE), F32)],
        compiler_params=_cparams(("arbitrary", "arbitrary")), name="nsa_prompt",
    )(qn, kc, vc, bias_c, ovl, slc_b, swa_b, tt, gates)


def _nsa_sample_kernel(pt_ref, q_ref, kc_ref, vc_ref, bc_ref, ovl_ref, bs_ref, sn_ref, bsn_ref, st_ref, wn_ref,
                       bw_ref, bwn_ref, gate_ref, *rest, g, nq, nbs, n_cmp, n_sel, buf):
    o_ref, sel_ref, ocmp_ref, oswa_ref, m_ref, l_ref, acc_ref = rest[nbs * g:]
    s_idx = pl.program_id(1)
    last = pl.num_programs(1) - 1
    rows = NSA_HEADS * nq
    qi = lax.broadcasted_iota(I32, (rows, 1), 0) % nq
    qs = [jnp.concatenate([q_ref[bl, :, h * LANE:(h + 1) * LANE] for h in range(NSA_HEADS)], axis=0)
          for bl in range(nbs)]

    def first_step(bl):
        q = qs[bl]
        kc_r, vc_r, st_r, wn_r = kc_ref.at[bl], vc_ref.at[bl], st_ref.at[bl], wn_ref.at[bl]
        _nsa_sample_first(q, qi, kc_r, vc_r, bc_ref, ovl_ref, st_r, wn_r, bw_ref, bwn_ref, sel_ref.at[bl], ocmp_ref.at[bl],
                          oswa_ref.at[bl], m_ref.at[bl], l_ref.at[bl], acc_ref.at[bl], nq=nq, n_cmp=n_cmp, n_sel=n_sel, buf=buf)

    @pl.when(s_idx == 0)
    def _():
        for bl in range(nbs):
            first_step(bl)

    nk = g * rest[0].shape[1]
    sb = lax.broadcasted_iota(I32, (sel_ref.shape[2], nk), 0)
    kt = lax.broadcasted_iota(I32, (sel_ref.shape[2], nk), 1)
    expand = jnp.where(sb == s_idx * (nk // SEL_BLOCK) + kt // SEL_BLOCK, 1.0, 0.0).astype(BF16)
    for bl in range(nbs):
        pages = rest[bl * g:(bl + 1) * g]
        kcat = jnp.concatenate([pages[u][:LANE].astype(BF16) for u in range(g)], axis=1)
        vcat = jnp.concatenate([pages[u][LANE:].astype(BF16) for u in range(g)], axis=1)
        sel_kq = jnp.concatenate([sel_ref[bl, kv * NSA_GROUP * nq:kv * NSA_GROUP * nq + nq] for kv in range(NSA_KV_HEADS)],
                                 axis=0)
        pen_kq = jnp.where(_dot(sel_kq, expand) > 0.5, 0.0, NEG)
        pen = jnp.concatenate([pen_kq[kv * nq:(kv + 1) * nq] for kv in range(NSA_KV_HEADS) for _ in range(NSA_GROUP)], axis=0)
        s = _dot(qs[bl], kcat) + bs_ref[jnp.where(s_idx == last, 1, 0)] + pen
        m_prev = m_ref[bl]
        m_new = jnp.maximum(m_prev, _rowmax(s))
        p = jnp.exp2(s - m_new)
        alpha = jnp.exp2(m_prev - m_new)
        l_ref[bl] = alpha * l_ref[bl] + _rowsum(p)
        acc_ref[bl] = alpha * acc_ref[bl] + _dot_nt(p.astype(BF16), vcat)
        m_ref[bl] = m_new

    @pl.when(s_idx == last)
    def _():
        for bl in range(nbs):
            q = qs[bl]
            m_r, l_r, acc_r = m_ref.at[bl], l_ref.at[bl], acc_ref.at[bl]
            sn = sn_ref[bl]
            s_n = _dot_nt(q, sn[:, :LANE]) + bsn_ref[...]
            mask_n = (lax.broadcasted_iota(I32, (rows, LANE), 1) <= qi) & (sel_ref[bl, :, n_sel - 1:n_sel] > 0.5)
            _flash_step(s_n, mask_n, sn[:, LANE:], m_r, l_r, acc_r)
            o_slc = _flash_out(l_r, acc_r)
            o_cmp, o_swa = ocmp_ref[bl], oswa_ref[bl]
            gates = gate_ref[bl]
            for h in range(NSA_HEADS):
                kv, gg = divmod(h, NSA_GROUP)
                r = slice(h * nq, (h + 1) * nq)
                c = kv * LANE + gg
                o = (gates[:, c:c + 1] * o_cmp[r] + gates[:, c + NSA_GROUP:c + NSA_GROUP + 1] * o_slc[r]
                     + gates[:, c + 2 * NSA_GROUP:c + 2 * NSA_GROUP + 1] * o_swa[r])
                o_ref[bl, :, h * LANE:(h + 1) * LANE] = o.astype(o_ref.dtype)


def _nsa_sample_first(q, qi, kc_ref, vc_ref, bc_ref, ovl_ref, st_ref, wn_ref, bw_ref, bwn_ref, sel_ref, ocmp_ref, oswa_ref,
                      m_ref, l_ref, acc_ref, *, nq, n_cmp, n_sel, buf):
    rows = NSA_HEADS * nq

    def compressed_and_select():
        nhp = kc_ref.shape[0]
        s = _dot_nt(q, kc_ref[...]) + bc_ref[...]
        mask_c = lax.broadcasted_iota(I32, (rows, nhp), 1) < n_cmp
        p_cmp = _softmax_masked(s, mask_c)
        ocmp_ref[...] = _dot(p_cmp.astype(BF16), vc_ref[...])
        psums = []
        for kv in range(NSA_KV_HEADS):
            ps = p_cmp[kv * NSA_GROUP * nq:kv * NSA_GROUP * nq + nq]
            for gg in range(1, NSA_GROUP):
                lo = (kv * NSA_GROUP + gg) * nq
                ps = ps + p_cmp[lo:lo + nq]
            psums.append(ps)
        n_kq = NSA_KV_HEADS * nq
        psum = jnp.concatenate(psums + [jnp.zeros((LANE - n_kq, nhp), F32)], axis=0)
        n_blk = ovl_ref.shape[0]
        n_sel_rows = -(-n_sel // 8) * 8
        imp_t = _dot_nt(ovl_ref[...], _split3(psum))[:n_sel_rows]
        blk = lax.broadcasted_iota(I32, imp_t.shape, 0)
        cur = n_sel - 1
        forced = (blk == 0) | (blk == cur) | (blk == cur - 1)
        imp_t = jnp.where(blk >= n_sel, -jnp.inf, jnp.where(forced, jnp.inf, imp_t))
        sel_t = _topk_mask(imp_t, min(SEL_TOPN, n_sel), axis=0)
        if n_sel_rows < n_blk:
            sel_t = jnp.concatenate([sel_t, jnp.zeros((n_blk - n_sel_rows, LANE), F32)], axis=0)
        sel = sel_t.T.astype(BF16)
        sel_ref[...] = jnp.concatenate(
            [sel[kv * nq:(kv + 1) * nq] for kv in range(NSA_KV_HEADS) for _ in range(NSA_GROUP)], axis=0)

    def window():
        st = st_ref[...]
        s_w = _dot(q, st[:LANE].astype(BF16)) + bw_ref[...]
        d_w = buf + qi - lax.broadcasted_iota(I32, (rows, buf), 1)
        mask_w = (d_w >= 0) & (d_w < WINDOW)
        wn = wn_ref[...]
        s_n = _dot_nt(q, wn[:, :LANE]) + bwn_ref[...]
        mask_n = lax.broadcasted_iota(I32, (rows, LANE), 1) <= qi
        s_w = jnp.where(mask_w, s_w, NEG)
        s_n = jnp.where(mask_n, s_n, NEG)
        m = jnp.maximum(_rowmax(s_w), _rowmax(s_n))
        p_w = jnp.where(mask_w, jnp.exp2(s_w - m), 0.0)
        p_n = jnp.where(mask_n, jnp.exp2(s_n - m), 0.0)
        l = _rowsum(p_w) + _rowsum(p_n)
        o = _dot_nt(p_w.astype(BF16), st[LANE:].astype(BF16)) + _dot(p_n.astype(BF16), wn[:, LANE:])
        oswa_ref[...] = o / jnp.where(l > 0.0, l, 1.0)

    compressed_and_select()
    window()
    _flash_init(m_ref, l_ref, acc_ref)


def _nsa_sample(page_table, qn, kc, vc, bias_c, ovl, bias_s, slc_new, bias_sn, state_swa, swa_new, bias_w, bias_wn,
                gates, cache_slc, *, g, nbs, n_cmp, n_sel):
    nb, nq = qn.shape[0], qn.shape[1]
    n_pages = page_table.shape[1]
    rows = NSA_HEADS * nq
    buf = state_swa.shape[2]
    hw = NSA_HEADS * LANE
    bmap = lambda b, s, pt: (b, 0, 0)
    c2 = lambda b, s, pt: (0, 0)
    per_elem = lambda a: pl.BlockSpec((nbs,) + a.shape[1:], bmap)
    gs = pltpu.PrefetchScalarGridSpec(
        num_scalar_prefetch=1, grid=(nb // nbs, n_pages // g),
        in_specs=[
            per_elem(qn), per_elem(kc), per_elem(vc),
            pl.BlockSpec(bias_c.shape, c2),
            pl.BlockSpec(ovl.shape, c2),
            pl.BlockSpec(bias_s.shape, lambda b, s, pt: (0, 0, 0)),
            per_elem(slc_new),
            pl.BlockSpec(bias_sn.shape, c2),
            per_elem(state_swa), per_elem(swa_new),
            pl.BlockSpec(bias_w.shape, c2),
            pl.BlockSpec(bias_wn.shape, c2),
            per_elem(gates),
        ] + _page_specs(g, (None,) + cache_slc.shape[1:], nbs=nbs),
        out_specs=pl.BlockSpec((nbs, nq, hw), bmap),
        scratch_shapes=[pltpu.VMEM((nbs, rows, ovl.shape[0]), BF16), pltpu.VMEM((nbs, rows, LANE), F32),
                        pltpu.VMEM((nbs, rows, LANE), F32), pltpu.VMEM((nbs, rows, 1), F32), pltpu.VMEM((nbs, rows, 1), F32),
                        pltpu.VMEM((nbs, rows, LANE), F32)],
    )
    return pl.pallas_call(
        functools.partial(_nsa_sample_kernel, g=g, nq=nq, nbs=nbs, n_cmp=n_cmp, n_sel=n_sel, buf=buf), grid_spec=gs,
        out_shape=jax.ShapeDtypeStruct((nb, nq, hw), BF16),
        compiler_params=_cparams(("arbitrary", "arbitrary")), name="nsa_sample",
    )(page_table, qn, kc, vc, bias_c, ovl, bias_s, slc_new, bias_sn, state_swa, swa_new, bias_w, bias_wn, gates,
      *([cache_slc] * (nbs * g)))


def _outproj_kernel(x_ref, om_ref, on_ref, g1_ref, sc_ref, sh_ref, wom_ref, won_ref, gn2_ref, wr_ref, br_ref,
                    x1_ref, h2_ref, ti_ref, tw_ref):
    mix = _dot(om_ref[...], wom_ref[...]) + _dot(on_ref[...], won_ref[...])
    x1 = x_ref[...] + g1_ref[...] * mix
    x1_ref[...] = x1
    h2 = _rms(x1, gn2_ref[...]) * (1.0 + sc_ref[...]) + sh_ref[...]
    h2_ref[...] = h2.astype(h2_ref.dtype)
    h3, w3 = _split3(h2), _split3(wr_ref[...])
    d = h2.shape[1]
    h_cat = jnp.concatenate([h3[:, :d], h3[:, :d], h3[:, d:2 * d]], axis=1)
    w_cat = jnp.concatenate([w3[:, :d], w3[:, d:2 * d], w3[:, :d]], axis=1)
    logits = _dot_nt(h_cat, w_cat) + br_ref[...]
    lane = lax.broadcasted_iota(I32, logits.shape, 1)
    v = logits
    vals, idxs = [], []
    for _ in range(TOP_K):
        m = jnp.max(v, axis=-1, keepdims=True)
        idx = jnp.min(jnp.where(v == m, lane, 1 << 20), axis=-1, keepdims=True)
        vals.append(m)
        idxs.append(idx)
        v = jnp.where(lane == idx, -jnp.inf, v)
    es = [jnp.exp(m - vals[0]) for m in vals]
    tot = es[0]
    for e in es[1:]:
        tot = tot + e
    ti = jnp.zeros(logits.shape, I32)
    tw = jnp.zeros(logits.shape, F32)
    for k in range(TOP_K):
        ti = jnp.where(lane == k, idxs[k], ti)
        tw = jnp.where(lane == k, es[k] / tot, tw)
    ti_ref[...] = ti
    tw_ref[...] = tw


def _outproj(x, o_mla, o_nsa, g1, sc, sh, W, *, rows_per_mod, tm):
    t, d = x.shape
    nt = t // tm
    if rows_per_mod == 1:
        tiles_per_mod = (t // g1.shape[0]) // tm
        mod_map = lambda i: (i // tiles_per_mod, 0, 0)
    else:
        mod_map = lambda i: (i, 0, 0)
    row = lambda i: (i, 0)
    c2 = lambda i: (0, 0)
    mod = pl.BlockSpec((None, rows_per_mod, d), mod_map)
    return pl.pallas_call(
        _outproj_kernel, grid=(nt,),
        in_specs=[pl.BlockSpec((tm, d), row), pl.BlockSpec((tm, o_mla.shape[1]), row), pl.BlockSpec((tm, o_nsa.shape[1]), row),
                  mod, mod, mod, pl.BlockSpec(W["w_o_mla"].shape, c2), pl.BlockSpec(W["w_o_nsa"].shape, c2),
                  pl.BlockSpec((1, d), c2), pl.BlockSpec(W["w_router"].shape, c2), pl.BlockSpec((1, LANE), c2)],
        out_specs=[pl.BlockSpec((tm, d), row), pl.BlockSpec((tm, d), row), pl.BlockSpec((tm, LANE), row),
                   pl.BlockSpec((tm, LANE), row)],
        out_shape=[jax.ShapeDtypeStruct((t, d), F32), jax.ShapeDtypeStruct((t, d), F32),
                   jax.ShapeDtypeStruct((t, LANE), I32), jax.ShapeDtypeStruct((t, LANE), F32)],
        compiler_params=_cparams(("arbitrary",)), name="out_proj",
    )(x, o_mla, o_nsa, g1, sc, sh, W["w_o_mla"], W["w_o_nsa"], W["g_norm2"], W["w_router"], W["b_router"])


def _expert_kernel(te_ref, tv_ref, x_ref, wgu_ref, bgu_ref, wd_ref, bd_ref, o_ref, wgu_bf, wd_bf):
    t = pl.program_id(0)

    @pl.when((t == 0) | (te_ref[t] != te_ref[jnp.maximum(t - 1, 0)]))
    def _():
        wgu_bf[...] = wgu_ref[...].astype(BF16)
        wd_bf[...] = wd_ref[...].astype(BF16)

    @pl.when(tv_ref[t] > 0)
    def _():
        d_ff = wd_ref.shape[0]
        gu = _dot(x_ref[...].astype(BF16), wgu_bf[...]) + bgu_ref[...]
        glu = jnp.minimum(gu[:, :d_ff], SWIGLU_LIMIT)
        lin = jnp.clip(gu[:, d_ff:], -SWIGLU_LIMIT, SWIGLU_LIMIT)
        act = glu * _sigmoid(SWIGLU_ALPHA * glu) * (lin + 1.0)
        o_ref[...] = _dot(act.astype(BF16), wd_bf[...]) + bd_ref[...]

    @pl.when(tv_ref[t] == 0)
    def _():
        o_ref[...] = jnp.zeros_like(o_ref)


def _experts(tile_expert, tile_valid, x_sorted, W, *, tm):
    ns, d = x_sorted.shape
    gs = pltpu.PrefetchScalarGridSpec(
        num_scalar_prefetch=2, grid=(ns // tm,),
        in_specs=[pl.BlockSpec((tm, d), lambda t, te, tv: (t, 0)),
                  pl.BlockSpec((None,) + W["w_gate_up"].shape[1:], lambda t, te, tv: (te[t], 0, 0)),
                  pl.BlockSpec((None,) + W["b_gate_up"].shape[1:], lambda t, te, tv: (te[t], 0, 0)),
                  pl.BlockSpec((None,) + W["w_down"].shape[1:], lambda t, te, tv: (te[t], 0, 0)),
                  pl.BlockSpec((None,) + W["b_down"].shape[1:], lambda t, te, tv: (te[t], 0, 0))],
        out_specs=pl.BlockSpec((tm, d), lambda t, te, tv: (t, 0)),
        scratch_shapes=[pltpu.VMEM(W["w_gate_up"].shape[1:], BF16), pltpu.VMEM(W["w_down"].shape[1:], BF16)],
    )
    return pl.pallas_call(
        _expert_kernel, grid_spec=gs, out_shape=jax.ShapeDtypeStruct((ns, d), F32),
        compiler_params=_cparams(("arbitrary",)), name="experts",
    )(tile_expert, tile_valid, x_sorted, W["w_gate_up"], W["b_gate_up"], W["w_down"], W["b_down"])


def _final_kernel(x1_ref, tw_ref, *rest):
    parts, (g2_ref, gf_ref, o_ref) = rest[:TOP_K], rest[TOP_K:]
    tw = tw_ref[...]
    moe = tw[:, 0:1] * parts[0][...]
    for k in range(1, TOP_K):
        moe = moe + tw[:, k:k + 1] * parts[k][...]
    o_ref[...] = _rms(x1_ref[...] + g2_ref[...] * moe, gf_ref[...])


def _final(x1, top_w, moe_parts, g2, g_final, *, rows_per_mod, tm):
    t, d = x1.shape
    if rows_per_mod == 1:
        tiles_per_mod = (t // g2.shape[0]) // tm
        mod_map = lambda i: (i // tiles_per_mod, 0, 0)
    else:
        mod_map = lambda i: (i, 0, 0)
    row = lambda i: (i, 0)
    return pl.pallas_call(
        _final_kernel, grid=(t // tm,),
        in_specs=[pl.BlockSpec((tm, d), row), pl.BlockSpec((tm, LANE), row)]
        + [pl.BlockSpec((tm, d), functools.partial(lambda i, k: (k * (t // tm) + i, 0), k=k)) for k in range(TOP_K)]
        + [pl.BlockSpec((None, rows_per_mod, d), mod_map), pl.BlockSpec((1, d), lambda i: (0, 0))],
        out_specs=pl.BlockSpec((tm, d), row), out_shape=jax.ShapeDtypeStruct((t, d), F32),
        compiler_params=_cparams(("arbitrary",)), name="final_norm")(x1, top_w, *([moe_parts] * TOP_K), g2, g_final)


def _t5_bucket(dist):
    n = jnp.maximum(dist, 0)
    max_exact = NUM_BUCKETS // 2
    nf = jnp.maximum(n, 1).astype(F32)
    large = max_exact + (jnp.log(nf / max_exact) / math.log(MAX_DISTANCE / max_exact)
                         * (NUM_BUCKETS - max_exact)).astype(I32)
    return jnp.where(n < max_exact, n, jnp.minimum(large, NUM_BUCKETS - 1))


def _bias_rows(rel_bias, dist):
    bucket = _t5_bucket(dist)
    out = jnp.zeros((rel_bias.shape[1],) + dist.shape, F32)
    for b in range(NUM_BUCKETS):
        out = out + jnp.where(bucket == b, 1.0, 0.0)[None] * rel_bias[b][:, None, None]
    return out * LOG2E


def _rope_table(pos):
    half = MLA_ROPE // 2
    inv_freq = 1.0 / (ROPE_THETA ** (jnp.arange(half, dtype=F32) / half))
    ang = pos.astype(F32)[:, None] * inv_freq[None, :]
    pad = jnp.zeros((pos.shape[0], LANE - MLA_ROPE), F32)
    cos, sin = jnp.cos(ang), jnp.sin(ang)
    return jnp.concatenate([cos, cos, pad, sin, sin, pad], axis=1)


def _rot_cols(w):
    half = MLA_ROPE // 2
    return jnp.concatenate([-w[..., half:], w[..., :half]], axis=-1)


def _pad_last(w, n):
    return jnp.pad(w, [(0, 0)] * (w.ndim - 1) + [(0, n - w.shape[-1])])


def _pack_weights(w_in, g_norm1, g_norm2, g_q_a, w_q_b, g_kv_a, w_kv_b, cmp_pe, cmp_w1, cmp_w2, w_o, w_router,
                  b_router, w_gate_up, b_gate_up, w_down, b_down):
    d = w_in.shape[0]
    sizes = (MLA_Q_LORA, MLA_KV_LORA, MLA_ROPE, NSA_HEADS * HEAD_DIM, KV_COLS, KV_COLS, KV_COLS, 3 * NSA_HEADS)
    offs = [0]
    for s in sizes:
        offs.append(offs[-1] + s)
    w_qa, w_ckv, w_kr, w_qn, w_cmp, w_slc, w_swa, w_g = [w_in[:, offs[k]:offs[k + 1]] for k in range(8)]
    wq = w_qn.reshape(d, NSA_KV_HEADS, NSA_GROUP, HEAD_DIM)
    wq_pad = jnp.concatenate(
        [jnp.pad(wq[:, kv], ((0, 0), (0, 0), (kv * HEAD_DIM, LANE - (kv + 1) * HEAD_DIM))).reshape(d, NSA_GROUP * LANE)
         for kv in range(NSA_KV_HEADS)], axis=1)
    wg = jnp.transpose(w_g.reshape(d, NSA_KV_HEADS, NSA_GROUP, 3), (0, 1, 3, 2)).reshape(d, NSA_KV_HEADS, 3 * NSA_GROUP)
    wg_pad = _pad_last(wg, LANE).reshape(d, NSA_KV_HEADS * LANE)
    w_in_p = jnp.concatenate([w_qa, w_ckv, wq_pad, w_cmp, w_slc, w_swa, _pad_last(w_kr, LANE),
                              _pad_last(_rot_cols(w_kr), LANE), wg_pad], axis=1).astype(BF16)
    assert w_in_p.shape[1] == _C_END
    nope = _pad_last(w_q_b[:, :, :MLA_NOPE], LANE).reshape(MLA_Q_LORA, MLA_HEADS * LANE)
    rp = w_q_b[:, :, MLA_NOPE:]
    w_qb = jnp.concatenate([nope, _pad_last(rp, LANE).reshape(MLA_Q_LORA, -1),
                            _pad_last(_rot_cols(rp), LANE).reshape(MLA_Q_LORA, -1)], axis=1).astype(BF16)
    w_kn = jnp.transpose(w_kv_b[:, :, :MLA_NOPE], (1, 2, 0))
    w_kn = jnp.pad(w_kn, ((0, 0), (0, LANE - MLA_NOPE), (0, 0))).astype(BF16)
    wv = jnp.transpose(w_kv_b[:, :, MLA_NOPE:], (1, 0, 2))
    wv_pad = jnp.stack([jnp.pad(wv[h], ((0, 0), (h * MLA_V, (MLA_HEADS - 1 - h) * MLA_V))) for h in range(MLA_HEADS)]).astype(BF16)
    n_mla = MLA_HEADS * MLA_V
    won = w_o[n_mla:].reshape(NSA_KV_HEADS, NSA_GROUP, HEAD_DIM, d)
    won_pad = jnp.concatenate(
        [jnp.pad(won[kv], ((0, 0), (kv * HEAD_DIM, LANE - (kv + 1) * HEAD_DIM), (0, 0))).reshape(NSA_GROUP * LANE, d)
         for kv in range(NSA_KV_HEADS)], axis=0).astype(BF16)
    base = jnp.concatenate([cmp_w1[:, :CMP_STRIDE], cmp_w1[:, CMP_STRIDE:]], axis=-1)
    z = jnp.zeros_like(base)
    blk = jnp.concatenate([jnp.concatenate([base, z], axis=-1), jnp.concatenate([z, base], axis=-1)], axis=2)
    w_pair = blk.reshape(2, CMP_STRIDE // 2, 2 * NSA_KV_HEADS * HEAD_DIM, NSA_KV_HEADS * 2 * CMP_HIDDEN).astype(BF16)
    pe8 = jnp.broadcast_to(cmp_pe.reshape(2, 1, CMP_LEN * HEAD_DIM), (2, 8, CMP_LEN * HEAD_DIM))
    w1r = cmp_w1.reshape(2, CMP_LEN * HEAD_DIM, CMP_HIDDEN).astype(BF16)
    w2pad = jnp.stack([jnp.stack([jnp.pad(cmp_w2[j], ((0, 0), (kv * HEAD_DIM, LANE - (kv + 1) * HEAD_DIM)))
                                  for kv in range(NSA_KV_HEADS)]) for j in range(2)]).astype(BF16)
    return dict(
        w_in=w_in_p, g_norm1=g_norm1[None], g_norm2=g_norm2[None], g_q_a=g_q_a[None], g_kv_a=g_kv_a[None],
        w_qb=w_qb, w_kn=w_kn, wv=wv_pad, wv_cat=w_kv_b[:, :, MLA_NOPE:].reshape(MLA_KV_LORA, n_mla).astype(BF16),
        w_o_mla=w_o[:n_mla].astype(BF16), w_o_nsa=won_pad,
        w_pair=w_pair, pe8=pe8, w1r=w1r, w2pad=w2pad,
        w_router=_pad_last(w_router, LANE).T, b_router=jnp.pad(b_router, (0, LANE - N_EXPERTS), constant_values=NEG)[None],
        w_gate_up=w_gate_up, b_gate_up=b_gate_up[:, None, :], w_down=w_down, b_down=b_down[:, None, :])


def _overlap(n_half_pad, n_sel, n_sel_pad):
    c_start = jnp.arange(n_half_pad) * CMP_STRIDE
    s_start = jnp.arange(n_sel_pad) * SEL_BLOCK
    ov = (c_start[:, None] < s_start[None, :] + SEL_BLOCK) & (c_start[:, None] + CMP_LEN > s_start[None, :])
    return (ov & (jnp.arange(n_sel_pad) < n_sel)[None, :]).astype(F32)


def _moe_dispatch(top_i, tm):
    t = top_i.shape[0]
    a = t * TOP_K
    n_tiles = -(-a // tm) + N_EXPERTS
    ns = n_tiles * tm
    e_flat = top_i[:, :TOP_K].reshape(a)
    onehot = (e_flat[:, None] == jnp.arange(N_EXPERTS, dtype=I32)[None, :]).astype(I32)
    csum = jnp.cumsum(onehot, axis=0)
    counts = csum[-1]
    padded = ((counts + tm - 1) // tm) * tm
    pend = jnp.cumsum(padded)
    pstart = pend - padded
    start = jnp.cumsum(counts) - counts
    slot_of_assign = jnp.sum(onehot * (pstart[None, :] + csum - 1), axis=1).reshape(t, TOP_K)
    _, order = lax.sort((e_flat, jnp.arange(a, dtype=I32)), num_keys=1, is_stable=True)
    tile_start = jnp.arange(n_tiles, dtype=I32) * tm
    tile_expert = jnp.minimum(jnp.sum((pend[None, :] <= tile_start[:, None]).astype(I32), axis=1), N_EXPERTS - 1)
    tile_valid = (tile_start < pend[-1]).astype(I32)
    e_hot = (tile_expert[:, None] == jnp.arange(N_EXPERTS, dtype=I32)[None, :]).astype(I32)
    rank0 = tile_start - jnp.sum(e_hot * pstart[None, :], axis=1)
    lane = jnp.arange(tm, dtype=I32)[None, :]
    base = jnp.sum(e_hot * start[None, :], axis=1) + rank0
    run = order[jnp.clip(base[:, None] + lane, 0, a - 1)]
    rank = rank0[:, None] + lane
    valid = (rank < jnp.sum(e_hot * counts[None, :], axis=1, keepdims=True)) & (tile_valid[:, None] > 0)
    tok_of_slot = jnp.where(valid, run // TOP_K, 0).reshape(ns)
    return tok_of_slot, slot_of_assign, tile_expert, tile_valid


def kernel(x_prompt, x_sample, c_prompt, c_sample, cache_mla, cache_nsa_cmp, cache_nsa_slc, state_nsa_swa, page_table, rel_bias, w_ada, b_ada, g_norm1, g_norm2, w_in, g_q_a, w_q_b, g_kv_a, w_kv_b, cmp_pe, cmp_w1, cmp_w2, w_o, w_router, b_router, w_gate_up, b_gate_up, w_down, b_down, g_final):
    depth = w_in.shape[0]
    assert depth == 1, "single-layer decoder step"
    nb, seq, d = x_prompt.shape
    nbd, nq, _ = x_sample.shape
    n_pages = page_table.shape[1]
    page = cache_mla.shape[2]
    past = n_pages * page
    buf = state_nsa_swa.shape[2]
    assert page == LANE and nq <= 8 and seq % 512 == 0 and past % SEL_BLOCK == 0 and seq >= WINDOW
    tp, ts = nb * seq, nbd * nq
    W = _pack_weights(w_in[0], g_norm1[0], g_norm2[0], g_q_a[0], w_q_b[0], g_kv_a[0], w_kv_b[0], cmp_pe[0], cmp_w1[0],
                      cmp_w2[0], w_o[0], w_router[0], b_router[0], w_gate_up[0], b_gate_up[0], w_down[0], b_down[0])

    n_c = nb + nbd
    n_c_pad = -(-n_c // 8) * 8
    c_all = jnp.pad(jnp.concatenate([c_prompt, c_sample], axis=0), ((0, n_c_pad - n_c), (0, 0)))
    mod = _ada_mod(c_all, w_ada[0].astype(BF16), b_ada[0][None])
    mod_p = [m[:, None, :] for m in jnp.split(mod[:nb], 6, axis=-1)]
    tm_s = min(TOKEN_TILE, ts)
    mod_s = [jnp.repeat(m, nq, axis=0).reshape(ts // tm_s, tm_s, d) for m in jnp.split(mod[nb:n_c], 6, axis=-1)]

    tm_p = TOKEN_TILE
    pos_p = jnp.arange(seq)
    pos_s = past + jnp.arange(nq)
    P = _proj(x_prompt.reshape(tp, d), mod_p[1], mod_p[0], _rope_table(pos_p), W, rows_per_mod=1,
              cs_period_tiles=seq // tm_p, tm=tm_p, states_t_batches=nb)
    cs_s = jnp.tile(_rope_table(pos_s), (tm_s // nq, 1))
    S_ = _proj(x_sample.reshape(ts, d), mod_s[1], mod_s[0], cs_s, W, rows_per_mod=tm_s, cs_period_tiles=1, tm=tm_s)
    p_qmla, p_kq, p_mla_t, p_qn, p_cmp, p_cmp_t, p_slc_t, p_swa_t, p_slcb, p_swab, p_gate = P
    s_qmla, s_kq, s_mla, s_qn, _, s_cmp, s_slc, s_swa, s_slcb, s_swab, s_gate = S_

    def pad_new(a):
        return jnp.pad(a.reshape(nbd, nq, a.shape[1]), ((0, 0), (0, LANE - nq), (0, 0)))

    o_mla_p = _mla_prompt(p_qmla, p_kq, W["wv"], nb=nb, seq=seq)
    g_pages = min(PAGES_PER_STEP, n_pages)
    cache_mla_t = jnp.swapaxes(cache_mla[0], 1, 2)
    cache_cmp_t = jnp.moveaxis(cache_nsa_cmp[0], 1, -1).reshape(-1, KV_COLS, page)
    cache_slc_t = jnp.moveaxis(cache_nsa_slc[0], 1, -1).reshape(-1, KV_COLS, page)
    state_swa_t = jnp.moveaxis(state_nsa_swa[0], 1, -1).reshape(nbd, KV_COLS, buf)
    o_mla_s = _mla_sample(page_table, s_qmla.reshape(nbd, nq, -1), pad_new(s_kq), W["wv_cat"], cache_mla_t, g=g_pages,
                          nbs=2 if nbd % 2 == 0 else 1)

    assert (past + nq) // CMP_STRIDE == past // CMP_STRIDE
    n_pp = tp // LANE
    kc_p, vc_p = _compress(jnp.arange(n_pp, dtype=I32), p_cmp.reshape(n_pp, LANE, KV_COLS), W["w_pair"], W["pe8"],
                           W["w1r"], W["w2pad"], nb=nb, transposed=False)
    kc_s, vc_s = _compress(page_table.reshape(nbd * n_pages), cache_cmp_t, W["w_pair"], W["pe8"], W["w1r"], W["w2pad"],
                           nb=nbd, transposed=True)

    nh_p = seq // CMP_STRIDE
    n_sel_p = -(-seq // SEL_BLOCK)
    tq = QUERY_TILE
    far_h = _bias_rows(rel_bias, jnp.full((1, 1), MAX_DISTANCE))
    d_win = (jnp.arange(tq)[:, None] - CMP_STRIDE * (jnp.arange(LANE)[None, :] - CMP_WIN_LO) - (CMP_LEN - 1))
    assert CMP_WIN_LO * CMP_STRIDE >= MAX_DISTANCE + CMP_LEN and LANE - CMP_WIN_LO >= tq // CMP_STRIDE
    bias_c_p = jnp.where((d_win >= 0)[None], _bias_rows(rel_bias, d_win) - far_h, 0.0)
    bias_c_p = bias_c_p.reshape(NSA_HEADS * tq, LANE)
    di = jnp.arange(tq)[:, None] - jnp.arange(tq)[None, :]
    tt = jnp.stack([_bias_rows(rel_bias, di), _bias_rows(rel_bias, di + tq),
                    _bias_rows(rel_bias, jnp.full((tq, tq), MAX_DISTANCE))], axis=1)
    tt = tt.transpose(1, 0, 2, 3).reshape(3, NSA_HEADS * tq, tq)
    assert n_sel_p <= LANE
    ovl_p = jnp.tile(_overlap(nh_p, n_sel_p, LANE).T, (1, 3)).astype(BF16)
    o_nsa_p = _nsa_prompt(p_qn, kc_p, vc_p, bias_c_p, ovl_p, p_slcb.reshape(nb, seq, KV_COLS),
                          p_swab.reshape(nb, seq, KV_COLS), tt, p_gate, nb=nb, seq=seq, n_cmp=nh_p - 1, n_sel=n_sel_p)

    nh_s = past // CMP_STRIDE
    n_sel_s = -(-(past + nq) // SEL_BLOCK)
    assert n_sel_s == past // SEL_BLOCK + 1
    n_sel_pad = -(-n_sel_s // LANE) * LANE
    rows_s = NSA_HEADS * nq
    cmp_end_s = jnp.arange(nh_s) * CMP_STRIDE + CMP_LEN - 1
    bias_c_s = _bias_rows(rel_bias, pos_s[:, None] - cmp_end_s[None, :]).reshape(rows_s, nh_s)
    nk_step = g_pages * page
    far = jnp.broadcast_to(_bias_rows(rel_bias, jnp.full((nq, 1), MAX_DISTANCE)).reshape(rows_s, 1), (rows_s, nk_step))
    tail_pos = past - nk_step + jnp.arange(nk_step)
    bias_s = jnp.stack([far, _bias_rows(rel_bias, pos_s[:, None] - tail_pos[None, :]).reshape(rows_s, nk_step)])
    new_pos = past + jnp.arange(LANE)
    bias_new = _bias_rows(rel_bias, pos_s[:, None] - new_pos[None, :]).reshape(rows_s, LANE)
    swa_pos = past - buf + jnp.arange(buf)
    bias_w = _bias_rows(rel_bias, pos_s[:, None] - swa_pos[None, :]).reshape(rows_s, buf)
    ovl_s = jnp.tile(_overlap(nh_s, n_sel_s, n_sel_pad).T, (1, 3)).astype(BF16)
    o_nsa_s = _nsa_sample(page_table, s_qn.reshape(nbd, nq, -1), kc_s, vc_s, bias_c_s, ovl_s, bias_s, pad_new(s_slcb),
                          bias_new, state_swa_t, pad_new(s_swab), bias_w, bias_new,
                          s_gate.reshape(nbd, nq, -1), cache_slc_t, g=g_pages, nbs=2 if nbd % 2 == 0 else 1,
                          n_cmp=nh_s - 1, n_sel=n_sel_s)

    x1_p, h2_p, ti_p, tw_p = _outproj(x_prompt.reshape(tp, d), o_mla_p, o_nsa_p, mod_p[2], mod_p[4], mod_p[3], W,
                                      rows_per_mod=1, tm=2 * tm_p)
    x1_s, h2_s, ti_s, tw_s = _outproj(x_sample.reshape(ts, d), o_mla_s.reshape(ts, -1), o_nsa_s.reshape(ts, -1),
                                      mod_s[2], mod_s[4], mod_s[3], W, rows_per_mod=tm_s, tm=tm_s)

    tm_e = EXPERT_TILE
    h2 = jnp.concatenate([h2_p, h2_s], axis=0)
    tok_of_slot, slot_of_assign, tile_expert, tile_valid = _moe_dispatch(jnp.concatenate([ti_p, ti_s], axis=0), tm_e)
    y_sorted = _experts(tile_expert, tile_valid, h2[tok_of_slot], W, tm=tm_e)
    parts_p = y_sorted[slot_of_assign[:tp].T.reshape(-1)]
    parts_s = y_sorted[slot_of_assign[tp:].T.reshape(-1)]
    y_p = _final(x1_p, tw_p, parts_p, mod_p[5], g_final[None], rows_per_mod=1, tm=tm_p)
    y_s = _final(x1_s, tw_s, parts_s, mod_s[5], g_final[None], rows_per_mod=tm_s, tm=tm_s)

    kv_tail = (2, NSA_KV_HEADS, HEAD_DIM)
    keep_p = min(WINDOW, seq)
    swa_keys = jnp.concatenate([state_nsa_swa[0], s_swa.reshape((nbd, nq) + kv_tail)], axis=1)
    keep_s = min(WINDOW, buf + nq)

    def rows_last(a_t):
        return jnp.moveaxis(a_t.reshape((nb,) + kv_tail + (a_t.shape[-1],)), -1, 1)[None]

    return (y_p.reshape(nb, seq, d), y_s.reshape(nbd, nq, d),
            jnp.swapaxes(p_mla_t, 1, 2)[None], s_mla.reshape(1, nbd, nq, -1),
            rows_last(p_cmp_t), s_cmp.reshape((1, nbd, nq) + kv_tail),
            rows_last(p_slc_t), s_slc.reshape((1, nbd, nq) + kv_tail),
            rows_last(p_swa_t[:, :, seq - keep_p:]),
            swa_keys[None, :, buf + nq - keep_s:])
```
